```python
import jax
import jax.numpy as jnp
from jax import lax
import numpy as np

D_MODEL = 2048
BATCH = 4
SEQ = 2048
DEPTH = 4

N_A_LAYERS = DEPTH // 2
N_B_LAYERS = DEPTH - N_A_LAYERS
MEM_LEN = 256
MEM_HEADS = 4
MEM_WIDTH = D_MODEL // 4
MEM_HEAD_DIM = MEM_WIDTH // MEM_HEADS
MIX_WIDTH = D_MODEL - MEM_WIDTH
LRU_WIDTH = MIX_WIDTH
LRU_BLOCKS = 12
LRU_BLOCK = LRU_WIDTH // LRU_BLOCKS
CONV_WIDTH = 4
LRU_C = 8.0
V_DIM = 128
MLA_HEADS = MIX_WIDTH // V_DIM
QK_NOPE = 128
QK_ROPE = 64
QK_DIM = QK_NOPE + QK_ROPE
Q_RANK = 512
KV_RANK = 256
ROPE_THETA = 10000.0
Q_BLOCK = 128
N_GROUPS = 8
EXPERTS_PER_GROUP = 4
TOP_K_IN_GROUP = 2
D_EXPERT = 512
EPS = 1e-6
IN_A_WIDTH = 2 * LRU_WIDTH + MEM_WIDTH
IN_B_WIDTH = Q_RANK + MEM_WIDTH
F32 = jnp.float32

kernel_name = 'hawk_mla_yoco_hier_moe_trunk'


def rmsnorm(x, g):
    xf = x.astype(F32)
    y = xf * lax.rsqrt(jnp.mean(xf * xf, axis=-1, keepdims=True) + EPS)
    return (y * g.astype(F32)).astype(x.dtype)


def rope(t, positions):
    half = QK_ROPE // 2
    inv_freq = ROPE_THETA ** (-jnp.arange(half, dtype=F32) / half)
    ang = positions.astype(F32)[..., None] * inv_freq
    cos = jnp.cos(ang)[:, :, None, :]
    sin = jnp.sin(ang)[:, :, None, :]
    tf = t.astype(F32)
    t1, t2 = tf[..., :half], tf[..., half:]
    out = jnp.concatenate([t1 * cos - t2 * sin, t2 * cos + t1 * sin], axis=-1)
    return out.astype(t.dtype)


def head_norm_rope(t, g, positions):
    t = rmsnorm(t, g)
    return jnp.concatenate([t[..., :QK_NOPE], rope(t[..., QK_NOPE:], positions)], axis=-1)


def causal_block_attention(q, k, v):
    B, S, H, _ = q.shape
    nb = S // Q_BLOCK
    scale = QK_DIM ** -0.5
    qb = q.reshape(B, nb, Q_BLOCK, H, QK_DIM).transpose(1, 0, 2, 3, 4)
    key_idx = jnp.arange(S)

    def one_block(args):
        q_blk, start = args
        s = jnp.einsum('bqhd,bkhd->bhqk', q_blk, k, preferred_element_type=F32) * scale
        q_idx = start + jnp.arange(Q_BLOCK)
        mask = key_idx[None, :] <= q_idx[:, None]
        s = jnp.where(mask[None, None], s, jnp.finfo(F32).min)
        p = jax.nn.softmax(s, axis=-1).astype(v.dtype)
        return jnp.einsum('bhqk,bkhd->bqhd', p, v)

    out = lax.map(one_block, (qb, jnp.arange(nb) * Q_BLOCK))
    return out.transpose(1, 0, 2, 3, 4).reshape(B, S, H * V_DIM).astype(q.dtype)


def memory_attention(q, mem, mem_norm_g, w_mem_kv, q_g, k_g):
    B, S, _ = q.shape
    qh = rmsnorm(q.reshape(B, S, MEM_HEADS, MEM_HEAD_DIM), q_g)
    kv = rmsnorm(mem, mem_norm_g) @ w_mem_kv
    M = kv.shape[1]
    kh = rmsnorm(kv[..., :MEM_WIDTH].reshape(B, M, MEM_HEADS, MEM_HEAD_DIM), k_g)
    vh = kv[..., MEM_WIDTH:].reshape(B, M, MEM_HEADS, MEM_HEAD_DIM)
    s = jnp.einsum('bshd,bmhd->bhsm', qh, kh, preferred_element_type=F32) * (MEM_HEAD_DIM ** -0.5)
    p = jax.nn.softmax(s, axis=-1).astype(vh.dtype)
    out = jnp.einsum('bhsm,bmhd->bshd', p, vh)
    return out.reshape(B, S, MEM_WIDTH).astype(q.dtype)


def rglru_branch(u, conv_w, conv_b, w_r, b_r, w_i, b_i, lam):
    B, S, W = u.shape
    up = jnp.pad(u, ((0, 0), (CONV_WIDTH - 1, 0), (0, 0)))
    y = conv_b
    for kk in range(CONV_WIDTH):
        y = y + conv_w[kk] * up[:, CONV_WIDTH - 1 - kk: CONV_WIDTH - 1 - kk + S]
    yb = y.reshape(B, S, LRU_BLOCKS, LRU_BLOCK)
    r = jax.nn.sigmoid(jnp.einsum('bsni,nij->bsnj', yb, w_r).reshape(B, S, W).astype(F32) + b_r.astype(F32))
    i = jax.nn.sigmoid(jnp.einsum('bsni,nij->bsnj', yb, w_i).reshape(B, S, W).astype(F32) + b_i.astype(F32))
    log_a = -LRU_C * r * jax.nn.softplus(-lam.astype(F32))
    a = jnp.exp(log_a)
    b = jnp.sqrt(-jnp.expm1(2.0 * log_a)) * (i * y.astype(F32))

    def combine(left, right):
        a1, b1 = left
        a2, b2 = right
        return a1 * a2, a2 * b1 + b2

    _, h = lax.associative_scan(combine, (a, b), axis=1)
    return h.astype(u.dtype)


def shared_latent_kv(h, positions, kv_in_norm_g, w_dkv, kv_latent_norm_g, w_uk, w_uv, k_head_norm_g):
    B, S, _ = h.shape
    ckv = rmsnorm(h, kv_in_norm_g) @ w_dkv
    c = rmsnorm(ckv[..., :KV_RANK], kv_latent_norm_g)
    k_pe = ckv[..., KV_RANK:]
    k_nope = (c @ w_uk).reshape(B, S, MLA_HEADS, QK_NOPE)
    v = (c @ w_uv).reshape(B, S, MLA_HEADS, V_DIM)
    k = jnp.concatenate([k_nope, jnp.broadcast_to(k_pe[:, :, None, :], (B, S, MLA_HEADS, QK_ROPE))], axis=-1)
    k = head_norm_rope(k, k_head_norm_g, positions)
    return k, v


def hier_moe(x, w_rg, b_rg, w_re, b_re, w_gate, w_up, w_down):
    B, S, D = x.shape
    xt = x.reshape(B * S, D)
    xf = xt.astype(F32)
    grp_logits = xf @ w_rg.astype(F32) + b_rg.astype(F32)
    p_grp = jax.nn.softmax(grp_logits, axis=-1)
    p_top, g_idx = lax.top_k(p_grp, 1)
    exp_logits = (xf @ w_re.astype(F32) + b_re.astype(F32)).reshape(-1, N_GROUPS, EXPERTS_PER_GROUP)
    sel = jnp.take_along_axis(exp_logits, g_idx[:, :, None], axis=1)[:, 0]
    p_in = jax.nn.softmax(sel, axis=-1)
    w_top, e_idx = lax.top_k(p_in, TOP_K_IN_GROUP)
    w_top = w_top / jnp.sum(w_top, axis=-1, keepdims=True)
    w_in = jnp.sum(jax.nn.one_hot(e_idx, EXPERTS_PER_GROUP, dtype=F32) * w_top[..., None], axis=1)
    gate = jax.nn.one_hot(g_idx[:, 0], N_GROUPS, dtype=F32)[:, :, None] * (p_top[:, :, None] * w_in[:, None, :])
    out = jnp.zeros(xf.shape, F32)
    for g in range(N_GROUPS):
        hg = jnp.einsum('td,edf->tef', xt, w_gate[g])
        hu = jnp.einsum('td,edf->tef', xt, w_up[g])
        hh = jax.nn.silu(hg) * hu * gate[:, g, :, None].astype(xt.dtype)
        out = out + jnp.einsum('tef,efd->td', hh, w_down[g]).astype(F32)
    return out.astype(x.dtype).reshape(B, S, D)


def setup_inputs(seed: int = 0) -> dict:
    key = jax.random.key(seed)
    ks = iter(jax.random.split(key, 40))

    def nrm(shape, scale):
        return jax.random.normal(next(ks), shape, F32) * scale

    def gain(shape):
        return 1.0 + 0.02 * jax.random.normal(next(ks), shape, F32)

    x = nrm((BATCH, SEQ, D_MODEL), 1.0)
    mem = nrm((BATCH, MEM_LEN, D_MODEL), 1.0)
    positions = jax.random.randint(next(ks), (BATCH, 1), 0, 1024, dtype=jnp.int32) + jnp.arange(SEQ, dtype=jnp.int32)[None, :]
    a0 = jax.random.uniform(next(ks), (N_A_LAYERS, LRU_WIDTH), F32, minval=0.9, maxval=0.999)
    lru_lambda = jnp.log(a0) - jnp.log1p(-a0)
    return {
        'x': x,
        'mem': mem,
        'positions': positions,
        'norm_mix_g': gain((DEPTH, D_MODEL)),
        'norm_ffn_g': gain((DEPTH, D_MODEL)),
        'w_o': nrm((DEPTH, D_MODEL, D_MODEL), D_MODEL ** -0.5),
        'mem_norm_g': gain((DEPTH, D_MODEL)),
        'w_mem_kv': nrm((DEPTH, D_MODEL, 2 * MEM_WIDTH), D_MODEL ** -0.5),
        'mem_q_norm_g': gain((DEPTH, MEM_HEAD_DIM)),
        'mem_k_norm_g': gain((DEPTH, MEM_HEAD_DIM)),
        'w_in_a': nrm((N_A_LAYERS, D_MODEL, IN_A_WIDTH), D_MODEL ** -0.5),
        'conv_w': nrm((N_A_LAYERS, CONV_WIDTH, LRU_WIDTH), CONV_WIDTH ** -0.5),
        'conv_b': nrm((N_A_LAYERS, LRU_WIDTH), 0.01),
        'w_lru_r': nrm((N_A_LAYERS, LRU_BLOCKS, LRU_BLOCK, LRU_BLOCK), LRU_BLOCK ** -0.5),
        'b_lru_r': nrm((N_A_LAYERS, LRU_WIDTH), 0.01),
        'w_lru_i': nrm((N_A_LAYERS, LRU_BLOCKS, LRU_BLOCK, LRU_BLOCK), LRU_BLOCK ** -0.5),
        'b_lru_i': nrm((N_A_LAYERS, LRU_WIDTH), 0.01),
        'lru_lambda': lru_lambda,
        'kv_in_norm_g': gain((D_MODEL,)),
        'w_dkv': nrm((D_MODEL, KV_RANK + QK_ROPE), D_MODEL ** -0.5),
        'kv_latent_norm_g': gain((KV_RANK,)),
        'w_uk': nrm((KV_RANK, MLA_HEADS * QK_NOPE), KV_RANK ** -0.5),
        'w_uv': nrm((KV_RANK, MLA_HEADS * V_DIM), KV_RANK ** -0.5),
        'k_head_norm_g': gain((QK_DIM,)),
        'w_in_b': nrm((N_B_LAYERS, D_MODEL, IN_B_WIDTH), D_MODEL ** -0.5),
        'q_latent_norm_g': gain((N_B_LAYERS, Q_RANK)),
        'w_uq': nrm((N_B_LAYERS, Q_RANK, MLA_HEADS * QK_DIM), Q_RANK ** -0.5),
        'q_head_norm_g': gain((N_B_LAYERS, QK_DIM)),
        'w_router_grp': nrm((DEPTH, D_MODEL, N_GROUPS), D_MODEL ** -0.5),
        'b_router_grp': nrm((DEPTH, N_GROUPS), 0.01),
        'w_router_exp': nrm((DEPTH, D_MODEL, N_GROUPS * EXPERTS_PER_GROUP), D_MODEL ** -0.5),
        'b_router_exp': nrm((DEPTH, N_GROUPS * EXPERTS_PER_GROUP), 0.01),
        'w_exp_gate': nrm((DEPTH, N_GROUPS, EXPERTS_PER_GROUP, D_MODEL, D_EXPERT), D_MODEL ** -0.5),
        'w_exp_up': nrm((DEPTH, N_GROUPS, EXPERTS_PER_GROUP, D_MODEL, D_EXPERT), D_MODEL ** -0.5),
        'w_exp_down': nrm((DEPTH, N_GROUPS, EXPERTS_PER_GROUP, D_EXPERT, D_MODEL), D_EXPERT ** -0.5),
    }


def reference(x, mem, positions, norm_mix_g, norm_ffn_g, w_o, mem_norm_g, w_mem_kv, mem_q_norm_g, mem_k_norm_g,
              w_in_a, conv_w, conv_b, w_lru_r, b_lru_r, w_lru_i, b_lru_i, lru_lambda,
              kv_in_norm_g, w_dkv, kv_latent_norm_g, w_uk, w_uv, k_head_norm_g,
              w_in_b, q_latent_norm_g, w_uq, q_head_norm_g,
              w_router_grp, b_router_grp, w_router_exp, b_router_exp, w_exp_gate, w_exp_up, w_exp_down):
    B, S, _ = x.shape
    k_sh = None
    v_sh = None
    for l in range(DEPTH):
        xn = rmsnorm(x, norm_mix_g[l])
        if l < N_A_LAYERS:
            i = l
            proj = xn @ w_in_a[i]
            u = proj[..., :LRU_WIDTH]
            g_br = proj[..., LRU_WIDTH:2 * LRU_WIDTH]
            q_mem = proj[..., 2 * LRU_WIDTH:]
            h = rglru_branch(u, conv_w[i], conv_b[i], w_lru_r[i], b_lru_r[i], w_lru_i[i], b_lru_i[i], lru_lambda[i])
            y_mix = jax.nn.gelu(g_br) * h
        else:
            j = l - N_A_LAYERS
            proj = xn @ w_in_b[j]
            c_q = rmsnorm(proj[..., :Q_RANK], q_latent_norm_g[j])
            q_mem = proj[..., Q_RANK:]
            q = (c_q @ w_uq[j]).reshape(B, S, MLA_HEADS, QK_DIM)
            q = head_norm_rope(q, q_head_norm_g[j], positions)
            y_mix = causal_block_attention(q, k_sh, v_sh)
        y_mem = memory_attention(q_mem, mem, mem_norm_g[l], w_mem_kv[l], mem_q_norm_g[l], mem_k_norm_g[l])
        x = x + jnp.concatenate([y_mix, y_mem], axis=-1) @ w_o[l]
        x = x + hier_moe(rmsnorm(x, norm_ffn_g[l]), w_router_grp[l], b_router_grp[l], w_router_exp[l], b_router_exp[l],
                         w_exp_gate[l], w_exp_up[l], w_exp_down[l])
        if l == N_A_LAYERS - 1:
            k_sh, v_sh = shared_latent_kv(x, positions, kv_in_norm_g, w_dkv, kv_latent_norm_g, w_uk, w_uv, k_head_norm_g)
    return x
```

```python
import functools

import jax
import jax.numpy as jnp
from jax import lax
from jax.experimental import pallas as pl
from jax.experimental.pallas import tpu as pltpu

F32 = jnp.float32
BF16 = jnp.bfloat16
EPS = 1e-6

MEM_HEADS = 4
CONV_WIDTH = 4
LRU_C = 8.0
QK_NOPE = 128
QK_ROPE = 64
QK_DIM = QK_NOPE + QK_ROPE
V_DIM = 128
ROPE_THETA = 10000.0
TOP_K_IN_GROUP = 2

LANES = 128
HEAD_PAD = 2 * LANES
MIB = 1024 * 1024


def _params(sem, vmem_mib):
    return pltpu.CompilerParams(dimension_semantics=sem, vmem_limit_bytes=vmem_mib * MIB)


def _rms_rows(x, g):
    x = x.astype(F32)
    ms = jnp.mean(x * x, axis=-1, keepdims=True)
    return x * lax.rsqrt(ms + EPS) * g


def _norm_mm_kernel(x_ref, g_ref, w_ref, o_ref, xn_ref, *, tm, chunk):
    @pl.when(pl.program_id(1) == 0)
    def _():
        for r0 in range(0, tm, chunk):
            xn_ref[r0:r0 + chunk, :] = _rms_rows(x_ref[r0:r0 + chunk, :], g_ref[...]).astype(BF16)

    o_ref[...] = jnp.dot(xn_ref[...], w_ref[...].astype(BF16),
                         preferred_element_type=F32).astype(o_ref.dtype)


def norm_matmul(x, g, w, out_dtype, tm, tn):
    T, K = x.shape
    N = w.shape[1]
    chunk = min(tm, 256)
    return pl.pallas_call(
        functools.partial(_norm_mm_kernel, tm=tm, chunk=chunk),
        grid=(T // tm, N // tn),
        in_specs=[pl.BlockSpec((tm, K), lambda i, j: (i, 0)),
                  pl.BlockSpec((1, K), lambda i, j: (0, 0)),
                  pl.BlockSpec((K, tn), lambda i, j: (0, j))],
        out_specs=pl.BlockSpec((tm, tn), lambda i, j: (i, j)),
        out_shape=jax.ShapeDtypeStruct((T, N), out_dtype),
        scratch_shapes=[pltpu.VMEM((tm, K), BF16)],
        compiler_params=_params(("parallel", "arbitrary"), 48),
        name="norm_matmul",
    )(x, g.reshape(1, K), w)


def _out_proj_kernel(ya_ref, yb_ref, wa_ref, wb_ref, x_ref, o_ref):
    acc = jnp.dot(ya_ref[...], wa_ref[...].astype(BF16), preferred_element_type=F32)
    acc += jnp.dot(yb_ref[...], wb_ref[...].astype(BF16), preferred_element_type=F32)
    o_ref[...] = x_ref[...] + acc


def out_proj(y_mix, y_mem, w_o, x, tm, tn):
    T, Wa = y_mix.shape
    Wb = y_mem.shape[1]
    N = w_o.shape[1]
    assert Wa % Wb == 0
    return pl.pallas_call(
        _out_proj_kernel,
        grid=(T // tm, N // tn),
        in_specs=[pl.BlockSpec((tm, Wa), lambda i, j: (i, 0)),
                  pl.BlockSpec((tm, Wb), lambda i, j: (i, 0)),
                  pl.BlockSpec((Wa, tn), lambda i, j: (0, j)),
                  pl.BlockSpec((Wb, tn), lambda i, j: (Wa // Wb, j)),
                  pl.BlockSpec((tm, tn), lambda i, j: (i, j))],
        out_specs=pl.BlockSpec((tm, tn), lambda i, j: (i, j)),
        out_shape=jax.ShapeDtypeStruct((T, N), F32),
        compiler_params=_params(("parallel", "arbitrary"), 48),
        name="out_proj",
    )(y_mix, y_mem, w_o, w_o, x)


def _mem_attn_kernel(q_ref, kv_ref, qg_ref, kg_ref, o_ref, *, heads, hd):
    width = heads * hd
    scale = hd ** -0.5
    for h in range(heads):
        qn = _rms_rows(q_ref[:, h * hd:(h + 1) * hd], qg_ref[...]) * scale
        kn = _rms_rows(kv_ref[0, :, h * hd:(h + 1) * hd], kg_ref[...])
        v = kv_ref[0, :, width + h * hd:width + (h + 1) * hd]
        s = lax.dot_general(qn.astype(BF16), kn.astype(BF16), (((1,), (1,)), ((), ())),
                            preferred_element_type=F32)
        m = jnp.max(s, axis=-1, keepdims=True)
        p = jnp.exp(s - m)
        l = jnp.sum(p, axis=-1, keepdims=True)
        o = jnp.dot(p.astype(BF16), v, preferred_element_type=F32) / l
        o_ref[:, h * hd:(h + 1) * hd] = o.astype(o_ref.dtype)


def mem_attention(proj, q_col_block, kv, q_g, k_g, batch, ts):
    T = proj.shape[0]
    _, M, two_w = kv.shape
    width = two_w // 2
    hd = width // MEM_HEADS
    per_b = T // batch // ts
    return pl.pallas_call(
        functools.partial(_mem_attn_kernel, heads=MEM_HEADS, hd=hd),
        grid=(batch, per_b),
        in_specs=[pl.BlockSpec((ts, width), lambda b, i: (b * per_b + i, q_col_block)),
                  pl.BlockSpec((1, M, two_w), lambda b, i: (b, 0, 0)),
                  pl.BlockSpec((1, hd), lambda b, i: (0, 0)),
                  pl.BlockSpec((1, hd), lambda b, i: (0, 0))],
        out_specs=pl.BlockSpec((ts, width), lambda b, i: (b * per_b + i, 0)),
        out_shape=jax.ShapeDtypeStruct((T, width), BF16),
        compiler_params=_params(("parallel", "arbitrary"), 32),
        name="mem_attention",
    )(proj, kv, q_g.reshape(1, hd), k_g.reshape(1, hd))


def _gelu_tanh(x):
    return 0.5 * x * (1.0 + jnp.tanh(0.7978845608028654 * (x + 0.044715 * x * x * x)))


def _rglru_kernel(u_ref, gb_ref, cw_ref, cb_ref, wr_ref, wi_ref, br_ref, bi_ref, lam_ref,
                  o_ref, a_s, b_s, *, seq, nb, blk, tc):
    wb = nb * blk
    neg_lam = -lam_ref[...]
    softplus = jnp.maximum(neg_lam, 0.0) + jnp.log1p(jnp.exp(-jnp.abs(neg_lam)))

    def gates(c, carry):
        t0 = pl.multiple_of(c * tc, tc)
        cur = u_ref[0, pl.ds(t0, tc), :].astype(F32)
        p0 = pl.multiple_of(jnp.maximum(t0 - 16, 0), 16)
        prev = u_ref[0, pl.ds(p0, 16), :].astype(F32)
        prev = jnp.where(c > 0, prev, 0.0)
        full = jnp.concatenate([prev, cur], axis=0)
        y = cb_ref[...] + cw_ref[0:1, :] * cur
        for k in range(1, CONV_WIDTH):
            y = y + cw_ref[k:k + 1, :] * full[16 - k:16 - k + tc, :]
        r_parts, i_parts = [], []
        for n in range(nb):
            yb = y[:, n * blk:(n + 1) * blk].astype(BF16)
            r_parts.append(jnp.dot(yb, wr_ref[n].astype(BF16), preferred_element_type=F32))
            i_parts.append(jnp.dot(yb, wi_ref[n].astype(BF16), preferred_element_type=F32))
        r = jax.nn.sigmoid(jnp.concatenate(r_parts, axis=1) + br_ref[...])
        ig = jax.nn.sigmoid(jnp.concatenate(i_parts, axis=1) + bi_ref[...])
        log_a = (-LRU_C) * r * softplus
        a = jnp.exp(log_a)
        a_s[pl.ds(t0, tc), :] = a
        b_s[pl.ds(t0, tc), :] = jnp.sqrt(-jnp.tanh(log_a) * (a * a + 1.0)) * (ig * y)
        return carry

    lax.fori_loop(0, seq // tc, gates, 0)

    row = lax.broadcasted_iota(jnp.int32, (8, wb), 0)

    def scan(c, h):
        t0 = pl.multiple_of(c * 8, 8)
        a = a_s[pl.ds(t0, 8), :]
        b = b_s[pl.ds(t0, 8), :]
        for s in (1, 2, 4):
            a_sh = pltpu.roll(a, s, axis=0)
            b_sh = pltpu.roll(b, s, axis=0)
            keep = row >= s
            b = jnp.where(keep, a * b_sh + b, b)
            a = jnp.where(keep, a * a_sh, a)
        hc = a * h + b
        b_s[pl.ds(t0, 8), :] = hc
        return hc[7:8, :]

    lax.fori_loop(0, seq // 8, scan, jnp.zeros((1, wb), F32))

    def gate_out(c, carry):
        t0 = pl.multiple_of(c * tc, tc)
        g = gb_ref[0, pl.ds(t0, tc), :].astype(F32)
        o_ref[0, pl.ds(t0, tc), :] = (_gelu_tanh(g) * b_s[pl.ds(t0, tc), :]).astype(o_ref.dtype)
        return carry

    lax.fori_loop(0, seq // tc, gate_out, 0)


def rglru(proj, conv_w, conv_b, w_r, b_r, w_i, b_i, lam, nb, tc):
    B, S, _ = proj.shape
    W = lam.shape[0]
    n_blocks, blk, _ = w_r.shape
    wb = nb * blk
    ncb = W // wb
    vec = lambda: pl.BlockSpec((1, wb), lambda b, j: (0, j))
    return pl.pallas_call(
        functools.partial(_rglru_kernel, seq=S, nb=nb, blk=blk, tc=tc),
        grid=(B, ncb),
        in_specs=[pl.BlockSpec((1, S, wb), lambda b, j: (b, 0, j)),
                  pl.BlockSpec((1, S, wb), lambda b, j: (b, 0, ncb + j)),
                  pl.BlockSpec((CONV_WIDTH, wb), lambda b, j: (0, j)),
                  vec(),
                  pl.BlockSpec((nb, blk, blk), lambda b, j: (j, 0, 0)),
                  pl.BlockSpec((nb, blk, blk), lambda b, j: (j, 0, 0)),
                  vec(), vec(), vec()],
        out_specs=pl.BlockSpec((1, S, wb), lambda b, j: (b, 0, j)),
        out_shape=jax.ShapeDtypeStruct((B, S, W), BF16),
        scratch_shapes=[pltpu.VMEM((S, wb), F32), pltpu.VMEM((S, wb), F32)],
        compiler_params=_params(("parallel", "arbitrary"), 32),
        name="rglru",
    )(proj, proj, conv_w, conv_b.reshape(1, W), w_r, w_i, b_r.reshape(1, W), b_i.reshape(1, W),
      lam.reshape(1, W))


def _heads_kernel(*refs, rank, heads, with_v):
    if with_v:
        lat_ref, gl_ref, w_ref, hg_ref, cos_ref, sin_ref, wv_ref, o_ref, v_ref = refs
    else:
        lat_ref, gl_ref, w_ref, hg_ref, cos_ref, sin_ref, o_ref = refs
    lat = lat_ref[...].astype(F32)
    cn = _rms_rows(lat[:, :rank], gl_ref[...])
    full = cn if lat.shape[1] == rank else jnp.concatenate([cn, lat[:, rank:]], axis=1)
    full = full.astype(BF16)
    lane = lax.broadcasted_iota(jnp.int32, cos_ref.shape, 1)
    half = QK_ROPE // 2
    for h in range(heads):
        t = jnp.dot(full, w_ref[:, h * HEAD_PAD:(h + 1) * HEAD_PAD], preferred_element_type=F32)
        ss = jnp.sum(t * t, axis=-1, keepdims=True) * (1.0 / QK_DIM)
        tn = t * lax.rsqrt(ss + EPS) * hg_ref[...]
        rp = tn[:, QK_NOPE:]
        swapped = jnp.where(lane < half, pltpu.roll(rp, LANES - half, axis=1),
                            pltpu.roll(rp, half, axis=1))
        rot = rp * cos_ref[...] + swapped * sin_ref[...]
        o_ref[:, h * HEAD_PAD:h * HEAD_PAD + QK_NOPE] = tn[:, :QK_NOPE].astype(o_ref.dtype)
        o_ref[:, h * HEAD_PAD + QK_NOPE:(h + 1) * HEAD_PAD] = rot.astype(o_ref.dtype)
    if with_v:
        v_ref[...] = jnp.dot(cn.astype(BF16), wv_ref[...], preferred_element_type=F32).astype(v_ref.dtype)


def latent_heads(lat, lat_cols, rank, g_lat, w_pad, head_gain, cos_t, sin_t, heads, tm, w_v=None):
    T = lat.shape[0]
    n_out = heads * HEAD_PAD
    in_specs = [pl.BlockSpec((tm, lat_cols), lambda i: (i, 0)),
                pl.BlockSpec((1, rank), lambda i: (0, 0)),
                pl.BlockSpec((lat_cols, n_out), lambda i: (0, 0)),
                pl.BlockSpec((1, HEAD_PAD), lambda i: (0, 0)),
                pl.BlockSpec((tm, LANES), lambda i: (i, 0)),
                pl.BlockSpec((tm, LANES), lambda i: (i, 0))]
    args = [lat, g_lat.reshape(1, rank), w_pad, head_gain, cos_t, sin_t]
    out_shape = jax.ShapeDtypeStruct((T, n_out), BF16)
    out_specs = pl.BlockSpec((tm, n_out), lambda i: (i, 0))
    if w_v is not None:
        nv = w_v.shape[1]
        in_specs.append(pl.BlockSpec((rank, nv), lambda i: (0, 0)))
        args.append(w_v)
        out_shape = (out_shape, jax.ShapeDtypeStruct((T, nv), BF16))
        out_specs = (out_specs, pl.BlockSpec((tm, nv), lambda i: (i, 0)))
    return pl.pallas_call(
        functools.partial(_heads_kernel, rank=rank, heads=heads, with_v=w_v is not None),
        grid=(T // tm,),
        in_specs=in_specs, out_specs=out_specs, out_shape=out_shape,
        compiler_params=_params(("parallel",), 48),
        name="latent_heads_kv" if w_v is not None else "latent_heads_q",
    )(*args)


def _flash_kernel(q_ref, k_ref, v_ref, o_ref, m_s, l_s, acc_s, *, tq):
    i = pl.program_id(2)
    q = q_ref[0]
    m_s[...] = jnp.full(m_s.shape, -1e30, F32)
    l_s[...] = jnp.zeros(l_s.shape, F32)
    acc_s[...] = jnp.zeros(acc_s.shape, F32)

    def step(kt, masked):
        k0 = pl.multiple_of(kt * tq, tq)
        k = k_ref[0, pl.ds(k0, tq), :]
        v = v_ref[0, pl.ds(k0, tq), :]
        s = lax.dot_general(q, k, (((1,), (1,)), ((), ())), preferred_element_type=F32)
        if masked:
            r = lax.broadcasted_iota(jnp.int32, s.shape, 0)
            c = lax.broadcasted_iota(jnp.int32, s.shape, 1)
            s = jnp.where(c <= r, s, -1e30)
        m_prev = m_s[...]
        m_new = jnp.maximum(m_prev, jnp.max(s, axis=-1, keepdims=True))
        alpha = jnp.exp(m_prev - m_new)
        p = jnp.exp(s - m_new)
        l_s[...] = alpha * l_s[...] + jnp.sum(p, axis=-1, keepdims=True)
        acc_s[...] = alpha * acc_s[...] + jnp.dot(p.astype(BF16), v, preferred_element_type=F32)
        m_s[...] = m_new

    def body(kt, carry):
        step(kt, False)
        return carry

    lax.fori_loop(0, i, body, 0)
    step(i, True)
    o_ref[0] = (acc_s[...] / l_s[...]).astype(o_ref.dtype)


def flash_attention(q, k, v, heads, tq):
    B, S, _ = q.shape
    return pl.pallas_call(
        functools.partial(_flash_kernel, tq=tq),
        grid=(B, heads, S // tq),
        in_specs=[pl.BlockSpec((1, tq, HEAD_PAD), lambda b, h, i: (b, i, h)),
                  pl.BlockSpec((1, S, HEAD_PAD), lambda b, h, i: (b, 0, h)),
                  pl.BlockSpec((1, S, V_DIM), lambda b, h, i: (b, 0, h))],
        out_specs=pl.BlockSpec((1, tq, V_DIM), lambda b, h, i: (b, i, h)),
        out_shape=jax.ShapeDtypeStruct((B, S, heads * V_DIM), BF16),
        scratch_shapes=[pltpu.VMEM((tq, 1), F32), pltpu.VMEM((tq, 1), F32),
                        pltpu.VMEM((tq, V_DIM), F32)],
        compiler_params=_params(("parallel", "parallel", "arbitrary"), 32),
        name="flash_attention",
    )(q, k, v)


def _router_kernel(x_ref, g_ref, w_ref, b_ref, o_ref, *, tm, chunk):
    for r0 in range(0, tm, chunk):
        xn = _rms_rows(x_ref[r0:r0 + chunk, :], g_ref[...])
        o_ref[r0:r0 + chunk, :] = jnp.dot(xn, w_ref[...], preferred_element_type=F32,
                                          precision=lax.Precision.HIGHEST) + b_ref[...]


def router_logits(x, g, w_pad, b_pad, tm):
    T, D = x.shape
    return pl.pallas_call(
        functools.partial(_router_kernel, tm=tm, chunk=256),
        grid=(T // tm,),
        in_specs=[pl.BlockSpec((tm, D), lambda i: (i, 0)),
                  pl.BlockSpec((1, D), lambda i: (0, 0)),
                  pl.BlockSpec((D, LANES), lambda i: (0, 0)),
                  pl.BlockSpec((1, LANES), lambda i: (0, 0))],
        out_specs=pl.BlockSpec((tm, LANES), lambda i: (i, 0)),
        out_shape=jax.ShapeDtypeStruct((T, LANES), F32),
        compiler_params=_params(("parallel",), 32),
        name="router_logits",
    )(x, g.reshape(1, D), w_pad, b_pad)


def _start_row_gather(src_hbm, idx_ref, base, n, dst_row, sem):
    def body(r, carry):
        tok = idx_ref[base + r]
        pltpu.make_async_copy(src_hbm.at[pl.ds(tok, 1)], dst_row(r), sem).start()
        return carry
    lax.fori_loop(0, n, body, 0, unroll=8)


def _wait_row_gather(src_hbm, n, dst_all, sem):
    pltpu.make_async_copy(src_hbm.at[pl.ds(0, n)], dst_all, sem).wait()


def _gmm_kernel(tile_e_ref, tile_first_ref, n_used_ref, row_tok_ref,
                x_hbm, g_ref, gate_ref, wg_ref, wu_ref, wd_ref, o_ref,
                xbuf, sem, wg_bf, wu_bf, wd_bf, *, tm):
    i = pl.program_id(0)
    n_used = n_used_ref[0]
    slot = i % 2

    def start(tile, s):
        _start_row_gather(x_hbm, row_tok_ref, tile * tm, tm,
                          lambda r: xbuf.at[s, pl.ds(r, 1)], sem.at[s])

    @pl.when(jnp.logical_and(i == 0, n_used > 0))
    def _():
        start(0, 0)

    @pl.when(i < n_used)
    def _():
        _wait_row_gather(x_hbm, tm, xbuf.at[slot], sem.at[slot])

        @pl.when(i + 1 < n_used)
        def _():
            start(i + 1, 1 - slot)

        @pl.when(tile_first_ref[i] == 1)
        def _():
            wg_bf[...] = wg_ref[0].astype(BF16)
            wu_bf[...] = wu_ref[0].astype(BF16)
            wd_bf[...] = wd_ref[0].astype(BF16)

        xn = _rms_rows(xbuf[slot], g_ref[...]).astype(BF16)
        hg = jnp.dot(xn, wg_bf[...], preferred_element_type=F32)
        hu = jnp.dot(xn, wu_bf[...], preferred_element_type=F32)
        hh = (hg * jax.nn.sigmoid(hg)) * hu * gate_ref[...]
        o_ref[...] = jnp.dot(hh.astype(BF16), wd_bf[...], preferred_element_type=F32)

    @pl.when(i >= n_used)
    def _():
        o_ref[...] = jnp.zeros(o_ref.shape, o_ref.dtype)


def grouped_expert_mlp(x, g, plan, w_gate, w_up, w_down, tm):
    T, D = x.shape
    E, _, F = w_gate.shape
    n_tiles = plan["tile_e"].shape[0]
    R = n_tiles * tm
    grid_spec = pltpu.PrefetchScalarGridSpec(
        num_scalar_prefetch=4,
        grid=(n_tiles,),
        in_specs=[pl.BlockSpec(memory_space=pl.ANY),
                  pl.BlockSpec((1, D), lambda i, te, tf, nu, rt: (0, 0)),
                  pl.BlockSpec((tm, 1), lambda i, te, tf, nu, rt: (i, 0)),
                  pl.BlockSpec((1, D, F), lambda i, te, tf, nu, rt: (te[i], 0, 0)),
                  pl.BlockSpec((1, D, F), lambda i, te, tf, nu, rt: (te[i], 0, 0)),
                  pl.BlockSpec((1, F, D), lambda i, te, tf, nu, rt: (te[i], 0, 0))],
        out_specs=pl.BlockSpec((tm, D), lambda i, te, tf, nu, rt: (i, 0)),
        scratch_shapes=[pltpu.VMEM((2, tm, D), F32), pltpu.SemaphoreType.DMA((2,)),
                        pltpu.VMEM((D, F), BF16), pltpu.VMEM((D, F), BF16), pltpu.VMEM((F, D), BF16)],
    )
    return pl.pallas_call(
        functools.partial(_gmm_kernel, tm=tm),
        grid_spec=grid_spec,
        out_shape=jax.ShapeDtypeStruct((R, D), F32),
        compiler_params=_params(("arbitrary",), 56),
        name="grouped_expert_mlp",
    )(plan["tile_e"], plan["tile_first"], plan["n_used"], plan["row_tok"],
      x, g.reshape(1, D), plan["row_gate"], w_gate, w_up, w_down)


def _combine_kernel(pos_ref, x_ref, y_hbm, o_ref, ybuf, sem, *, tc, top_k):
    i = pl.program_id(0)
    n = pl.num_programs(0)
    slot = i % 2

    def start(tile, s):
        for k in range(top_k):
            def body(r, carry, k=k):
                p = pos_ref[(tile * tc + r) * top_k + k]
                pltpu.make_async_copy(y_hbm.at[pl.ds(p, 1)], ybuf.at[s, k, pl.ds(r, 1)],
                                      sem.at[s]).start()
                return carry
            lax.fori_loop(0, tc, body, 0, unroll=8)

    @pl.when(i == 0)
    def _():
        start(0, 0)

    for k in range(top_k):
        _wait_row_gather(y_hbm, tc, ybuf.at[slot, k], sem.at[slot])

    @pl.when(i + 1 < n)
    def _():
        start(i + 1, 1 - slot)

    acc = x_ref[...]
    for k in range(top_k):
        acc = acc + ybuf[slot, k]
    o_ref[...] = acc


def moe_combine(x, y_sorted, pos, tc):
    T, D = x.shape
    top_k = pos.shape[1]
    grid_spec = pltpu.PrefetchScalarGridSpec(
        num_scalar_prefetch=1,
        grid=(T // tc,),
        in_specs=[pl.BlockSpec((tc, D), lambda i, p: (i, 0)),
                  pl.BlockSpec(memory_space=pl.ANY)],
        out_specs=pl.BlockSpec((tc, D), lambda i, p: (i, 0)),
        scratch_shapes=[pltpu.VMEM((2, top_k, tc, D), F32), pltpu.SemaphoreType.DMA((2,))],
    )
    return pl.pallas_call(
        functools.partial(_combine_kernel, tc=tc, top_k=top_k),
        grid_spec=grid_spec,
        out_shape=jax.ShapeDtypeStruct((T, D), F32),
        compiler_params=_params(("arbitrary",), 32),
        name="moe_combine",
    )(pos.reshape(-1), x, y_sorted)


def _routing_plan(logits, n_groups, epg, tm):
    T = logits.shape[0]
    E = n_groups * epg
    gl = logits[:, :n_groups]
    el = logits[:, n_groups:n_groups + E].reshape(T, n_groups, epg)
    g_idx = jnp.argmax(gl, axis=-1)
    p_top = 1.0 / jnp.sum(jnp.exp(gl - jnp.max(gl, axis=-1, keepdims=True)), axis=-1)
    sel = jnp.take_along_axis(el, g_idx[:, None, None], axis=1)[:, 0]
    l_top, e_idx = lax.top_k(sel, TOP_K_IN_GROUP)
    ex = jnp.exp(l_top - l_top[:, :1])
    w_top = ex / jnp.sum(ex, axis=-1, keepdims=True)
    gates = (p_top[:, None] * w_top).reshape(-1)
    flat_e = (g_idx[:, None] * epg + e_idx).reshape(-1).astype(jnp.int32)
    A = flat_e.shape[0]
    n_tiles = A // tm + E

    onehot = (flat_e[:, None] == jnp.arange(E, dtype=jnp.int32)[None, :]).astype(jnp.int32)
    csum = jnp.cumsum(onehot, axis=0)
    counts = csum[-1]
    rank = jnp.take_along_axis(csum, flat_e[:, None], axis=1)[:, 0] - 1
    tiles_e = (counts + tm - 1) // tm
    tile_end = jnp.cumsum(tiles_e)
    tile_start = tile_end - tiles_e
    pos = (tile_start[flat_e] * tm + rank).astype(jnp.int32)
    n_used = tile_end[-1].astype(jnp.int32)
    tile_ids = jnp.arange(n_tiles, dtype=jnp.int32)
    tile_e = jnp.minimum(jnp.searchsorted(tile_end, tile_ids, side="right"), E - 1).astype(jnp.int32)
    tile_first = (tile_ids == tile_start[tile_e]).astype(jnp.int32)
    row_tok = jnp.zeros((n_tiles * tm,), jnp.int32).at[pos].set(
        jnp.arange(A, dtype=jnp.int32) // TOP_K_IN_GROUP)
    row_gate = jnp.zeros((n_tiles * tm,), F32).at[pos].set(gates).reshape(-1, 1)
    plan = dict(tile_e=tile_e, tile_first=tile_first, n_used=n_used.reshape(1),
                row_tok=row_tok, row_gate=row_gate)
    return plan, pos.reshape(T, TOP_K_IN_GROUP)


def hier_moe_residual(x, g, w_rg, b_rg, w_re, b_re, w_gate, w_up, w_down):
    T, D = x.shape
    n_groups = w_rg.shape[1]
    epg = w_gate.shape[1]
    E = n_groups * epg
    F = w_gate.shape[-1]
    w_r = jnp.concatenate([w_rg, w_re, jnp.zeros((D, LANES - n_groups - E), F32)], axis=1)
    b_r = jnp.concatenate([b_rg, b_re, jnp.zeros((LANES - n_groups - E,), F32)]).reshape(1, LANES)
    logits = router_logits(x, g, w_r, b_r, tm=1024)
    tm = 256
    plan, pos = _routing_plan(logits, n_groups, epg, tm)
    y_sorted = grouped_expert_mlp(x, g, plan, w_gate.reshape(E, D, F), w_up.reshape(E, D, F),
                                  w_down.reshape(E, F, D), tm)
    return moe_combine(x, y_sorted, pos, tc=256)


def _pad_head_cols(w, heads, real):
    r = w.shape[0]
    return jnp.pad(w.reshape(r, heads, real), ((0, 0), (0, 0), (0, HEAD_PAD - real))).reshape(r, heads * HEAD_PAD)


def _rope_tables(positions):
    half = QK_ROPE // 2
    inv_freq = ROPE_THETA ** (-jnp.arange(half, dtype=F32) / half)
    ang = positions.astype(F32).reshape(-1, 1) * inv_freq[None, :]
    cos, sin = jnp.cos(ang), jnp.sin(ang)
    zeros = jnp.zeros((ang.shape[0], LANES - QK_ROPE), F32)
    return (jnp.concatenate([cos, cos, zeros], axis=1),
            jnp.concatenate([-sin, sin, zeros], axis=1))


def kernel(x, mem, positions, norm_mix_g, norm_ffn_g, w_o, mem_norm_g, w_mem_kv, mem_q_norm_g, mem_k_norm_g, w_in_a, conv_w, conv_b, w_lru_r, b_lru_r, w_lru_i, b_lru_i, lru_lambda, kv_in_norm_g, w_dkv, kv_latent_norm_g, w_uk, w_uv, k_head_norm_g, w_in_b, q_latent_norm_g, w_uq, q_head_norm_g, w_router_grp, b_router_grp, w_router_exp, b_router_exp, w_exp_gate, w_exp_up, w_exp_down):
    B, S, D = x.shape
    T = B * S
    M = mem.shape[1]
    depth = norm_mix_g.shape[0]
    n_a = w_in_a.shape[0]
    lru_w = lru_lambda.shape[1]
    mem_w = w_mem_kv.shape[2] // 2
    heads = w_uv.shape[1] // V_DIM
    kv_rank = kv_latent_norm_g.shape[0]
    q_rank = q_latent_norm_g.shape[1]

    xr = x.reshape(T, D)
    mem2 = mem.reshape(B * M, D)
    cos_t, sin_t = _rope_tables(positions)

    k_sh = v_sh = None
    for l in range(depth):
        kv_mem = norm_matmul(mem2, mem_norm_g[l], w_mem_kv[l], BF16, tm=B * M, tn=512).reshape(B, M, 2 * mem_w)
        if l < n_a:
            proj = norm_matmul(xr, norm_mix_g[l], w_in_a[l], BF16, tm=1024, tn=512)
            y_mix = rglru(proj.reshape(B, S, -1), conv_w[l], conv_b[l], w_lru_r[l], b_lru_r[l],
                          w_lru_i[l], b_lru_i[l], lru_lambda[l], nb=3, tc=256).reshape(T, lru_w)
            q_col = 2 * lru_w // mem_w
        else:
            j = l - n_a
            proj = norm_matmul(xr, norm_mix_g[l], w_in_b[j], BF16, tm=1024, tn=512)
            wq = _pad_head_cols(w_uq[j], heads, QK_DIM).astype(BF16)
            qg = jnp.pad(q_head_norm_g[j] * (QK_DIM ** -0.5), (0, HEAD_PAD - QK_DIM)).reshape(1, HEAD_PAD)
            q = latent_heads(proj, q_rank, q_rank, q_latent_norm_g[j], wq, qg, cos_t, sin_t, heads, tm=512)
            y_mix = flash_attention(q.reshape(B, S, -1), k_sh, v_sh, heads, tq=512).reshape(T, heads * V_DIM)
            q_col = q_rank // mem_w
        y_mem = mem_attention(proj, q_col, kv_mem, mem_q_norm_g[l], mem_k_norm_g[l], B, ts=512)
        xr = out_proj(y_mix, y_mem, w_o[l], xr, tm=1024, tn=512)
        xr = hier_moe_residual(xr, norm_ffn_g[l], w_router_grp[l], b_router_grp[l], w_router_exp[l],
                               b_router_exp[l], w_exp_gate[l], w_exp_up[l], w_exp_down[l])
        if l == n_a - 1:
            lat_cols = kv_rank + LANES
            w_dkv_pad = jnp.pad(w_dkv, ((0, 0), (0, lat_cols - w_dkv.shape[1])))
            ckv = norm_matmul(xr, kv_in_norm_g, w_dkv_pad, F32, tm=1024, tn=lat_cols)
            eye = jnp.pad(jnp.eye(QK_ROPE, dtype=F32), ((0, LANES - QK_ROPE), (0, 0)))
            eye3 = jnp.broadcast_to(eye[:, None, :], (LANES, heads, QK_ROPE))
            rope_rows = jnp.pad(eye3, ((0, 0), (0, 0), (QK_NOPE, HEAD_PAD - QK_DIM))).reshape(LANES, heads * HEAD_PAD)
            wk = jnp.concatenate([_pad_head_cols(w_uk, heads, QK_NOPE), rope_rows], axis=0).astype(BF16)
            kg = jnp.pad(k_head_norm_g, (0, HEAD_PAD - QK_DIM)).reshape(1, HEAD_PAD)
            k_flat, v_flat = latent_heads(ckv, lat_cols, kv_rank, kv_latent_norm_g, wk, kg, cos_t, sin_t,
                                          heads, tm=512, w_v=w_uv.astype(BF16))
            k_sh = k_flat.reshape(B, S, -1)
            v_sh = v_flat.reshape(B, S, -1)
    return xr.reshape(B, S, D)
```

```python
import functools
import math

import jax
import jax.numpy as jnp
from jax import lax
from jax.experimental import pallas as pl
from jax.experimental.pallas import tpu as pltpu

F32 = jnp.float32
BF16 = jnp.bfloat16
EPS = 1e-6

MEM_HEADS = 4
CONV_WIDTH = 4
LRU_C = 8.0
QK_NOPE = 128
QK_ROPE = 64
QK_DIM = QK_NOPE + QK_ROPE
V_DIM = 128
ROPE_THETA = 10000.0
TOP_K_IN_GROUP = 2

LANES = 128
HEAD_PAD = 2 * LANES
MIB = 1024 * 1024

TM_PROJ = 1024
TN_PROJ = 512
TS_MEM = 512
TQ_FLASH = 512
SUB_FLASH = 256
TM_HEADS = 512
TM_EXPERT = 256
TC_COMBINE = 256
LRU_BLOCKS_PER_STEP = 3
TC_LRU = 256
PLAN_CHUNK = 256


def _params(sem, vmem_mib):
    return pltpu.CompilerParams(dimension_semantics=sem, vmem_limit_bytes=vmem_mib * MIB)


def _rms_rows(x, g):
    x = x.astype(F32)
    ms = jnp.mean(x * x, axis=-1, keepdims=True)
    return x * lax.rsqrt(ms + EPS) * g


def _norm_mm_kernel(x_ref, g_ref, w_ref, o_ref, xn_ref, *, tm, chunk):
    @pl.when(pl.program_id(1) == 0)
    def _():
        for r0 in range(0, tm, chunk):
            xn_ref[r0:r0 + chunk, :] = _rms_rows(x_ref[r0:r0 + chunk, :], g_ref[...]).astype(BF16)

    o_ref[...] = jnp.dot(xn_ref[...], w_ref[...].astype(BF16),
                         preferred_element_type=F32).astype(o_ref.dtype)


def norm_matmul(x, g, w, layer, out_dtype, tm, tn):
    T, K = x.shape
    N = w.shape[2]
    chunk = min(tm, 256)
    return pl.pallas_call(
        functools.partial(_norm_mm_kernel, tm=tm, chunk=chunk),
        grid=(T // tm, N // tn),
        in_specs=[pl.BlockSpec((tm, K), lambda i, j: (i, 0)),
                  pl.BlockSpec((1, K), lambda i, j: (0, 0)),
                  pl.BlockSpec((None, K, tn), lambda i, j: (layer, 0, j))],
        out_specs=pl.BlockSpec((tm, tn), lambda i, j: (i, j)),
        out_shape=jax.ShapeDtypeStruct((T, N), out_dtype),
        scratch_shapes=[pltpu.VMEM((tm, K), BF16)],
        compiler_params=_params(("parallel", "arbitrary"), 48),
        name="norm_matmul",
    )(x, g.reshape(1, K), w)


def _out_proj_kernel(ya_ref, yb_ref, wa_ref, wb_ref, x_ref, o_ref):
    acc = jnp.dot(ya_ref[...], wa_ref[...].astype(BF16), preferred_element_type=F32)
    acc += jnp.dot(yb_ref[...], wb_ref[...].astype(BF16), preferred_element_type=F32)
    o_ref[...] = x_ref[...] + acc


def out_proj(y_mix, y_mem, w_o, layer, x, tm, tn):
    T, Wa = y_mix.shape
    Wb = y_mem.shape[1]
    N = w_o.shape[2]
    assert Wa % Wb == 0
    return pl.pallas_call(
        _out_proj_kernel,
        grid=(T // tm, N // tn),
        in_specs=[pl.BlockSpec((tm, Wa), lambda i, j: (i, 0)),
                  pl.BlockSpec((tm, Wb), lambda i, j: (i, 0)),
                  pl.BlockSpec((None, Wa, tn), lambda i, j: (layer, 0, j)),
                  pl.BlockSpec((None, Wb, tn), lambda i, j: (layer, Wa // Wb, j)),
                  pl.BlockSpec((tm, tn), lambda i, j: (i, j))],
        out_specs=pl.BlockSpec((tm, tn), lambda i, j: (i, j)),
        out_shape=jax.ShapeDtypeStruct((T, N), F32),
        compiler_params=_params(("parallel", "arbitrary"), 48),
        name="out_proj",
    )(y_mix, y_mem, w_o, w_o, x)


def _mem_attn_kernel(q_ref, kv_ref, qg_ref, kg_ref, o_ref, *, heads, hd):
    width = heads * hd
    scale = hd ** -0.5
    for h in range(heads):
        qn = _rms_rows(q_ref[:, h * hd:(h + 1) * hd], qg_ref[...]) * scale
        kn = _rms_rows(kv_ref[0, :, h * hd:(h + 1) * hd], kg_ref[...])
        v = kv_ref[0, :, width + h * hd:width + (h + 1) * hd]
        s = lax.dot_general(qn.astype(BF16), kn.astype(BF16), (((1,), (1,)), ((), ())),
                            preferred_element_type=F32)
        m = jnp.max(s, axis=-1, keepdims=True)
        p = jnp.exp(s - m)
        l = jnp.sum(p, axis=-1, keepdims=True)
        o = jnp.dot(p.astype(BF16), v, preferred_element_type=F32) / l
        o_ref[:, h * hd:(h + 1) * hd] = o.astype(o_ref.dtype)


def mem_attention(proj, q_col_block, kv, q_g, k_g, batch, ts):
    T = proj.shape[0]
    _, M, two_w = kv.shape
    width = two_w // 2
    hd = width // MEM_HEADS
    per_b = T // batch // ts
    return pl.pallas_call(
        functools.partial(_mem_attn_kernel, heads=MEM_HEADS, hd=hd),
        grid=(batch, per_b),
        in_specs=[pl.BlockSpec((ts, width), lambda b, i: (b * per_b + i, q_col_block)),
                  pl.BlockSpec((1, M, two_w), lambda b, i: (b, 0, 0)),
                  pl.BlockSpec((1, hd), lambda b, i: (0, 0)),
                  pl.BlockSpec((1, hd), lambda b, i: (0, 0))],
        out_specs=pl.BlockSpec((ts, width), lambda b, i: (b * per_b + i, 0)),
        out_shape=jax.ShapeDtypeStruct((T, width), BF16),
        compiler_params=_params(("parallel", "arbitrary"), 32),
        name="mem_attention",
    )(proj, kv, q_g.reshape(1, hd), k_g.reshape(1, hd))


def _gelu_tanh(x):
    return 0.5 * x * (1.0 + jnp.tanh(0.7978845608028654 * (x + 0.044715 * x * x * x)))


def _rglru_kernel(u_ref, gb_ref, cw_ref, cb_ref, wr_ref, wi_ref, br_ref, bi_ref, lam_ref,
                  o_ref, a_s, b_s, *, seq, nb, blk, tc):
    wb = nb * blk
    neg_lam = -lam_ref[...]
    softplus = jnp.maximum(neg_lam, 0.0) + jnp.log1p(jnp.exp(-jnp.abs(neg_lam)))

    def gates(c, carry):
        t0 = pl.multiple_of(c * tc, tc)
        cur = u_ref[0, pl.ds(t0, tc), :].astype(F32)
        p0 = pl.multiple_of(jnp.maximum(t0 - 16, 0), 16)
        prev = u_ref[0, pl.ds(p0, 16), :].astype(F32)
        prev = jnp.where(c > 0, prev, 0.0)
        full = jnp.concatenate([prev, cur], axis=0)
        y = cb_ref[...] + cw_ref[0:1, :] * cur
        for k in range(1, CONV_WIDTH):
            y = y + cw_ref[k:k + 1, :] * full[16 - k:16 - k + tc, :]
        r_parts, i_parts = [], []
        for n in range(nb):
            yb = y[:, n * blk:(n + 1) * blk].astype(BF16)
            r_parts.append(jnp.dot(yb, wr_ref[n].astype(BF16), preferred_element_type=F32))
            i_parts.append(jnp.dot(yb, wi_ref[n].astype(BF16), preferred_element_type=F32))
        r = jax.nn.sigmoid(jnp.concatenate(r_parts, axis=1) + br_ref[...])
        ig = jax.nn.sigmoid(jnp.concatenate(i_parts, axis=1) + bi_ref[...])
        log_a = (-LRU_C) * r * softplus
        a = jnp.exp(log_a)
        a_s[pl.ds(t0, tc), :] = a
        b_s[pl.ds(t0, tc), :] = jnp.sqrt(-jnp.tanh(log_a) * (a * a + 1.0)) * (ig * y)
        return carry

    lax.fori_loop(0, seq // tc, gates, 0)

    row = lax.broadcasted_iota(jnp.int32, (8, wb), 0)

    def scan(c, h):
        t0 = pl.multiple_of(c * 8, 8)
        a = a_s[pl.ds(t0, 8), :]
        b = b_s[pl.ds(t0, 8), :]
        for s in (1, 2, 4):
            a_sh = pltpu.roll(a, s, axis=0)
            b_sh = pltpu.roll(b, s, axis=0)
            keep = row >= s
            b = jnp.where(keep, a * b_sh + b, b)
            a = jnp.where(keep, a * a_sh, a)
        hc = a * h + b
        b_s[pl.ds(t0, 8), :] = hc
        return hc[7:8, :]

    lax.fori_loop(0, seq // 8, scan, jnp.zeros((1, wb), F32))

    def gate_out(c, carry):
        t0 = pl.multiple_of(c * tc, tc)
        g = gb_ref[0, pl.ds(t0, tc), :].astype(F32)
        o_ref[0, pl.ds(t0, tc), :] = (_gelu_tanh(g) * b_s[pl.ds(t0, tc), :]).astype(o_ref.dtype)
        return carry

    lax.fori_loop(0, seq // tc, gate_out, 0)


def rglru(proj, conv_w, conv_b, w_r, b_r, w_i, b_i, lam, nb, tc):
    B, S, _ = proj.shape
    W = lam.shape[0]
    n_blocks, blk, _ = w_r.shape
    wb = nb * blk
    ncb = W // wb
    vec = lambda: pl.BlockSpec((1, wb), lambda b, j: (0, j))
    return pl.pallas_call(
        functools.partial(_rglru_kernel, seq=S, nb=nb, blk=blk, tc=tc),
        grid=(B, ncb),
        in_specs=[pl.BlockSpec((1, S, wb), lambda b, j: (b, 0, j)),
                  pl.BlockSpec((1, S, wb), lambda b, j: (b, 0, ncb + j)),
                  pl.BlockSpec((CONV_WIDTH, wb), lambda b, j: (0, j)),
                  vec(),
                  pl.BlockSpec((nb, blk, blk), lambda b, j: (j, 0, 0)),
                  pl.BlockSpec((nb, blk, blk), lambda b, j: (j, 0, 0)),
                  vec(), vec(), vec()],
        out_specs=pl.BlockSpec((1, S, wb), lambda b, j: (b, 0, j)),
        out_shape=jax.ShapeDtypeStruct((B, S, W), BF16),
        scratch_shapes=[pltpu.VMEM((S, wb), F32), pltpu.VMEM((S, wb), F32)],
        compiler_params=_params(("parallel", "arbitrary"), 32),
        name="rglru",
    )(proj, proj, conv_w, conv_b.reshape(1, W), w_r, w_i, b_r.reshape(1, W), b_i.reshape(1, W),
      lam.reshape(1, W))


def _heads_kernel(*refs, rank, heads, with_v):
    if with_v:
        lat_ref, gl_ref, w_ref, hg_ref, cos_ref, sin_ref, wv_ref, o_ref, v_ref = refs
    else:
        lat_ref, gl_ref, w_ref, hg_ref, cos_ref, sin_ref, o_ref = refs
    lat = lat_ref[...].astype(F32)
    cn = _rms_rows(lat[:, :rank], gl_ref[...])
    full = cn if lat.shape[1] == rank else jnp.concatenate([cn, lat[:, rank:]], axis=1)
    full = full.astype(BF16)
    lane = lax.broadcasted_iota(jnp.int32, cos_ref.shape, 1)
    half = QK_ROPE // 2
    for h in range(heads):
        t = jnp.dot(full, w_ref[:, h * HEAD_PAD:(h + 1) * HEAD_PAD], preferred_element_type=F32)
        ss = jnp.sum(t * t, axis=-1, keepdims=True) * (1.0 / QK_DIM)
        tn = t * lax.rsqrt(ss + EPS) * hg_ref[...]
        rp = tn[:, QK_NOPE:]
        swapped = jnp.where(lane < half, pltpu.roll(rp, LANES - half, axis=1),
                            pltpu.roll(rp, half, axis=1))
        rot = rp * cos_ref[...] + swapped * sin_ref[...]
        o_ref[:, h * HEAD_PAD:h * HEAD_PAD + QK_NOPE] = tn[:, :QK_NOPE].astype(o_ref.dtype)
        o_ref[:, h * HEAD_PAD + QK_NOPE:(h + 1) * HEAD_PAD] = rot.astype(o_ref.dtype)
    if with_v:
        v = jnp.dot(cn.astype(BF16), wv_ref[...], preferred_element_type=F32).astype(v_ref.dtype)
        ones = jnp.ones((v.shape[0], HEAD_PAD - V_DIM), v_ref.dtype)
        for h in range(heads):
            v_ref[:, h * HEAD_PAD:h * HEAD_PAD + V_DIM] = v[:, h * V_DIM:(h + 1) * V_DIM]
            v_ref[:, h * HEAD_PAD + V_DIM:(h + 1) * HEAD_PAD] = ones


def latent_heads(lat, lat_cols, rank, g_lat, w_pad, head_gain, cos_t, sin_t, heads, tm, w_v=None):
    T = lat.shape[0]
    n_out = heads * HEAD_PAD
    in_specs = [pl.BlockSpec((tm, lat_cols), lambda i: (i, 0)),
                pl.BlockSpec((1, rank), lambda i: (0, 0)),
                pl.BlockSpec((lat_cols, n_out), lambda i: (0, 0)),
                pl.BlockSpec((1, HEAD_PAD), lambda i: (0, 0)),
                pl.BlockSpec((tm, LANES), lambda i: (i, 0)),
                pl.BlockSpec((tm, LANES), lambda i: (i, 0))]
    args = [lat, g_lat.reshape(1, rank), w_pad, head_gain, cos_t, sin_t]
    out_shape = jax.ShapeDtypeStruct((T, n_out), BF16)
    out_specs = pl.BlockSpec((tm, n_out), lambda i: (i, 0))
    if w_v is not None:
        in_specs.append(pl.BlockSpec(w_v.shape, lambda i: (0, 0)))
        args.append(w_v)
        out_shape = (out_shape, jax.ShapeDtypeStruct((T, n_out), BF16))
        out_specs = (out_specs, pl.BlockSpec((tm, n_out), lambda i: (i, 0)))
    return pl.pallas_call(
        functools.partial(_heads_kernel, rank=rank, heads=heads, with_v=w_v is not None),
        grid=(T // tm,),
        in_specs=in_specs, out_specs=out_specs, out_shape=out_shape,
        compiler_params=_params(("parallel",), 48),
        name="latent_heads_kv" if w_v is not None else "latent_heads_q",
    )(*args)


def _flash_kernel(q_ref, k_ref, v_ref, o_ref, m_s, acc_s, *, tq, sub):
    i = pl.program_id(2)
    m_s[...] = jnp.full(m_s.shape, -1e30, F32)
    acc_s[...] = jnp.zeros(acc_s.shape, F32)

    def step(kt, masked):
        k0 = pl.multiple_of(kt * tq, tq)
        for h in range(tq // sub):
            rows = slice(h * sub, (h + 1) * sub)
            nk = (h + 1) * sub if masked else tq
            k = k_ref[0, pl.ds(k0, nk), :]
            v = v_ref[0, pl.ds(k0, nk), :]
            s = lax.dot_general(q_ref[0, rows, :], k, (((1,), (1,)), ((), ())), preferred_element_type=F32)
            if masked:
                r = lax.broadcasted_iota(jnp.int32, s.shape, 0) + h * sub
                c = lax.broadcasted_iota(jnp.int32, s.shape, 1)
                s = jnp.where(c <= r, s, -1e30)
            m_prev = m_s[rows, :]
            m_new = jnp.maximum(m_prev, jnp.max(s, axis=-1, keepdims=True))
            alpha = jnp.exp2(m_prev - m_new)
            p = jnp.exp2(s - jnp.concatenate([m_new] * (nk // LANES), axis=1))
            acc_s[rows, :] = (jnp.concatenate([alpha, alpha], axis=1) * acc_s[rows, :]
                              + jnp.dot(p.astype(BF16), v, preferred_element_type=F32))
            m_s[rows, :] = m_new

    def body(kt, carry):
        step(kt, False)
        return carry

    lax.fori_loop(0, i, body, 0)
    step(i, True)
    o_ref[0] = (acc_s[:, :V_DIM] / acc_s[:, V_DIM:]).astype(o_ref.dtype)


def flash_attention(q, k, v, heads, tq):
    B, S, _ = q.shape
    return pl.pallas_call(
        functools.partial(_flash_kernel, tq=tq, sub=SUB_FLASH),
        grid=(B, heads, S // tq),
        in_specs=[pl.BlockSpec((1, tq, HEAD_PAD), lambda b, h, i: (b, i, h)),
                  pl.BlockSpec((1, S, HEAD_PAD), lambda b, h, i: (b, 0, h)),
                  pl.BlockSpec((1, S, HEAD_PAD), lambda b, h, i: (b, 0, h))],
        out_specs=pl.BlockSpec((1, tq, V_DIM), lambda b, h, i: (b, i, h)),
        out_shape=jax.ShapeDtypeStruct((B, S, heads * V_DIM), BF16),
        scratch_shapes=[pltpu.VMEM((tq, LANES), F32), pltpu.VMEM((tq, HEAD_PAD), F32)],
        compiler_params=_params(("parallel", "parallel", "arbitrary"), 32),
        name="flash_attention",
    )(q, k, v)


def _router_kernel(x_ref, g_ref, w_ref, b_ref, o_ref, *, tm, chunk):
    for r0 in range(0, tm, chunk):
        xn = _rms_rows(x_ref[r0:r0 + chunk, :], g_ref[...])
        o_ref[r0:r0 + chunk, :] = jnp.dot(xn, w_ref[...], preferred_element_type=F32,
                                          precision=lax.Precision.HIGHEST) + b_ref[...]


def router_logits(x, g, w_pad, b_pad, tm):
    T, D = x.shape
    return pl.pallas_call(
        functools.partial(_router_kernel, tm=tm, chunk=256),
        grid=(T // tm,),
        in_specs=[pl.BlockSpec((tm, D), lambda i: (i, 0)),
                  pl.BlockSpec((1, D), lambda i: (0, 0)),
                  pl.BlockSpec((D, LANES), lambda i: (0, 0)),
                  pl.BlockSpec((1, LANES), lambda i: (0, 0))],
        out_specs=pl.BlockSpec((tm, LANES), lambda i: (i, 0)),
        out_shape=jax.ShapeDtypeStruct((T, LANES), F32),
        compiler_params=_params(("parallel",), 32),
        name="router_logits",
    )(x, g.reshape(1, D), w_pad, b_pad)


def _plan_kernel(lg_ref, out_ref, tiles_ref, c_s, info_s, *, n_tok, n_groups, epg, tm, chunk):
    n_exp = n_groups * epg
    lane = lax.broadcasted_iota(jnp.int32, (chunk, LANES), 1).astype(F32)
    rr = lax.broadcasted_iota(jnp.int32, (chunk, chunk), 0)
    cc = lax.broadcasted_iota(jnp.int32, (chunk, chunk), 1)
    tri = jnp.where(cc <= rr, 1.0, 0.0).astype(BF16)
    neg_inf = -jnp.inf

    def first_argmax(vals, vmax):
        return jnp.min(jnp.where(vals == vmax, lane, float(LANES)), axis=1, keepdims=True)

    def decide(c, carry):
        r0 = pl.multiple_of(c * chunk, chunk)
        lg = lg_ref[pl.ds(r0, chunk), :]
        gl = jnp.where(lane < n_groups, lg, neg_inf)
        gmax = jnp.max(gl, axis=1, keepdims=True)
        g_idx = first_argmax(gl, gmax)
        p_top = 1.0 / jnp.sum(jnp.exp(gl - gmax), axis=1, keepdims=True)
        lo = n_groups + g_idx * epg
        el = jnp.where(jnp.logical_and(lane >= lo, lane < lo + epg), lg, neg_inf)
        l1 = jnp.max(el, axis=1, keepdims=True)
        i1 = first_argmax(el, l1)
        el2 = jnp.where(lane == i1, neg_inf, el)
        l2 = jnp.max(el2, axis=1, keepdims=True)
        i2 = first_argmax(el2, l2)
        d = jnp.exp(l2 - l1)
        w1 = 1.0 / (1.0 + d)
        e1 = i1 - n_groups
        e2 = i2 - n_groups
        onehot = jnp.where(jnp.logical_or(lane == e1, lane == e2), 1.0, 0.0)
        cs = jnp.dot(tri, onehot.astype(BF16), preferred_element_type=F32) + carry
        c_s[pl.ds(r0, chunk), :] = cs
        info_s[pl.ds(r0, chunk), :] = jnp.where(
            lane == 0, e1, jnp.where(lane == 1, e2, jnp.where(
                lane == 2, p_top * w1, jnp.where(lane == 3, p_top * (d * w1), 0.0))))
        return cs[chunk - 1:chunk, :]

    counts = lax.fori_loop(0, n_tok // chunk, decide, jnp.zeros((1, LANES), F32))

    tiles = jnp.floor((counts + (tm - 1)) * (1.0 / tm))
    jj = lax.broadcasted_iota(jnp.int32, (LANES, LANES), 0)
    ee = lax.broadcasted_iota(jnp.int32, (LANES, LANES), 1)
    upper = jnp.where(jj <= ee, 1.0, 0.0).astype(BF16)
    tile_end = jnp.dot(jnp.broadcast_to(tiles, (8, LANES)).astype(BF16), upper,
                       preferred_element_type=F32)[0:1, :]
    tile_start = tile_end - tiles
    row_start = tile_start * tm

    def place(c, carry):
        r0 = pl.multiple_of(c * chunk, chunk)
        info = info_s[pl.ds(r0, chunk), :]
        base = row_start + c_s[pl.ds(r0, chunk), :] - 1.0
        pos1 = jnp.sum(jnp.where(lane == info[:, 0:1], base, 0.0), axis=1, keepdims=True)
        pos2 = jnp.sum(jnp.where(lane == info[:, 1:2], base, 0.0), axis=1, keepdims=True)
        out_ref[pl.ds(r0, chunk), :] = jnp.where(lane == 0, pos1, jnp.where(lane == 1, pos2, info))
        return carry

    lax.fori_loop(0, n_tok // chunk, place, 0)

    ti = jj.astype(F32)
    lane_sq = ee.astype(F32)
    is_exp = lane_sq < n_exp
    n_used = tile_end[:, n_exp - 1:n_exp]
    ti_c = jnp.minimum(ti, n_used - 1.0)
    tile_e = jnp.sum(jnp.where(jnp.logical_and(is_exp, tile_end <= ti_c), 1.0, 0.0), axis=1, keepdims=True)
    first = jnp.sum(jnp.where(jnp.logical_and(jnp.logical_and(is_exp, tile_start == ti), tiles > 0.0),
                              1.0, 0.0), axis=1, keepdims=True)
    tiles_ref[...] = jnp.where(lane_sq == 0, tile_e, jnp.where(lane_sq == 1, first, jnp.where(
        lane_sq == 2, n_used, 0.0))).astype(jnp.int32)


def routing_plan(logits, n_groups, epg, tm):
    T = logits.shape[0]
    n_tiles = T * TOP_K_IN_GROUP // tm + n_groups * epg
    assert n_tiles <= LANES
    out, tiles = pl.pallas_call(
        functools.partial(_plan_kernel, n_tok=T, n_groups=n_groups, epg=epg, tm=tm, chunk=PLAN_CHUNK),
        grid=(1,),
        in_specs=[pl.BlockSpec((T, LANES), lambda i: (0, 0))],
        out_specs=(pl.BlockSpec((T, LANES), lambda i: (0, 0)),
                   pl.BlockSpec((LANES, LANES), lambda i: (0, 0))),
        out_shape=(jax.ShapeDtypeStruct((T, LANES), F32), jax.ShapeDtypeStruct((LANES, LANES), jnp.int32)),
        scratch_shapes=[pltpu.VMEM((T, LANES), F32), pltpu.VMEM((T, LANES), F32)],
        compiler_params=_params(("arbitrary",), 40),
        name="routing_plan",
    )(logits)
    pos = out[:, :TOP_K_IN_GROUP].astype(jnp.int32).reshape(-1)
    return out, pos, tiles[:n_tiles, 0], tiles[:n_tiles, 1], tiles[0:1, 2], n_tiles


def _invert_kernel(pos_ref, out_ref, *, n_assign, n_rows, top_k):
    def zero(r, carry):
        out_ref[r] = 0
        return carry
    lax.fori_loop(0, n_rows, zero, 0, unroll=8)

    def put(t, carry):
        for k in range(top_k):
            out_ref[pos_ref[t * top_k + k]] = t
        return carry
    lax.fori_loop(0, n_assign // top_k, put, 0, unroll=8)


def invert_positions(pos, n_rows):
    n_assign = pos.shape[0]
    return pl.pallas_call(
        functools.partial(_invert_kernel, n_assign=n_assign, n_rows=n_rows, top_k=TOP_K_IN_GROUP),
        grid_spec=pltpu.PrefetchScalarGridSpec(
            num_scalar_prefetch=1, grid=(1,), in_specs=[],
            out_specs=pl.BlockSpec(memory_space=pltpu.SMEM)),
        out_shape=jax.ShapeDtypeStruct((n_rows,), jnp.int32),
        compiler_params=_params(("arbitrary",), 16),
        name="invert_positions",
    )(pos)


def _row_copy(src_hbm, tok, dst_row, sem):
    return pltpu.make_async_copy(src_hbm.at[pl.ds(tok, 1)], dst_row, sem)


def _start_row_gather(src_hbm, idx_ref, base, n, dst_row, sem):
    def body(r, carry):
        _row_copy(src_hbm, idx_ref[base + r], dst_row(r), sem).start()
        return carry
    lax.fori_loop(0, n, body, 0, unroll=8)


def _wait_row_gather(src_hbm, n, dst_all, sem):
    pltpu.make_async_copy(src_hbm.at[pl.ds(0, n)], dst_all, sem).wait()


def _gmm_kernel(tile_e_ref, tile_first_ref, n_used_ref, row_tok_ref,
                x_hbm, g_ref, wg_ref, wu_ref, wd_ref, o_ref,
                xbuf, sem, wg_bf, wu_bf, wd_bf, *, tm, n_tiles):
    i = pl.program_id(0)
    n_used = n_used_ref[0]
    slot = i % 2

    @pl.when(jnp.logical_and(i == 0, n_used > 0))
    def _():
        _start_row_gather(x_hbm, row_tok_ref, 0, tm, lambda r: xbuf.at[0, pl.ds(r, 1)], sem.at[0])

    @pl.when(i < n_used)
    def _():
        _wait_row_gather(x_hbm, tm, xbuf.at[slot], sem.at[slot])

        @pl.when(tile_first_ref[i] == 1)
        def _():
            wg_bf[...] = wg_ref[0].astype(BF16)
            wu_bf[...] = wu_ref[0].astype(BF16)
            wd_bf[...] = wd_ref[0].astype(BF16)

        xn = _rms_rows(xbuf[slot], g_ref[...]).astype(BF16)
        base = jnp.minimum(i + 1, n_tiles - 1) * tm
        for r in range(tm):
            _row_copy(x_hbm, row_tok_ref[base + r], xbuf.at[1 - slot, pl.ds(r, 1)], sem.at[1 - slot]).start()
        hg = jnp.dot(xn, wg_bf[...], preferred_element_type=F32)
        hu = jnp.dot(xn, wu_bf[...], preferred_element_type=F32)
        hh = (hg * jax.nn.sigmoid(hg)) * hu
        o_ref[...] = jnp.dot(hh.astype(BF16), wd_bf[...], preferred_element_type=F32)

        @pl.when(i == n_tiles - 1)
        def _():
            _wait_row_gather(x_hbm, tm, xbuf.at[1 - slot], sem.at[1 - slot])

    @pl.when(i >= n_used)
    def _():
        @pl.when(jnp.logical_and(i == n_used, i > 0))
        def _():
            _wait_row_gather(x_hbm, tm, xbuf.at[slot], sem.at[slot])
        o_ref[...] = jnp.zeros(o_ref.shape, o_ref.dtype)


def grouped_expert_mlp(x, g, tile_e, tile_first, n_used, row_tok, w_gate, w_up, w_down, layer, tm):
    T, D = x.shape
    F = w_gate.shape[2]
    n_tiles = tile_e.shape[0]
    layer_idx, depth = layer
    off = layer_idx * (w_gate.shape[0] // depth)
    w_idx = lambda i, te, tf, nu, rt: (off + te[i], 0, 0)
    grid_spec = pltpu.PrefetchScalarGridSpec(
        num_scalar_prefetch=4,
        grid=(n_tiles,),
        in_specs=[pl.BlockSpec(memory_space=pl.ANY),
                  pl.BlockSpec((1, D), lambda i, te, tf, nu, rt: (0, 0)),
                  pl.BlockSpec((1, D, F), w_idx),
                  pl.BlockSpec((1, D, F), w_idx),
                  pl.BlockSpec((1, F, D), w_idx)],
        out_specs=pl.BlockSpec((tm, D), lambda i, te, tf, nu, rt: (i, 0)),
        scratch_shapes=[pltpu.VMEM((2, tm, D), F32), pltpu.SemaphoreType.DMA((2,)),
                        pltpu.VMEM((D, F), BF16), pltpu.VMEM((D, F), BF16), pltpu.VMEM((F, D), BF16)],
    )
    return pl.pallas_call(
        functools.partial(_gmm_kernel, tm=tm, n_tiles=n_tiles),
        grid_spec=grid_spec,
        out_shape=jax.ShapeDtypeStruct((n_tiles * tm, D), F32),
        compiler_params=_params(("arbitrary",), 56),
        name="grouped_expert_mlp",
    )(tile_e, tile_first, n_used, row_tok, x, g.reshape(1, D), w_gate, w_up, w_down)


def _combine_kernel(pos_ref, x_ref, plan_ref, y_hbm, o_ref, ybuf, sem, *, tc, top_k):
    i = pl.program_id(0)
    n = pl.num_programs(0)
    slot = i % 2

    def start(tile, s):
        for k in range(top_k):
            def body(r, carry, k=k):
                p = pos_ref[(tile * tc + r) * top_k + k]
                _row_copy(y_hbm, p, ybuf.at[s, k, pl.ds(r, 1)], sem.at[s]).start()
                return carry
            lax.fori_loop(0, tc, body, 0, unroll=8)

    @pl.when(i == 0)
    def _():
        start(0, 0)

    for k in range(top_k):
        _wait_row_gather(y_hbm, tc, ybuf.at[slot, k], sem.at[slot])

    @pl.when(i + 1 < n)
    def _():
        start(i + 1, 1 - slot)

    acc = x_ref[...]
    for k in range(top_k):
        acc = acc + plan_ref[:, top_k + k:top_k + k + 1] * ybuf[slot, k]
    o_ref[...] = acc


def moe_combine(x, y_sorted, plan_out, pos, tc):
    T, D = x.shape
    top_k = TOP_K_IN_GROUP
    grid_spec = pltpu.PrefetchScalarGridSpec(
        num_scalar_prefetch=1,
        grid=(T // tc,),
        in_specs=[pl.BlockSpec((tc, D), lambda i, p: (i, 0)),
                  pl.BlockSpec((tc, LANES), lambda i, p: (i, 0)),
                  pl.BlockSpec(memory_space=pl.ANY)],
        out_specs=pl.BlockSpec((tc, D), lambda i, p: (i, 0)),
        scratch_shapes=[pltpu.VMEM((2, top_k, tc, D), F32), pltpu.SemaphoreType.DMA((2,))],
    )
    return pl.pallas_call(
        functools.partial(_combine_kernel, tc=tc, top_k=top_k),
        grid_spec=grid_spec,
        out_shape=jax.ShapeDtypeStruct((T, D), F32),
        compiler_params=_params(("arbitrary",), 32),
        name="moe_combine",
    )(pos, x, plan_out, y_sorted)


def hier_moe_residual(x, g, w_r, b_r, n_groups, epg, w_gate, w_up, w_down, layer):
    logits = router_logits(x, g, w_r, b_r, tm=TM_PROJ)
    plan_out, pos, tile_e, tile_first, n_used, n_tiles = routing_plan(logits, n_groups, epg, TM_EXPERT)
    row_tok = invert_positions(pos, n_tiles * TM_EXPERT)
    y_sorted = grouped_expert_mlp(x, g, tile_e, tile_first, n_used, row_tok, w_gate, w_up, w_down,
                                  layer, TM_EXPERT)
    return moe_combine(x, y_sorted, plan_out, pos, TC_COMBINE)


def _pad_head_cols(w, heads, real):
    r = w.shape[0]
    return jnp.pad(w.reshape(r, heads, real), ((0, 0), (0, 0), (0, HEAD_PAD - real))).reshape(r, heads * HEAD_PAD)


def _rope_tables(positions):
    half = QK_ROPE // 2
    inv_freq = ROPE_THETA ** (-jnp.arange(half, dtype=F32) / half)
    ang = positions.astype(F32).reshape(-1, 1) * inv_freq[None, :]
    cos, sin = jnp.cos(ang), jnp.sin(ang)
    zeros = jnp.zeros((ang.shape[0], LANES - QK_ROPE), F32)
    return (jnp.concatenate([cos, cos, zeros], axis=1),
            jnp.concatenate([-sin, sin, zeros], axis=1))


def kernel(x, mem, positions, norm_mix_g, norm_ffn_g, w_o, mem_norm_g, w_mem_kv, mem_q_norm_g, mem_k_norm_g, w_in_a, conv_w, conv_b, w_lru_r, b_lru_r, w_lru_i, b_lru_i, lru_lambda, kv_in_norm_g, w_dkv, kv_latent_norm_g, w_uk, w_uv, k_head_norm_g, w_in_b, q_latent_norm_g, w_uq, q_head_norm_g, w_router_grp, b_router_grp, w_router_exp, b_router_exp, w_exp_gate, w_exp_up, w_exp_down):
    B, S, D = x.shape
    T = B * S
    M = mem.shape[1]
    depth = norm_mix_g.shape[0]
    n_a = w_in_a.shape[0]
    lru_w = lru_lambda.shape[1]
    mem_w = w_mem_kv.shape[2] // 2
    heads = w_uv.shape[1] // V_DIM
    kv_rank = kv_latent_norm_g.shape[0]
    q_rank = q_latent_norm_g.shape[1]
    n_groups, epg = w_exp_gate.shape[1], w_exp_gate.shape[2]
    n_exp = n_groups * epg
    d_exp = w_exp_gate.shape[-1]

    xr = x.reshape(T, D)
    mem2 = mem.reshape(B * M, D)
    cos_t, sin_t = _rope_tables(positions)
    wg_all = w_exp_gate.reshape(depth * n_exp, D, d_exp)
    wu_all = w_exp_up.reshape(depth * n_exp, D, d_exp)
    wd_all = w_exp_down.reshape(depth * n_exp, d_exp, D)
    pad_r = LANES - n_groups - n_exp
    w_router = jnp.concatenate([w_router_grp, w_router_exp, jnp.zeros((depth, D, pad_r), F32)], axis=2)
    b_router = jnp.concatenate([b_router_grp, b_router_exp, jnp.zeros((depth, pad_r), F32)], axis=1)

    k_sh = v_sh = None
    for l in range(depth):
        kv_mem = norm_matmul(mem2, mem_norm_g[l], w_mem_kv, l, BF16, tm=B * M, tn=TN_PROJ).reshape(B, M, 2 * mem_w)
        if l < n_a:
            proj = norm_matmul(xr, norm_mix_g[l], w_in_a, l, BF16, tm=TM_PROJ, tn=TN_PROJ)
            y_mix = rglru(proj.reshape(B, S, -1), conv_w[l], conv_b[l], w_lru_r[l], b_lru_r[l],
                          w_lru_i[l], b_lru_i[l], lru_lambda[l], nb=LRU_BLOCKS_PER_STEP,
                          tc=TC_LRU).reshape(T, lru_w)
            q_col = 2 * lru_w // mem_w
        else:
            j = l - n_a
            proj = norm_matmul(xr, norm_mix_g[l], w_in_b, j, BF16, tm=TM_PROJ, tn=TN_PROJ)
            wq = _pad_head_cols(w_uq[j], heads, QK_DIM).astype(BF16)
            qg = jnp.pad(q_head_norm_g[j] * (QK_DIM ** -0.5 * math.log2(math.e)),
                         (0, HEAD_PAD - QK_DIM)).reshape(1, HEAD_PAD)
            q = latent_heads(proj, q_rank, q_rank, q_latent_norm_g[j], wq, qg, cos_t, sin_t, heads, tm=TM_HEADS)
            y_mix = flash_attention(q.reshape(B, S, -1), k_sh, v_sh, heads, tq=TQ_FLASH).reshape(T, heads * V_DIM)
            q_col = q_rank // mem_w
        y_mem = mem_attention(proj, q_col, kv_mem, mem_q_norm_g[l], mem_k_norm_g[l], B, ts=TS_MEM)
        xr = out_proj(y_mix, y_mem, w_o, l, xr, tm=TM_PROJ, tn=TN_PROJ)
        xr = hier_moe_residual(xr, norm_ffn_g[l], w_router[l], b_router[l].reshape(1, LANES), n_groups, epg,
                               wg_all, wu_all, wd_all, (l, depth))
        if l == n_a - 1:
            lat_cols = kv_rank + LANES
            w_dkv_pad = jnp.pad(w_dkv, ((0, 0), (0, lat_cols - w_dkv.shape[1])))[None]
            ckv = norm_matmul(xr, kv_in_norm_g, w_dkv_pad, 0, F32, tm=TM_PROJ, tn=lat_cols)
            eye = jnp.pad(jnp.eye(QK_ROPE, dtype=F32), ((0, LANES - QK_ROPE), (0, 0)))
            eye3 = jnp.broadcast_to(eye[:, None, :], (LANES, heads, QK_ROPE))
            rope_rows = jnp.pad(eye3, ((0, 0), (0, 0), (QK_NOPE, HEAD_PAD - QK_DIM))).reshape(LANES, heads * HEAD_PAD)
            wk = jnp.concatenate([_pad_head_cols(w_uk, heads, QK_NOPE), rope_rows], axis=0).astype(BF16)
            kg = jnp.pad(k_head_norm_g, (0, HEAD_PAD - QK_DIM)).reshape(1, HEAD_PAD)
            k_flat, v_flat = latent_heads(ckv, lat_cols, kv_rank, kv_latent_norm_g, wk, kg, cos_t, sin_t,
                                          heads, tm=TM_HEADS, w_v=w_uv.astype(BF16))
            k_sh = k_flat.reshape(B, S, -1)
            v_sh = v_flat.reshape(B, S, -1)
    return xr.reshape(B, S, D)
```

```python
import functools
import math

import jax
import jax.numpy as jnp
from jax import lax
from jax.experimental import pallas as pl
from jax.experimental.pallas import tpu as pltpu

F32 = jnp.float32
BF16 = jnp.bfloat16
EPS = 1e-6

MEM_HEADS = 4
CONV_WIDTH = 4
LRU_C = 8.0
QK_NOPE = 128
QK_ROPE = 64
QK_DIM = QK_NOPE + QK_ROPE
V_DIM = 128
ROPE_THETA = 10000.0
TOP_K_IN_GROUP = 2

LANES = 128
HEAD_PAD = 2 * LANES
MIB = 1024 * 1024

TM_PROJ = 1024
TN_PROJ = 512
TS_MEM = 512
TQ_FLASH = 512
SUB_FLASH = 256
TM_HEADS = 512
TM_EXPERT = 256
WEIGHT_DMA_CHUNKS = 4
TC_COMBINE = 256
LRU_BLOCKS_PER_STEP = 3
TC_LRU = 256
PLAN_CHUNK = 256


def _params(sem, vmem_mib):
    return pltpu.CompilerParams(dimension_semantics=sem, vmem_limit_bytes=vmem_mib * MIB)


def _rms_rows(x, g):
    x = x.astype(F32)
    ms = jnp.mean(x * x, axis=-1, keepdims=True)
    return x * lax.rsqrt(ms + EPS) * g


def _norm_mm_kernel(x_ref, g_ref, w_ref, o_ref, xn_ref, *, tm, chunk):
    @pl.when(pl.program_id(1) == 0)
    def _():
        for r0 in range(0, tm, chunk):
            xn_ref[r0:r0 + chunk, :] = _rms_rows(x_ref[r0:r0 + chunk, :], g_ref[...]).astype(BF16)

    o_ref[...] = jnp.dot(xn_ref[...], w_ref[...].astype(BF16),
                         preferred_element_type=F32).astype(o_ref.dtype)


def norm_matmul(x, g, w, layer, out_dtype, tm, tn):
    T, K = x.shape
    N = w.shape[2]
    chunk = min(tm, 256)
    return pl.pallas_call(
        functools.partial(_norm_mm_kernel, tm=tm, chunk=chunk),
        grid=(T // tm, N // tn),
        in_specs=[pl.BlockSpec((tm, K), lambda i, j: (i, 0)),
                  pl.BlockSpec((1, K), lambda i, j: (0, 0)),
                  pl.BlockSpec((None, K, tn), lambda i, j: (layer, 0, j))],
        out_specs=pl.BlockSpec((tm, tn), lambda i, j: (i, j)),
        out_shape=jax.ShapeDtypeStruct((T, N), out_dtype),
        scratch_shapes=[pltpu.VMEM((tm, K), BF16)],
        compiler_params=_params(("parallel", "arbitrary"), 48),
        name="norm_matmul",
    )(x, g.reshape(1, K), w)


def _out_proj_kernel(ya_ref, yb_ref, wa_ref, wb_ref, x_ref, o_ref):
    acc = jnp.dot(ya_ref[...], wa_ref[...].astype(BF16), preferred_element_type=F32)
    acc += jnp.dot(yb_ref[...], wb_ref[...].astype(BF16), preferred_element_type=F32)
    o_ref[...] = x_ref[...] + acc


def out_proj(y_mix, y_mem, w_o, layer, x, tm, tn):
    T, Wa = y_mix.shape
    Wb = y_mem.shape[1]
    N = w_o.shape[2]
    assert Wa % Wb == 0
    return pl.pallas_call(
        _out_proj_kernel,
        grid=(T // tm, N // tn),
        in_specs=[pl.BlockSpec((tm, Wa), lambda i, j: (i, 0)),
                  pl.BlockSpec((tm, Wb), lambda i, j: (i, 0)),
                  pl.BlockSpec((None, Wa, tn), lambda i, j: (layer, 0, j)),
                  pl.BlockSpec((None, Wb, tn), lambda i, j: (layer, Wa // Wb, j)),
                  pl.BlockSpec((tm, tn), lambda i, j: (i, j))],
        out_specs=pl.BlockSpec((tm, tn), lambda i, j: (i, j)),
        out_shape=jax.ShapeDtypeStruct((T, N), F32),
        compiler_params=_params(("parallel", "arbitrary"), 48),
        name="out_proj",
    )(y_mix, y_mem, w_o, w_o, x)


def _mem_attn_kernel(q_ref, kv_ref, qg_ref, kg_ref, o_ref, *, heads, hd):
    width = heads * hd
    scale = hd ** -0.5
    for h in range(heads):
        qn = _rms_rows(q_ref[:, h * hd:(h + 1) * hd], qg_ref[...]) * scale
        kn = _rms_rows(kv_ref[0, :, h * hd:(h + 1) * hd], kg_ref[...])
        v = kv_ref[0, :, width + h * hd:width + (h + 1) * hd]
        s = lax.dot_general(qn.astype(BF16), kn.astype(BF16), (((1,), (1,)), ((), ())),
                            preferred_element_type=F32)
        m = jnp.max(s, axis=-1, keepdims=True)
        p = jnp.exp(s - m)
        l = jnp.sum(p, axis=-1, keepdims=True)
        o = jnp.dot(p.astype(BF16), v, preferred_element_type=F32) / l
        o_ref[:, h * hd:(h + 1) * hd] = o.astype(o_ref.dtype)


def mem_attention(proj, q_col_block, kv, q_g, k_g, batch, ts):
    T = proj.shape[0]
    _, M, two_w = kv.shape
    width = two_w // 2
    hd = width // MEM_HEADS
    per_b = T // batch // ts
    return pl.pallas_call(
        functools.partial(_mem_attn_kernel, heads=MEM_HEADS, hd=hd),
        grid=(batch, per_b),
        in_specs=[pl.BlockSpec((ts, width), lambda b, i: (b * per_b + i, q_col_block)),
                  pl.BlockSpec((1, M, two_w), lambda b, i: (b, 0, 0)),
                  pl.BlockSpec((1, hd), lambda b, i: (0, 0)),
                  pl.BlockSpec((1, hd), lambda b, i: (0, 0))],
        out_specs=pl.BlockSpec((ts, width), lambda b, i: (b * per_b + i, 0)),
        out_shape=jax.ShapeDtypeStruct((T, width), BF16),
        compiler_params=_params(("parallel", "arbitrary"), 32),
        name="mem_attention",
    )(proj, kv, q_g.reshape(1, hd), k_g.reshape(1, hd))


def _gelu_tanh(x):
    return 0.5 * x * (1.0 + jnp.tanh(0.7978845608028654 * (x + 0.044715 * x * x * x)))


def _rglru_kernel(u_ref, gb_ref, cw_ref, cb_ref, wr_ref, wi_ref, br_ref, bi_ref, lam_ref,
                  o_ref, a_s, b_s, *, seq, nb, blk, tc):
    wb = nb * blk
    neg_lam = -lam_ref[...]
    softplus = jnp.maximum(neg_lam, 0.0) + jnp.log1p(jnp.exp(-jnp.abs(neg_lam)))

    def gates(c, carry):
        t0 = pl.multiple_of(c * tc, tc)
        cur = u_ref[0, pl.ds(t0, tc), :].astype(F32)
        p0 = pl.multiple_of(jnp.maximum(t0 - 16, 0), 16)
        prev = u_ref[0, pl.ds(p0, 16), :].astype(F32)
        prev = jnp.where(c > 0, prev, 0.0)
        full = jnp.concatenate([prev, cur], axis=0)
        y = cb_ref[...] + cw_ref[0:1, :] * cur
        for k in range(1, CONV_WIDTH):
            y = y + cw_ref[k:k + 1, :] * full[16 - k:16 - k + tc, :]
        r_parts, i_parts = [], []
        for n in range(nb):
            yb = y[:, n * blk:(n + 1) * blk].astype(BF16)
            r_parts.append(jnp.dot(yb, wr_ref[n].astype(BF16), preferred_element_type=F32))
            i_parts.append(jnp.dot(yb, wi_ref[n].astype(BF16), preferred_element_type=F32))
        r = jax.nn.sigmoid(jnp.concatenate(r_parts, axis=1) + br_ref[...])
        ig = jax.nn.sigmoid(jnp.concatenate(i_parts, axis=1) + bi_ref[...])
        log_a = (-LRU_C) * r * softplus
        a = jnp.exp(log_a)
        a_s[pl.ds(t0, tc), :] = a
        b_s[pl.ds(t0, tc), :] = jnp.sqrt(-jnp.tanh(log_a) * (a * a + 1.0)) * (ig * y)
        return carry

    lax.fori_loop(0, seq // tc, gates, 0)

    row = lax.broadcasted_iota(jnp.int32, (8, wb), 0)

    def scan(c, h):
        t0 = pl.multiple_of(c * 8, 8)
        a = a_s[pl.ds(t0, 8), :]
        b = b_s[pl.ds(t0, 8), :]
        for s in (1, 2, 4):
            a_sh = pltpu.roll(a, s, axis=0)
            b_sh = pltpu.roll(b, s, axis=0)
            keep = row >= s
            b = jnp.where(keep, a * b_sh + b, b)
            a = jnp.where(keep, a * a_sh, a)
        hc = a * h + b
        b_s[pl.ds(t0, 8), :] = hc
        return hc[7:8, :]

    lax.fori_loop(0, seq // 8, scan, jnp.zeros((1, wb), F32))

    def gate_out(c, carry):
        t0 = pl.multiple_of(c * tc, tc)
        g = gb_ref[0, pl.ds(t0, tc), :].astype(F32)
        o_ref[0, pl.ds(t0, tc), :] = (_gelu_tanh(g) * b_s[pl.ds(t0, tc), :]).astype(o_ref.dtype)
        return carry

    lax.fori_loop(0, seq // tc, gate_out, 0)


def rglru(proj, conv_w, conv_b, w_r, b_r, w_i, b_i, lam, nb, tc):
    B, S, _ = proj.shape
    W = lam.shape[0]
    n_blocks, blk, _ = w_r.shape
    wb = nb * blk
    ncb = W // wb
    vec = lambda: pl.BlockSpec((1, wb), lambda b, j: (0, j))
    return pl.pallas_call(
        functools.partial(_rglru_kernel, seq=S, nb=nb, blk=blk, tc=tc),
        grid=(B, ncb),
        in_specs=[pl.BlockSpec((1, S, wb), lambda b, j: (b, 0, j)),
                  pl.BlockSpec((1, S, wb), lambda b, j: (b, 0, ncb + j)),
                  pl.BlockSpec((CONV_WIDTH, wb), lambda b, j: (0, j)),
                  vec(),
                  pl.BlockSpec((nb, blk, blk), lambda b, j: (j, 0, 0)),
                  pl.BlockSpec((nb, blk, blk), lambda b, j: (j, 0, 0)),
                  vec(), vec(), vec()],
        out_specs=pl.BlockSpec((1, S, wb), lambda b, j: (b, 0, j)),
        out_shape=jax.ShapeDtypeStruct((B, S, W), BF16),
        scratch_shapes=[pltpu.VMEM((S, wb), F32), pltpu.VMEM((S, wb), F32)],
        compiler_params=_params(("parallel", "arbitrary"), 32),
        name="rglru",
    )(proj, proj, conv_w, conv_b.reshape(1, W), w_r, w_i, b_r.reshape(1, W), b_i.reshape(1, W),
      lam.reshape(1, W))


def _heads_kernel(*refs, rank, heads, with_v):
    if with_v:
        lat_ref, gl_ref, w_ref, hg_ref, cos_ref, sin_ref, wv_ref, o_ref, v_ref = refs
    else:
        lat_ref, gl_ref, w_ref, hg_ref, cos_ref, sin_ref, o_ref = refs
    lat = lat_ref[...].astype(F32)
    cn = _rms_rows(lat[:, :rank], gl_ref[...])
    full = cn if lat.shape[1] == rank else jnp.concatenate([cn, lat[:, rank:]], axis=1)
    full = full.astype(BF16)
    lane = lax.broadcasted_iota(jnp.int32, cos_ref.shape, 1)
    half = QK_ROPE // 2
    for h in range(heads):
        t = jnp.dot(full, w_ref[:, h * HEAD_PAD:(h + 1) * HEAD_PAD], preferred_element_type=F32)
        ss = jnp.sum(t * t, axis=-1, keepdims=True) * (1.0 / QK_DIM)
        tn = t * lax.rsqrt(ss + EPS) * hg_ref[...]
        rp = tn[:, QK_NOPE:]
        swapped = jnp.where(lane < half, pltpu.roll(rp, LANES - half, axis=1),
                            pltpu.roll(rp, half, axis=1))
        rot = rp * cos_ref[...] + swapped * sin_ref[...]
        o_ref[:, h * HEAD_PAD:h * HEAD_PAD + QK_NOPE] = tn[:, :QK_NOPE].astype(o_ref.dtype)
        o_ref[:, h * HEAD_PAD + QK_NOPE:(h + 1) * HEAD_PAD] = rot.astype(o_ref.dtype)
    if with_v:
        v = jnp.dot(cn.astype(BF16), wv_ref[...], preferred_element_type=F32).astype(v_ref.dtype)
        ones = jnp.ones((v.shape[0], HEAD_PAD - V_DIM), v_ref.dtype)
        for h in range(heads):
            v_ref[:, h * HEAD_PAD:h * HEAD_PAD + V_DIM] = v[:, h * V_DIM:(h + 1) * V_DIM]
            v_ref[:, h * HEAD_PAD + V_DIM:(h + 1) * HEAD_PAD] = ones


def latent_heads(lat, lat_cols, rank, g_lat, w_pad, head_gain, cos_t, sin_t, heads, tm, w_v=None):
    T = lat.shape[0]
    n_out = heads * HEAD_PAD
    in_specs = [pl.BlockSpec((tm, lat_cols), lambda i: (i, 0)),
                pl.BlockSpec((1, rank), lambda i: (0, 0)),
                pl.BlockSpec((lat_cols, n_out), lambda i: (0, 0)),
                pl.BlockSpec((1, HEAD_PAD), lambda i: (0, 0)),
                pl.BlockSpec((tm, LANES), lambda i: (i, 0)),
                pl.BlockSpec((tm, LANES), lambda i: (i, 0))]
    args = [lat, g_lat.reshape(1, rank), w_pad, head_gain, cos_t, sin_t]
    out_shape = jax.ShapeDtypeStruct((T, n_out), BF16)
    out_specs = pl.BlockSpec((tm, n_out), lambda i: (i, 0))
    if w_v is not None:
        in_specs.append(pl.BlockSpec(w_v.shape, lambda i: (0, 0)))
        args.append(w_v)
        out_shape = (out_shape, jax.ShapeDtypeStruct((T, n_out), BF16))
        out_specs = (out_specs, pl.BlockSpec((tm, n_out), lambda i: (i, 0)))
    return pl.pallas_call(
        functools.partial(_heads_kernel, rank=rank, heads=heads, with_v=w_v is not None),
        grid=(T // tm,),
        in_specs=in_specs, out_specs=out_specs, out_shape=out_shape,
        compiler_params=_params(("parallel",), 48),
        name="latent_heads_kv" if w_v is not None else "latent_heads_q",
    )(*args)


def _flash_kernel(q_ref, k_ref, v_ref, o_ref, s_s, m_s, acc_s, *, tq, sub, n_kt):
    i = pl.program_id(2)
    m_s[...] = jnp.full(m_s.shape, -1e30, F32)
    acc_s[...] = jnp.zeros(acc_s.shape, F32)
    n_sub = tq // sub

    def scores(kt, slot):
        k = k_ref[0, pl.ds(kt * tq, tq), :]
        for h in range(n_sub):
            rows = slice(h * sub, (h + 1) * sub)
            s_s[slot, rows, :] = lax.dot_general(q_ref[0, rows, :], k, (((1,), (1,)), ((), ())),
                                                 preferred_element_type=F32)

    def softmax_pv(k0, slot, masked):
        for h in range(n_sub):
            rows = slice(h * sub, (h + 1) * sub)
            nk = (h + 1) * sub if masked else tq
            s = s_s[slot, rows, :nk]
            if masked:
                r = lax.broadcasted_iota(jnp.int32, s.shape, 0) + h * sub
                c = lax.broadcasted_iota(jnp.int32, s.shape, 1)
                s = jnp.where(c <= r, s, -1e30)
            v = v_ref[0, pl.ds(k0, nk), :]
            m_prev = m_s[rows, :]
            m_new = jnp.maximum(m_prev, jnp.max(s, axis=-1, keepdims=True))
            alpha = jnp.exp2(m_prev - m_new)
            p = jnp.exp2(s - jnp.concatenate([m_new] * (nk // LANES), axis=1))
            acc_s[rows, :] = (jnp.concatenate([alpha, alpha], axis=1) * acc_s[rows, :]
                              + jnp.dot(p.astype(BF16), v, preferred_element_type=F32))
            m_s[rows, :] = m_new

    scores(0, 0)
    for kt in range(n_kt - 1):
        @pl.when(kt < i)
        def _(kt=kt):
            scores(kt + 1, (kt + 1) % 2)
            softmax_pv(kt * tq, kt % 2, False)
    softmax_pv(pl.multiple_of(i * tq, tq), i % 2, True)
    o_ref[0] = (acc_s[:, :V_DIM] / acc_s[:, V_DIM:]).astype(o_ref.dtype)


def flash_attention(q, k, v, heads, tq):
    B, S, _ = q.shape
    return pl.pallas_call(
        functools.partial(_flash_kernel, tq=tq, sub=SUB_FLASH, n_kt=S // tq),
        grid=(B, heads, S // tq),
        in_specs=[pl.BlockSpec((1, tq, HEAD_PAD), lambda b, h, i: (b, i, h)),
                  pl.BlockSpec((1, S, HEAD_PAD), lambda b, h, i: (b, 0, h)),
                  pl.BlockSpec((1, S, HEAD_PAD), lambda b, h, i: (b, 0, h))],
        out_specs=pl.BlockSpec((1, tq, V_DIM), lambda b, h, i: (b, i, h)),
        out_shape=jax.ShapeDtypeStruct((B, S, heads * V_DIM), BF16),
        scratch_shapes=[pltpu.VMEM((2, tq, tq), F32), pltpu.VMEM((tq, LANES), F32),
                        pltpu.VMEM((tq, HEAD_PAD), F32)],
        compiler_params=_params(("parallel", "parallel", "arbitrary"), 32),
        name="flash_attention",
    )(q, k, v)


def _router_kernel(x_ref, g_ref, w_ref, b_ref, o_ref, *, tm, chunk):
    w = w_ref[...]
    w_hi = w.astype(BF16)
    w_lo = (w - w_hi.astype(F32)).astype(BF16)
    for r0 in range(0, tm, chunk):
        xn = _rms_rows(x_ref[r0:r0 + chunk, :], g_ref[...])
        x_hi = xn.astype(BF16)
        x_lo = (xn - x_hi.astype(F32)).astype(BF16)
        acc = jnp.dot(x_hi, w_hi, preferred_element_type=F32)
        acc += jnp.dot(x_lo, w_hi, preferred_element_type=F32)
        acc += jnp.dot(x_hi, w_lo, preferred_element_type=F32)
        o_ref[r0:r0 + chunk, :] = acc + b_ref[...]


def router_logits(x, g, w_pad, b_pad, tm):
    T, D = x.shape
    return pl.pallas_call(
        functools.partial(_router_kernel, tm=tm, chunk=256),
        grid=(T // tm,),
        in_specs=[pl.BlockSpec((tm, D), lambda i: (i, 0)),
                  pl.BlockSpec((1, D), lambda i: (0, 0)),
                  pl.BlockSpec((D, LANES), lambda i: (0, 0)),
                  pl.BlockSpec((1, LANES), lambda i: (0, 0))],
        out_specs=pl.BlockSpec((tm, LANES), lambda i: (i, 0)),
        out_shape=jax.ShapeDtypeStruct((T, LANES), F32),
        compiler_params=_params(("parallel",), 32),
        name="router_logits",
    )(x, g.reshape(1, D), w_pad, b_pad)


def _plan_kernel(lg_ref, out_ref, tiles_ref, c_s, info_s, *, n_tok, n_groups, epg, tm, chunk):
    n_exp = n_groups * epg
    lane = lax.broadcasted_iota(jnp.int32, (chunk, LANES), 1).astype(F32)
    rr = lax.broadcasted_iota(jnp.int32, (chunk, chunk), 0)
    cc = lax.broadcasted_iota(jnp.int32, (chunk, chunk), 1)
    tri = jnp.where(cc <= rr, 1.0, 0.0).astype(BF16)
    neg_inf = -jnp.inf

    def first_argmax(vals, vmax):
        return jnp.min(jnp.where(vals == vmax, lane, float(LANES)), axis=1, keepdims=True)

    def decide(c, carry):
        r0 = pl.multiple_of(c * chunk, chunk)
        lg = lg_ref[pl.ds(r0, chunk), :]
        gl = jnp.where(lane < n_groups, lg, neg_inf)
        gmax = jnp.max(gl, axis=1, keepdims=True)
        g_idx = first_argmax(gl, gmax)
        p_top = 1.0 / jnp.sum(jnp.exp(gl - gmax), axis=1, keepdims=True)
        lo = n_groups + g_idx * epg
        el = jnp.where(jnp.logical_and(lane >= lo, lane < lo + epg), lg, neg_inf)
        l1 = jnp.max(el, axis=1, keepdims=True)
        i1 = first_argmax(el, l1)
        el2 = jnp.where(lane == i1, neg_inf, el)
        l2 = jnp.max(el2, axis=1, keepdims=True)
        i2 = first_argmax(el2, l2)
        d = jnp.exp(l2 - l1)
        w1 = 1.0 / (1.0 + d)
        e1 = i1 - n_groups
        e2 = i2 - n_groups
        onehot = jnp.where(jnp.logical_or(lane == e1, lane == e2), 1.0, 0.0)
        cs = jnp.dot(tri, onehot.astype(BF16), preferred_element_type=F32) + carry
        c_s[pl.ds(r0, chunk), :] = cs
        info_s[pl.ds(r0, chunk), :] = jnp.where(
            lane == 0, e1, jnp.where(lane == 1, e2, jnp.where(
                lane == 2, p_top * w1, jnp.where(lane == 3, p_top * (d * w1), 0.0))))
        return cs[chunk - 1:chunk, :]

    counts = lax.fori_loop(0, n_tok // chunk, decide, jnp.zeros((1, LANES), F32))

    tiles = jnp.floor((counts + (tm - 1)) * (1.0 / tm))
    jj = lax.broadcasted_iota(jnp.int32, (LANES, LANES), 0)
    ee = lax.broadcasted_iota(jnp.int32, (LANES, LANES), 1)
    upper = jnp.where(jj <= ee, 1.0, 0.0).astype(BF16)
    tile_end = jnp.dot(jnp.broadcast_to(tiles, (8, LANES)).astype(BF16), upper,
                       preferred_element_type=F32)[0:1, :]
    tile_start = tile_end - tiles
    row_start = tile_start * tm

    def place(c, carry):
        r0 = pl.multiple_of(c * chunk, chunk)
        info = info_s[pl.ds(r0, chunk), :]
        base = row_start + c_s[pl.ds(r0, chunk), :] - 1.0
        pos1 = jnp.sum(jnp.where(lane == info[:, 0:1], base, 0.0), axis=1, keepdims=True)
        pos2 = jnp.sum(jnp.where(lane == info[:, 1:2], base, 0.0), axis=1, keepdims=True)
        out_ref[pl.ds(r0, chunk), :] = jnp.where(lane == 0, pos1, jnp.where(lane == 1, pos2, info))
        return carry

    lax.fori_loop(0, n_tok // chunk, place, 0)

    ti = jj.astype(F32)
    lane_sq = ee.astype(F32)
    is_exp = lane_sq < n_exp
    n_used = tile_end[:, n_exp - 1:n_exp]
    ti_c = jnp.minimum(ti, n_used - 1.0)
    tile_e = jnp.sum(jnp.where(jnp.logical_and(is_exp, tile_end <= ti_c), 1.0, 0.0), axis=1, keepdims=True)
    first = jnp.sum(jnp.where(jnp.logical_and(jnp.logical_and(is_exp, tile_start == ti), tiles > 0.0),
                              1.0, 0.0), axis=1, keepdims=True)
    nonempty = jnp.logical_and(is_exp, tiles > 0.0)
    next_e = jnp.min(jnp.where(jnp.logical_and(nonempty, lane_sq > tile_e), lane_sq, float(LANES)),
                     axis=1, keepdims=True)
    next_e = jnp.where(next_e >= LANES, tile_e, next_e)
    group = jnp.sum(jnp.where(jnp.logical_and(nonempty, lane_sq < tile_e), 1.0, 0.0), axis=1, keepdims=True)
    wslot = group - 2.0 * jnp.floor(group * 0.5)
    diag = jj == ee
    pad_lo = jnp.sum(jnp.where(diag, row_start + counts, 0.0), axis=1, keepdims=True)
    pad_hi = jnp.sum(jnp.where(diag, row_start + tiles * tm, 0.0), axis=1, keepdims=True)
    fields = (tile_e, first, n_used, next_e, wslot, pad_lo, pad_hi)
    packed = jnp.zeros((LANES, LANES), F32)
    for f, val in enumerate(fields):
        packed = jnp.where(lane_sq == f, val, packed)
    tiles_ref[...] = packed.astype(jnp.int32)


N_TILE_FIELDS = 5


def routing_plan(logits, n_groups, epg, tm):
    T = logits.shape[0]
    n_exp = n_groups * epg
    n_tiles = T * TOP_K_IN_GROUP // tm + n_exp
    assert n_tiles <= LANES
    out, tiles = pl.pallas_call(
        functools.partial(_plan_kernel, n_tok=T, n_groups=n_groups, epg=epg, tm=tm, chunk=PLAN_CHUNK),
        grid=(1,),
        in_specs=[pl.BlockSpec((T, LANES), lambda i: (0, 0))],
        out_specs=(pl.BlockSpec((T, LANES), lambda i: (0, 0)),
                   pl.BlockSpec((LANES, LANES), lambda i: (0, 0))),
        out_shape=(jax.ShapeDtypeStruct((T, LANES), F32), jax.ShapeDtypeStruct((LANES, LANES), jnp.int32)),
        scratch_shapes=[pltpu.VMEM((T, LANES), F32), pltpu.VMEM((T, LANES), F32)],
        compiler_params=_params(("arbitrary",), 40),
        name="routing_plan",
    )(logits)
    pos = out[:, :TOP_K_IN_GROUP].astype(jnp.int32).reshape(-1)
    tile_info = tiles[:n_tiles, :N_TILE_FIELDS].T.reshape(-1)
    pads = tiles[:n_exp, N_TILE_FIELDS:N_TILE_FIELDS + 2].T.reshape(-1)
    return out, pos, tile_info, pads, n_tiles


def _invert_kernel(pos_ref, pads_ref, info_ref, out_ref, *, n_tok, top_k, n_exp, tm, n_tiles):
    def zero(r, carry):
        out_ref[r] = 0
        return carry

    for e in range(n_exp):
        lax.fori_loop(pads_ref[e], pads_ref[n_exp + e], zero, 0)

    def zero8(g, carry):
        for u in range(8):
            out_ref[g * 8 + u] = 0
        return carry
    lax.fori_loop(info_ref[2 * n_tiles] * (tm // 8), n_tiles * (tm // 8), zero8, 0)

    def put(t, carry):
        for k in range(top_k):
            out_ref[pos_ref[t * top_k + k]] = t
        return carry
    lax.fori_loop(0, n_tok, put, 0, unroll=8)


def invert_positions(pos, pads, tile_info, n_tiles, tm):
    top_k = TOP_K_IN_GROUP
    return pl.pallas_call(
        functools.partial(_invert_kernel, n_tok=pos.shape[0] // top_k, top_k=top_k,
                          n_exp=pads.shape[0] // 2, tm=tm, n_tiles=n_tiles),
        grid_spec=pltpu.PrefetchScalarGridSpec(
            num_scalar_prefetch=3, grid=(1,), in_specs=[],
            out_specs=pl.BlockSpec(memory_space=pltpu.SMEM)),
        out_shape=jax.ShapeDtypeStruct((n_tiles * tm,), jnp.int32),
        compiler_params=_params(("arbitrary",), 16),
        name="invert_positions",
    )(pos, pads, tile_info)


def _row_copy(src_hbm, tok, dst_row, sem):
    return pltpu.make_async_copy(src_hbm.at[pl.ds(tok, 1)], dst_row, sem)


def _start_row_gather(src_hbm, idx_ref, base, n, dst_row, sem):
    def body(r, carry):
        _row_copy(src_hbm, idx_ref[base + r], dst_row(r), sem).start()
        return carry
    lax.fori_loop(0, n, body, 0, unroll=8)


def _wait_row_gather(src_hbm, n, dst_all, sem):
    pltpu.make_async_copy(src_hbm.at[pl.ds(0, n)], dst_all, sem).wait()


def _gmm_kernel(info_ref, row_tok_ref, x_hbm, g_ref, wg_hbm, wu_hbm, wd_hbm, o_ref,
                xbuf, sem, wg_f, wu_f, wd_f, wsem, wg_bf, wu_bf, wd_bf, *, tm, n_tiles, w_off, n_chunks):
    i = pl.program_id(0)
    expert = info_ref[i]
    first = info_ref[n_tiles + i]
    n_used = info_ref[2 * n_tiles]
    next_expert = info_ref[3 * n_tiles + i]
    wslot = info_ref[4 * n_tiles + i]
    slot = i % 2

    def weight_copies(e, s):
        copies = []
        for hbm, buf in ((wg_hbm, wg_f), (wu_hbm, wu_f), (wd_hbm, wd_f)):
            rows = hbm.shape[1] // n_chunks
            for c in range(n_chunks):
                copies.append(pltpu.make_async_copy(hbm.at[w_off + e, pl.ds(c * rows, rows)],
                                                    buf.at[s, pl.ds(c * rows, rows)], wsem.at[s]))
        return copies

    @pl.when(jnp.logical_and(i == 0, n_used > 0))
    def _():
        for cp in weight_copies(expert, 0):
            cp.start(priority=1)
        _start_row_gather(x_hbm, row_tok_ref, 0, tm, lambda r: xbuf.at[0, pl.ds(r, 1)], sem.at[0])

    @pl.when(i < n_used)
    def _():
        @pl.when(first == 1)
        def _():
            for cp in weight_copies(expert, wslot):
                cp.wait()

            @pl.when(next_expert != expert)
            def _():
                for cp in weight_copies(next_expert, 1 - wslot):
                    cp.start(priority=1)

            wg_bf[...] = wg_f[wslot].astype(BF16)
            wu_bf[...] = wu_f[wslot].astype(BF16)
            wd_bf[...] = wd_f[wslot].astype(BF16)

        _wait_row_gather(x_hbm, tm, xbuf.at[slot], sem.at[slot])
        xn = _rms_rows(xbuf[slot], g_ref[...]).astype(BF16)
        base = jnp.minimum(i + 1, n_tiles - 1) * tm
        for r in range(tm):
            _row_copy(x_hbm, row_tok_ref[base + r], xbuf.at[1 - slot, pl.ds(r, 1)], sem.at[1 - slot]).start()
        hg = jnp.dot(xn, wg_bf[...], preferred_element_type=F32)
        hu = jnp.dot(xn, wu_bf[...], preferred_element_type=F32)
        hh = (hg * jax.nn.sigmoid(hg)) * hu
        o_ref[...] = jnp.dot(hh.astype(BF16), wd_bf[...], preferred_element_type=F32)

        @pl.when(i == n_tiles - 1)
        def _():
            _wait_row_gather(x_hbm, tm, xbuf.at[1 - slot], sem.at[1 - slot])

    @pl.when(i >= n_used)
    def _():
        @pl.when(jnp.logical_and(i == n_used, i > 0))
        def _():
            _wait_row_gather(x_hbm, tm, xbuf.at[slot], sem.at[slot])
        o_ref[...] = jnp.zeros(o_ref.shape, o_ref.dtype)


def grouped_expert_mlp(x, g, tile_info, row_tok, w_gate, w_up, w_down, layer, n_tiles, tm):
    T, D = x.shape
    F = w_gate.shape[2]
    layer_idx, depth = layer
    off = layer_idx * (w_gate.shape[0] // depth)
    any_spec = pl.BlockSpec(memory_space=pl.ANY)
    grid_spec = pltpu.PrefetchScalarGridSpec(
        num_scalar_prefetch=2,
        grid=(n_tiles,),
        in_specs=[any_spec, pl.BlockSpec((1, D), lambda i, ti, rt: (0, 0)), any_spec, any_spec, any_spec],
        out_specs=pl.BlockSpec((tm, D), lambda i, ti, rt: (i, 0)),
        scratch_shapes=[pltpu.VMEM((2, tm, D), F32), pltpu.SemaphoreType.DMA((2,)),
                        pltpu.VMEM((2, D, F), F32), pltpu.VMEM((2, D, F), F32), pltpu.VMEM((2, F, D), F32),
                        pltpu.SemaphoreType.DMA((2,)),
                        pltpu.VMEM((D, F), BF16), pltpu.VMEM((D, F), BF16), pltpu.VMEM((F, D), BF16)],
    )
    return pl.pallas_call(
        functools.partial(_gmm_kernel, tm=tm, n_tiles=n_tiles, w_off=off, n_chunks=WEIGHT_DMA_CHUNKS),
        grid_spec=grid_spec,
        out_shape=jax.ShapeDtypeStruct((n_tiles * tm, D), F32),
        compiler_params=_params(("arbitrary",), 56),
        name="grouped_expert_mlp",
    )(tile_info, row_tok, x, g.reshape(1, D), w_gate, w_up, w_down)


def _combine_kernel(pos_ref, x_ref, plan_ref, y_hbm, o_ref, ybuf, sem, *, tc, top_k):
    i = pl.program_id(0)
    n = pl.num_programs(0)
    slot = i % 2

    def start(tile, s):
        base = tile * (tc * top_k)
        for r in range(tc):
            for k in range(top_k):
                _row_copy(y_hbm, pos_ref[base + r * top_k + k], ybuf.at[s, k, pl.ds(r, 1)], sem.at[s]).start()

    @pl.when(i == 0)
    def _():
        start(0, 0)

    for k in range(top_k):
        _wait_row_gather(y_hbm, tc, ybuf.at[slot, k], sem.at[slot])

    @pl.when(i + 1 < n)
    def _():
        start(i + 1, 1 - slot)

    acc = x_ref[...]
    for k in range(top_k):
        acc = acc + plan_ref[:, top_k + k:top_k + k + 1] * ybuf[slot, k]
    o_ref[...] = acc


def moe_combine(x, y_sorted, plan_out, pos, tc):
    T, D = x.shape
    top_k = TOP_K_IN_GROUP
    grid_spec = pltpu.PrefetchScalarGridSpec(
        num_scalar_prefetch=1,
        grid=(T // tc,),
        in_specs=[pl.BlockSpec((tc, D), lambda i, p: (i, 0)),
                  pl.BlockSpec((tc, LANES), lambda i, p: (i, 0)),
                  pl.BlockSpec(memory_space=pl.ANY)],
        out_specs=pl.BlockSpec((tc, D), lambda i, p: (i, 0)),
        scratch_shapes=[pltpu.VMEM((2, top_k, tc, D), F32), pltpu.SemaphoreType.DMA((2,))],
    )
    return pl.pallas_call(
        functools.partial(_combine_kernel, tc=tc, top_k=top_k),
        grid_spec=grid_spec,
        out_shape=jax.ShapeDtypeStruct((T, D), F32),
        compiler_params=_params(("arbitrary",), 32),
        name="moe_combine",
    )(pos, x, plan_out, y_sorted)


def hier_moe_residual(x, g, w_r, b_r, n_groups, epg, w_gate, w_up, w_down, layer):
    logits = router_logits(x, g, w_r, b_r, tm=TM_PROJ)
    plan_out, pos, tile_info, pads, n_tiles = routing_plan(logits, n_groups, epg, TM_EXPERT)
    row_tok = invert_positions(pos, pads, tile_info, n_tiles, TM_EXPERT)
    y_sorted = grouped_expert_mlp(x, g, tile_info, row_tok, w_gate, w_up, w_down, layer, n_tiles, TM_EXPERT)
    return moe_combine(x, y_sorted, plan_out, pos, TC_COMBINE)


def _pad_head_cols(w, heads, real):
    r = w.shape[0]
    return jnp.pad(w.reshape(r, heads, real), ((0, 0), (0, 0), (0, HEAD_PAD - real))).reshape(r, heads * HEAD_PAD)


def _rope_tables(positions):
    half = QK_ROPE // 2
    inv_freq = ROPE_THETA ** (-jnp.arange(half, dtype=F32) / half)
    ang = positions.astype(F32).reshape(-1, 1) * inv_freq[None, :]
    cos, sin = jnp.cos(ang), jnp.sin(ang)
    zeros = jnp.zeros((ang.shape[0], LANES - QK_ROPE), F32)
    return (jnp.concatenate([cos, cos, zeros], axis=1),
            jnp.concatenate([-sin, sin, zeros], axis=1))


def kernel(x, mem, positions, norm_mix_g, norm_ffn_g, w_o, mem_norm_g, w_mem_kv, mem_q_norm_g, mem_k_norm_g, w_in_a, conv_w, conv_b, w_lru_r, b_lru_r, w_lru_i, b_lru_i, lru_lambda, kv_in_norm_g, w_dkv, kv_latent_norm_g, w_uk, w_uv, k_head_norm_g, w_in_b, q_latent_norm_g, w_uq, q_head_norm_g, w_router_grp, b_router_grp, w_router_exp, b_router_exp, w_exp_gate, w_exp_up, w_exp_down):
    B, S, D = x.shape
    T = B * S
    M = mem.shape[1]
    depth = norm_mix_g.shape[0]
    n_a = w_in_a.shape[0]
    lru_w = lru_lambda.shape[1]
    mem_w = w_mem_kv.shape[2] // 2
    heads = w_uv.shape[1] // V_DIM
    kv_rank = kv_latent_norm_g.shape[0]
    q_rank = q_latent_norm_g.shape[1]
    n_groups, epg = w_exp_gate.shape[1], w_exp_gate.shape[2]
    n_exp = n_groups * epg
    d_exp = w_exp_gate.shape[-1]

    xr = x.reshape(T, D)
    mem2 = mem.reshape(B * M, D)
    cos_t, sin_t = _rope_tables(positions)
    wg_all = w_exp_gate.reshape(depth * n_exp, D, d_exp)
    wu_all = w_exp_up.reshape(depth * n_exp, D, d_exp)
    wd_all = w_exp_down.reshape(depth * n_exp, d_exp, D)
    pad_r = LANES - n_groups - n_exp
    w_router = jnp.concatenate([w_router_grp, w_router_exp, jnp.zeros((depth, D, pad_r), F32)], axis=2)
    b_router = jnp.concatenate([b_router_grp, b_router_exp, jnp.zeros((depth, pad_r), F32)], axis=1)

    k_sh = v_sh = None
    for l in range(depth):
        kv_mem = norm_matmul(mem2, mem_norm_g[l], w_mem_kv, l, BF16, tm=B * M, tn=TN_PROJ).reshape(B, M, 2 * mem_w)
        if l < n_a:
            proj = norm_matmul(xr, norm_mix_g[l], w_in_a, l, BF16, tm=TM_PROJ, tn=TN_PROJ)
            y_mix = rglru(proj.reshape(B, S, -1), conv_w[l], conv_b[l], w_lru_r[l], b_lru_r[l],
                          w_lru_i[l], b_lru_i[l], lru_lambda[l], nb=LRU_BLOCKS_PER_STEP,
                          tc=TC_LRU).reshape(T, lru_w)
            q_col = 2 * lru_w // mem_w
        else:
            j = l - n_a
            proj = norm_matmul(xr, norm_mix_g[l], w_in_b, j, BF16, tm=TM_PROJ, tn=TN_PROJ)
            wq = _pad_head_cols(w_uq[j], heads, QK_DIM).astype(BF16)
            qg = jnp.pad(q_head_norm_g[j] * (QK_DIM ** -0.5 * math.log2(math.e)),
                         (0, HEAD_PAD - QK_DIM)).reshape(1, HEAD_PAD)
            q = latent_heads(proj, q_rank, q_rank, q_latent_norm_g[j], wq, qg, cos_t, sin_t, heads, tm=TM_HEADS)
            y_mix = flash_attention(q.reshape(B, S, -1), k_sh, v_sh, heads, tq=TQ_FLASH).reshape(T, heads * V_DIM)
            q_col = q_rank // mem_w
        y_mem = mem_attention(proj, q_col, kv_mem, mem_q_norm_g[l], mem_k_norm_g[l], B, ts=TS_MEM)
        xr = out_proj(y_mix, y_mem, w_o, l, xr, tm=TM_PROJ, tn=TN_PROJ)
        xr = hier_moe_residual(xr, norm_ffn_g[l], w_router[l], b_router[l].reshape(1, LANES), n_groups, epg,
                               wg_all, wu_all, wd_all, (l, depth))
        if l == n_a - 1:
            lat_cols = kv_rank + LANES
            w_dkv_pad = jnp.pad(w_dkv, ((0, 0), (0, lat_cols - w_dkv.shape[1])))[None]
            ckv = norm_matmul(xr, kv_in_norm_g, w_dkv_pad, 0, F32, tm=TM_PROJ, tn=lat_cols)
            eye = jnp.pad(jnp.eye(QK_ROPE, dtype=F32), ((0, LANES - QK_ROPE), (0, 0)))
            eye3 = jnp.broadcast_to(eye[:, None, :], (LANES, heads, QK_ROPE))
            rope_rows = jnp.pad(eye3, ((0, 0), (0, 0), (QK_NOPE, HEAD_PAD - QK_DIM))).reshape(LANES, heads * HEAD_PAD)
            wk = jnp.concatenate([_pad_head_cols(w_uk, heads, QK_NOPE), rope_rows], axis=0).astype(BF16)
            kg = jnp.pad(k_head_norm_g, (0, HEAD_PAD - QK_DIM)).reshape(1, HEAD_PAD)
            k_flat, v_flat = latent_heads(ckv, lat_cols, kv_rank, kv_latent_norm_g, wk, kg, cos_t, sin_t,
                                          heads, tm=TM_HEADS, w_v=w_uv.astype(BF16))
            k_sh = k_flat.reshape(B, S, -1)
            v_sh = v_flat.reshape(B, S, -1)
    return xr.reshape(B, S, D)
```

```python
import functools
import math

import jax
import jax.numpy as jnp
from jax import lax
from jax.experimental import pallas as pl
from jax.experimental.pallas import tpu as pltpu

F32 = jnp.float32
BF16 = jnp.bfloat16
EPS = 1e-6

MEM_HEADS = 4
CONV_WIDTH = 4
LRU_C = 8.0
QK_NOPE = 128
QK_ROPE = 64
QK_DIM = QK_NOPE + QK_ROPE
V_DIM = 128
ROPE_THETA = 10000.0
TOP_K_IN_GROUP = 2

LANES = 128
HEAD_PAD = 2 * LANES
MIB = 1024 * 1024

TM_PROJ = 512
TN_PROJ = 512
TS_MEM = 512
TQ_FLASH = 512
SUB_FLASH = 256
TM_HEADS = 512
TM_EXPERT = 256
WEIGHT_DMA_CHUNKS = 4
GATHER_BUFS = 3
TC_COMBINE = 256
LRU_BLOCKS_PER_STEP = 3
TC_LRU = 256
PLAN_CHUNK = 256


def _params(sem, vmem_mib):
    return pltpu.CompilerParams(dimension_semantics=sem, vmem_limit_bytes=vmem_mib * MIB)


def _rms_rows(x, g):
    x = x.astype(F32)
    ms = jnp.mean(x * x, axis=-1, keepdims=True)
    return x * lax.rsqrt(ms + EPS) * g


def _col_tile(n):
    return TN_PROJ if n % TN_PROJ == 0 else n


def _norm_mm_kernel(x_ref, g_ref, w_ref, o_ref, xn_ref, *, tm, chunk, tn):
    for r0 in range(0, tm, chunk):
        xn_ref[r0:r0 + chunk, :] = _rms_rows(x_ref[r0:r0 + chunk, :], g_ref[...]).astype(BF16)

    def cols(j, carry):
        c0 = pl.multiple_of(j * tn, tn)
        o_ref[:, pl.ds(c0, tn)] = jnp.dot(xn_ref[...], w_ref[:, pl.ds(c0, tn)],
                                          preferred_element_type=F32).astype(o_ref.dtype)
        return carry
    lax.fori_loop(0, o_ref.shape[1] // tn, cols, 0)


def norm_matmul(x, g, w, out_dtype, tm):
    T, K = x.shape
    N = w.shape[1]
    tm = min(tm, T)
    chunk = min(tm, 256)
    return pl.pallas_call(
        functools.partial(_norm_mm_kernel, tm=tm, chunk=chunk, tn=_col_tile(N)),
        grid=(T // tm,),
        in_specs=[pl.BlockSpec((tm, K), lambda i: (i, 0)),
                  pl.BlockSpec((1, K), lambda i: (0, 0)),
                  pl.BlockSpec((K, N), lambda i: (0, 0))],
        out_specs=pl.BlockSpec((tm, N), lambda i: (i, 0)),
        out_shape=jax.ShapeDtypeStruct((T, N), out_dtype),
        scratch_shapes=[pltpu.VMEM((tm, K), BF16)],
        compiler_params=_params(("parallel",), 56),
        name="norm_matmul",
    )(x, g.reshape(1, K), w)


def _out_proj_kernel(ya_ref, yb_ref, wa_ref, wb_ref, x_ref, o_ref, *, tn):
    def cols(j, carry):
        c0 = pl.multiple_of(j * tn, tn)
        acc = jnp.dot(ya_ref[...], wa_ref[:, pl.ds(c0, tn)], preferred_element_type=F32)
        acc += jnp.dot(yb_ref[...], wb_ref[:, pl.ds(c0, tn)], preferred_element_type=F32)
        o_ref[:, pl.ds(c0, tn)] = x_ref[:, pl.ds(c0, tn)] + acc
        return carry
    lax.fori_loop(0, o_ref.shape[1] // tn, cols, 0)


def out_proj(y_mix, y_mem, w_o, x, tm):
    T, Wa = y_mix.shape
    Wb = y_mem.shape[1]
    N = w_o.shape[1]
    assert Wa % Wb == 0
    return pl.pallas_call(
        functools.partial(_out_proj_kernel, tn=_col_tile(N)),
        grid=(T // tm,),
        in_specs=[pl.BlockSpec((tm, Wa), lambda i: (i, 0)),
                  pl.BlockSpec((tm, Wb), lambda i: (i, 0)),
                  pl.BlockSpec((Wa, N), lambda i: (0, 0)),
                  pl.BlockSpec((Wb, N), lambda i: (Wa // Wb, 0)),
                  pl.BlockSpec((tm, N), lambda i: (i, 0))],
        out_specs=pl.BlockSpec((tm, N), lambda i: (i, 0)),
        out_shape=jax.ShapeDtypeStruct((T, N), F32),
        compiler_params=_params(("parallel",), 48),
        name="out_proj",
    )(y_mix, y_mem, w_o, w_o, x)


def _mem_attn_kernel(q_ref, kv_ref, qg_ref, kg_ref, o_ref, *, heads, hd):
    width = heads * hd
    scale = hd ** -0.5
    for h in range(heads):
        qn = _rms_rows(q_ref[:, h * hd:(h + 1) * hd], qg_ref[...]) * scale
        kn = _rms_rows(kv_ref[0, :, h * hd:(h + 1) * hd], kg_ref[...])
        v = kv_ref[0, :, width + h * hd:width + (h + 1) * hd]
        s = lax.dot_general(qn.astype(BF16), kn.astype(BF16), (((1,), (1,)), ((), ())),
                            preferred_element_type=F32)
        m = jnp.max(s, axis=-1, keepdims=True)
        p = jnp.exp(s - m)
        l = jnp.sum(p, axis=-1, keepdims=True)
        o = jnp.dot(p.astype(BF16), v, preferred_element_type=F32) / l
        o_ref[:, h * hd:(h + 1) * hd] = o.astype(o_ref.dtype)


def mem_attention(proj, q_col_block, kv, q_g, k_g, batch, ts):
    T = proj.shape[0]
    _, M, two_w = kv.shape
    width = two_w // 2
    hd = width // MEM_HEADS
    per_b = T // batch // ts
    return pl.pallas_call(
        functools.partial(_mem_attn_kernel, heads=MEM_HEADS, hd=hd),
        grid=(batch, per_b),
        in_specs=[pl.BlockSpec((ts, width), lambda b, i: (b * per_b + i, q_col_block)),
                  pl.BlockSpec((1, M, two_w), lambda b, i: (b, 0, 0)),
                  pl.BlockSpec((1, hd), lambda b, i: (0, 0)),
                  pl.BlockSpec((1, hd), lambda b, i: (0, 0))],
        out_specs=pl.BlockSpec((ts, width), lambda b, i: (b * per_b + i, 0)),
        out_shape=jax.ShapeDtypeStruct((T, width), BF16),
        compiler_params=_params(("parallel", "arbitrary"), 32),
        name="mem_attention",
    )(proj, kv, q_g.reshape(1, hd), k_g.reshape(1, hd))


def _gelu_tanh(x):
    return 0.5 * x * (1.0 + jnp.tanh(0.7978845608028654 * (x + 0.044715 * x * x * x)))


def _rglru_kernel(u_ref, gb_ref, cw_ref, cb_ref, wr_ref, wi_ref, br_ref, bi_ref, lam_ref,
                  o_ref, a_s, b_s, *, seq, nb, blk, tc):
    wb = nb * blk
    neg_lam = -lam_ref[...]
    softplus = jnp.maximum(neg_lam, 0.0) + jnp.log1p(jnp.exp(-jnp.abs(neg_lam)))

    def gates(c, carry):
        t0 = pl.multiple_of(c * tc, tc)
        cur = u_ref[0, pl.ds(t0, tc), :].astype(F32)
        p0 = pl.multiple_of(jnp.maximum(t0 - 16, 0), 16)
        prev = u_ref[0, pl.ds(p0, 16), :].astype(F32)
        prev = jnp.where(c > 0, prev, 0.0)
        full = jnp.concatenate([prev, cur], axis=0)
        y = cb_ref[...] + cw_ref[0:1, :] * cur
        for k in range(1, CONV_WIDTH):
            y = y + cw_ref[k:k + 1, :] * full[16 - k:16 - k + tc, :]
        r_parts, i_parts = [], []
        for n in range(nb):
            yb = y[:, n * blk:(n + 1) * blk].astype(BF16)
            r_parts.append(jnp.dot(yb, wr_ref[n].astype(BF16), preferred_element_type=F32))
            i_parts.append(jnp.dot(yb, wi_ref[n].astype(BF16), preferred_element_type=F32))
        r = jax.nn.sigmoid(jnp.concatenate(r_parts, axis=1) + br_ref[...])
        ig = jax.nn.sigmoid(jnp.concatenate(i_parts, axis=1) + bi_ref[...])
        log_a = (-LRU_C) * r * softplus
        a = jnp.exp(log_a)
        a_s[pl.ds(t0, tc), :] = a
        b_s[pl.ds(t0, tc), :] = jnp.sqrt(-jnp.tanh(log_a) * (a * a + 1.0)) * (ig * y)
        return carry

    lax.fori_loop(0, seq // tc, gates, 0)

    row = lax.broadcasted_iota(jnp.int32, (8, wb), 0)

    def scan(c, h):
        t0 = pl.multiple_of(c * 8, 8)
        a = a_s[pl.ds(t0, 8), :]
        b = b_s[pl.ds(t0, 8), :]
        for s in (1, 2, 4):
            a_sh = pltpu.roll(a, s, axis=0)
            b_sh = pltpu.roll(b, s, axis=0)
            keep = row >= s
            b = jnp.where(keep, a * b_sh + b, b)
            a = jnp.where(keep, a * a_sh, a)
        hc = a * h + b
        b_s[pl.ds(t0, 8), :] = hc
        return hc[7:8, :]

    lax.fori_loop(0, seq // 8, scan, jnp.zeros((1, wb), F32))

    def gate_out(c, carry):
        t0 = pl.multiple_of(c * tc, tc)
        g = gb_ref[0, pl.ds(t0, tc), :].astype(F32)
        o_ref[0, pl.ds(t0, tc), :] = (_gelu_tanh(g) * b_s[pl.ds(t0, tc), :]).astype(o_ref.dtype)
        return carry

    lax.fori_loop(0, seq // tc, gate_out, 0)


def rglru(proj, conv_w, conv_b, w_r, b_r, w_i, b_i, lam, nb, tc):
    B, S, _ = proj.shape
    W = lam.shape[0]
    n_blocks, blk, _ = w_r.shape
    wb = nb * blk
    ncb = W // wb
    vec = lambda: pl.BlockSpec((1, wb), lambda b, j: (0, j))
    return pl.pallas_call(
        functools.partial(_rglru_kernel, seq=S, nb=nb, blk=blk, tc=tc),
        grid=(B, ncb),
        in_specs=[pl.BlockSpec((1, S, wb), lambda b, j: (b, 0, j)),
                  pl.BlockSpec((1, S, wb), lambda b, j: (b, 0, ncb + j)),
                  pl.BlockSpec((CONV_WIDTH, wb), lambda b, j: (0, j)),
                  vec(),
                  pl.BlockSpec((nb, blk, blk), lambda b, j: (j, 0, 0)),
                  pl.BlockSpec((nb, blk, blk), lambda b, j: (j, 0, 0)),
                  vec(), vec(), vec()],
        out_specs=pl.BlockSpec((1, S, wb), lambda b, j: (b, 0, j)),
        out_shape=jax.ShapeDtypeStruct((B, S, W), BF16),
        scratch_shapes=[pltpu.VMEM((S, wb), F32), pltpu.VMEM((S, wb), F32)],
        compiler_params=_params(("parallel", "arbitrary"), 32),
        name="rglru",
    )(proj, proj, conv_w, conv_b.reshape(1, W), w_r, w_i, b_r.reshape(1, W), b_i.reshape(1, W),
      lam.reshape(1, W))


def _heads_kernel(*refs, rank, heads, with_v):
    if with_v:
        lat_ref, gl_ref, w_ref, hg_ref, cos_ref, sin_ref, wv_ref, o_ref, v_ref = refs
    else:
        lat_ref, gl_ref, w_ref, hg_ref, cos_ref, sin_ref, o_ref = refs
    lat = lat_ref[...].astype(F32)
    cn = _rms_rows(lat[:, :rank], gl_ref[...])
    full = cn if lat.shape[1] == rank else jnp.concatenate([cn, lat[:, rank:]], axis=1)
    full = full.astype(BF16)
    lane = lax.broadcasted_iota(jnp.int32, cos_ref.shape, 1)
    half = QK_ROPE // 2
    for h in range(heads):
        t = jnp.dot(full, w_ref[:, h * HEAD_PAD:(h + 1) * HEAD_PAD], preferred_element_type=F32)
        ss = jnp.sum(t * t, axis=-1, keepdims=True) * (1.0 / QK_DIM)
        tn = t * lax.rsqrt(ss + EPS) * hg_ref[...]
        rp = tn[:, QK_NOPE:]
        swapped = jnp.where(lane < half, pltpu.roll(rp, LANES - half, axis=1),
                            pltpu.roll(rp, half, axis=1))
        rot = rp * cos_ref[...] + swapped * sin_ref[...]
        o_ref[:, h * HEAD_PAD:h * HEAD_PAD + QK_NOPE] = tn[:, :QK_NOPE].astype(o_ref.dtype)
        o_ref[:, h * HEAD_PAD + QK_NOPE:(h + 1) * HEAD_PAD] = rot.astype(o_ref.dtype)
    if with_v:
        v = jnp.dot(cn.astype(BF16), wv_ref[...], preferred_element_type=F32).astype(v_ref.dtype)
        ones = jnp.ones((v.shape[0], HEAD_PAD - V_DIM), v_ref.dtype)
        for h in range(heads):
            v_ref[:, h * HEAD_PAD:h * HEAD_PAD + V_DIM] = v[:, h * V_DIM:(h + 1) * V_DIM]
            v_ref[:, h * HEAD_PAD + V_DIM:(h + 1) * HEAD_PAD] = ones


def latent_heads(lat, lat_cols, rank, g_lat, w_pad, head_gain, cos_t, sin_t, heads, tm, w_v=None):
    T = lat.shape[0]
    n_out = heads * HEAD_PAD
    in_specs = [pl.BlockSpec((tm, lat_cols), lambda i: (i, 0)),
                pl.BlockSpec((1, rank), lambda i: (0, 0)),
                pl.BlockSpec((lat_cols, n_out), lambda i: (0, 0)),
                pl.BlockSpec((1, HEAD_PAD), lambda i: (0, 0)),
                pl.BlockSpec((tm, LANES), lambda i: (i, 0)),
                pl.BlockSpec((tm, LANES), lambda i: (i, 0))]
    args = [lat, g_lat.reshape(1, rank), w_pad, head_gain, cos_t, sin_t]
    out_shape = jax.ShapeDtypeStruct((T, n_out), BF16)
    out_specs = pl.BlockSpec((tm, n_out), lambda i: (i, 0))
    if w_v is not None:
        in_specs.append(pl.BlockSpec(w_v.shape, lambda i: (0, 0)))
        args.append(w_v)
        out_shape = (out_shape, jax.ShapeDtypeStruct((T, n_out), BF16))
        out_specs = (out_specs, pl.BlockSpec((tm, n_out), lambda i: (i, 0)))
    return pl.pallas_call(
        functools.partial(_heads_kernel, rank=rank, heads=heads, with_v=w_v is not None),
        grid=(T // tm,),
        in_specs=in_specs, out_specs=out_specs, out_shape=out_shape,
        compiler_params=_params(("parallel",), 48),
        name="latent_heads_kv" if w_v is not None else "latent_heads_q",
    )(*args)


def _flash_kernel(q_ref, k_ref, v_ref, o_ref, s_s, m_s, acc_s, *, tq, sub, n_kt):
    i = pl.program_id(2)
    m_s[...] = jnp.full(m_s.shape, -1e30, F32)
    acc_s[...] = jnp.zeros(acc_s.shape, F32)
    n_sub = tq // sub

    def scores(kt, slot):
        k = k_ref[0, pl.ds(kt * tq, tq), :]
        for h in range(n_sub):
            rows = slice(h * sub, (h + 1) * sub)
            s_s[slot, rows, :] = lax.dot_general(q_ref[0, rows, :], k, (((1,), (1,)), ((), ())),
                                                 preferred_element_type=F32)

    def softmax_pv(k0, slot, masked):
        for h in range(n_sub):
            rows = slice(h * sub, (h + 1) * sub)
            nk = (h + 1) * sub if masked else tq
            s = s_s[slot, rows, :nk]
            if masked:
                r = lax.broadcasted_iota(jnp.int32, s.shape, 0) + h * sub
                c = lax.broadcasted_iota(jnp.int32, s.shape, 1)
                s = jnp.where(c <= r, s, -1e30)
            v = v_ref[0, pl.ds(k0, nk), :]
            m_prev = m_s[rows, :]
            m_new = jnp.maximum(m_prev, jnp.max(s, axis=-1, keepdims=True))
            alpha = jnp.exp2(m_prev - m_new)
            p = jnp.exp2(s - jnp.concatenate([m_new] * (nk // LANES), axis=1))
            acc_s[rows, :] = (jnp.concatenate([alpha, alpha], axis=1) * acc_s[rows, :]
                              + jnp.dot(p.astype(BF16), v, preferred_element_type=F32))
            m_s[rows, :] = m_new

    scores(0, 0)
    for kt in range(n_kt - 1):
        @pl.when(kt < i)
        def _(kt=kt):
            scores(kt + 1, (kt + 1) % 2)
            softmax_pv(kt * tq, kt % 2, False)
    softmax_pv(pl.multiple_of(i * tq, tq), i % 2, True)
    o_ref[0] = (acc_s[:, :V_DIM] / acc_s[:, V_DIM:]).astype(o_ref.dtype)


def flash_attention(q, k, v, heads, tq):
    B, S, _ = q.shape
    return pl.pallas_call(
        functools.partial(_flash_kernel, tq=tq, sub=SUB_FLASH, n_kt=S // tq),
        grid=(B, heads, S // tq),
        in_specs=[pl.BlockSpec((1, tq, HEAD_PAD), lambda b, h, i: (b, i, h)),
                  pl.BlockSpec((1, S, HEAD_PAD), lambda b, h, i: (b, 0, h)),
                  pl.BlockSpec((1, S, HEAD_PAD), lambda b, h, i: (b, 0, h))],
        out_specs=pl.BlockSpec((1, tq, V_DIM), lambda b, h, i: (b, i, h)),
        out_shape=jax.ShapeDtypeStruct((B, S, heads * V_DIM), BF16),
        scratch_shapes=[pltpu.VMEM((2, tq, tq), F32), pltpu.VMEM((tq, LANES), F32),
                        pltpu.VMEM((tq, HEAD_PAD), F32)],
        compiler_params=_params(("parallel", "parallel", "arbitrary"), 32),
        name="flash_attention",
    )(q, k, v)


def _router_kernel(x_ref, g_ref, w_ref, b_ref, o_ref, *, tm, chunk):
    w = w_ref[...]
    w_hi = w.astype(BF16)
    w_lo = (w - w_hi.astype(F32)).astype(BF16)
    for r0 in range(0, tm, chunk):
        xn = _rms_rows(x_ref[r0:r0 + chunk, :], g_ref[...])
        x_hi = xn.astype(BF16)
        x_lo = (xn - x_hi.astype(F32)).astype(BF16)
        acc = jnp.dot(x_hi, w_hi, preferred_element_type=F32)
        acc += jnp.dot(x_lo, w_hi, preferred_element_type=F32)
        acc += jnp.dot(x_hi, w_lo, preferred_element_type=F32)
        o_ref[r0:r0 + chunk, :] = acc + b_ref[...]


def router_logits(x, g, w_pad, b_pad, tm):
    T, D = x.shape
    return pl.pallas_call(
        functools.partial(_router_kernel, tm=tm, chunk=256),
        grid=(T // tm,),
        in_specs=[pl.BlockSpec((tm, D), lambda i: (i, 0)),
                  pl.BlockSpec((1, D), lambda i: (0, 0)),
                  pl.BlockSpec((D, LANES), lambda i: (0, 0)),
                  pl.BlockSpec((1, LANES), lambda i: (0, 0))],
        out_specs=pl.BlockSpec((tm, LANES), lambda i: (i, 0)),
        out_shape=jax.ShapeDtypeStruct((T, LANES), F32),
        compiler_params=_params(("parallel",), 32),
        name="router_logits",
    )(x, g.reshape(1, D), w_pad, b_pad)


def _plan_kernel(lg_ref, out_ref, tiles_ref, c_s, info_s, *, n_tok, n_groups, epg, tm, chunk):
    n_exp = n_groups * epg
    lane = lax.broadcasted_iota(jnp.int32, (chunk, LANES), 1).astype(F32)
    rr = lax.broadcasted_iota(jnp.int32, (chunk, chunk), 0)
    cc = lax.broadcasted_iota(jnp.int32, (chunk, chunk), 1)
    tri = jnp.where(cc <= rr, 1.0, 0.0).astype(BF16)
    neg_inf = -jnp.inf

    def first_argmax(vals, vmax):
        return jnp.min(jnp.where(vals == vmax, lane, float(LANES)), axis=1, keepdims=True)

    def decide(c, carry):
        r0 = pl.multiple_of(c * chunk, chunk)
        lg = lg_ref[pl.ds(r0, chunk), :]
        gl = jnp.where(lane < n_groups, lg, neg_inf)
        gmax = jnp.max(gl, axis=1, keepdims=True)
        g_idx = first_argmax(gl, gmax)
        p_top = 1.0 / jnp.sum(jnp.exp(gl - gmax), axis=1, keepdims=True)
        lo = n_groups + g_idx * epg
        el = jnp.where(jnp.logical_and(lane >= lo, lane < lo + epg), lg, neg_inf)
        l1 = jnp.max(el, axis=1, keepdims=True)
        i1 = first_argmax(el, l1)
        el2 = jnp.where(lane == i1, neg_inf, el)
        l2 = jnp.max(el2, axis=1, keepdims=True)
        i2 = first_argmax(el2, l2)
        d = jnp.exp(l2 - l1)
        w1 = 1.0 / (1.0 + d)
        e1 = i1 - n_groups
        e2 = i2 - n_groups
        onehot = jnp.where(jnp.logical_or(lane == e1, lane == e2), 1.0, 0.0)
        cs = jnp.dot(tri, onehot.astype(BF16), preferred_element_type=F32) + carry
        c_s[pl.ds(r0, chunk), :] = cs
        info_s[pl.ds(r0, chunk), :] = jnp.where(
            lane == 0, e1, jnp.where(lane == 1, e2, jnp.where(
                lane == 2, p_top * w1, jnp.where(lane == 3, p_top * (d * w1), 0.0))))
        return cs[chunk - 1:chunk, :]

    counts = lax.fori_loop(0, n_tok // chunk, decide, jnp.zeros((1, LANES), F32))

    tiles = jnp.floor((counts + (tm - 1)) * (1.0 / tm))
    jj = lax.broadcasted_iota(jnp.int32, (LANES, LANES), 0)
    ee = lax.broadcasted_iota(jnp.int32, (LANES, LANES), 1)
    upper = jnp.where(jj <= ee, 1.0, 0.0).astype(BF16)
    tile_end = jnp.dot(jnp.broadcast_to(tiles, (8, LANES)).astype(BF16), upper,
                       preferred_element_type=F32)[0:1, :]
    tile_start = tile_end - tiles
    row_start = tile_start * tm

    def place(c, carry):
        r0 = pl.multiple_of(c * chunk, chunk)
        info = info_s[pl.ds(r0, chunk), :]
        base = row_start + c_s[pl.ds(r0, chunk), :] - 1.0
        pos1 = jnp.sum(jnp.where(lane == info[:, 0:1], base, 0.0), axis=1, keepdims=True)
        pos2 = jnp.sum(jnp.where(lane == info[:, 1:2], base, 0.0), axis=1, keepdims=True)
        out_ref[pl.ds(r0, chunk), :] = jnp.where(lane == 0, pos1, jnp.where(lane == 1, pos2, info))
        return carry

    lax.fori_loop(0, n_tok // chunk, place, 0)

    ti = jj.astype(F32)
    lane_sq = ee.astype(F32)
    is_exp = lane_sq < n_exp
    n_used = tile_end[:, n_exp - 1:n_exp]
    ti_c = jnp.minimum(ti, n_used - 1.0)
    tile_e = jnp.sum(jnp.where(jnp.logical_and(is_exp, tile_end <= ti_c), 1.0, 0.0), axis=1, keepdims=True)
    first = jnp.sum(jnp.where(jnp.logical_and(jnp.logical_and(is_exp, tile_start == ti), tiles > 0.0),
                              1.0, 0.0), axis=1, keepdims=True)
    nonempty = jnp.logical_and(is_exp, tiles > 0.0)
    next_e = jnp.min(jnp.where(jnp.logical_and(nonempty, lane_sq > tile_e), lane_sq, float(LANES)),
                     axis=1, keepdims=True)
    next_e = jnp.where(next_e >= LANES, tile_e, next_e)
    group = jnp.sum(jnp.where(jnp.logical_and(nonempty, lane_sq < tile_e), 1.0, 0.0), axis=1, keepdims=True)
    wslot = group - 2.0 * jnp.floor(group * 0.5)
    diag = jj == ee
    pad_lo = jnp.sum(jnp.where(diag, row_start + counts, 0.0), axis=1, keepdims=True)
    pad_hi = jnp.sum(jnp.where(diag, row_start + tiles * tm, 0.0), axis=1, keepdims=True)
    fields = (tile_e, first, n_used, next_e, wslot, pad_lo, pad_hi)
    packed = jnp.zeros((LANES, LANES), F32)
    for f, val in enumerate(fields):
        packed = jnp.where(lane_sq == f, val, packed)
    tiles_ref[...] = packed.astype(jnp.int32)


N_TILE_FIELDS = 5


def routing_plan(logits, n_groups, epg, tm):
    T = logits.shape[0]
    n_exp = n_groups * epg
    n_tiles = T * TOP_K_IN_GROUP // tm + n_exp
    assert n_tiles <= LANES
    out, tiles = pl.pallas_call(
        functools.partial(_plan_kernel, n_tok=T, n_groups=n_groups, epg=epg, tm=tm, chunk=PLAN_CHUNK),
        grid=(1,),
        in_specs=[pl.BlockSpec((T, LANES), lambda i: (0, 0))],
        out_specs=(pl.BlockSpec((T, LANES), lambda i: (0, 0)),
                   pl.BlockSpec((LANES, LANES), lambda i: (0, 0))),
        out_shape=(jax.ShapeDtypeStruct((T, LANES), F32), jax.ShapeDtypeStruct((LANES, LANES), jnp.int32)),
        scratch_shapes=[pltpu.VMEM((T, LANES), F32), pltpu.VMEM((T, LANES), F32)],
        compiler_params=_params(("arbitrary",), 40),
        name="routing_plan",
    )(logits)
    pos = out[:, :TOP_K_IN_GROUP].astype(jnp.int32).reshape(-1)
    tile_info = tiles[:n_tiles, :N_TILE_FIELDS].T.reshape(-1)
    pads = tiles[:n_exp, N_TILE_FIELDS:N_TILE_FIELDS + 2].T.reshape(-1)
    return out, pos, tile_info, pads, n_tiles


def _invert_kernel(pos_ref, pads_ref, info_ref, out_ref, *, n_tok, top_k, n_exp, tm, n_tiles):
    def zero(r, carry):
        out_ref[r] = 0
        return carry

    for e in range(n_exp):
        lax.fori_loop(pads_ref[e], pads_ref[n_exp + e], zero, 0)

    def zero8(g, carry):
        for u in range(8):
            out_ref[g * 8 + u] = 0
        return carry
    lax.fori_loop(info_ref[2 * n_tiles] * (tm // 8), n_tiles * (tm // 8), zero8, 0)

    def put(t, carry):
        for k in range(top_k):
            out_ref[pos_ref[t * top_k + k]] = t
        return carry
    lax.fori_loop(0, n_tok, put, 0, unroll=8)


def invert_positions(pos, pads, tile_info, n_tiles, tm):
    top_k = TOP_K_IN_GROUP
    return pl.pallas_call(
        functools.partial(_invert_kernel, n_tok=pos.shape[0] // top_k, top_k=top_k,
                          n_exp=pads.shape[0] // 2, tm=tm, n_tiles=n_tiles),
        grid_spec=pltpu.PrefetchScalarGridSpec(
            num_scalar_prefetch=3, grid=(1,), in_specs=[],
            out_specs=pl.BlockSpec(memory_space=pltpu.SMEM)),
        out_shape=jax.ShapeDtypeStruct((n_tiles * tm,), jnp.int32),
        compiler_params=_params(("arbitrary",), 16),
        name="invert_positions",
    )(pos, pads, tile_info)


def _row_copy(src_hbm, tok, dst_row, sem):
    return pltpu.make_async_copy(src_hbm.at[pl.ds(tok, 1)], dst_row, sem)


def _start_row_gather(src_hbm, idx_ref, base, n, dst_row, sem):
    def body(r, carry):
        _row_copy(src_hbm, idx_ref[base + r], dst_row(r), sem).start()
        return carry
    lax.fori_loop(0, n, body, 0, unroll=8)


def _wait_row_gather(src_hbm, n, dst_all, sem):
    pltpu.make_async_copy(src_hbm.at[pl.ds(0, n)], dst_all, sem).wait()


def _gmm_kernel(info_ref, row_tok_ref, x_hbm, g_ref, wg_hbm, wu_hbm, wd_hbm, o_ref,
                xbuf, sem, wg_f, wu_f, wd_f, wsem, wg_bf, wu_bf, wd_bf, *, tm, n_tiles, w_off, n_chunks):
    i = pl.program_id(0)
    expert = info_ref[i]
    first = info_ref[n_tiles + i]
    n_used = info_ref[2 * n_tiles]
    next_expert = info_ref[3 * n_tiles + i]
    wslot = info_ref[4 * n_tiles + i]
    slot = i % GATHER_BUFS
    ahead = GATHER_BUFS - 1

    def wait_slot(s):
        _wait_row_gather(x_hbm, tm, xbuf.at[s], sem.at[s])

    def weight_copies(e, s):
        copies = []
        for hbm, buf in ((wg_hbm, wg_f), (wu_hbm, wu_f), (wd_hbm, wd_f)):
            rows = hbm.shape[1] // n_chunks
            for c in range(n_chunks):
                copies.append(pltpu.make_async_copy(hbm.at[w_off + e, pl.ds(c * rows, rows)],
                                                    buf.at[s, pl.ds(c * rows, rows)], wsem.at[s]))
        return copies

    @pl.when(jnp.logical_and(i == 0, n_used > 0))
    def _():
        for cp in weight_copies(expert, 0):
            cp.start(priority=1)
        for t in range(ahead):
            _start_row_gather(x_hbm, row_tok_ref, min(t, n_tiles - 1) * tm, tm,
                              lambda r, t=t: xbuf.at[t, pl.ds(r, 1)], sem.at[t])

    @pl.when(i < n_used)
    def _():
        @pl.when(first == 1)
        def _():
            for cp in weight_copies(expert, wslot):
                cp.wait()

            @pl.when(next_expert != expert)
            def _():
                for cp in weight_copies(next_expert, 1 - wslot):
                    cp.start(priority=1)

            wg_bf[...] = wg_f[wslot].astype(BF16)
            wu_bf[...] = wu_f[wslot].astype(BF16)
            wd_bf[...] = wd_f[wslot].astype(BF16)

        wait_slot(slot)
        xn = _rms_rows(xbuf[slot], g_ref[...]).astype(BF16)
        base = jnp.minimum(i + ahead, n_tiles - 1) * tm
        nslot = (i + ahead) % GATHER_BUFS
        for r in range(tm):
            _row_copy(x_hbm, row_tok_ref[base + r], xbuf.at[nslot, pl.ds(r, 1)], sem.at[nslot]).start()
        hg = jnp.dot(xn, wg_bf[...], preferred_element_type=F32)
        hu = jnp.dot(xn, wu_bf[...], preferred_element_type=F32)
        hh = (hg * jax.nn.sigmoid(hg)) * hu
        o_ref[...] = jnp.dot(hh.astype(BF16), wd_bf[...], preferred_element_type=F32)

    @pl.when(i >= n_used)
    def _():
        @pl.when(jnp.logical_and(i < n_used + ahead, n_used > 0))
        def _():
            wait_slot(slot)
        o_ref[...] = jnp.zeros(o_ref.shape, o_ref.dtype)

    @pl.when(i == n_tiles - 1)
    def _():
        for d in range(1, ahead + 1):
            @pl.when(n_used >= n_tiles - ahead + d)
            def _(d=d):
                wait_slot((n_tiles - 1 + d) % GATHER_BUFS)


def grouped_expert_mlp(x, g, tile_info, row_tok, w_gate, w_up, w_down, layer, n_tiles, tm):
    T, D = x.shape
    F = w_gate.shape[2]
    layer_idx, depth = layer
    off = layer_idx * (w_gate.shape[0] // depth)
    any_spec = pl.BlockSpec(memory_space=pl.ANY)
    grid_spec = pltpu.PrefetchScalarGridSpec(
        num_scalar_prefetch=2,
        grid=(n_tiles,),
        in_specs=[any_spec, pl.BlockSpec((1, D), lambda i, ti, rt: (0, 0)), any_spec, any_spec, any_spec],
        out_specs=pl.BlockSpec((tm, D), lambda i, ti, rt: (i, 0)),
        scratch_shapes=[pltpu.VMEM((GATHER_BUFS, tm, D), F32), pltpu.SemaphoreType.DMA((GATHER_BUFS,)),
                        pltpu.VMEM((2, D, F), F32), pltpu.VMEM((2, D, F), F32), pltpu.VMEM((2, F, D), F32),
                        pltpu.SemaphoreType.DMA((2,)),
                        pltpu.VMEM((D, F), BF16), pltpu.VMEM((D, F), BF16), pltpu.VMEM((F, D), BF16)],
    )
    return pl.pallas_call(
        functools.partial(_gmm_kernel, tm=tm, n_tiles=n_tiles, w_off=off, n_chunks=WEIGHT_DMA_CHUNKS),
        grid_spec=grid_spec,
        out_shape=jax.ShapeDtypeStruct((n_tiles * tm, D), F32),
        compiler_params=_params(("arbitrary",), 56),
        name="grouped_expert_mlp",
    )(tile_info, row_tok, x, g.reshape(1, D), w_gate, w_up, w_down)


def _combine_kernel(pos_ref, x_ref, plan_ref, y_hbm, o_ref, ybuf, sem, *, tc, top_k):
    i = pl.program_id(0)
    n = pl.num_programs(0)
    slot = i % 2

    def start(tile, s):
        base = tile * (tc * top_k)
        for r in range(tc):
            for k in range(top_k):
                _row_copy(y_hbm, pos_ref[base + r * top_k + k], ybuf.at[s, k, pl.ds(r, 1)], sem.at[s]).start()

    @pl.when(i == 0)
    def _():
        start(0, 0)

    for k in range(top_k):
        _wait_row_gather(y_hbm, tc, ybuf.at[slot, k], sem.at[slot])

    @pl.when(i + 1 < n)
    def _():
        start(i + 1, 1 - slot)

    acc = x_ref[...]
    for k in range(top_k):
        acc = acc + plan_ref[:, top_k + k:top_k + k + 1] * ybuf[slot, k]
    o_ref[...] = acc


def moe_combine(x, y_sorted, plan_out, pos, tc):
    T, D = x.shape
    top_k = TOP_K_IN_GROUP
    grid_spec = pltpu.PrefetchScalarGridSpec(
        num_scalar_prefetch=1,
        grid=(T // tc,),
        in_specs=[pl.BlockSpec((tc, D), lambda i, p: (i, 0)),
                  pl.BlockSpec((tc, LANES), lambda i, p: (i, 0)),
                  pl.BlockSpec(memory_space=pl.ANY)],
        out_specs=pl.BlockSpec((tc, D), lambda i, p: (i, 0)),
        scratch_shapes=[pltpu.VMEM((2, top_k, tc, D), F32), pltpu.SemaphoreType.DMA((2,))],
    )
    return pl.pallas_call(
        functools.partial(_combine_kernel, tc=tc, top_k=top_k),
        grid_spec=grid_spec,
        out_shape=jax.ShapeDtypeStruct((T, D), F32),
        compiler_params=_params(("arbitrary",), 32),
        name="moe_combine",
    )(pos, x, plan_out, y_sorted)


def hier_moe_residual(x, g, w_r, b_r, n_groups, epg, w_gate, w_up, w_down, layer):
    logits = router_logits(x, g, w_r, b_r, tm=TM_PROJ)
    plan_out, pos, tile_info, pads, n_tiles = routing_plan(logits, n_groups, epg, TM_EXPERT)
    row_tok = invert_positions(pos, pads, tile_info, n_tiles, TM_EXPERT)
    y_sorted = grouped_expert_mlp(x, g, tile_info, row_tok, w_gate, w_up, w_down, layer, n_tiles, TM_EXPERT)
    return moe_combine(x, y_sorted, plan_out, pos, TC_COMBINE)


def _pad_head_cols(w, heads, real):
    r = w.shape[0]
    return jnp.pad(w.reshape(r, heads, real), ((0, 0), (0, 0), (0, HEAD_PAD - real))).reshape(r, heads * HEAD_PAD)


def _rope_tables(positions):
    half = QK_ROPE // 2
    inv_freq = ROPE_THETA ** (-jnp.arange(half, dtype=F32) / half)
    ang = positions.astype(F32).reshape(-1, 1) * inv_freq[None, :]
    cos, sin = jnp.cos(ang), jnp.sin(ang)
    zeros = jnp.zeros((ang.shape[0], LANES - QK_ROPE), F32)
    return (jnp.concatenate([cos, cos, zeros], axis=1),
            jnp.concatenate([-sin, sin, zeros], axis=1))


def kernel(x, mem, positions, norm_mix_g, norm_ffn_g, w_o, mem_norm_g, w_mem_kv, mem_q_norm_g, mem_k_norm_g, w_in_a, conv_w, conv_b, w_lru_r, b_lru_r, w_lru_i, b_lru_i, lru_lambda, kv_in_norm_g, w_dkv, kv_latent_norm_g, w_uk, w_uv, k_head_norm_g, w_in_b, q_latent_norm_g, w_uq, q_head_norm_g, w_router_grp, b_router_grp, w_router_exp, b_router_exp, w_exp_gate, w_exp_up, w_exp_down):
    B, S, D = x.shape
    T = B * S
    M = mem.shape[1]
    depth = norm_mix_g.shape[0]
    n_a = w_in_a.shape[0]
    lru_w = lru_lambda.shape[1]
    mem_w = w_mem_kv.shape[2] // 2
    heads = w_uv.shape[1] // V_DIM
    kv_rank = kv_latent_norm_g.shape[0]
    q_rank = q_latent_norm_g.shape[1]
    n_groups, epg = w_exp_gate.shape[1], w_exp_gate.shape[2]
    n_exp = n_groups * epg
    d_exp = w_exp_gate.shape[-1]

    xr = x.reshape(T, D)
    mem2 = mem.reshape(B * M, D)
    cos_t, sin_t = _rope_tables(positions)
    wg_all = w_exp_gate.reshape(depth * n_exp, D, d_exp)
    wu_all = w_exp_up.reshape(depth * n_exp, D, d_exp)
    wd_all = w_exp_down.reshape(depth * n_exp, d_exp, D)
    pad_r = LANES - n_groups - n_exp
    w_router = jnp.concatenate([w_router_grp, w_router_exp, jnp.zeros((depth, D, pad_r), F32)], axis=2)
    b_router = jnp.concatenate([b_router_grp, b_router_exp, jnp.zeros((depth, pad_r), F32)], axis=1)

    k_sh = v_sh = None
    for l in range(depth):
        kv_mem = norm_matmul(mem2, mem_norm_g[l], w_mem_kv[l].astype(BF16), BF16, tm=TM_PROJ).reshape(B, M, 2 * mem_w)
        if l < n_a:
            proj = norm_matmul(xr, norm_mix_g[l], w_in_a[l].astype(BF16), BF16, tm=TM_PROJ)
            y_mix = rglru(proj.reshape(B, S, -1), conv_w[l], conv_b[l], w_lru_r[l], b_lru_r[l],
                          w_lru_i[l], b_lru_i[l], lru_lambda[l], nb=LRU_BLOCKS_PER_STEP,
                          tc=TC_LRU).reshape(T, lru_w)
            q_col = 2 * lru_w // mem_w
        else:
            j = l - n_a
            proj = norm_matmul(xr, norm_mix_g[l], w_in_b[j].astype(BF16), BF16, tm=TM_PROJ)
            wq = _pad_head_cols(w_uq[j], heads, QK_DIM).astype(BF16)
            qg = jnp.pad(q_head_norm_g[j] * (QK_DIM ** -0.5 * math.log2(math.e)),
                         (0, HEAD_PAD - QK_DIM)).reshape(1, HEAD_PAD)
            q = latent_heads(proj, q_rank, q_rank, q_latent_norm_g[j], wq, qg, cos_t, sin_t, heads, tm=TM_HEADS)
            y_mix = flash_attention(q.reshape(B, S, -1), k_sh, v_sh, heads, tq=TQ_FLASH).reshape(T, heads * V_DIM)
            q_col = q_rank // mem_w
        y_mem = mem_attention(proj, q_col, kv_mem, mem_q_norm_g[l], mem_k_norm_g[l], B, ts=TS_MEM)
        xr = out_proj(y_mix, y_mem, w_o[l].astype(BF16), xr, tm=TM_PROJ)
        xr = hier_moe_residual(xr, norm_ffn_g[l], w_router[l], b_router[l].reshape(1, LANES), n_groups, epg,
                               wg_all, wu_all, wd_all, (l, depth))
        if l == n_a - 1:
            lat_cols = kv_rank + LANES
            w_dkv_pad = jnp.pad(w_dkv, ((0, 0), (0, lat_cols - w_dkv.shape[1]))).astype(BF16)
            ckv = norm_matmul(xr, kv_in_norm_g, w_dkv_pad, F32, tm=TM_PROJ)
            eye = jnp.pad(jnp.eye(QK_ROPE, dtype=F32), ((0, LANES - QK_ROPE), (0, 0)))
            eye3 = jnp.broadcast_to(eye[:, None, :], (LANES, heads, QK_ROPE))
            rope_rows = jnp.pad(eye3, ((0, 0), (0, 0), (QK_NOPE, HEAD_PAD - QK_DIM))).reshape(LANES, heads * HEAD_PAD)
            wk = jnp.concatenate([_pad_head_cols(w_uk, heads, QK_NOPE), rope_rows], axis=0).astype(BF16)
            kg = jnp.pad(k_head_norm_g, (0, HEAD_PAD - QK_DIM)).reshape(1, HEAD_PAD)
            k_flat, v_flat = latent_heads(ckv, lat_cols, kv_rank, kv_latent_norm_g, wk, kg, cos_t, sin_t,
                                          heads, tm=TM_HEADS, w_v=w_uv.astype(BF16))
            k_sh = k_flat.reshape(B, S, -1)
            v_sh = v_flat.reshape(B, S, -1)
    return xr.reshape(B, S, D)
```

```python
import functools
import math

import jax
import jax.numpy as jnp
from jax import lax
from jax.experimental import pallas as pl
from jax.experimental.pallas import tpu as pltpu

F32 = jnp.float32
BF16 = jnp.bfloat16
EPS = 1e-6

MEM_HEADS = 4
CONV_WIDTH = 4
LRU_C = 8.0
QK_NOPE = 128
QK_ROPE = 64
QK_DIM = QK_NOPE + QK_ROPE
V_DIM = 128
ROPE_THETA = 10000.0
TOP_K_IN_GROUP = 2

LANES = 128
HEAD_PAD = 2 * LANES
MIB = 1024 * 1024

TM_PROJ = 512
TN_PROJ = 512
TS_MEM = 512
TQ_FLASH = 512
SUB_FLASH = 256
TM_HEADS = 512
TM_EXPERT = 256
WEIGHT_DMA_CHUNKS = 4
GATHER_BUFS = 3
TC_COMBINE = 256
LRU_BLOCKS_PER_STEP = 3
TC_LRU = 256
PLAN_CHUNK = 256


def _params(sem, vmem_mib):
    return pltpu.CompilerParams(dimension_semantics=sem, vmem_limit_bytes=vmem_mib * MIB)


def _rms_rows(x, g):
    x = x.astype(F32)
    ms = jnp.mean(x * x, axis=-1, keepdims=True)
    return x * lax.rsqrt(ms + EPS) * g


def _col_tile(n):
    return TN_PROJ if n % TN_PROJ == 0 else n


def _stage_weight_bf16(w_hbm, layer, w_bf, stage, sem, tn):
    n = w_bf.shape[1] // tn

    def copy(j):
        return pltpu.make_async_copy(w_hbm.at[layer, :, pl.ds(j * tn, tn)], stage.at[j % 2], sem.at[j % 2])

    copy(0).start()
    for j in range(n):
        if j + 1 < n:
            copy(j + 1).start()
        copy(j).wait()
        w_bf[:, j * tn:(j + 1) * tn] = stage[j % 2].astype(BF16)


def _norm_mm_kernel(x_ref, g_ref, w_hbm, o_ref, xn_ref, w_bf, stage, sem, *, layer, tm, chunk, tn):
    @pl.when(pl.program_id(0) == 0)
    def _():
        _stage_weight_bf16(w_hbm, layer, w_bf, stage, sem, tn)

    for r0 in range(0, tm, chunk):
        xn_ref[r0:r0 + chunk, :] = _rms_rows(x_ref[r0:r0 + chunk, :], g_ref[...]).astype(BF16)

    def cols(j, carry):
        c0 = pl.multiple_of(j * tn, tn)
        o_ref[:, pl.ds(c0, tn)] = jnp.dot(xn_ref[...], w_bf[:, pl.ds(c0, tn)],
                                          preferred_element_type=F32).astype(o_ref.dtype)
        return carry
    lax.fori_loop(0, o_ref.shape[1] // tn, cols, 0)


def norm_matmul(x, g, w, layer, out_dtype, tm):
    T, K = x.shape
    N = w.shape[2]
    tm = min(tm, T)
    chunk = min(tm, 256)
    tn = _col_tile(N)
    return pl.pallas_call(
        functools.partial(_norm_mm_kernel, layer=layer, tm=tm, chunk=chunk, tn=tn),
        grid=(T // tm,),
        in_specs=[pl.BlockSpec((tm, K), lambda i: (i, 0)),
                  pl.BlockSpec((1, K), lambda i: (0, 0)),
                  pl.BlockSpec(memory_space=pl.ANY)],
        out_specs=pl.BlockSpec((tm, N), lambda i: (i, 0)),
        out_shape=jax.ShapeDtypeStruct((T, N), out_dtype),
        scratch_shapes=[pltpu.VMEM((tm, K), BF16), pltpu.VMEM((K, N), BF16), pltpu.VMEM((2, K, tn), F32),
                        pltpu.SemaphoreType.DMA((2,))],
        compiler_params=_params(("arbitrary",), 56),
        name="norm_matmul",
    )(x, g.reshape(1, K), w)


def _out_proj_kernel(ya_ref, yb_ref, w_hbm, x_ref, o_ref, w_bf, stage, sem, *, layer, tn):
    @pl.when(pl.program_id(0) == 0)
    def _():
        _stage_weight_bf16(w_hbm, layer, w_bf, stage, sem, tn)

    wa_rows = ya_ref.shape[1]

    def cols(j, carry):
        c0 = pl.multiple_of(j * tn, tn)
        acc = jnp.dot(ya_ref[...], w_bf[:wa_rows, pl.ds(c0, tn)], preferred_element_type=F32)
        acc += jnp.dot(yb_ref[...], w_bf[wa_rows:, pl.ds(c0, tn)], preferred_element_type=F32)
        o_ref[:, pl.ds(c0, tn)] = x_ref[:, pl.ds(c0, tn)] + acc
        return carry
    lax.fori_loop(0, o_ref.shape[1] // tn, cols, 0)


def out_proj(y_mix, y_mem, w_o, layer, x, tm):
    T, Wa = y_mix.shape
    Wb = y_mem.shape[1]
    _, K, N = w_o.shape
    assert Wa + Wb == K
    tn = _col_tile(N)
    return pl.pallas_call(
        functools.partial(_out_proj_kernel, layer=layer, tn=tn),
        grid=(T // tm,),
        in_specs=[pl.BlockSpec((tm, Wa), lambda i: (i, 0)),
                  pl.BlockSpec((tm, Wb), lambda i: (i, 0)),
                  pl.BlockSpec(memory_space=pl.ANY),
                  pl.BlockSpec((tm, N), lambda i: (i, 0))],
        out_specs=pl.BlockSpec((tm, N), lambda i: (i, 0)),
        out_shape=jax.ShapeDtypeStruct((T, N), F32),
        scratch_shapes=[pltpu.VMEM((K, N), BF16), pltpu.VMEM((2, K, tn), F32), pltpu.SemaphoreType.DMA((2,))],
        compiler_params=_params(("arbitrary",), 48),
        name="out_proj",
    )(y_mix, y_mem, w_o, x)


def _mem_attn_kernel(q_ref, kv_ref, qg_ref, kg_ref, o_ref, *, heads, hd):
    width = heads * hd
    scale = hd ** -0.5
    for h in range(heads):
        qn = _rms_rows(q_ref[:, h * hd:(h + 1) * hd], qg_ref[...]) * scale
        kn = _rms_rows(kv_ref[0, :, h * hd:(h + 1) * hd], kg_ref[...])
        v = kv_ref[0, :, width + h * hd:width + (h + 1) * hd]
        s = lax.dot_general(qn.astype(BF16), kn.astype(BF16), (((1,), (1,)), ((), ())),
                            preferred_element_type=F32)
        m = jnp.max(s, axis=-1, keepdims=True)
        p = jnp.exp(s - m)
        l = jnp.sum(p, axis=-1, keepdims=True)
        o = jnp.dot(p.astype(BF16), v, preferred_element_type=F32) / l
        o_ref[:, h * hd:(h + 1) * hd] = o.astype(o_ref.dtype)


def mem_attention(proj, q_col_block, kv, q_g, k_g, batch, ts):
    T = proj.shape[0]
    _, M, two_w = kv.shape
    width = two_w // 2
    hd = width // MEM_HEADS
    per_b = T // batch // ts
    return pl.pallas_call(
        functools.partial(_mem_attn_kernel, heads=MEM_HEADS, hd=hd),
        grid=(batch, per_b),
        in_specs=[pl.BlockSpec((ts, width), lambda b, i: (b * per_b + i, q_col_block)),
                  pl.BlockSpec((1, M, two_w), lambda b, i: (b, 0, 0)),
                  pl.BlockSpec((1, hd), lambda b, i: (0, 0)),
                  pl.BlockSpec((1, hd), lambda b, i: (0, 0))],
        out_specs=pl.BlockSpec((ts, width), lambda b, i: (b * per_b + i, 0)),
        out_shape=jax.ShapeDtypeStruct((T, width), BF16),
        compiler_params=_params(("parallel", "arbitrary"), 32),
        name="mem_attention",
    )(proj, kv, q_g.reshape(1, hd), k_g.reshape(1, hd))


def _gelu_tanh(x):
    return 0.5 * x * (1.0 + jnp.tanh(0.7978845608028654 * (x + 0.044715 * x * x * x)))


def _rglru_kernel(u_ref, gb_ref, cw_ref, cb_ref, wr_ref, wi_ref, br_ref, bi_ref, lam_ref,
                  o_ref, a_s, b_s, *, seq, nb, blk, tc):
    wb = nb * blk
    neg_lam = -lam_ref[...]
    softplus = jnp.maximum(neg_lam, 0.0) + jnp.log1p(jnp.exp(-jnp.abs(neg_lam)))

    def gates(c, carry):
        t0 = pl.multiple_of(c * tc, tc)
        cur = u_ref[0, pl.ds(t0, tc), :].astype(F32)
        p0 = pl.multiple_of(jnp.maximum(t0 - 16, 0), 16)
        prev = u_ref[0, pl.ds(p0, 16), :].astype(F32)
        prev = jnp.where(c > 0, prev, 0.0)
        full = jnp.concatenate([prev, cur], axis=0)
        y = cb_ref[...] + cw_ref[0:1, :] * cur
        for k in range(1, CONV_WIDTH):
            y = y + cw_ref[k:k + 1, :] * full[16 - k:16 - k + tc, :]
        r_parts, i_parts = [], []
        for n in range(nb):
            yb = y[:, n * blk:(n + 1) * blk].astype(BF16)
            r_parts.append(jnp.dot(yb, wr_ref[n].astype(BF16), preferred_element_type=F32))
            i_parts.append(jnp.dot(yb, wi_ref[n].astype(BF16), preferred_element_type=F32))
        r = jax.nn.sigmoid(jnp.concatenate(r_parts, axis=1) + br_ref[...])
        ig = jax.nn.sigmoid(jnp.concatenate(i_parts, axis=1) + bi_ref[...])
        log_a = (-LRU_C) * r * softplus
        a = jnp.exp(log_a)
        a_s[pl.ds(t0, tc), :] = a
        b_s[pl.ds(t0, tc), :] = jnp.sqrt(-jnp.tanh(log_a) * (a * a + 1.0)) * (ig * y)
        return carry

    lax.fori_loop(0, seq // tc, gates, 0)

    row = lax.broadcasted_iota(jnp.int32, (8, wb), 0)

    def scan(c, h):
        t0 = pl.multiple_of(c * 8, 8)
        a = a_s[pl.ds(t0, 8), :]
        b = b_s[pl.ds(t0, 8), :]
        for s in (1, 2, 4):
            a_sh = pltpu.roll(a, s, axis=0)
            b_sh = pltpu.roll(b, s, axis=0)
            keep = row >= s
            b = jnp.where(keep, a * b_sh + b, b)
            a = jnp.where(keep, a * a_sh, a)
        hc = a * h + b
        b_s[pl.ds(t0, 8), :] = hc
        return hc[7:8, :]

    lax.fori_loop(0, seq // 8, scan, jnp.zeros((1, wb), F32))

    def gate_out(c, carry):
        t0 = pl.multiple_of(c * tc, tc)
        g = gb_ref[0, pl.ds(t0, tc), :].astype(F32)
        o_ref[0, pl.ds(t0, tc), :] = (_gelu_tanh(g) * b_s[pl.ds(t0, tc), :]).astype(o_ref.dtype)
        return carry

    lax.fori_loop(0, seq // tc, gate_out, 0)


def rglru(proj, conv_w, conv_b, w_r, b_r, w_i, b_i, lam, nb, tc):
    B, S, _ = proj.shape
    W = lam.shape[0]
    n_blocks, blk, _ = w_r.shape
    wb = nb * blk
    ncb = W // wb
    vec = lambda: pl.BlockSpec((1, wb), lambda b, j: (0, j))
    return pl.pallas_call(
        functools.partial(_rglru_kernel, seq=S, nb=nb, blk=blk, tc=tc),
        grid=(B, ncb),
        in_specs=[pl.BlockSpec((1, S, wb), lambda b, j: (b, 0, j)),
                  pl.BlockSpec((1, S, wb), lambda b, j: (b, 0, ncb + j)),
                  pl.BlockSpec((CONV_WIDTH, wb), lambda b, j: (0, j)),
                  vec(),
                  pl.BlockSpec((nb, blk, blk), lambda b, j: (j, 0, 0)),
                  pl.BlockSpec((nb, blk, blk), lambda b, j: (j, 0, 0)),
                  vec(), vec(), vec()],
        out_specs=pl.BlockSpec((1, S, wb), lambda b, j: (b, 0, j)),
        out_shape=jax.ShapeDtypeStruct((B, S, W), BF16),
        scratch_shapes=[pltpu.VMEM((S, wb), F32), pltpu.VMEM((S, wb), F32)],
        compiler_params=_params(("parallel", "arbitrary"), 32),
        name="rglru",
    )(proj, proj, conv_w, conv_b.reshape(1, W), w_r, w_i, b_r.reshape(1, W), b_i.reshape(1, W),
      lam.reshape(1, W))


def _heads_kernel(*refs, rank, heads, with_v):
    if with_v:
        lat_ref, gl_ref, w_ref, hg_ref, cos_ref, sin_ref, wv_ref, o_ref, v_ref = refs
    else:
        lat_ref, gl_ref, w_ref, hg_ref, cos_ref, sin_ref, o_ref = refs
    lat = lat_ref[...].astype(F32)
    cn = _rms_rows(lat[:, :rank], gl_ref[...])
    full = cn if lat.shape[1] == rank else jnp.concatenate([cn, lat[:, rank:]], axis=1)
    full = full.astype(BF16)
    lane = lax.broadcasted_iota(jnp.int32, cos_ref.shape, 1)
    half = QK_ROPE // 2
    for h in range(heads):
        t = jnp.dot(full, w_ref[:, h * HEAD_PAD:(h + 1) * HEAD_PAD], preferred_element_type=F32)
        ss = jnp.sum(t * t, axis=-1, keepdims=True) * (1.0 / QK_DIM)
        tn = t * lax.rsqrt(ss + EPS) * hg_ref[...]
        rp = tn[:, QK_NOPE:]
        swapped = jnp.where(lane < half, pltpu.roll(rp, LANES - half, axis=1),
                            pltpu.roll(rp, half, axis=1))
        rot = rp * cos_ref[...] + swapped * sin_ref[...]
        o_ref[:, h * HEAD_PAD:h * HEAD_PAD + QK_NOPE] = tn[:, :QK_NOPE].astype(o_ref.dtype)
        o_ref[:, h * HEAD_PAD + QK_NOPE:(h + 1) * HEAD_PAD] = rot.astype(o_ref.dtype)
    if with_v:
        v = jnp.dot(cn.astype(BF16), wv_ref[...], preferred_element_type=F32).astype(v_ref.dtype)
        ones = jnp.ones((v.shape[0], HEAD_PAD - V_DIM), v_ref.dtype)
        for h in range(heads):
            v_ref[:, h * HEAD_PAD:h * HEAD_PAD + V_DIM] = v[:, h * V_DIM:(h + 1) * V_DIM]
            v_ref[:, h * HEAD_PAD + V_DIM:(h + 1) * HEAD_PAD] = ones


def latent_heads(lat, lat_cols, rank, g_lat, w_pad, head_gain, cos_t, sin_t, heads, tm, w_v=None):
    T = lat.shape[0]
    n_out = heads * HEAD_PAD
    in_specs = [pl.BlockSpec((tm, lat_cols), lambda i: (i, 0)),
                pl.BlockSpec((1, rank), lambda i: (0, 0)),
                pl.BlockSpec((lat_cols, n_out), lambda i: (0, 0)),
                pl.BlockSpec((1, HEAD_PAD), lambda i: (0, 0)),
                pl.BlockSpec((tm, LANES), lambda i: (i, 0)),
                pl.BlockSpec((tm, LANES), lambda i: (i, 0))]
    args = [lat, g_lat.reshape(1, rank), w_pad, head_gain, cos_t, sin_t]
    out_shape = jax.ShapeDtypeStruct((T, n_out), BF16)
    out_specs = pl.BlockSpec((tm, n_out), lambda i: (i, 0))
    if w_v is not None:
        in_specs.append(pl.BlockSpec(w_v.shape, lambda i: (0, 0)))
        args.append(w_v)
        out_shape = (out_shape, jax.ShapeDtypeStruct((T, n_out), BF16))
        out_specs = (out_specs, pl.BlockSpec((tm, n_out), lambda i: (i, 0)))
    return pl.pallas_call(
        functools.partial(_heads_kernel, rank=rank, heads=heads, with_v=w_v is not None),
        grid=(T // tm,),
        in_specs=in_specs, out_specs=out_specs, out_shape=out_shape,
        compiler_params=_params(("parallel",), 48),
        name="latent_heads_kv" if w_v is not None else "latent_heads_q",
    )(*args)


def _flash_kernel(q_ref, k_ref, v_ref, o_ref, s_s, m_s, acc_s, *, tq, sub, n_kt):
    i = pl.program_id(2)
    m_s[...] = jnp.full(m_s.shape, -1e30, F32)
    acc_s[...] = jnp.zeros(acc_s.shape, F32)
    n_sub = tq // sub

    def scores(kt, slot):
        k = k_ref[0, pl.ds(kt * tq, tq), :]
        for h in range(n_sub):
            rows = slice(h * sub, (h + 1) * sub)
            s_s[slot, rows, :] = lax.dot_general(q_ref[0, rows, :], k, (((1,), (1,)), ((), ())),
                                                 preferred_element_type=F32)

    def softmax_pv(k0, slot, masked):
        for h in range(n_sub):
            rows = slice(h * sub, (h + 1) * sub)
            nk = (h + 1) * sub if masked else tq
            s = s_s[slot, rows, :nk]
            if masked:
                r = lax.broadcasted_iota(jnp.int32, s.shape, 0) + h * sub
                c = lax.broadcasted_iota(jnp.int32, s.shape, 1)
                s = jnp.where(c <= r, s, -1e30)
            v = v_ref[0, pl.ds(k0, nk), :]
            m_prev = m_s[rows, :]
            m_new = jnp.maximum(m_prev, jnp.max(s, axis=-1, keepdims=True))
            alpha = jnp.exp2(m_prev - m_new)
            p = jnp.exp2(s - jnp.concatenate([m_new] * (nk // LANES), axis=1))
            acc_s[rows, :] = (jnp.concatenate([alpha, alpha], axis=1) * acc_s[rows, :]
                              + jnp.dot(p.astype(BF16), v, preferred_element_type=F32))
            m_s[rows, :] = m_new

    scores(0, 0)
    for kt in range(n_kt - 1):
        @pl.when(kt < i)
        def _(kt=kt):
            scores(kt + 1, (kt + 1) % 2)
            softmax_pv(kt * tq, kt % 2, False)
    softmax_pv(pl.multiple_of(i * tq, tq), i % 2, True)
    o_ref[0] = (acc_s[:, :V_DIM] / acc_s[:, V_DIM:]).astype(o_ref.dtype)


def flash_attention(q, k, v, heads, tq):
    B, S, _ = q.shape
    return pl.pallas_call(
        functools.partial(_flash_kernel, tq=tq, sub=SUB_FLASH, n_kt=S // tq),
        grid=(B, heads, S // tq),
        in_specs=[pl.BlockSpec((1, tq, HEAD_PAD), lambda b, h, i: (b, i, h)),
                  pl.BlockSpec((1, S, HEAD_PAD), lambda b, h, i: (b, 0, h)),
                  pl.BlockSpec((1, S, HEAD_PAD), lambda b, h, i: (b, 0, h))],
        out_specs=pl.BlockSpec((1, tq, V_DIM), lambda b, h, i: (b, i, h)),
        out_shape=jax.ShapeDtypeStruct((B, S, heads * V_DIM), BF16),
        scratch_shapes=[pltpu.VMEM((2, tq, tq), F32), pltpu.VMEM((tq, LANES), F32),
                        pltpu.VMEM((tq, HEAD_PAD), F32)],
        compiler_params=_params(("parallel", "parallel", "arbitrary"), 32),
        name="flash_attention",
    )(q, k, v)


def _router_kernel(x_ref, g_ref, w_ref, b_ref, o_ref, xt_ref, *, tm, chunk):
    w = w_ref[...]
    w_hi = w.astype(BF16)
    w_lo = (w - w_hi.astype(F32)).astype(BF16)
    lane_tiles = x_ref.shape[1] // LANES
    for r0 in range(0, tm, chunk):
        xn = _rms_rows(x_ref[r0:r0 + chunk, :], g_ref[...])
        x_hi = xn.astype(BF16)
        x_lo = (xn - x_hi.astype(F32)).astype(BF16)
        acc = jnp.dot(x_hi, w_hi, preferred_element_type=F32)
        acc += jnp.dot(x_lo, w_hi, preferred_element_type=F32)
        acc += jnp.dot(x_hi, w_lo, preferred_element_type=F32)
        o_ref[r0:r0 + chunk, :] = acc + b_ref[...]
        for j in range(lane_tiles):
            xt_ref[pl.ds(r0 * lane_tiles + j, chunk, stride=lane_tiles), :] = xn[:, j * LANES:(j + 1) * LANES]


def router_logits(x, g, w_pad, b_pad, tm):
    T, D = x.shape
    lane_tiles = D // LANES
    return pl.pallas_call(
        functools.partial(_router_kernel, tm=tm, chunk=256),
        grid=(T // tm,),
        in_specs=[pl.BlockSpec((tm, D), lambda i: (i, 0)),
                  pl.BlockSpec((1, D), lambda i: (0, 0)),
                  pl.BlockSpec((D, LANES), lambda i: (0, 0)),
                  pl.BlockSpec((1, LANES), lambda i: (0, 0))],
        out_specs=(pl.BlockSpec((tm, LANES), lambda i: (i, 0)),
                   pl.BlockSpec((tm * lane_tiles, LANES), lambda i: (i, 0))),
        out_shape=(jax.ShapeDtypeStruct((T, LANES), F32),
                   jax.ShapeDtypeStruct((T * lane_tiles, LANES), F32)),
        compiler_params=_params(("parallel",), 40),
        name="router_logits",
    )(x, g.reshape(1, D), w_pad, b_pad)


def _plan_kernel(lg_ref, out_ref, tiles_ref, c_s, info_s, *, n_tok, n_groups, epg, tm, chunk):
    n_exp = n_groups * epg
    lane = lax.broadcasted_iota(jnp.int32, (chunk, LANES), 1).astype(F32)
    rr = lax.broadcasted_iota(jnp.int32, (chunk, chunk), 0)
    cc = lax.broadcasted_iota(jnp.int32, (chunk, chunk), 1)
    tri = jnp.where(cc <= rr, 1.0, 0.0).astype(BF16)
    neg_inf = -jnp.inf

    def first_argmax(vals, vmax):
        return jnp.min(jnp.where(vals == vmax, lane, float(LANES)), axis=1, keepdims=True)

    def decide(c, carry):
        r0 = pl.multiple_of(c * chunk, chunk)
        lg = lg_ref[pl.ds(r0, chunk), :]
        gl = jnp.where(lane < n_groups, lg, neg_inf)
        gmax = jnp.max(gl, axis=1, keepdims=True)
        g_idx = first_argmax(gl, gmax)
        p_top = 1.0 / jnp.sum(jnp.exp(gl - gmax), axis=1, keepdims=True)
        lo = n_groups + g_idx * epg
        el = jnp.where(jnp.logical_and(lane >= lo, lane < lo + epg), lg, neg_inf)
        l1 = jnp.max(el, axis=1, keepdims=True)
        i1 = first_argmax(el, l1)
        el2 = jnp.where(lane == i1, neg_inf, el)
        l2 = jnp.max(el2, axis=1, keepdims=True)
        i2 = first_argmax(el2, l2)
        d = jnp.exp(l2 - l1)
        w1 = 1.0 / (1.0 + d)
        e1 = i1 - n_groups
        e2 = i2 - n_groups
        onehot = jnp.where(jnp.logical_or(lane == e1, lane == e2), 1.0, 0.0)
        cs = jnp.dot(tri, onehot.astype(BF16), preferred_element_type=F32) + carry
        c_s[pl.ds(r0, chunk), :] = cs
        info_s[pl.ds(r0, chunk), :] = jnp.where(
            lane == 0, e1, jnp.where(lane == 1, e2, jnp.where(
                lane == 2, p_top * w1, jnp.where(lane == 3, p_top * (d * w1), 0.0))))
        return cs[chunk - 1:chunk, :]

    counts = lax.fori_loop(0, n_tok // chunk, decide, jnp.zeros((1, LANES), F32))

    tiles = jnp.floor((counts + (tm - 1)) * (1.0 / tm))
    jj = lax.broadcasted_iota(jnp.int32, (LANES, LANES), 0)
    ee = lax.broadcasted_iota(jnp.int32, (LANES, LANES), 1)
    upper = jnp.where(jj <= ee, 1.0, 0.0).astype(BF16)
    tile_end = jnp.dot(jnp.broadcast_to(tiles, (8, LANES)).astype(BF16), upper,
                       preferred_element_type=F32)[0:1, :]
    tile_start = tile_end - tiles
    row_start = tile_start * tm

    def place(c, carry):
        r0 = pl.multiple_of(c * chunk, chunk)
        info = info_s[pl.ds(r0, chunk), :]
        base = row_start + c_s[pl.ds(r0, chunk), :] - 1.0
        pos1 = jnp.sum(jnp.where(lane == info[:, 0:1], base, 0.0), axis=1, keepdims=True)
        pos2 = jnp.sum(jnp.where(lane == info[:, 1:2], base, 0.0), axis=1, keepdims=True)
        out_ref[pl.ds(r0, chunk), :] = jnp.where(lane == 0, pos1, jnp.where(lane == 1, pos2, info))
        return carry

    lax.fori_loop(0, n_tok // chunk, place, 0)

    ti = jj.astype(F32)
    lane_sq = ee.astype(F32)
    is_exp = lane_sq < n_exp
    n_used = tile_end[:, n_exp - 1:n_exp]
    ti_c = jnp.minimum(ti, n_used - 1.0)
    tile_e = jnp.sum(jnp.where(jnp.logical_and(is_exp, tile_end <= ti_c), 1.0, 0.0), axis=1, keepdims=True)
    first = jnp.sum(jnp.where(jnp.logical_and(jnp.logical_and(is_exp, tile_start == ti), tiles > 0.0),
                              1.0, 0.0), axis=1, keepdims=True)
    nonempty = jnp.logical_and(is_exp, tiles > 0.0)
    next_e = jnp.min(jnp.where(jnp.logical_and(nonempty, lane_sq > tile_e), lane_sq, float(LANES)),
                     axis=1, keepdims=True)
    next_e = jnp.where(next_e >= LANES, tile_e, next_e)
    group = jnp.sum(jnp.where(jnp.logical_and(nonempty, lane_sq < tile_e), 1.0, 0.0), axis=1, keepdims=True)
    wslot = group - 2.0 * jnp.floor(group * 0.5)
    diag = jj == ee
    pad_lo = jnp.sum(jnp.where(diag, row_start + counts, 0.0), axis=1, keepdims=True)
    pad_hi = jnp.sum(jnp.where(diag, row_start + tiles * tm, 0.0), axis=1, keepdims=True)
    fields = (tile_e, first, n_used, next_e, wslot, pad_lo, pad_hi)
    packed = jnp.zeros((LANES, LANES), F32)
    for f, val in enumerate(fields):
        packed = jnp.where(lane_sq == f, val, packed)
    tiles_ref[...] = packed.astype(jnp.int32)


N_TILE_FIELDS = 5


def routing_plan(logits, n_groups, epg, tm):
    T = logits.shape[0]
    n_exp = n_groups * epg
    n_tiles = T * TOP_K_IN_GROUP // tm + n_exp
    assert n_tiles <= LANES
    out, tiles = pl.pallas_call(
        functools.partial(_plan_kernel, n_tok=T, n_groups=n_groups, epg=epg, tm=tm, chunk=PLAN_CHUNK),
        grid=(1,),
        in_specs=[pl.BlockSpec((T, LANES), lambda i: (0, 0))],
        out_specs=(pl.BlockSpec((T, LANES), lambda i: (0, 0)),
                   pl.BlockSpec((LANES, LANES), lambda i: (0, 0))),
        out_shape=(jax.ShapeDtypeStruct((T, LANES), F32), jax.ShapeDtypeStruct((LANES, LANES), jnp.int32)),
        scratch_shapes=[pltpu.VMEM((T, LANES), F32), pltpu.VMEM((T, LANES), F32)],
        compiler_params=_params(("arbitrary",), 40),
        name="routing_plan",
    )(logits)
    pos = out[:, :TOP_K_IN_GROUP].astype(jnp.int32).reshape(-1)
    tile_info = tiles[:n_tiles, :N_TILE_FIELDS].T.reshape(-1)
    pads = tiles[:n_exp, N_TILE_FIELDS:N_TILE_FIELDS + 2].T.reshape(-1)
    return out, pos, tile_info, pads, n_tiles


def _invert_kernel(pos_ref, pads_ref, info_ref, out_ref, *, n_tok, top_k, n_exp, tm, n_tiles):
    def zero(r, carry):
        out_ref[r] = 0
        return carry

    for e in range(n_exp):
        lax.fori_loop(pads_ref[e], pads_ref[n_exp + e], zero, 0)

    def zero8(g, carry):
        for u in range(8):
            out_ref[g * 8 + u] = 0
        return carry
    lax.fori_loop(info_ref[2 * n_tiles] * (tm // 8), n_tiles * (tm // 8), zero8, 0)

    def put(t, carry):
        for k in range(top_k):
            out_ref[pos_ref[t * top_k + k]] = t
        return carry
    lax.fori_loop(0, n_tok, put, 0, unroll=8)


def invert_positions(pos, pads, tile_info, n_tiles, tm):
    top_k = TOP_K_IN_GROUP
    return pl.pallas_call(
        functools.partial(_invert_kernel, n_tok=pos.shape[0] // top_k, top_k=top_k,
                          n_exp=pads.shape[0] // 2, tm=tm, n_tiles=n_tiles),
        grid_spec=pltpu.PrefetchScalarGridSpec(
            num_scalar_prefetch=3, grid=(1,), in_specs=[],
            out_specs=pl.BlockSpec(memory_space=pltpu.SMEM)),
        out_shape=jax.ShapeDtypeStruct((n_tiles * tm,), jnp.int32),
        compiler_params=_params(("arbitrary",), 16),
        name="invert_positions",
    )(pos, pads, tile_info)


def _row_copy(src_hbm, tok, dst_row, sem, rows=1):
    start = tok if rows == 1 else pl.multiple_of(tok * rows, rows)
    return pltpu.make_async_copy(src_hbm.at[pl.ds(start, rows)], dst_row, sem)


def _start_row_gather(src_hbm, idx_ref, base, n, dst_row, sem, rows=1):
    def body(r, carry):
        _row_copy(src_hbm, idx_ref[base + r], dst_row(r), sem, rows).start()
        return carry
    lax.fori_loop(0, n, body, 0, unroll=8)


def _wait_row_gather(src_hbm, n, dst_all, sem):
    pltpu.make_async_copy(src_hbm.at[pl.ds(0, n)], dst_all, sem).wait()


def _gmm_kernel(info_ref, row_tok_ref, x_hbm, wg_hbm, wu_hbm, wd_hbm, o_ref,
                xbuf, sem, wg_f, wu_f, wd_f, wsem, wg_bf, wu_bf, wd_bf, *, tm, n_tiles, w_off, n_chunks, tok_rows):
    i = pl.program_id(0)
    expert = info_ref[i]
    first = info_ref[n_tiles + i]
    n_used = info_ref[2 * n_tiles]
    next_expert = info_ref[3 * n_tiles + i]
    wslot = info_ref[4 * n_tiles + i]
    slot = i % GATHER_BUFS
    ahead = GATHER_BUFS - 1

    def wait_slot(s):
        _wait_row_gather(x_hbm, tm * tok_rows, xbuf.at[s], sem.at[s])

    def weight_copies(e, s):
        copies = []
        for hbm, buf in ((wg_hbm, wg_f), (wu_hbm, wu_f), (wd_hbm, wd_f)):
            rows = hbm.shape[1] // n_chunks
            for c in range(n_chunks):
                copies.append(pltpu.make_async_copy(hbm.at[w_off + e, pl.ds(c * rows, rows)],
                                                    buf.at[s, pl.ds(c * rows, rows)], wsem.at[s]))
        return copies

    @pl.when(jnp.logical_and(i == 0, n_used > 0))
    def _():
        for cp in weight_copies(expert, 0):
            cp.start(priority=1)
        for t in range(ahead):
            _start_row_gather(x_hbm, row_tok_ref, min(t, n_tiles - 1) * tm, tm,
                              lambda r, t=t: xbuf.at[t, pl.ds(pl.multiple_of(r * tok_rows, tok_rows), tok_rows)],
                              sem.at[t], tok_rows)

    @pl.when(i < n_used)
    def _():
        @pl.when(first == 1)
        def _():
            for cp in weight_copies(expert, wslot):
                cp.wait()

            @pl.when(next_expert != expert)
            def _():
                for cp in weight_copies(next_expert, 1 - wslot):
                    cp.start(priority=1)

            wg_bf[...] = wg_f[wslot].astype(BF16)
            wu_bf[...] = wu_f[wslot].astype(BF16)
            wd_bf[...] = wd_f[wslot].astype(BF16)

        wait_slot(slot)
        xn = jnp.concatenate([xbuf[slot, pl.ds(j, tm, stride=tok_rows), :].astype(BF16)
                              for j in range(tok_rows)], axis=1)
        base = jnp.minimum(i + ahead, n_tiles - 1) * tm
        nslot = (i + ahead) % GATHER_BUFS
        for r in range(tm):
            _row_copy(x_hbm, row_tok_ref[base + r], xbuf.at[nslot, pl.ds(r * tok_rows, tok_rows)],
                      sem.at[nslot], tok_rows).start()
        hg = jnp.dot(xn, wg_bf[...], preferred_element_type=F32)
        hu = jnp.dot(xn, wu_bf[...], preferred_element_type=F32)
        hh = (hg * jax.nn.sigmoid(hg)) * hu
        o_ref[...] = jnp.dot(hh.astype(BF16), wd_bf[...], preferred_element_type=F32)

    @pl.when(i >= n_used)
    def _():
        @pl.when(jnp.logical_and(i < n_used + ahead, n_used > 0))
        def _():
            wait_slot(slot)
        o_ref[...] = jnp.zeros(o_ref.shape, o_ref.dtype)

    @pl.when(i == n_tiles - 1)
    def _():
        for d in range(1, ahead + 1):
            @pl.when(n_used >= n_tiles - ahead + d)
            def _(d=d):
                wait_slot((n_tiles - 1 + d) % GATHER_BUFS)


def grouped_expert_mlp(xt, tile_info, row_tok, w_gate, w_up, w_down, layer, n_tiles, tm):
    _, D, F = w_gate.shape
    tok_rows = D // LANES
    layer_idx, depth = layer
    off = layer_idx * (w_gate.shape[0] // depth)
    any_spec = pl.BlockSpec(memory_space=pl.ANY)
    grid_spec = pltpu.PrefetchScalarGridSpec(
        num_scalar_prefetch=2,
        grid=(n_tiles,),
        in_specs=[any_spec, any_spec, any_spec, any_spec],
        out_specs=pl.BlockSpec((tm, D), lambda i, ti, rt: (i, 0)),
        scratch_shapes=[pltpu.VMEM((GATHER_BUFS, tm * tok_rows, LANES), F32),
                        pltpu.SemaphoreType.DMA((GATHER_BUFS,)),
                        pltpu.VMEM((2, D, F), F32), pltpu.VMEM((2, D, F), F32), pltpu.VMEM((2, F, D), F32),
                        pltpu.SemaphoreType.DMA((2,)),
                        pltpu.VMEM((D, F), BF16), pltpu.VMEM((D, F), BF16), pltpu.VMEM((F, D), BF16)],
    )
    return pl.pallas_call(
        functools.partial(_gmm_kernel, tm=tm, n_tiles=n_tiles, w_off=off, n_chunks=WEIGHT_DMA_CHUNKS,
                          tok_rows=tok_rows),
        grid_spec=grid_spec,
        out_shape=jax.ShapeDtypeStruct((n_tiles * tm, D), F32),
        compiler_params=_params(("arbitrary",), 56),
        name="grouped_expert_mlp",
    )(tile_info, row_tok, xt, w_gate, w_up, w_down)


def _combine_kernel(pos_ref, x_ref, plan_ref, y_hbm, o_ref, ybuf, sem, *, tc, top_k):
    i = pl.program_id(0)
    n = pl.num_programs(0)
    slot = i % 2

    def start(tile, s):
        base = tile * (tc * top_k)
        for r in range(tc):
            for k in range(top_k):
                _row_copy(y_hbm, pos_ref[base + r * top_k + k], ybuf.at[s, k, pl.ds(r, 1)], sem.at[s]).start()

    @pl.when(i == 0)
    def _():
        start(0, 0)

    for k in range(top_k):
        _wait_row_gather(y_hbm, tc, ybuf.at[slot, k], sem.at[slot])

    @pl.when(i + 1 < n)
    def _():
        start(i + 1, 1 - slot)

    acc = x_ref[...]
    for k in range(top_k):
        acc = acc + plan_ref[:, top_k + k:top_k + k + 1] * ybuf[slot, k]
    o_ref[...] = acc


def moe_combine(x, y_sorted, plan_out, pos, tc):
    T, D = x.shape
    top_k = TOP_K_IN_GROUP
    grid_spec = pltpu.PrefetchScalarGridSpec(
        num_scalar_prefetch=1,
        grid=(T // tc,),
        in_specs=[pl.BlockSpec((tc, D), lambda i, p: (i, 0)),
                  pl.BlockSpec((tc, LANES), lambda i, p: (i, 0)),
                  pl.BlockSpec(memory_space=pl.ANY)],
        out_specs=pl.BlockSpec((tc, D), lambda i, p: (i, 0)),
        scratch_shapes=[pltpu.VMEM((2, top_k, tc, D), F32), pltpu.SemaphoreType.DMA((2,))],
    )
    return pl.pallas_call(
        functools.partial(_combine_kernel, tc=tc, top_k=top_k),
        grid_spec=grid_spec,
        out_shape=jax.ShapeDtypeStruct((T, D), F32),
        compiler_params=_params(("arbitrary",), 32),
        name="moe_combine",
    )(pos, x, plan_out, y_sorted)


def hier_moe_residual(x, g, w_r, b_r, n_groups, epg, w_gate, w_up, w_down, layer):
    logits, xt = router_logits(x, g, w_r, b_r, tm=TM_PROJ)
    plan_out, pos, tile_info, pads, n_tiles = routing_plan(logits, n_groups, epg, TM_EXPERT)
    row_tok = invert_positions(pos, pads, tile_info, n_tiles, TM_EXPERT)
    y_sorted = grouped_expert_mlp(xt, tile_info, row_tok, w_gate, w_up, w_down, layer, n_tiles, TM_EXPERT)
    return moe_combine(x, y_sorted, plan_out, pos, TC_COMBINE)


def _pad_head_cols(w, heads, real):
    r = w.shape[0]
    return jnp.pad(w.reshape(r, heads, real), ((0, 0), (0, 0), (0, HEAD_PAD - real))).reshape(r, heads * HEAD_PAD)


def _rope_tables(positions):
    half = QK_ROPE // 2
    inv_freq = ROPE_THETA ** (-jnp.arange(half, dtype=F32) / half)
    ang = positions.astype(F32).reshape(-1, 1) * inv_freq[None, :]
    cos, sin = jnp.cos(ang), jnp.sin(ang)
    zeros = jnp.zeros((ang.shape[0], LANES - QK_ROPE), F32)
    return (jnp.concatenate([cos, cos, zeros], axis=1),
            jnp.concatenate([-sin, sin, zeros], axis=1))


def kernel(x, mem, positions, norm_mix_g, norm_ffn_g, w_o, mem_norm_g, w_mem_kv, mem_q_norm_g, mem_k_norm_g, w_in_a, conv_w, conv_b, w_lru_r, b_lru_r, w_lru_i, b_lru_i, lru_lambda, kv_in_norm_g, w_dkv, kv_latent_norm_g, w_uk, w_uv, k_head_norm_g, w_in_b, q_latent_norm_g, w_uq, q_head_norm_g, w_router_grp, b_router_grp, w_router_exp, b_router_exp, w_exp_gate, w_exp_up, w_exp_down):
    B, S, D = x.shape
    T = B * S
    M = mem.shape[1]
    depth = norm_mix_g.shape[0]
    n_a = w_in_a.shape[0]
    lru_w = lru_lambda.shape[1]
    mem_w = w_mem_kv.shape[2] // 2
    heads = w_uv.shape[1] // V_DIM
    kv_rank = kv_latent_norm_g.shape[0]
    q_rank = q_latent_norm_g.shape[1]
    n_groups, epg = w_exp_gate.shape[1], w_exp_gate.shape[2]
    n_exp = n_groups * epg
    d_exp = w_exp_gate.shape[-1]

    xr = x.reshape(T, D)
    mem2 = mem.reshape(B * M, D)
    cos_t, sin_t = _rope_tables(positions)
    wg_all = w_exp_gate.reshape(depth * n_exp, D, d_exp)
    wu_all = w_exp_up.reshape(depth * n_exp, D, d_exp)
    wd_all = w_exp_down.reshape(depth * n_exp, d_exp, D)
    pad_r = LANES - n_groups - n_exp
    w_router = jnp.concatenate([w_router_grp, w_router_exp, jnp.zeros((depth, D, pad_r), F32)], axis=2)
    b_router = jnp.concatenate([b_router_grp, b_router_exp, jnp.zeros((depth, pad_r), F32)], axis=1)

    k_sh = v_sh = None
    for l in range(depth):
        kv_mem = norm_matmul(mem2, mem_norm_g[l], w_mem_kv, l, BF16, tm=TM_PROJ).reshape(B, M, 2 * mem_w)
        if l < n_a:
            proj = norm_matmul(xr, norm_mix_g[l], w_in_a, l, BF16, tm=TM_PROJ)
            y_mix = rglru(proj.reshape(B, S, -1), conv_w[l], conv_b[l], w_lru_r[l], b_lru_r[l],
                          w_lru_i[l], b_lru_i[l], lru_lambda[l], nb=LRU_BLOCKS_PER_STEP,
                          tc=TC_LRU).reshape(T, lru_w)
            q_col = 2 * lru_w // mem_w
        else:
            j = l - n_a
            proj = norm_matmul(xr, norm_mix_g[l], w_in_b, j, BF16, tm=TM_PROJ)
            wq = _pad_head_cols(w_uq[j], heads, QK_DIM).astype(BF16)
            qg = jnp.pad(q_head_norm_g[j] * (QK_DIM ** -0.5 * math.log2(math.e)),
                         (0, HEAD_PAD - QK_DIM)).reshape(1, HEAD_PAD)
            q = latent_heads(proj, q_rank, q_rank, q_latent_norm_g[j], wq, qg, cos_t, sin_t, heads, tm=TM_HEADS)
            y_mix = flash_attention(q.reshape(B, S, -1), k_sh, v_sh, heads, tq=TQ_FLASH).reshape(T, heads * V_DIM)
            q_col = q_rank // mem_w
        y_mem = mem_attention(proj, q_col, kv_mem, mem_q_norm_g[l], mem_k_norm_g[l], B, ts=TS_MEM)
        xr = out_proj(y_mix, y_mem, w_o, l, xr, tm=TM_PROJ)
        xr = hier_moe_residual(xr, norm_ffn_g[l], w_router[l], b_router[l].reshape(1, LANES), n_groups, epg,
                               wg_all, wu_all, wd_all, (l, depth))
        if l == n_a - 1:
            lat_cols = kv_rank + LANES
            w_dkv_pad = jnp.pad(w_dkv, ((0, 0), (0, lat_cols - w_dkv.shape[1])))[None]
            ckv = norm_matmul(xr, kv_in_norm_g, w_dkv_pad, 0, F32, tm=TM_PROJ)
            eye = jnp.pad(jnp.eye(QK_ROPE, dtype=F32), ((0, LANES - QK_ROPE), (0, 0)))
            eye3 = jnp.broadcast_to(eye[:, None, :], (LANES, heads, QK_ROPE))
            rope_rows = jnp.pad(eye3, ((0, 0), (0, 0), (QK_NOPE, HEAD_PAD - QK_DIM))).reshape(LANES, heads * HEAD_PAD)
            wk = jnp.concatenate([_pad_head_cols(w_uk, heads, QK_NOPE), rope_rows], axis=0).astype(BF16)
            kg = jnp.pad(k_head_norm_g, (0, HEAD_PAD - QK_DIM)).reshape(1, HEAD_PAD)
            k_flat, v_flat = latent_heads(ckv, lat_cols, kv_rank, kv_latent_norm_g, wk, kg, cos_t, sin_t,
                                          heads, tm=TM_HEADS, w_v=w_uv.astype(BF16))
            k_sh = k_flat.reshape(B, S, -1)
            v_sh = v_flat.reshape(B, S, -1)
    return xr.reshape(B, S, D)
```

```python
import functools
import math

import jax
import jax.numpy as jnp
from jax import lax
from jax.experimental import pallas as pl
from jax.experimental.pallas import tpu as pltpu

F32 = jnp.float32
BF16 = jnp.bfloat16
EPS = 1e-6

MEM_HEADS = 4
CONV_WIDTH = 4
LRU_C = 8.0
QK_NOPE = 128
QK_ROPE = 64
QK_DIM = QK_NOPE + QK_ROPE
V_DIM = 128
ROPE_THETA = 10000.0
TOP_K_IN_GROUP = 2

LANES = 128
HEAD_PAD = 2 * LANES
MIB = 1024 * 1024

TM_PROJ = 512
TN_PROJ = 512
TS_MEM = 512
TQ_FLASH = 512
SUB_FLASH = 256
TM_HEADS = 512
TM_EXPERT = 256
WEIGHT_DMA_CHUNKS = 4
GATHER_BUFS = 3
TC_COMBINE = 256
LRU_BLOCKS_PER_STEP = 3
TC_LRU = 256
PLAN_CHUNK = 256
N_TILE_FIELDS = 5


def _params(sem, vmem_mib):
    return pltpu.CompilerParams(dimension_semantics=sem, vmem_limit_bytes=vmem_mib * MIB)


def _rms_rows(x, g):
    x = x.astype(F32)
    ms = jnp.mean(x * x, axis=-1, keepdims=True)
    return x * lax.rsqrt(ms + EPS) * g


def _col_tile(n):
    return TN_PROJ if n % TN_PROJ == 0 else n


def _stage_weight_bf16(w_hbm, layer, w_bf, stage, sem, tn):
    n = w_bf.shape[1] // tn

    def copy(j):
        return pltpu.make_async_copy(w_hbm.at[layer, :, pl.ds(j * tn, tn)], stage.at[j % 2], sem.at[j % 2])

    copy(0).start()
    for j in range(n):
        if j + 1 < n:
            copy(j + 1).start()
        copy(j).wait()
        w_bf[:, j * tn:(j + 1) * tn] = stage[j % 2].astype(BF16)


def _norm_mm_kernel(x_ref, g_ref, w_hbm, o_ref, xn_ref, w_bf, stage, sem, *, layer, tm, chunk, tn):
    @pl.when(pl.program_id(0) == 0)
    def _():
        _stage_weight_bf16(w_hbm, layer, w_bf, stage, sem, tn)

    for r0 in range(0, tm, chunk):
        xn_ref[r0:r0 + chunk, :] = _rms_rows(x_ref[r0:r0 + chunk, :], g_ref[...]).astype(BF16)

    def cols(j, carry):
        c0 = pl.multiple_of(j * tn, tn)
        o_ref[:, pl.ds(c0, tn)] = jnp.dot(xn_ref[...], w_bf[:, pl.ds(c0, tn)],
                                          preferred_element_type=F32).astype(o_ref.dtype)
        return carry
    lax.fori_loop(0, o_ref.shape[1] // tn, cols, 0)


def norm_matmul(x, g, w, layer, out_dtype, tm):
    T, K = x.shape
    N = w.shape[2]
    tm = min(tm, T)
    chunk = min(tm, 256)
    tn = _col_tile(N)
    return pl.pallas_call(
        functools.partial(_norm_mm_kernel, layer=layer, tm=tm, chunk=chunk, tn=tn),
        grid=(T // tm,),
        in_specs=[pl.BlockSpec((tm, K), lambda i: (i, 0)),
                  pl.BlockSpec((1, K), lambda i: (0, 0)),
                  pl.BlockSpec(memory_space=pl.ANY)],
        out_specs=pl.BlockSpec((tm, N), lambda i: (i, 0)),
        out_shape=jax.ShapeDtypeStruct((T, N), out_dtype),
        scratch_shapes=[pltpu.VMEM((tm, K), BF16), pltpu.VMEM((K, N), BF16), pltpu.VMEM((2, K, tn), F32),
                        pltpu.SemaphoreType.DMA((2,))],
        compiler_params=_params(("arbitrary",), 56),
        name="norm_matmul",
    )(x, g.reshape(1, K), w)


def _out_proj_kernel(ya_ref, yb_ref, w_hbm, x_ref, g_ref, wr_ref, br_ref, o_ref, lg_ref, xt_ref,
                     w_bf, stage, sem, *, layer, tm, tn):
    @pl.when(pl.program_id(0) == 0)
    def _():
        _stage_weight_bf16(w_hbm, layer, w_bf, stage, sem, tn)

    wa_rows = ya_ref.shape[1]

    def cols(j, carry):
        c0 = pl.multiple_of(j * tn, tn)
        acc = jnp.dot(ya_ref[...], w_bf[:wa_rows, pl.ds(c0, tn)], preferred_element_type=F32)
        acc += jnp.dot(yb_ref[...], w_bf[wa_rows:, pl.ds(c0, tn)], preferred_element_type=F32)
        o_ref[:, pl.ds(c0, tn)] = x_ref[:, pl.ds(c0, tn)] + acc
        return carry
    lax.fori_loop(0, o_ref.shape[1] // tn, cols, 0)
    _router_rows(o_ref, g_ref, wr_ref, br_ref, lg_ref, xt_ref, tm, min(tm, 256))


def out_proj(y_mix, y_mem, w_o, layer, x, g_ffn, w_router, b_router, tm):
    T, Wa = y_mix.shape
    Wb = y_mem.shape[1]
    _, K, N = w_o.shape
    assert Wa + Wb == K
    tn = _col_tile(N)
    lane_tiles = N // LANES
    return pl.pallas_call(
        functools.partial(_out_proj_kernel, layer=layer, tm=tm, tn=tn),
        grid=(T // tm,),
        in_specs=[pl.BlockSpec((tm, Wa), lambda i: (i, 0)),
                  pl.BlockSpec((tm, Wb), lambda i: (i, 0)),
                  pl.BlockSpec(memory_space=pl.ANY),
                  pl.BlockSpec((tm, N), lambda i: (i, 0)),
                  pl.BlockSpec((1, N), lambda i: (0, 0)),
                  pl.BlockSpec((N, LANES), lambda i: (0, 0)),
                  pl.BlockSpec((1, LANES), lambda i: (0, 0))],
        out_specs=(pl.BlockSpec((tm, N), lambda i: (i, 0)),
                   pl.BlockSpec((tm, LANES), lambda i: (i, 0)),
                   pl.BlockSpec((tm * lane_tiles, LANES), lambda i: (i, 0))),
        out_shape=(jax.ShapeDtypeStruct((T, N), F32),
                   jax.ShapeDtypeStruct((T, LANES), F32),
                   jax.ShapeDtypeStruct((T * lane_tiles, LANES), F32)),
        scratch_shapes=[pltpu.VMEM((K, N), BF16), pltpu.VMEM((2, K, tn), F32), pltpu.SemaphoreType.DMA((2,))],
        compiler_params=_params(("arbitrary",), 56),
        name="out_proj",
    )(y_mix, y_mem, w_o, x, g_ffn.reshape(1, N), w_router, b_router)


def _mem_attn_kernel(q_ref, kv_ref, qg_ref, kg_ref, o_ref, *, heads, hd):
    width = heads * hd
    scale = hd ** -0.5
    for h in range(heads):
        qn = _rms_rows(q_ref[:, h * hd:(h + 1) * hd], qg_ref[...]) * scale
        kn = _rms_rows(kv_ref[0, :, h * hd:(h + 1) * hd], kg_ref[...])
        v = kv_ref[0, :, width + h * hd:width + (h + 1) * hd]
        s = lax.dot_general(qn.astype(BF16), kn.astype(BF16), (((1,), (1,)), ((), ())),
                            preferred_element_type=F32)
        m = jnp.max(s, axis=-1, keepdims=True)
        p = jnp.exp(s - m)
        l = jnp.sum(p, axis=-1, keepdims=True)
        o = jnp.dot(p.astype(BF16), v, preferred_element_type=F32) / l
        o_ref[:, h * hd:(h + 1) * hd] = o.astype(o_ref.dtype)


def mem_attention(proj, q_col_block, kv, q_g, k_g, batch, ts):
    T = proj.shape[0]
    _, M, two_w = kv.shape
    width = two_w // 2
    hd = width // MEM_HEADS
    per_b = T // batch // ts
    return pl.pallas_call(
        functools.partial(_mem_attn_kernel, heads=MEM_HEADS, hd=hd),
        grid=(batch, per_b),
        in_specs=[pl.BlockSpec((ts, width), lambda b, i: (b * per_b + i, q_col_block)),
                  pl.BlockSpec((1, M, two_w), lambda b, i: (b, 0, 0)),
                  pl.BlockSpec((1, hd), lambda b, i: (0, 0)),
                  pl.BlockSpec((1, hd), lambda b, i: (0, 0))],
        out_specs=pl.BlockSpec((ts, width), lambda b, i: (b * per_b + i, 0)),
        out_shape=jax.ShapeDtypeStruct((T, width), BF16),
        compiler_params=_params(("parallel", "arbitrary"), 32),
        name="mem_attention",
    )(proj, kv, q_g.reshape(1, hd), k_g.reshape(1, hd))


def _gelu_tanh(x):
    return 0.5 * x * (1.0 + jnp.tanh(0.7978845608028654 * (x + 0.044715 * x * x * x)))


def _rglru_kernel(u_ref, gb_ref, cw_ref, cb_ref, wr_ref, wi_ref, br_ref, bi_ref, lam_ref,
                  o_ref, a_s, b_s, *, seq, nb, blk, tc):
    wb = nb * blk
    neg_lam = -lam_ref[...]
    softplus = jnp.maximum(neg_lam, 0.0) + jnp.log1p(jnp.exp(-jnp.abs(neg_lam)))

    def gates(c, carry):
        t0 = pl.multiple_of(c * tc, tc)
        cur = u_ref[0, pl.ds(t0, tc), :].astype(F32)
        p0 = pl.multiple_of(jnp.maximum(t0 - 16, 0), 16)
        prev = u_ref[0, pl.ds(p0, 16), :].astype(F32)
        prev = jnp.where(c > 0, prev, 0.0)
        full = jnp.concatenate([prev, cur], axis=0)
        y = cb_ref[...] + cw_ref[0:1, :] * cur
        for k in range(1, CONV_WIDTH):
            y = y + cw_ref[k:k + 1, :] * full[16 - k:16 - k + tc, :]
        r_parts, i_parts = [], []
        for n in range(nb):
            yb = y[:, n * blk:(n + 1) * blk].astype(BF16)
            r_parts.append(jnp.dot(yb, wr_ref[n].astype(BF16), preferred_element_type=F32))
            i_parts.append(jnp.dot(yb, wi_ref[n].astype(BF16), preferred_element_type=F32))
        r = jax.nn.sigmoid(jnp.concatenate(r_parts, axis=1) + br_ref[...])
        ig = jax.nn.sigmoid(jnp.concatenate(i_parts, axis=1) + bi_ref[...])
        log_a = (-LRU_C) * r * softplus
        a = jnp.exp(log_a)
        a_s[pl.ds(t0, tc), :] = a
        b_s[pl.ds(t0, tc), :] = jnp.sqrt(-jnp.tanh(log_a) * (a * a + 1.0)) * (ig * y)
        return carry

    lax.fori_loop(0, seq // tc, gates, 0)

    row = lax.broadcasted_iota(jnp.int32, (8, wb), 0)

    def scan(c, h):
        t0 = pl.multiple_of(c * 8, 8)
        a = a_s[pl.ds(t0, 8), :]
        b = b_s[pl.ds(t0, 8), :]
        for s in (1, 2, 4):
            a_sh = pltpu.roll(a, s, axis=0)
            b_sh = pltpu.roll(b, s, axis=0)
            keep = row >= s
            b = jnp.where(keep, a * b_sh + b, b)
            a = jnp.where(keep, a * a_sh, a)
        hc = a * h + b
        b_s[pl.ds(t0, 8), :] = hc
        return hc[7:8, :]

    lax.fori_loop(0, seq // 8, scan, jnp.zeros((1, wb), F32), unroll=4)

    def gate_out(c, carry):
        t0 = pl.multiple_of(c * tc, tc)
        g = gb_ref[0, pl.ds(t0, tc), :].astype(F32)
        o_ref[0, pl.ds(t0, tc), :] = (_gelu_tanh(g) * b_s[pl.ds(t0, tc), :]).astype(o_ref.dtype)
        return carry

    lax.fori_loop(0, seq // tc, gate_out, 0)


def rglru(proj, conv_w, conv_b, w_r, b_r, w_i, b_i, lam, nb, tc):
    B, S, _ = proj.shape
    W = lam.shape[0]
    n_blocks, blk, _ = w_r.shape
    wb = nb * blk
    ncb = W // wb
    vec = lambda: pl.BlockSpec((1, wb), lambda b, j: (0, j))
    return pl.pallas_call(
        functools.partial(_rglru_kernel, seq=S, nb=nb, blk=blk, tc=tc),
        grid=(B, ncb),
        in_specs=[pl.BlockSpec((1, S, wb), lambda b, j: (b, 0, j)),
                  pl.BlockSpec((1, S, wb), lambda b, j: (b, 0, ncb + j)),
                  pl.BlockSpec((CONV_WIDTH, wb), lambda b, j: (0, j)),
                  vec(),
                  pl.BlockSpec((nb, blk, blk), lambda b, j: (j, 0, 0)),
                  pl.BlockSpec((nb, blk, blk), lambda b, j: (j, 0, 0)),
                  vec(), vec(), vec()],
        out_specs=pl.BlockSpec((1, S, wb), lambda b, j: (b, 0, j)),
        out_shape=jax.ShapeDtypeStruct((B, S, W), BF16),
        scratch_shapes=[pltpu.VMEM((S, wb), F32), pltpu.VMEM((S, wb), F32)],
        compiler_params=_params(("parallel", "arbitrary"), 32),
        name="rglru",
    )(proj, proj, conv_w, conv_b.reshape(1, W), w_r, w_i, b_r.reshape(1, W), b_i.reshape(1, W),
      lam.reshape(1, W))


def _heads_kernel(*refs, rank, heads, with_v):
    if with_v:
        lat_ref, gl_ref, w_ref, hg_ref, cos_ref, sin_ref, wv_ref, o_ref, v_ref = refs
    else:
        lat_ref, gl_ref, w_ref, hg_ref, cos_ref, sin_ref, o_ref = refs
    lat = lat_ref[...].astype(F32)
    cn = _rms_rows(lat[:, :rank], gl_ref[...])
    full = cn if lat.shape[1] == rank else jnp.concatenate([cn, lat[:, rank:]], axis=1)
    full = full.astype(BF16)
    lane = lax.broadcasted_iota(jnp.int32, cos_ref.shape, 1)
    half = QK_ROPE // 2
    for h in range(heads):
        t = jnp.dot(full, w_ref[:, h * HEAD_PAD:(h + 1) * HEAD_PAD], preferred_element_type=F32)
        ss = jnp.sum(t * t, axis=-1, keepdims=True) * (1.0 / QK_DIM)
        tn = t * lax.rsqrt(ss + EPS) * hg_ref[...]
        rp = tn[:, QK_NOPE:]
        swapped = jnp.where(lane < half, pltpu.roll(rp, LANES - half, axis=1),
                            pltpu.roll(rp, half, axis=1))
        rot = rp * cos_ref[...] + swapped * sin_ref[...]
        o_ref[:, h * HEAD_PAD:h * HEAD_PAD + QK_NOPE] = tn[:, :QK_NOPE].astype(o_ref.dtype)
        o_ref[:, h * HEAD_PAD + QK_NOPE:(h + 1) * HEAD_PAD] = rot.astype(o_ref.dtype)
    if with_v:
        v = jnp.dot(cn.astype(BF16), wv_ref[...], preferred_element_type=F32).astype(v_ref.dtype)
        ones = jnp.ones((v.shape[0], HEAD_PAD - V_DIM), v_ref.dtype)
        for h in range(heads):
            v_ref[:, h * HEAD_PAD:h * HEAD_PAD + V_DIM] = v[:, h * V_DIM:(h + 1) * V_DIM]
            v_ref[:, h * HEAD_PAD + V_DIM:(h + 1) * HEAD_PAD] = ones


def latent_heads(lat, lat_cols, rank, g_lat, w_pad, head_gain, cos_t, sin_t, heads, tm, w_v=None):
    T = lat.shape[0]
    n_out = heads * HEAD_PAD
    in_specs = [pl.BlockSpec((tm, lat_cols), lambda i: (i, 0)),
                pl.BlockSpec((1, rank), lambda i: (0, 0)),
                pl.BlockSpec((lat_cols, n_out), lambda i: (0, 0)),
                pl.BlockSpec((1, HEAD_PAD), lambda i: (0, 0)),
                pl.BlockSpec((tm, LANES), lambda i: (i, 0)),
                pl.BlockSpec((tm, LANES), lambda i: (i, 0))]
    args = [lat, g_lat.reshape(1, rank), w_pad, head_gain, cos_t, sin_t]
    out_shape = jax.ShapeDtypeStruct((T, n_out), BF16)
    out_specs = pl.BlockSpec((tm, n_out), lambda i: (i, 0))
    if w_v is not None:
        in_specs.append(pl.BlockSpec(w_v.shape, lambda i: (0, 0)))
        args.append(w_v)
        out_shape = (out_shape, jax.ShapeDtypeStruct((T, n_out), BF16))
        out_specs = (out_specs, pl.BlockSpec((tm, n_out), lambda i: (i, 0)))
    return pl.pallas_call(
        functools.partial(_heads_kernel, rank=rank, heads=heads, with_v=w_v is not None),
        grid=(T // tm,),
        in_specs=in_specs, out_specs=out_specs, out_shape=out_shape,
        compiler_params=_params(("parallel",), 48),
        name="latent_heads_kv" if w_v is not None else "latent_heads_q",
    )(*args)


def _flash_kernel(q_ref, k_ref, v_ref, o_ref, s_s, m_s, acc_s, *, tq, sub, n_kt):
    i = pl.program_id(2)
    m_s[...] = jnp.full(m_s.shape, -1e30, F32)
    acc_s[...] = jnp.zeros(acc_s.shape, F32)
    n_sub = tq // sub

    def scores(kt, slot):
        k = k_ref[0, pl.ds(kt * tq, tq), :]
        for h in range(n_sub):
            rows = slice(h * sub, (h + 1) * sub)
            s_s[slot, rows, :] = lax.dot_general(q_ref[0, rows, :], k, (((1,), (1,)), ((), ())),
                                                 preferred_element_type=F32)

    def softmax_pv(k0, slot, masked):
        for h in range(n_sub):
            rows = slice(h * sub, (h + 1) * sub)
            nk = (h + 1) * sub if masked else tq
            s = s_s[slot, rows, :nk]
            if masked:
                r = lax.broadcasted_iota(jnp.int32, s.shape, 0) + h * sub
                c = lax.broadcasted_iota(jnp.int32, s.shape, 1)
                s = jnp.where(c <= r, s, -1e30)
            v = v_ref[0, pl.ds(k0, nk), :]
            m_prev = m_s[rows, :]
            m_new = jnp.maximum(m_prev, jnp.max(s, axis=-1, keepdims=True))
            alpha = jnp.exp2(m_prev - m_new)
            p = jnp.exp2(s - jnp.concatenate([m_new] * (nk // LANES), axis=1))
            acc_s[rows, :] = (jnp.concatenate([alpha, alpha], axis=1) * acc_s[rows, :]
                              + jnp.dot(p.astype(BF16), v, preferred_element_type=F32))
            m_s[rows, :] = m_new

    scores(0, 0)
    for kt in range(n_kt - 1):
        @pl.when(kt < i)
        def _(kt=kt):
            scores(kt + 1, (kt + 1) % 2)
            softmax_pv(kt * tq, kt % 2, False)
    softmax_pv(pl.multiple_of(i * tq, tq), i % 2, True)
    o_ref[0] = (acc_s[:, :V_DIM] / acc_s[:, V_DIM:]).astype(o_ref.dtype)


def flash_attention(q, k, v, heads, tq):
    B, S, _ = q.shape
    return pl.pallas_call(
        functools.partial(_flash_kernel, tq=tq, sub=SUB_FLASH, n_kt=S // tq),
        grid=(B, heads, S // tq),
        in_specs=[pl.BlockSpec((1, tq, HEAD_PAD), lambda b, h, i: (b, i, h)),
                  pl.BlockSpec((1, S, HEAD_PAD), lambda b, h, i: (b, 0, h)),
                  pl.BlockSpec((1, S, HEAD_PAD), lambda b, h, i: (b, 0, h))],
        out_specs=pl.BlockSpec((1, tq, V_DIM), lambda b, h, i: (b, i, h)),
        out_shape=jax.ShapeDtypeStruct((B, S, heads * V_DIM), BF16),
        scratch_shapes=[pltpu.VMEM((2, tq, tq), F32), pltpu.VMEM((tq, LANES), F32),
                        pltpu.VMEM((tq, HEAD_PAD), F32)],
        compiler_params=_params(("parallel", "parallel", "arbitrary"), 32),
        name="flash_attention",
    )(q, k, v)


def _router_rows(x_ref, g_ref, w_ref, b_ref, o_ref, xt_ref, tm, chunk):
    w = w_ref[...]
    w_hi = w.astype(BF16)
    w_lo = (w - w_hi.astype(F32)).astype(BF16)
    lane_tiles = x_ref.shape[1] // LANES
    for r0 in range(0, tm, chunk):
        xn = _rms_rows(x_ref[r0:r0 + chunk, :], g_ref[...])
        x_hi = xn.astype(BF16)
        x_lo = (xn - x_hi.astype(F32)).astype(BF16)
        acc = jnp.dot(x_hi, w_hi, preferred_element_type=F32)
        acc += jnp.dot(x_lo, w_hi, preferred_element_type=F32)
        acc += jnp.dot(x_hi, w_lo, preferred_element_type=F32)
        o_ref[r0:r0 + chunk, :] = acc + b_ref[...]
        for j in range(lane_tiles):
            xt_ref[pl.ds(r0 * lane_tiles + j, chunk, stride=lane_tiles), :] = xn[:, j * LANES:(j + 1) * LANES]


def _plan_kernel(lg_ref, out_ref, tiles_ref, row_tok_ref, c_s, info_s, inv_s, *, n_tok, n_groups, epg, tm, chunk):
    n_exp = n_groups * epg
    lane = lax.broadcasted_iota(jnp.int32, (chunk, LANES), 1).astype(F32)
    rr = lax.broadcasted_iota(jnp.int32, (chunk, chunk), 0)
    cc = lax.broadcasted_iota(jnp.int32, (chunk, chunk), 1)
    tri = jnp.where(cc <= rr, 1.0, 0.0).astype(BF16)
    neg_inf = -jnp.inf

    def first_argmax(vals, vmax):
        return jnp.min(jnp.where(vals == vmax, lane, float(LANES)), axis=1, keepdims=True)

    def decide(c, carry):
        r0 = pl.multiple_of(c * chunk, chunk)
        lg = lg_ref[pl.ds(r0, chunk), :]
        gl = jnp.where(lane < n_groups, lg, neg_inf)
        gmax = jnp.max(gl, axis=1, keepdims=True)
        g_idx = first_argmax(gl, gmax)
        p_top = 1.0 / jnp.sum(jnp.exp(gl - gmax), axis=1, keepdims=True)
        lo = n_groups + g_idx * epg
        el = jnp.where(jnp.logical_and(lane >= lo, lane < lo + epg), lg, neg_inf)
        l1 = jnp.max(el, axis=1, keepdims=True)
        i1 = first_argmax(el, l1)
        el2 = jnp.where(lane == i1, neg_inf, el)
        l2 = jnp.max(el2, axis=1, keepdims=True)
        i2 = first_argmax(el2, l2)
        d = jnp.exp(l2 - l1)
        w1 = 1.0 / (1.0 + d)
        e1 = i1 - n_groups
        e2 = i2 - n_groups
        onehot = jnp.where(jnp.logical_or(lane == e1, lane == e2), 1.0, 0.0)
        cs = jnp.dot(tri, onehot.astype(BF16), preferred_element_type=F32) + carry
        c_s[pl.ds(r0, chunk), :] = cs
        info_s[pl.ds(r0, chunk), :] = jnp.where(
            lane == 0, e1, jnp.where(lane == 1, e2, jnp.where(
                lane == 2, p_top * w1, jnp.where(lane == 3, p_top * (d * w1), 0.0))))
        return cs[chunk - 1:chunk, :]

    counts = lax.fori_loop(0, n_tok // chunk, decide, jnp.zeros((1, LANES), F32))

    tiles = jnp.floor((counts + (tm - 1)) * (1.0 / tm))
    jj = lax.broadcasted_iota(jnp.int32, (LANES, LANES), 0)
    ee = lax.broadcasted_iota(jnp.int32, (LANES, LANES), 1)
    upper = jnp.where(jj <= ee, 1.0, 0.0).astype(BF16)
    tile_end = jnp.dot(jnp.broadcast_to(tiles, (8, LANES)).astype(BF16), upper,
                       preferred_element_type=F32)[0:1, :]
    tile_start = tile_end - tiles
    row_start = tile_start * tm

    inv_s[...] = jnp.zeros(inv_s.shape, F32)
    off_lane = lax.broadcasted_iota(jnp.int32, (chunk, tm), 1).astype(F32)
    tile_row = lax.broadcasted_iota(jnp.int32, (LANES, chunk), 0).astype(F32)
    tok_local = lax.broadcasted_iota(jnp.int32, (1, chunk), 1)

    def place(c, carry):
        r0 = pl.multiple_of(c * chunk, chunk)
        info = info_s[pl.ds(r0, chunk), :]
        base = row_start + c_s[pl.ds(r0, chunk), :] - 1.0
        pos = [jnp.sum(jnp.where(lane == info[:, k:k + 1], base, 0.0), axis=1, keepdims=True)
               for k in range(TOP_K_IN_GROUP)]
        packed = jnp.where(lane == 0, pos[0], jnp.where(lane == 1, pos[1], info))
        out_ref[pl.ds(r0, chunk), :] = packed
        pos_rows = jnp.transpose(packed)
        tok = r0 + tok_local
        tok_hi = lax.shift_right_logical(tok, LANES.bit_length() - 1).astype(F32)
        tok_lo = (tok & (LANES - 1)).astype(F32)
        for k in range(TOP_K_IN_GROUP):
            tile_of = jnp.floor(pos[k] * (1.0 / tm))
            onehot_off = jnp.where(off_lane == pos[k] - tile_of * tm, 1.0, 0.0).astype(BF16)
            sel = tile_row == jnp.floor(pos_rows[k:k + 1, :] * (1.0 / tm))
            ids = jnp.concatenate([jnp.where(sel, tok_hi, 0.0), jnp.where(sel, tok_lo, 0.0)], axis=0)
            inv_s[...] += jnp.dot(ids.astype(BF16), onehot_off, preferred_element_type=F32)
        return carry

    lax.fori_loop(0, n_tok // chunk, place, 0)
    row_tok_ref[...] = (inv_s[:LANES, :] * LANES + inv_s[LANES:, :]).astype(jnp.int32)

    ti = jj.astype(F32)
    lane_sq = ee.astype(F32)
    is_exp = lane_sq < n_exp
    n_used = tile_end[:, n_exp - 1:n_exp]
    ti_c = jnp.minimum(ti, n_used - 1.0)
    tile_e = jnp.sum(jnp.where(jnp.logical_and(is_exp, tile_end <= ti_c), 1.0, 0.0), axis=1, keepdims=True)
    first = jnp.sum(jnp.where(jnp.logical_and(jnp.logical_and(is_exp, tile_start == ti), tiles > 0.0),
                              1.0, 0.0), axis=1, keepdims=True)
    nonempty = jnp.logical_and(is_exp, tiles > 0.0)
    next_e = jnp.min(jnp.where(jnp.logical_and(nonempty, lane_sq > tile_e), lane_sq, float(LANES)),
                     axis=1, keepdims=True)
    next_e = jnp.where(next_e >= LANES, tile_e, next_e)
    group = jnp.sum(jnp.where(jnp.logical_and(nonempty, lane_sq < tile_e), 1.0, 0.0), axis=1, keepdims=True)
    wslot = group - 2.0 * jnp.floor(group * 0.5)
    fields = (tile_e, first, n_used, next_e, wslot)
    assert len(fields) == N_TILE_FIELDS
    packed = jnp.zeros((LANES, LANES), F32)
    for f, val in enumerate(fields):
        packed = jnp.where(lane_sq == f, val, packed)
    tiles_ref[...] = packed.astype(jnp.int32)


def routing_plan(logits, n_groups, epg, tm):
    T = logits.shape[0]
    n_exp = n_groups * epg
    n_tiles = T * TOP_K_IN_GROUP // tm + n_exp
    assert n_tiles <= LANES and T <= LANES * LANES
    out, tiles, row_tok = pl.pallas_call(
        functools.partial(_plan_kernel, n_tok=T, n_groups=n_groups, epg=epg, tm=tm, chunk=PLAN_CHUNK),
        grid=(1,),
        in_specs=[pl.BlockSpec((T, LANES), lambda i: (0, 0))],
        out_specs=(pl.BlockSpec((T, LANES), lambda i: (0, 0)),
                   pl.BlockSpec((LANES, LANES), lambda i: (0, 0)),
                   pl.BlockSpec((LANES, tm), lambda i: (0, 0))),
        out_shape=(jax.ShapeDtypeStruct((T, LANES), F32), jax.ShapeDtypeStruct((LANES, LANES), jnp.int32),
                   jax.ShapeDtypeStruct((LANES, tm), jnp.int32)),
        scratch_shapes=[pltpu.VMEM((T, LANES), F32), pltpu.VMEM((T, LANES), F32),
                        pltpu.VMEM((2 * LANES, tm), F32)],
        compiler_params=_params(("arbitrary",), 40),
        name="routing_plan",
    )(logits)
    pos = out[:, :TOP_K_IN_GROUP].astype(jnp.int32).reshape(-1)
    tile_info = tiles[:n_tiles, :N_TILE_FIELDS].T.reshape(-1)
    return out, pos, tile_info, row_tok[:n_tiles].reshape(-1), n_tiles


def _row_copy(src_hbm, tok, dst_row, sem, rows=1):
    start = tok if rows == 1 else pl.multiple_of(tok * rows, rows)
    return pltpu.make_async_copy(src_hbm.at[pl.ds(start, rows)], dst_row, sem)


def _start_row_gather(src_hbm, idx_ref, base, n, dst_row, sem, rows=1):
    def body(r, carry):
        _row_copy(src_hbm, idx_ref[base + r], dst_row(r), sem, rows).start()
        return carry
    lax.fori_loop(0, n, body, 0, unroll=8)


def _wait_row_gather(src_hbm, n, dst_all, sem):
    pltpu.make_async_copy(src_hbm.at[pl.ds(0, n)], dst_all, sem).wait()


def _gmm_kernel(info_ref, row_tok_ref, x_hbm, wg_hbm, wu_hbm, wd_hbm, o_ref,
                xbuf, sem, wg_f, wu_f, wd_f, wsem, wg_bf, wu_bf, wd_bf, *, tm, n_tiles, w_off, n_chunks, tok_rows):
    i = pl.program_id(0)
    expert = info_ref[i]
    first = info_ref[n_tiles + i]
    n_used = info_ref[2 * n_tiles]
    next_expert = info_ref[3 * n_tiles + i]
    wslot = info_ref[4 * n_tiles + i]
    slot = i % GATHER_BUFS
    ahead = GATHER_BUFS - 1

    def wait_slot(s):
        _wait_row_gather(x_hbm, tm * tok_rows, xbuf.at[s], sem.at[s])

    def weight_copies(e, s):
        copies = []
        for hbm, buf in ((wg_hbm, wg_f), (wu_hbm, wu_f), (wd_hbm, wd_f)):
            rows = hbm.shape[1] // n_chunks
            for c in range(n_chunks):
                copies.append(pltpu.make_async_copy(hbm.at[w_off + e, pl.ds(c * rows, rows)],
                                                    buf.at[s, pl.ds(c * rows, rows)], wsem.at[s]))
        return copies

    @pl.when(jnp.logical_and(i == 0, n_used > 0))
    def _():
        for cp in weight_copies(expert, 0):
            cp.start(priority=1)
        for t in range(ahead):
            _start_row_gather(x_hbm, row_tok_ref, min(t, n_tiles - 1) * tm, tm,
                              lambda r, t=t: xbuf.at[t, pl.ds(pl.multiple_of(r * tok_rows, tok_rows), tok_rows)],
                              sem.at[t], tok_rows)

    @pl.when(i < n_used)
    def _():
        @pl.when(first == 1)
        def _():
            for cp in weight_copies(expert, wslot):
                cp.wait()

            @pl.when(next_expert != expert)
            def _():
                for cp in weight_copies(next_expert, 1 - wslot):
                    cp.start(priority=1)

            wg_bf[...] = wg_f[wslot].astype(BF16)
            wu_bf[...] = wu_f[wslot].astype(BF16)
            wd_bf[...] = wd_f[wslot].astype(BF16)

        wait_slot(slot)
        xn = jnp.concatenate([xbuf[slot, pl.ds(j, tm, stride=tok_rows), :].astype(BF16)
                              for j in range(tok_rows)], axis=1)
        base = jnp.minimum(i + ahead, n_tiles - 1) * tm
        nslot = (i + ahead) % GATHER_BUFS
        for r in range(tm):
            _row_copy(x_hbm, row_tok_ref[base + r], xbuf.at[nslot, pl.ds(r * tok_rows, tok_rows)],
                      sem.at[nslot], tok_rows).start()
        hg = jnp.dot(xn, wg_bf[...], preferred_element_type=F32)
        hu = jnp.dot(xn, wu_bf[...], preferred_element_type=F32)
        hh = (hg * jax.nn.sigmoid(hg)) * hu
        o_ref[...] = jnp.dot(hh.astype(BF16), wd_bf[...], preferred_element_type=F32)

    @pl.when(i >= n_used)
    def _():
        @pl.when(jnp.logical_and(i < n_used + ahead, n_used > 0))
        def _():
            wait_slot(slot)
        o_ref[...] = jnp.zeros(o_ref.shape, o_ref.dtype)

    @pl.when(i == n_tiles - 1)
    def _():
        for d in range(1, ahead + 1):
            @pl.when(n_used >= n_tiles - ahead + d)
            def _(d=d):
                wait_slot((n_tiles - 1 + d) % GATHER_BUFS)


def grouped_expert_mlp(xt, tile_info, row_tok, w_gate, w_up, w_down, layer, n_tiles, tm):
    _, D, F = w_gate.shape
    tok_rows = D // LANES
    layer_idx, depth = layer
    off = layer_idx * (w_gate.shape[0] // depth)
    any_spec = pl.BlockSpec(memory_space=pl.ANY)
    grid_spec = pltpu.PrefetchScalarGridSpec(
        num_scalar_prefetch=2,
        grid=(n_tiles,),
        in_specs=[any_spec, any_spec, any_spec, any_spec],
        out_specs=pl.BlockSpec((tm, D), lambda i, ti, rt: (i, 0)),
        scratch_shapes=[pltpu.VMEM((GATHER_BUFS, tm * tok_rows, LANES), F32),
                        pltpu.SemaphoreType.DMA((GATHER_BUFS,)),
                        pltpu.VMEM((2, D, F), F32), pltpu.VMEM((2, D, F), F32), pltpu.VMEM((2, F, D), F32),
                        pltpu.SemaphoreType.DMA((2,)),
                        pltpu.VMEM((D, F), BF16), pltpu.VMEM((D, F), BF16), pltpu.VMEM((F, D), BF16)],
    )
    return pl.pallas_call(
        functools.partial(_gmm_kernel, tm=tm, n_tiles=n_tiles, w_off=off, n_chunks=WEIGHT_DMA_CHUNKS,
                          tok_rows=tok_rows),
        grid_spec=grid_spec,
        out_shape=jax.ShapeDtypeStruct((n_tiles * tm, D), F32),
        compiler_params=_params(("arbitrary",), 56),
        name="grouped_expert_mlp",
    )(tile_info, row_tok, xt, w_gate, w_up, w_down)


def _combine_kernel(pos_ref, x_ref, plan_ref, y_hbm, o_ref, ybuf, sem, *, tc, top_k):
    i = pl.program_id(0)
    n = pl.num_programs(0)
    slot = i % 2

    def start(tile, s):
        base = tile * (tc * top_k)
        for r in range(tc):
            for k in range(top_k):
                _row_copy(y_hbm, pos_ref[base + r * top_k + k], ybuf.at[s, k, pl.ds(r, 1)], sem.at[s]).start()

    @pl.when(i == 0)
    def _():
        start(0, 0)

    for k in range(top_k):
        _wait_row_gather(y_hbm, tc, ybuf.at[slot, k], sem.at[slot])

    @pl.when(i + 1 < n)
    def _():
        start(i + 1, 1 - slot)

    acc = x_ref[...]
    for k in range(top_k):
        acc = acc + plan_ref[:, top_k + k:top_k + k + 1] * ybuf[slot, k]
    o_ref[...] = acc


def moe_combine(x, y_sorted, plan_out, pos, tc):
    T, D = x.shape
    top_k = TOP_K_IN_GROUP
    grid_spec = pltpu.PrefetchScalarGridSpec(
        num_scalar_prefetch=1,
        grid=(T // tc,),
        in_specs=[pl.BlockSpec((tc, D), lambda i, p: (i, 0)),
                  pl.BlockSpec((tc, LANES), lambda i, p: (i, 0)),
                  pl.BlockSpec(memory_space=pl.ANY)],
        out_specs=pl.BlockSpec((tc, D), lambda i, p: (i, 0)),
        scratch_shapes=[pltpu.VMEM((2, top_k, tc, D), F32), pltpu.SemaphoreType.DMA((2,))],
    )
    return pl.pallas_call(
        functools.partial(_combine_kernel, tc=tc, top_k=top_k),
        grid_spec=grid_spec,
        out_shape=jax.ShapeDtypeStruct((T, D), F32),
        compiler_params=_params(("arbitrary",), 32),
        name="moe_combine",
    )(pos, x, plan_out, y_sorted)


def hier_moe_residual(x, logits, xt, n_groups, epg, w_gate, w_up, w_down, layer):
    plan_out, pos, tile_info, row_tok, n_tiles = routing_plan(logits, n_groups, epg, TM_EXPERT)
    y_sorted = grouped_expert_mlp(xt, tile_info, row_tok, w_gate, w_up, w_down, layer, n_tiles, TM_EXPERT)
    return moe_combine(x, y_sorted, plan_out, pos, TC_COMBINE)


def _pad_head_cols(w, heads, real):
    r = w.shape[0]
    return jnp.pad(w.reshape(r, heads, real), ((0, 0), (0, 0), (0, HEAD_PAD - real))).reshape(r, heads * HEAD_PAD)


def _rope_tables(positions):
    half = QK_ROPE // 2
    inv_freq = ROPE_THETA ** (-jnp.arange(half, dtype=F32) / half)
    ang = positions.astype(F32).reshape(-1, 1) * inv_freq[None, :]
    cos, sin = jnp.cos(ang), jnp.sin(ang)
    zeros = jnp.zeros((ang.shape[0], LANES - QK_ROPE), F32)
    return (jnp.concatenate([cos, cos, zeros], axis=1),
            jnp.concatenate([-sin, sin, zeros], axis=1))


def kernel(x, mem, positions, norm_mix_g, norm_ffn_g, w_o, mem_norm_g, w_mem_kv, mem_q_norm_g, mem_k_norm_g, w_in_a, conv_w, conv_b, w_lru_r, b_lru_r, w_lru_i, b_lru_i, lru_lambda, kv_in_norm_g, w_dkv, kv_latent_norm_g, w_uk, w_uv, k_head_norm_g, w_in_b, q_latent_norm_g, w_uq, q_head_norm_g, w_router_grp, b_router_grp, w_router_exp, b_router_exp, w_exp_gate, w_exp_up, w_exp_down):
    B, S, D = x.shape
    T = B * S
    M = mem.shape[1]
    depth = norm_mix_g.shape[0]
    n_a = w_in_a.shape[0]
    lru_w = lru_lambda.shape[1]
    mem_w = w_mem_kv.shape[2] // 2
    heads = w_uv.shape[1] // V_DIM
    kv_rank = kv_latent_norm_g.shape[0]
    q_rank = q_latent_norm_g.shape[1]
    n_groups, epg = w_exp_gate.shape[1], w_exp_gate.shape[2]
    n_exp = n_groups * epg
    d_exp = w_exp_gate.shape[-1]

    xr = x.reshape(T, D)
    mem2 = mem.reshape(B * M, D)
    cos_t, sin_t = _rope_tables(positions)
    wg_all = w_exp_gate.reshape(depth * n_exp, D, d_exp)
    wu_all = w_exp_up.reshape(depth * n_exp, D, d_exp)
    wd_all = w_exp_down.reshape(depth * n_exp, d_exp, D)
    pad_r = LANES - n_groups - n_exp
    w_router = jnp.concatenate([w_router_grp, w_router_exp, jnp.zeros((depth, D, pad_r), F32)], axis=2)
    b_router = jnp.concatenate([b_router_grp, b_router_exp, jnp.zeros((depth, pad_r), F32)], axis=1)

    k_sh = v_sh = None
    for l in range(depth):
        kv_mem = norm_matmul(mem2, mem_norm_g[l], w_mem_kv, l, BF16, tm=TM_PROJ).reshape(B, M, 2 * mem_w)
        if l < n_a:
            proj = norm_matmul(xr, norm_mix_g[l], w_in_a, l, BF16, tm=TM_PROJ)
            y_mix = rglru(proj.reshape(B, S, -1), conv_w[l], conv_b[l], w_lru_r[l], b_lru_r[l],
                          w_lru_i[l], b_lru_i[l], lru_lambda[l], nb=LRU_BLOCKS_PER_STEP,
                          tc=TC_LRU).reshape(T, lru_w)
            q_col = 2 * lru_w // mem_w
        else:
            j = l - n_a
            proj = norm_matmul(xr, norm_mix_g[l], w_in_b, j, BF16, tm=TM_PROJ)
            wq = _pad_head_cols(w_uq[j], heads, QK_DIM).astype(BF16)
            qg = jnp.pad(q_head_norm_g[j] * (QK_DIM ** -0.5 * math.log2(math.e)),
                         (0, HEAD_PAD - QK_DIM)).reshape(1, HEAD_PAD)
            q = latent_heads(proj, q_rank, q_rank, q_latent_norm_g[j], wq, qg, cos_t, sin_t, heads, tm=TM_HEADS)
            y_mix = flash_attention(q.reshape(B, S, -1), k_sh, v_sh, heads, tq=TQ_FLASH).reshape(T, heads * V_DIM)
            q_col = q_rank // mem_w
        y_mem = mem_attention(proj, q_col, kv_mem, mem_q_norm_g[l], mem_k_norm_g[l], B, ts=TS_MEM)
        xr, logits, xt = out_proj(y_mix, y_mem, w_o, l, xr, norm_ffn_g[l], w_router[l],
                                  b_router[l].reshape(1, LANES), tm=TM_PROJ)
        xr = hier_moe_residual(xr, logits, xt, n_groups, epg, wg_all, wu_all, wd_all, (l, depth))
        if l == n_a - 1:
            lat_cols = kv_rank + LANES
            w_dkv_pad = jnp.pad(w_dkv, ((0, 0), (0, lat_cols - w_dkv.shape[1])))[None]
            ckv = norm_matmul(xr, kv_in_norm_g, w_dkv_pad, 0, F32, tm=TM_PROJ)
            eye = jnp.pad(jnp.eye(QK_ROPE, dtype=F32), ((0, LANES - QK_ROPE), (0, 0)))
            eye3 = jnp.broadcast_to(eye[:, None, :], (LANES, heads, QK_ROPE))
            rope_rows = jnp.pad(eye3, ((0, 0), (0, 0), (QK_NOPE, HEAD_PAD - QK_DIM))).reshape(LANES, heads * HEAD_PAD)
            wk = jnp.concatenate([_pad_head_cols(w_uk, heads, QK_NOPE), rope_rows], axis=0).astype(BF16)
            kg = jnp.pad(k_head_norm_g, (0, HEAD_PAD - QK_DIM)).reshape(1, HEAD_PAD)
            k_flat, v_flat = latent_heads(ckv, lat_cols, kv_rank, kv_latent_norm_g, wk, kg, cos_t, sin_t,
                                          heads, tm=TM_HEADS, w_v=w_uv.astype(BF16))
            k_sh = k_flat.reshape(B, S, -1)
            v_sh = v_flat.reshape(B, S, -1)
    return xr.reshape(B, S, D)
```

```python
import functools
import math

import jax
import jax.numpy as jnp
from jax import lax
from jax.experimental import pallas as pl
from jax.experimental.pallas import tpu as pltpu

F32 = jnp.float32
BF16 = jnp.bfloat16
EPS = 1e-6

MEM_HEADS = 4
CONV_WIDTH = 4
LRU_C = 8.0
QK_NOPE = 128
QK_ROPE = 64
QK_DIM = QK_NOPE + QK_ROPE
V_DIM = 128
ROPE_THETA = 10000.0
TOP_K_IN_GROUP = 2

LANES = 128
HEAD_PAD = 2 * LANES
MIB = 1024 * 1024
DMA_PRIORITIES = 2

TM_PROJ = 512
TN_PROJ = 512
TS_MEM = 512
TQ_FLASH = 512
SUB_FLASH = 256
TM_HEADS = 512
TM_EXPERT = 256
WEIGHT_DMA_CHUNKS = 4
GATHER_BUFS = 3
TC_COMBINE = 256
LRU_BLOCKS_PER_STEP = 3
TC_LRU = 256
PLAN_CHUNK = 256
N_TILE_FIELDS = 5


def _params(sem, vmem_mib):
    return pltpu.CompilerParams(dimension_semantics=sem, vmem_limit_bytes=vmem_mib * MIB)


def _rms_rows(x, g):
    x = x.astype(F32)
    ms = jnp.mean(x * x, axis=-1, keepdims=True)
    return x * lax.rsqrt(ms + EPS) * g


def _col_tile(n):
    return TN_PROJ if n % TN_PROJ == 0 else n


def _stage_weight_bf16(w_hbm, layer, w_bf, stage, sem, tn):
    n = w_bf.shape[1] // tn

    def copy(j):
        return pltpu.make_async_copy(w_hbm.at[layer, :, pl.ds(j * tn, tn)], stage.at[j % 2], sem.at[j % 2])

    copy(0).start()
    for j in range(n):
        if j + 1 < n:
            copy(j + 1).start()
        copy(j).wait()
        w_bf[:, j * tn:(j + 1) * tn] = stage[j % 2].astype(BF16)


def _norm_mm_kernel(x_ref, g_ref, w_hbm, o_ref, xn_ref, w_bf, stage, sem, *, layer, tm, chunk, tn):
    @pl.when(pl.program_id(0) == 0)
    def _():
        _stage_weight_bf16(w_hbm, layer, w_bf, stage, sem, tn)

    for r0 in range(0, tm, chunk):
        xn_ref[r0:r0 + chunk, :] = _rms_rows(x_ref[r0:r0 + chunk, :], g_ref[...]).astype(BF16)

    def cols(j, carry):
        c0 = pl.multiple_of(j * tn, tn)
        o_ref[:, pl.ds(c0, tn)] = jnp.dot(xn_ref[...], w_bf[:, pl.ds(c0, tn)],
                                          preferred_element_type=F32).astype(o_ref.dtype)
        return carry
    lax.fori_loop(0, o_ref.shape[1] // tn, cols, 0)


def norm_matmul(x, g, w, layer, out_dtype, tm):
    T, K = x.shape
    N = w.shape[2]
    tm = min(tm, T)
    chunk = min(tm, 256)
    tn = _col_tile(N)
    return pl.pallas_call(
        functools.partial(_norm_mm_kernel, layer=layer, tm=tm, chunk=chunk, tn=tn),
        grid=(T // tm,),
        in_specs=[pl.BlockSpec((tm, K), lambda i: (i, 0)),
                  pl.BlockSpec((1, K), lambda i: (0, 0)),
                  pl.BlockSpec(memory_space=pl.ANY)],
        out_specs=pl.BlockSpec((tm, N), lambda i: (i, 0)),
        out_shape=jax.ShapeDtypeStruct((T, N), out_dtype),
        scratch_shapes=[pltpu.VMEM((tm, K), BF16), pltpu.VMEM((K, N), BF16), pltpu.VMEM((2, K, tn), F32),
                        pltpu.SemaphoreType.DMA((2,))],
        compiler_params=_params(("arbitrary",), 56),
        name="norm_matmul",
    )(x, g.reshape(1, K), w)


def _out_proj_kernel(ya_ref, yb_ref, w_hbm, x_ref, g_ref, wr_ref, br_ref, o_ref, lg_ref, xt_ref,
                     w_bf, stage, sem, *, layer, tm, tn):
    @pl.when(pl.program_id(0) == 0)
    def _():
        _stage_weight_bf16(w_hbm, layer, w_bf, stage, sem, tn)

    wa_rows = ya_ref.shape[1]

    def cols(j, carry):
        c0 = pl.multiple_of(j * tn, tn)
        acc = jnp.dot(ya_ref[...], w_bf[:wa_rows, pl.ds(c0, tn)], preferred_element_type=F32)
        acc += jnp.dot(yb_ref[...], w_bf[wa_rows:, pl.ds(c0, tn)], preferred_element_type=F32)
        o_ref[:, pl.ds(c0, tn)] = x_ref[:, pl.ds(c0, tn)] + acc
        return carry
    lax.fori_loop(0, o_ref.shape[1] // tn, cols, 0)
    _router_rows(o_ref, g_ref, wr_ref, br_ref, lg_ref, xt_ref, tm, min(tm, 256))


def out_proj(y_mix, y_mem, w_o, layer, x, g_ffn, w_router, b_router, tm):
    T, Wa = y_mix.shape
    Wb = y_mem.shape[1]
    _, K, N = w_o.shape
    assert Wa + Wb == K
    tn = _col_tile(N)
    lane_tiles = N // LANES
    return pl.pallas_call(
        functools.partial(_out_proj_kernel, layer=layer, tm=tm, tn=tn),
        grid=(T // tm,),
        in_specs=[pl.BlockSpec((tm, Wa), lambda i: (i, 0)),
                  pl.BlockSpec((tm, Wb), lambda i: (i, 0)),
                  pl.BlockSpec(memory_space=pl.ANY),
                  pl.BlockSpec((tm, N), lambda i: (i, 0)),
                  pl.BlockSpec((1, N), lambda i: (0, 0)),
                  pl.BlockSpec((N, LANES), lambda i: (0, 0)),
                  pl.BlockSpec((1, LANES), lambda i: (0, 0))],
        out_specs=(pl.BlockSpec((tm, N), lambda i: (i, 0)),
                   pl.BlockSpec((tm, LANES), lambda i: (i, 0)),
                   pl.BlockSpec((tm * lane_tiles, LANES), lambda i: (i, 0))),
        out_shape=(jax.ShapeDtypeStruct((T, N), F32),
                   jax.ShapeDtypeStruct((T, LANES), F32),
                   jax.ShapeDtypeStruct((T * lane_tiles, LANES), F32)),
        scratch_shapes=[pltpu.VMEM((K, N), BF16), pltpu.VMEM((2, K, tn), F32), pltpu.SemaphoreType.DMA((2,))],
        compiler_params=_params(("arbitrary",), 56),
        name="out_proj",
    )(y_mix, y_mem, w_o, x, g_ffn.reshape(1, N), w_router, b_router)


def _mem_attn_kernel(q_ref, kv_ref, qg_ref, kg_ref, o_ref, *, heads, hd):
    width = heads * hd
    scale = hd ** -0.5
    for h in range(heads):
        qn = _rms_rows(q_ref[:, h * hd:(h + 1) * hd], qg_ref[...]) * scale
        kn = _rms_rows(kv_ref[0, :, h * hd:(h + 1) * hd], kg_ref[...])
        v = kv_ref[0, :, width + h * hd:width + (h + 1) * hd]
        s = lax.dot_general(qn.astype(BF16), kn.astype(BF16), (((1,), (1,)), ((), ())),
                            preferred_element_type=F32)
        m = jnp.max(s, axis=-1, keepdims=True)
        p = jnp.exp(s - m)
        l = jnp.sum(p, axis=-1, keepdims=True)
        o = jnp.dot(p.astype(BF16), v, preferred_element_type=F32) / l
        o_ref[:, h * hd:(h + 1) * hd] = o.astype(o_ref.dtype)


def mem_attention(proj, q_col_block, kv, q_g, k_g, batch, ts):
    T = proj.shape[0]
    _, M, two_w = kv.shape
    width = two_w // 2
    hd = width // MEM_HEADS
    per_b = T // batch // ts
    return pl.pallas_call(
        functools.partial(_mem_attn_kernel, heads=MEM_HEADS, hd=hd),
        grid=(batch, per_b),
        in_specs=[pl.BlockSpec((ts, width), lambda b, i: (b * per_b + i, q_col_block)),
                  pl.BlockSpec((1, M, two_w), lambda b, i: (b, 0, 0)),
                  pl.BlockSpec((1, hd), lambda b, i: (0, 0)),
                  pl.BlockSpec((1, hd), lambda b, i: (0, 0))],
        out_specs=pl.BlockSpec((ts, width), lambda b, i: (b * per_b + i, 0)),
        out_shape=jax.ShapeDtypeStruct((T, width), BF16),
        compiler_params=_params(("parallel", "arbitrary"), 32),
        name="mem_attention",
    )(proj, kv, q_g.reshape(1, hd), k_g.reshape(1, hd))


def _gelu_tanh(x):
    return 0.5 * x * (1.0 + jnp.tanh(0.7978845608028654 * (x + 0.044715 * x * x * x)))


def _rglru_kernel(u_ref, gb_ref, cw_ref, cb_ref, wr_ref, wi_ref, br_ref, bi_ref, lam_ref,
                  o_ref, a_s, b_s, *, seq, nb, blk, tc):
    wb = nb * blk
    neg_lam = -lam_ref[...]
    softplus = jnp.maximum(neg_lam, 0.0) + jnp.log1p(jnp.exp(-jnp.abs(neg_lam)))

    def gates(c, carry):
        t0 = pl.multiple_of(c * tc, tc)
        cur = u_ref[0, pl.ds(t0, tc), :].astype(F32)
        p0 = pl.multiple_of(jnp.maximum(t0 - 16, 0), 16)
        prev = u_ref[0, pl.ds(p0, 16), :].astype(F32)
        prev = jnp.where(c > 0, prev, 0.0)
        full = jnp.concatenate([prev, cur], axis=0)
        y = cb_ref[...] + cw_ref[0:1, :] * cur
        for k in range(1, CONV_WIDTH):
            y = y + cw_ref[k:k + 1, :] * full[16 - k:16 - k + tc, :]
        r_parts, i_parts = [], []
        for n in range(nb):
            yb = y[:, n * blk:(n + 1) * blk].astype(BF16)
            r_parts.append(jnp.dot(yb, wr_ref[n].astype(BF16), preferred_element_type=F32))
            i_parts.append(jnp.dot(yb, wi_ref[n].astype(BF16), preferred_element_type=F32))
        r = jax.nn.sigmoid(jnp.concatenate(r_parts, axis=1) + br_ref[...])
        ig = jax.nn.sigmoid(jnp.concatenate(i_parts, axis=1) + bi_ref[...])
        log_a = (-LRU_C) * r * softplus
        a = jnp.exp(log_a)
        a_s[pl.ds(t0, tc), :] = a
        b_s[pl.ds(t0, tc), :] = jnp.sqrt(-jnp.tanh(log_a) * (a * a + 1.0)) * (ig * y)
        return carry

    lax.fori_loop(0, seq // tc, gates, 0)

    row = lax.broadcasted_iota(jnp.int32, (8, wb), 0)

    def scan(c, h):
        t0 = pl.multiple_of(c * 8, 8)
        a = a_s[pl.ds(t0, 8), :]
        b = b_s[pl.ds(t0, 8), :]
        for s in (1, 2, 4):
            a_sh = pltpu.roll(a, s, axis=0)
            b_sh = pltpu.roll(b, s, axis=0)
            keep = row >= s
            b = jnp.where(keep, a * b_sh + b, b)
            a = jnp.where(keep, a * a_sh, a)
        hc = a * h + b
        b_s[pl.ds(t0, 8), :] = hc
        return hc[7:8, :]

    lax.fori_loop(0, seq // 8, scan, jnp.zeros((1, wb), F32), unroll=4)

    def gate_out(c, carry):
        t0 = pl.multiple_of(c * tc, tc)
        g = gb_ref[0, pl.ds(t0, tc), :].astype(F32)
        o_ref[0, pl.ds(t0, tc), :] = (_gelu_tanh(g) * b_s[pl.ds(t0, tc), :]).astype(o_ref.dtype)
        return carry

    lax.fori_loop(0, seq // tc, gate_out, 0)


def rglru(proj, conv_w, conv_b, w_r, b_r, w_i, b_i, lam, nb, tc):
    B, S, _ = proj.shape
    W = lam.shape[0]
    n_blocks, blk, _ = w_r.shape
    wb = nb * blk
    ncb = W // wb
    vec = lambda: pl.BlockSpec((1, wb), lambda b, j: (0, j))
    return pl.pallas_call(
        functools.partial(_rglru_kernel, seq=S, nb=nb, blk=blk, tc=tc),
        grid=(B, ncb),
        in_specs=[pl.BlockSpec((1, S, wb), lambda b, j: (b, 0, j)),
                  pl.BlockSpec((1, S, wb), lambda b, j: (b, 0, ncb + j)),
                  pl.BlockSpec((CONV_WIDTH, wb), lambda b, j: (0, j)),
                  vec(),
                  pl.BlockSpec((nb, blk, blk), lambda b, j: (j, 0, 0)),
                  pl.BlockSpec((nb, blk, blk), lambda b, j: (j, 0, 0)),
                  vec(), vec(), vec()],
        out_specs=pl.BlockSpec((1, S, wb), lambda b, j: (b, 0, j)),
        out_shape=jax.ShapeDtypeStruct((B, S, W), BF16),
        scratch_shapes=[pltpu.VMEM((S, wb), F32), pltpu.VMEM((S, wb), F32)],
        compiler_params=_params(("parallel", "arbitrary"), 32),
        name="rglru",
    )(proj, proj, conv_w, conv_b.reshape(1, W), w_r, w_i, b_r.reshape(1, W), b_i.reshape(1, W),
      lam.reshape(1, W))


def _heads_kernel(*refs, rank, heads, with_v):
    if with_v:
        lat_ref, gl_ref, w_ref, hg_ref, cos_ref, sin_ref, wv_ref, o_ref, v_ref = refs
    else:
        lat_ref, gl_ref, w_ref, hg_ref, cos_ref, sin_ref, o_ref = refs
    lat = lat_ref[...].astype(F32)
    cn = _rms_rows(lat[:, :rank], gl_ref[...])
    full = cn if lat.shape[1] == rank else jnp.concatenate([cn, lat[:, rank:]], axis=1)
    full = full.astype(BF16)
    lane = lax.broadcasted_iota(jnp.int32, cos_ref.shape, 1)
    half = QK_ROPE // 2
    for h in range(heads):
        t = jnp.dot(full, w_ref[:, h * HEAD_PAD:(h + 1) * HEAD_PAD], preferred_element_type=F32)
        ss = jnp.sum(t * t, axis=-1, keepdims=True) * (1.0 / QK_DIM)
        tn = t * lax.rsqrt(ss + EPS) * hg_ref[...]
        rp = tn[:, QK_NOPE:]
        swapped = jnp.where(lane < half, pltpu.roll(rp, LANES - half, axis=1),
                            pltpu.roll(rp, half, axis=1))
        rot = rp * cos_ref[...] + swapped * sin_ref[...]
        o_ref[:, h * HEAD_PAD:h * HEAD_PAD + QK_NOPE] = tn[:, :QK_NOPE].astype(o_ref.dtype)
        o_ref[:, h * HEAD_PAD + QK_NOPE:(h + 1) * HEAD_PAD] = rot.astype(o_ref.dtype)
    if with_v:
        v = jnp.dot(cn.astype(BF16), wv_ref[...], preferred_element_type=F32).astype(v_ref.dtype)
        ones = jnp.ones((v.shape[0], HEAD_PAD - V_DIM), v_ref.dtype)
        for h in range(heads):
            v_ref[:, h * HEAD_PAD:h * HEAD_PAD + V_DIM] = v[:, h * V_DIM:(h + 1) * V_DIM]
            v_ref[:, h * HEAD_PAD + V_DIM:(h + 1) * HEAD_PAD] = ones


def latent_heads(lat, lat_cols, rank, g_lat, w_pad, head_gain, cos_t, sin_t, heads, tm, w_v=None):
    T = lat.shape[0]
    n_out = heads * HEAD_PAD
    in_specs = [pl.BlockSpec((tm, lat_cols), lambda i: (i, 0)),
                pl.BlockSpec((1, rank), lambda i: (0, 0)),
                pl.BlockSpec((lat_cols, n_out), lambda i: (0, 0)),
                pl.BlockSpec((1, HEAD_PAD), lambda i: (0, 0)),
                pl.BlockSpec((tm, LANES), lambda i: (i, 0)),
                pl.BlockSpec((tm, LANES), lambda i: (i, 0))]
    args = [lat, g_lat.reshape(1, rank), w_pad, head_gain, cos_t, sin_t]
    out_shape = jax.ShapeDtypeStruct((T, n_out), BF16)
    out_specs = pl.BlockSpec((tm, n_out), lambda i: (i, 0))
    if w_v is not None:
        in_specs.append(pl.BlockSpec(w_v.shape, lambda i: (0, 0)))
        args.append(w_v)
        out_shape = (out_shape, jax.ShapeDtypeStruct((T, n_out), BF16))
        out_specs = (out_specs, pl.BlockSpec((tm, n_out), lambda i: (i, 0)))
    return pl.pallas_call(
        functools.partial(_heads_kernel, rank=rank, heads=heads, with_v=w_v is not None),
        grid=(T // tm,),
        in_specs=in_specs, out_specs=out_specs, out_shape=out_shape,
        compiler_params=_params(("parallel",), 48),
        name="latent_heads_kv" if w_v is not None else "latent_heads_q",
    )(*args)


def _flash_kernel(q_ref, k_ref, v_ref, o_ref, s_s, m_s, acc_s, *, tq, sub, n_kt):
    i = pl.program_id(2)
    m_s[...] = jnp.full(m_s.shape, -1e30, F32)
    acc_s[...] = jnp.zeros(acc_s.shape, F32)
    n_sub = tq // sub

    def scores(kt, slot):
        k = k_ref[0, pl.ds(kt * tq, tq), :]
        for h in range(n_sub):
            rows = slice(h * sub, (h + 1) * sub)
            s_s[slot, rows, :] = lax.dot_general(q_ref[0, rows, :], k, (((1,), (1,)), ((), ())),
                                                 preferred_element_type=F32)

    def softmax_pv(k0, slot, masked):
        for h in range(n_sub):
            rows = slice(h * sub, (h + 1) * sub)
            nk = (h + 1) * sub if masked else tq
            s = s_s[slot, rows, :nk]
            if masked:
                r = lax.broadcasted_iota(jnp.int32, s.shape, 0) + h * sub
                c = lax.broadcasted_iota(jnp.int32, s.shape, 1)
                s = jnp.where(c <= r, s, -1e30)
            v = v_ref[0, pl.ds(k0, nk), :]
            m_prev = m_s[rows, :]
            m_new = jnp.maximum(m_prev, jnp.max(s, axis=-1, keepdims=True))
            alpha = jnp.exp2(m_prev - m_new)
            p = jnp.exp2(s - jnp.concatenate([m_new] * (nk // LANES), axis=1))
            acc_s[rows, :] = (jnp.concatenate([alpha, alpha], axis=1) * acc_s[rows, :]
                              + jnp.dot(p.astype(BF16), v, preferred_element_type=F32))
            m_s[rows, :] = m_new

    scores(0, 0)
    for kt in range(n_kt - 1):
        @pl.when(kt < i)
        def _(kt=kt):
            scores(kt + 1, (kt + 1) % 2)
            softmax_pv(kt * tq, kt % 2, False)
    softmax_pv(pl.multiple_of(i * tq, tq), i % 2, True)
    o_ref[0] = (acc_s[:, :V_DIM] / acc_s[:, V_DIM:]).astype(o_ref.dtype)


def flash_attention(q, k, v, heads, tq):
    B, S, _ = q.shape
    return pl.pallas_call(
        functools.partial(_flash_kernel, tq=tq, sub=SUB_FLASH, n_kt=S // tq),
        grid=(B, heads, S // tq),
        in_specs=[pl.BlockSpec((1, tq, HEAD_PAD), lambda b, h, i: (b, i, h)),
                  pl.BlockSpec((1, S, HEAD_PAD), lambda b, h, i: (b, 0, h)),
                  pl.BlockSpec((1, S, HEAD_PAD), lambda b, h, i: (b, 0, h))],
        out_specs=pl.BlockSpec((1, tq, V_DIM), lambda b, h, i: (b, i, h)),
        out_shape=jax.ShapeDtypeStruct((B, S, heads * V_DIM), BF16),
        scratch_shapes=[pltpu.VMEM((2, tq, tq), F32), pltpu.VMEM((tq, LANES), F32),
                        pltpu.VMEM((tq, HEAD_PAD), F32)],
        compiler_params=_params(("parallel", "parallel", "arbitrary"), 32),
        name="flash_attention",
    )(q, k, v)


def _router_rows(x_ref, g_ref, w_ref, b_ref, o_ref, xt_ref, tm, chunk):
    w = w_ref[...]
    w_hi = w.astype(BF16)
    w_lo = (w - w_hi.astype(F32)).astype(BF16)
    lane_tiles = x_ref.shape[1] // LANES
    for r0 in range(0, tm, chunk):
        xn = _rms_rows(x_ref[r0:r0 + chunk, :], g_ref[...])
        x_hi = xn.astype(BF16)
        x_lo = (xn - x_hi.astype(F32)).astype(BF16)
        acc = jnp.dot(x_hi, w_hi, preferred_element_type=F32)
        acc += jnp.dot(x_lo, w_hi, preferred_element_type=F32)
        acc += jnp.dot(x_hi, w_lo, preferred_element_type=F32)
        o_ref[r0:r0 + chunk, :] = acc + b_ref[...]
        for j in range(lane_tiles):
            xt_ref[pl.ds(r0 * lane_tiles + j, chunk, stride=lane_tiles), :] = xn[:, j * LANES:(j + 1) * LANES]


def _plan_kernel(lg_ref, out_ref, tiles_ref, row_tok_ref, c_s, info_s, inv_s, *, n_tok, n_groups, epg, tm, chunk):
    n_exp = n_groups * epg
    lane = lax.broadcasted_iota(jnp.int32, (chunk, LANES), 1).astype(F32)
    rr = lax.broadcasted_iota(jnp.int32, (chunk, chunk), 0)
    cc = lax.broadcasted_iota(jnp.int32, (chunk, chunk), 1)
    tri = jnp.where(cc <= rr, 1.0, 0.0).astype(BF16)
    neg_inf = -jnp.inf

    def first_argmax(vals, vmax):
        return jnp.min(jnp.where(vals == vmax, lane, float(LANES)), axis=1, keepdims=True)

    def decide(c, carry):
        r0 = pl.multiple_of(c * chunk, chunk)
        lg = lg_ref[pl.ds(r0, chunk), :]
        gl = jnp.where(lane < n_groups, lg, neg_inf)
        gmax = jnp.max(gl, axis=1, keepdims=True)
        g_idx = first_argmax(gl, gmax)
        p_top = 1.0 / jnp.sum(jnp.exp(gl - gmax), axis=1, keepdims=True)
        lo = n_groups + g_idx * epg
        el = jnp.where(jnp.logical_and(lane >= lo, lane < lo + epg), lg, neg_inf)
        l1 = jnp.max(el, axis=1, keepdims=True)
        i1 = first_argmax(el, l1)
        el2 = jnp.where(lane == i1, neg_inf, el)
        l2 = jnp.max(el2, axis=1, keepdims=True)
        i2 = first_argmax(el2, l2)
        d = jnp.exp(l2 - l1)
        w1 = 1.0 / (1.0 + d)
        e1 = i1 - n_groups
        e2 = i2 - n_groups
        onehot = jnp.where(jnp.logical_or(lane == e1, lane == e2), 1.0, 0.0)
        cs = jnp.dot(tri, onehot.astype(BF16), preferred_element_type=F32) + carry
        c_s[pl.ds(r0, chunk), :] = cs
        info_s[pl.ds(r0, chunk), :] = jnp.where(
            lane == 0, e1, jnp.where(lane == 1, e2, jnp.where(
                lane == 2, p_top * w1, jnp.where(lane == 3, p_top * (d * w1), 0.0))))
        return cs[chunk - 1:chunk, :]

    counts = lax.fori_loop(0, n_tok // chunk, decide, jnp.zeros((1, LANES), F32))

    tiles = jnp.floor((counts + (tm - 1)) * (1.0 / tm))
    jj = lax.broadcasted_iota(jnp.int32, (LANES, LANES), 0)
    ee = lax.broadcasted_iota(jnp.int32, (LANES, LANES), 1)
    upper = jnp.where(jj <= ee, 1.0, 0.0).astype(BF16)
    tile_end = jnp.dot(jnp.broadcast_to(tiles, (8, LANES)).astype(BF16), upper,
                       preferred_element_type=F32)[0:1, :]
    tile_start = tile_end - tiles
    row_start = tile_start * tm

    inv_s[...] = jnp.zeros(inv_s.shape, F32)
    off_lane = lax.broadcasted_iota(jnp.int32, (chunk, tm), 1).astype(F32)
    tile_row = lax.broadcasted_iota(jnp.int32, (LANES, chunk), 0).astype(F32)
    tok_local = lax.broadcasted_iota(jnp.int32, (1, chunk), 1)

    def place(c, carry):
        r0 = pl.multiple_of(c * chunk, chunk)
        info = info_s[pl.ds(r0, chunk), :]
        base = row_start + c_s[pl.ds(r0, chunk), :] - 1.0
        pos = [jnp.sum(jnp.where(lane == info[:, k:k + 1], base, 0.0), axis=1, keepdims=True)
               for k in range(TOP_K_IN_GROUP)]
        packed = jnp.where(lane == 0, pos[0], jnp.where(lane == 1, pos[1], info))
        out_ref[pl.ds(r0, chunk), :] = packed
        pos_rows = jnp.transpose(packed)
        tok = r0 + tok_local
        tok_hi = lax.shift_right_logical(tok, LANES.bit_length() - 1).astype(F32)
        tok_lo = (tok & (LANES - 1)).astype(F32)
        for k in range(TOP_K_IN_GROUP):
            tile_of = jnp.floor(pos[k] * (1.0 / tm))
            onehot_off = jnp.where(off_lane == pos[k] - tile_of * tm, 1.0, 0.0).astype(BF16)
            sel = tile_row == jnp.floor(pos_rows[k:k + 1, :] * (1.0 / tm))
            ids = jnp.concatenate([jnp.where(sel, tok_hi, 0.0), jnp.where(sel, tok_lo, 0.0)], axis=0)
            inv_s[...] += jnp.dot(ids.astype(BF16), onehot_off, preferred_element_type=F32)
        return carry

    lax.fori_loop(0, n_tok // chunk, place, 0)
    row_tok_ref[...] = (inv_s[:LANES, :] * LANES + inv_s[LANES:, :]).astype(jnp.int32)

    ti = jj.astype(F32)
    lane_sq = ee.astype(F32)
    is_exp = lane_sq < n_exp
    n_used = tile_end[:, n_exp - 1:n_exp]
    ti_c = jnp.minimum(ti, n_used - 1.0)
    tile_e = jnp.sum(jnp.where(jnp.logical_and(is_exp, tile_end <= ti_c), 1.0, 0.0), axis=1, keepdims=True)
    first = jnp.sum(jnp.where(jnp.logical_and(jnp.logical_and(is_exp, tile_start == ti), tiles > 0.0),
                              1.0, 0.0), axis=1, keepdims=True)
    nonempty = jnp.logical_and(is_exp, tiles > 0.0)
    next_e = jnp.min(jnp.where(jnp.logical_and(nonempty, lane_sq > tile_e), lane_sq, float(LANES)),
                     axis=1, keepdims=True)
    next_e = jnp.where(next_e >= LANES, tile_e, next_e)
    group = jnp.sum(jnp.where(jnp.logical_and(nonempty, lane_sq < tile_e), 1.0, 0.0), axis=1, keepdims=True)
    wslot = group - 2.0 * jnp.floor(group * 0.5)
    fields = (tile_e, first, n_used, next_e, wslot)
    assert len(fields) == N_TILE_FIELDS
    packed = jnp.zeros((LANES, LANES), F32)
    for f, val in enumerate(fields):
        packed = jnp.where(lane_sq == f, val, packed)
    tiles_ref[...] = packed.astype(jnp.int32)


def routing_plan(logits, n_groups, epg, tm):
    T = logits.shape[0]
    n_exp = n_groups * epg
    n_tiles = T * TOP_K_IN_GROUP // tm + n_exp
    assert n_tiles <= LANES and T <= LANES * LANES
    out, tiles, row_tok = pl.pallas_call(
        functools.partial(_plan_kernel, n_tok=T, n_groups=n_groups, epg=epg, tm=tm, chunk=PLAN_CHUNK),
        grid=(1,),
        in_specs=[pl.BlockSpec((T, LANES), lambda i: (0, 0))],
        out_specs=(pl.BlockSpec((T, LANES), lambda i: (0, 0)),
                   pl.BlockSpec((LANES, LANES), lambda i: (0, 0)),
                   pl.BlockSpec((LANES, tm), lambda i: (0, 0))),
        out_shape=(jax.ShapeDtypeStruct((T, LANES), F32), jax.ShapeDtypeStruct((LANES, LANES), jnp.int32),
                   jax.ShapeDtypeStruct((LANES, tm), jnp.int32)),
        scratch_shapes=[pltpu.VMEM((T, LANES), F32), pltpu.VMEM((T, LANES), F32),
                        pltpu.VMEM((2 * LANES, tm), F32)],
        compiler_params=_params(("arbitrary",), 40),
        name="routing_plan",
    )(logits)
    pos = out[:, :TOP_K_IN_GROUP].astype(jnp.int32).reshape(-1)
    tile_info = tiles[:n_tiles, :N_TILE_FIELDS].T.reshape(-1)
    return out, pos, tile_info, row_tok[:n_tiles].reshape(-1), n_tiles


def _row_copy(src_hbm, tok, dst_row, sem, rows=1):
    start = tok if rows == 1 else pl.multiple_of(tok * rows, rows)
    return pltpu.make_async_copy(src_hbm.at[pl.ds(start, rows)], dst_row, sem)


def _start_row_gather(src_hbm, idx_ref, base, n, dst_row, sem, rows=1):
    def body(r, carry):
        _row_copy(src_hbm, idx_ref[base + r], dst_row(r), sem, rows).start()
        return carry
    lax.fori_loop(0, n, body, 0, unroll=8)


def _wait_row_gather(src_hbm, n, dst_all, sem):
    pltpu.make_async_copy(src_hbm.at[pl.ds(0, n)], dst_all, sem).wait()


def _gmm_kernel(info_ref, row_tok_ref, x_hbm, wg_hbm, wu_hbm, wd_hbm, o_ref,
                xbuf, sem, wg_f, wu_f, wd_f, wsem, wg_bf, wu_bf, wd_bf, *, tm, n_tiles, w_off, n_chunks, tok_rows):
    i = pl.program_id(0)
    expert = info_ref[i]
    first = info_ref[n_tiles + i]
    n_used = info_ref[2 * n_tiles]
    next_expert = info_ref[3 * n_tiles + i]
    wslot = info_ref[4 * n_tiles + i]
    slot = i % GATHER_BUFS
    ahead = GATHER_BUFS - 1

    def wait_slot(s):
        _wait_row_gather(x_hbm, tm * tok_rows, xbuf.at[s], sem.at[s])

    def weight_copies(e, s):
        copies = []
        for hbm, buf in ((wg_hbm, wg_f), (wu_hbm, wu_f), (wd_hbm, wd_f)):
            rows = hbm.shape[1] // n_chunks
            for c in range(n_chunks):
                copies.append(pltpu.make_async_copy(hbm.at[w_off + e, pl.ds(c * rows, rows)],
                                                    buf.at[s, pl.ds(c * rows, rows)], wsem.at[s]))
        return copies

    def start_weights(e, s):
        for c, cp in enumerate(weight_copies(e, s)):
            cp.start(priority=c % DMA_PRIORITIES)

    @pl.when(jnp.logical_and(i == 0, n_used > 0))
    def _():
        start_weights(expert, 0)
        for t in range(ahead):
            _start_row_gather(x_hbm, row_tok_ref, min(t, n_tiles - 1) * tm, tm,
                              lambda r, t=t: xbuf.at[t, pl.ds(pl.multiple_of(r * tok_rows, tok_rows), tok_rows)],
                              sem.at[t], tok_rows)

    @pl.when(i < n_used)
    def _():
        @pl.when(first == 1)
        def _():
            for cp in weight_copies(expert, wslot):
                cp.wait()

            @pl.when(next_expert != expert)
            def _():
                start_weights(next_expert, 1 - wslot)

            wg_bf[...] = wg_f[wslot].astype(BF16)
            wu_bf[...] = wu_f[wslot].astype(BF16)
            wd_bf[...] = wd_f[wslot].astype(BF16)

        wait_slot(slot)
        xn = jnp.concatenate([xbuf[slot, pl.ds(j, tm, stride=tok_rows), :].astype(BF16)
                              for j in range(tok_rows)], axis=1)
        base = jnp.minimum(i + ahead, n_tiles - 1) * tm
        nslot = (i + ahead) % GATHER_BUFS
        for r in range(tm):
            _row_copy(x_hbm, row_tok_ref[base + r], xbuf.at[nslot, pl.ds(r * tok_rows, tok_rows)],
                      sem.at[nslot], tok_rows).start()
        hg = jnp.dot(xn, wg_bf[...], preferred_element_type=F32)
        hu = jnp.dot(xn, wu_bf[...], preferred_element_type=F32)
        hh = (hg * jax.nn.sigmoid(hg)) * hu
        o_ref[...] = jnp.dot(hh.astype(BF16), wd_bf[...], preferred_element_type=F32)

    @pl.when(i >= n_used)
    def _():
        @pl.when(jnp.logical_and(i < n_used + ahead, n_used > 0))
        def _():
            wait_slot(slot)
        o_ref[...] = jnp.zeros(o_ref.shape, o_ref.dtype)

    @pl.when(i == n_tiles - 1)
    def _():
        for d in range(1, ahead + 1):
            @pl.when(n_used >= n_tiles - ahead + d)
            def _(d=d):
                wait_slot((n_tiles - 1 + d) % GATHER_BUFS)


def grouped_expert_mlp(xt, tile_info, row_tok, w_gate, w_up, w_down, layer, n_tiles, tm):
    _, D, F = w_gate.shape
    tok_rows = D // LANES
    layer_idx, depth = layer
    off = layer_idx * (w_gate.shape[0] // depth)
    any_spec = pl.BlockSpec(memory_space=pl.ANY)
    grid_spec = pltpu.PrefetchScalarGridSpec(
        num_scalar_prefetch=2,
        grid=(n_tiles,),
        in_specs=[any_spec, any_spec, any_spec, any_spec],
        out_specs=pl.BlockSpec((tm, D), lambda i, ti, rt: (i, 0)),
        scratch_shapes=[pltpu.VMEM((GATHER_BUFS, tm * tok_rows, LANES), F32),
                        pltpu.SemaphoreType.DMA((GATHER_BUFS,)),
                        pltpu.VMEM((2, D, F), F32), pltpu.VMEM((2, D, F), F32), pltpu.VMEM((2, F, D), F32),
                        pltpu.SemaphoreType.DMA((2,)),
                        pltpu.VMEM((D, F), BF16), pltpu.VMEM((D, F), BF16), pltpu.VMEM((F, D), BF16)],
    )
    return pl.pallas_call(
        functools.partial(_gmm_kernel, tm=tm, n_tiles=n_tiles, w_off=off, n_chunks=WEIGHT_DMA_CHUNKS,
                          tok_rows=tok_rows),
        grid_spec=grid_spec,
        out_shape=jax.ShapeDtypeStruct((n_tiles * tm, D), F32),
        compiler_params=_params(("arbitrary",), 56),
        name="grouped_expert_mlp",
    )(tile_info, row_tok, xt, w_gate, w_up, w_down)


def _combine_kernel(pos_ref, x_ref, plan_ref, y_hbm, o_ref, ybuf, sem, *, tc, top_k):
    i = pl.program_id(0)
    n = pl.num_programs(0)
    slot = i % 2

    def start(tile, s):
        base = tile * (tc * top_k)
        for r in range(tc):
            for k in range(top_k):
                _row_copy(y_hbm, pos_ref[base + r * top_k + k], ybuf.at[s, k, pl.ds(r, 1)], sem.at[s]).start()

    @pl.when(i == 0)
    def _():
        start(0, 0)

    for k in range(top_k):
        _wait_row_gather(y_hbm, tc, ybuf.at[slot, k], sem.at[slot])

    @pl.when(i + 1 < n)
    def _():
        start(i + 1, 1 - slot)

    acc = x_ref[...]
    for k in range(top_k):
        acc = acc + plan_ref[:, top_k + k:top_k + k + 1] * ybuf[slot, k]
    o_ref[...] = acc


def moe_combine(x, y_sorted, plan_out, pos, tc):
    T, D = x.shape
    top_k = TOP_K_IN_GROUP
    grid_spec = pltpu.PrefetchScalarGridSpec(
        num_scalar_prefetch=1,
        grid=(T // tc,),
        in_specs=[pl.BlockSpec((tc, D), lambda i, p: (i, 0)),
                  pl.BlockSpec((tc, LANES), lambda i, p: (i, 0)),
                  pl.BlockSpec(memory_space=pl.ANY)],
        out_specs=pl.BlockSpec((tc, D), lambda i, p: (i, 0)),
        scratch_shapes=[pltpu.VMEM((2, top_k, tc, D), F32), pltpu.SemaphoreType.DMA((2,))],
    )
    return pl.pallas_call(
        functools.partial(_combine_kernel, tc=tc, top_k=top_k),
        grid_spec=grid_spec,
        out_shape=jax.ShapeDtypeStruct((T, D), F32),
        compiler_params=_params(("arbitrary",), 32),
        name="moe_combine",
    )(pos, x, plan_out, y_sorted)


def _combine_mm_kernel(pos_ref, x_ref, plan_ref, y_hbm, g_ref, w_hbm, xnew_ref, o_ref,
                       ybuf, ysem, xn_ref, w_bf, stage, wsem, *, layer, tc, chunk, tn, top_k):
    i = pl.program_id(0)
    n = pl.num_programs(0)
    slot = i % 2

    def start(tile, s):
        base = tile * (tc * top_k)
        for r in range(tc):
            for k in range(top_k):
                _row_copy(y_hbm, pos_ref[base + r * top_k + k], ybuf.at[s, k, pl.ds(r, 1)], ysem.at[s]).start()

    def wait(s):
        for k in range(top_k):
            _wait_row_gather(y_hbm, tc, ybuf.at[s, k], ysem.at[s])

    @pl.when(i == 0)
    def _():
        start(0, 0)
        _stage_weight_bf16(w_hbm, layer, w_bf, stage, wsem, tn)

    wait(slot)
    for r0 in range(0, tc, chunk):
        rows = slice(r0, r0 + chunk)
        acc = x_ref[rows, :]
        for k in range(top_k):
            acc = acc + plan_ref[rows, top_k + k:top_k + k + 1] * ybuf[slot, k, rows, :]
        xnew_ref[rows, :] = acc
        xn_ref[rows, :] = _rms_rows(acc, g_ref[...]).astype(BF16)
    start(jnp.minimum(i + 1, n - 1), 1 - slot)
    for j in range(o_ref.shape[1] // tn):
        o_ref[:, j * tn:(j + 1) * tn] = jnp.dot(xn_ref[...], w_bf[:, j * tn:(j + 1) * tn],
                                                preferred_element_type=F32).astype(o_ref.dtype)

    @pl.when(i == n - 1)
    def _():
        wait(1 - slot)


def combine_norm_matmul(x, y_sorted, plan_out, pos, g, w, layer, out_dtype, tc):
    T, D = x.shape
    N = w.shape[2]
    top_k = TOP_K_IN_GROUP
    tn = _col_tile(N)
    grid_spec = pltpu.PrefetchScalarGridSpec(
        num_scalar_prefetch=1,
        grid=(T // tc,),
        in_specs=[pl.BlockSpec((tc, D), lambda i, p: (i, 0)),
                  pl.BlockSpec((tc, LANES), lambda i, p: (i, 0)),
                  pl.BlockSpec(memory_space=pl.ANY),
                  pl.BlockSpec((1, D), lambda i, p: (0, 0)),
                  pl.BlockSpec(memory_space=pl.ANY)],
        out_specs=(pl.BlockSpec((tc, D), lambda i, p: (i, 0)),
                   pl.BlockSpec((tc, N), lambda i, p: (i, 0))),
        scratch_shapes=[pltpu.VMEM((2, top_k, tc, D), F32), pltpu.SemaphoreType.DMA((2,)),
                        pltpu.VMEM((tc, D), BF16), pltpu.VMEM((D, N), BF16), pltpu.VMEM((2, D, tn), F32),
                        pltpu.SemaphoreType.DMA((2,))],
    )
    return pl.pallas_call(
        functools.partial(_combine_mm_kernel, layer=layer, tc=tc, chunk=min(tc, 128), tn=tn, top_k=top_k),
        grid_spec=grid_spec,
        out_shape=(jax.ShapeDtypeStruct((T, D), F32), jax.ShapeDtypeStruct((T, N), out_dtype)),
        compiler_params=_params(("arbitrary",), 56),
        name="combine_norm_matmul",
    )(pos, x, plan_out, y_sorted, g.reshape(1, D), w)


def hier_moe_experts(logits, xt, n_groups, epg, w_gate, w_up, w_down, layer):
    plan_out, pos, tile_info, row_tok, n_tiles = routing_plan(logits, n_groups, epg, TM_EXPERT)
    y_sorted = grouped_expert_mlp(xt, tile_info, row_tok, w_gate, w_up, w_down, layer, n_tiles, TM_EXPERT)
    return y_sorted, plan_out, pos


def _pad_head_cols(w, heads, real):
    r = w.shape[0]
    return jnp.pad(w.reshape(r, heads, real), ((0, 0), (0, 0), (0, HEAD_PAD - real))).reshape(r, heads * HEAD_PAD)


def _rope_tables(positions):
    half = QK_ROPE // 2
    inv_freq = ROPE_THETA ** (-jnp.arange(half, dtype=F32) / half)
    ang = positions.astype(F32).reshape(-1, 1) * inv_freq[None, :]
    cos, sin = jnp.cos(ang), jnp.sin(ang)
    zeros = jnp.zeros((ang.shape[0], LANES - QK_ROPE), F32)
    return (jnp.concatenate([cos, cos, zeros], axis=1),
            jnp.concatenate([-sin, sin, zeros], axis=1))


def kernel(x, mem, positions, norm_mix_g, norm_ffn_g, w_o, mem_norm_g, w_mem_kv, mem_q_norm_g, mem_k_norm_g, w_in_a, conv_w, conv_b, w_lru_r, b_lru_r, w_lru_i, b_lru_i, lru_lambda, kv_in_norm_g, w_dkv, kv_latent_norm_g, w_uk, w_uv, k_head_norm_g, w_in_b, q_latent_norm_g, w_uq, q_head_norm_g, w_router_grp, b_router_grp, w_router_exp, b_router_exp, w_exp_gate, w_exp_up, w_exp_down):
    B, S, D = x.shape
    T = B * S
    M = mem.shape[1]
    depth = norm_mix_g.shape[0]
    n_a = w_in_a.shape[0]
    lru_w = lru_lambda.shape[1]
    mem_w = w_mem_kv.shape[2] // 2
    heads = w_uv.shape[1] // V_DIM
    kv_rank = kv_latent_norm_g.shape[0]
    q_rank = q_latent_norm_g.shape[1]
    n_groups, epg = w_exp_gate.shape[1], w_exp_gate.shape[2]
    n_exp = n_groups * epg
    d_exp = w_exp_gate.shape[-1]

    xr = x.reshape(T, D)
    mem2 = mem.reshape(B * M, D)
    cos_t, sin_t = _rope_tables(positions)
    wg_all = w_exp_gate.reshape(depth * n_exp, D, d_exp)
    wu_all = w_exp_up.reshape(depth * n_exp, D, d_exp)
    wd_all = w_exp_down.reshape(depth * n_exp, d_exp, D)
    pad_r = LANES - n_groups - n_exp
    w_router = jnp.concatenate([w_router_grp, w_router_exp, jnp.zeros((depth, D, pad_r), F32)], axis=2)
    b_router = jnp.concatenate([b_router_grp, b_router_exp, jnp.zeros((depth, pad_r), F32)], axis=1)

    def shared_kv(x_res):
        lat_cols = kv_rank + LANES
        w_dkv_pad = jnp.pad(w_dkv, ((0, 0), (0, lat_cols - w_dkv.shape[1])))[None]
        ckv = norm_matmul(x_res, kv_in_norm_g, w_dkv_pad, 0, F32, tm=TM_PROJ)
        eye = jnp.pad(jnp.eye(QK_ROPE, dtype=F32), ((0, LANES - QK_ROPE), (0, 0)))
        eye3 = jnp.broadcast_to(eye[:, None, :], (LANES, heads, QK_ROPE))
        rope_rows = jnp.pad(eye3, ((0, 0), (0, 0), (QK_NOPE, HEAD_PAD - QK_DIM))).reshape(LANES, heads * HEAD_PAD)
        wk = jnp.concatenate([_pad_head_cols(w_uk, heads, QK_NOPE), rope_rows], axis=0).astype(BF16)
        kg = jnp.pad(k_head_norm_g, (0, HEAD_PAD - QK_DIM)).reshape(1, HEAD_PAD)
        k_flat, v_flat = latent_heads(ckv, lat_cols, kv_rank, kv_latent_norm_g, wk, kg, cos_t, sin_t,
                                      heads, tm=TM_HEADS, w_v=w_uv.astype(BF16))
        return k_flat.reshape(B, S, -1), v_flat.reshape(B, S, -1)

    def in_proj(x_res, pending, g, w, idx):
        if pending is None:
            return x_res, norm_matmul(x_res, g, w, idx, BF16, tm=TM_PROJ)
        return combine_norm_matmul(x_res, *pending, g, w, idx, BF16, TC_COMBINE)

    k_sh = v_sh = None
    pending = None
    for l in range(depth):
        kv_mem = norm_matmul(mem2, mem_norm_g[l], w_mem_kv, l, BF16, tm=TM_PROJ).reshape(B, M, 2 * mem_w)
        if l < n_a:
            xr, proj = in_proj(xr, pending, norm_mix_g[l], w_in_a, l)
            y_mix = rglru(proj.reshape(B, S, -1), conv_w[l], conv_b[l], w_lru_r[l], b_lru_r[l],
                          w_lru_i[l], b_lru_i[l], lru_lambda[l], nb=LRU_BLOCKS_PER_STEP,
                          tc=TC_LRU).reshape(T, lru_w)
            q_col = 2 * lru_w // mem_w
        else:
            j = l - n_a
            xr, proj = in_proj(xr, pending, norm_mix_g[l], w_in_b, j)
            if l == n_a:
                k_sh, v_sh = shared_kv(xr)
            wq = _pad_head_cols(w_uq[j], heads, QK_DIM).astype(BF16)
            qg = jnp.pad(q_head_norm_g[j] * (QK_DIM ** -0.5 * math.log2(math.e)),
                         (0, HEAD_PAD - QK_DIM)).reshape(1, HEAD_PAD)
            q = latent_heads(proj, q_rank, q_rank, q_latent_norm_g[j], wq, qg, cos_t, sin_t, heads, tm=TM_HEADS)
            y_mix = flash_attention(q.reshape(B, S, -1), k_sh, v_sh, heads, tq=TQ_FLASH).reshape(T, heads * V_DIM)
            q_col = q_rank // mem_w
        y_mem = mem_attention(proj, q_col, kv_mem, mem_q_norm_g[l], mem_k_norm_g[l], B, ts=TS_MEM)
        xr, logits, xt = out_proj(y_mix, y_mem, w_o, l, xr, norm_ffn_g[l], w_router[l],
                                  b_router[l].reshape(1, LANES), tm=TM_PROJ)
        pending = hier_moe_experts(logits, xt, n_groups, epg, wg_all, wu_all, wd_all, (l, depth))
    xr = moe_combine(xr, *pending, TC_COMBINE)
    return xr.reshape(B, S, D)
```

```python
import functools
import math

import jax
import jax.numpy as jnp
from jax import lax
from jax.experimental import pallas as pl
from jax.experimental.pallas import tpu as pltpu

F32 = jnp.float32
BF16 = jnp.bfloat16
EPS = 1e-6

MEM_HEADS = 4
CONV_WIDTH = 4
LRU_C = 8.0
QK_NOPE = 128
QK_ROPE = 64
QK_DIM = QK_NOPE + QK_ROPE
V_DIM = 128
ROPE_THETA = 10000.0
TOP_K_IN_GROUP = 2

LANES = 128
HEAD_PAD = 2 * LANES
MIB = 1024 * 1024
DMA_PRIORITIES = 2

TM_PROJ = 512
TN_PROJ = 512
TS_MEM = 512
TQ_FLASH = 512
SUB_FLASH = 256
TM_HEADS = 512
TM_EXPERT = 256
WEIGHT_DMA_CHUNKS = 4
GATHER_BUFS = 3
TC_COMBINE = 256
LRU_BLOCKS_PER_STEP = 3
TC_LRU = 256
PLAN_CHUNK = 256
N_TILE_FIELDS = 6


def _params(sem, vmem_mib):
    return pltpu.CompilerParams(dimension_semantics=sem, vmem_limit_bytes=vmem_mib * MIB)


def _sigmoid(z):
    return 0.5 * jnp.tanh(0.5 * z) + 0.5


def _rms_rows(x, g):
    x = x.astype(F32)
    ms = jnp.mean(x * x, axis=-1, keepdims=True)
    return x * lax.rsqrt(ms + EPS) * g


def _col_tile(n):
    return TN_PROJ if n % TN_PROJ == 0 else n


def _stage_weight_bf16(w_hbm, layer, w_bf, stage, sem, tn):
    n = w_bf.shape[1] // tn

    def copy(j):
        return pltpu.make_async_copy(w_hbm.at[layer, :, pl.ds(j * tn, tn)], stage.at[j % 2], sem.at[j % 2])

    copy(0).start()
    for j in range(n):
        if j + 1 < n:
            copy(j + 1).start()
        copy(j).wait()
        w_bf[:, j * tn:(j + 1) * tn] = stage[j % 2].astype(BF16)


def _norm_mm_kernel(x_ref, g_ref, w_hbm, o_ref, xn_ref, w_bf, stage, sem, *, layer, tm, chunk, tn):
    @pl.when(pl.program_id(0) == 0)
    def _():
        _stage_weight_bf16(w_hbm, layer, w_bf, stage, sem, tn)

    for r0 in range(0, tm, chunk):
        xn_ref[r0:r0 + chunk, :] = _rms_rows(x_ref[r0:r0 + chunk, :], g_ref[...]).astype(BF16)

    def cols(j, carry):
        c0 = pl.multiple_of(j * tn, tn)
        o_ref[:, pl.ds(c0, tn)] = jnp.dot(xn_ref[...], w_bf[:, pl.ds(c0, tn)],
                                          preferred_element_type=F32).astype(o_ref.dtype)
        return carry
    lax.fori_loop(0, o_ref.shape[1] // tn, cols, 0)


def norm_matmul(x, g, w, layer, out_dtype, tm):
    T, K = x.shape
    N = w.shape[2]
    tm = min(tm, T)
    chunk = min(tm, 256)
    tn = _col_tile(N)
    return pl.pallas_call(
        functools.partial(_norm_mm_kernel, layer=layer, tm=tm, chunk=chunk, tn=tn),
        grid=(T // tm,),
        in_specs=[pl.BlockSpec((tm, K), lambda i: (i, 0)),
                  pl.BlockSpec((1, K), lambda i: (0, 0)),
                  pl.BlockSpec(memory_space=pl.ANY)],
        out_specs=pl.BlockSpec((tm, N), lambda i: (i, 0)),
        out_shape=jax.ShapeDtypeStruct((T, N), out_dtype),
        scratch_shapes=[pltpu.VMEM((tm, K), BF16), pltpu.VMEM((K, N), BF16), pltpu.VMEM((2, K, tn), F32),
                        pltpu.SemaphoreType.DMA((2,))],
        compiler_params=_params(("arbitrary",), 56),
        name="norm_matmul",
    )(x, g.reshape(1, K), w)


def _out_proj_kernel(ya_ref, yb_ref, w_hbm, x_ref, g_ref, wr_ref, br_ref, o_ref, lg_ref, xt_ref,
                     w_bf, stage, sem, *, layer, tm, tn):
    @pl.when(pl.program_id(0) == 0)
    def _():
        _stage_weight_bf16(w_hbm, layer, w_bf, stage, sem, tn)

    wa_rows = ya_ref.shape[1]

    def cols(j, carry):
        c0 = pl.multiple_of(j * tn, tn)
        acc = jnp.dot(ya_ref[...], w_bf[:wa_rows, pl.ds(c0, tn)], preferred_element_type=F32)
        acc += jnp.dot(yb_ref[...], w_bf[wa_rows:, pl.ds(c0, tn)], preferred_element_type=F32)
        o_ref[:, pl.ds(c0, tn)] = x_ref[:, pl.ds(c0, tn)] + acc
        return carry
    lax.fori_loop(0, o_ref.shape[1] // tn, cols, 0)
    _router_rows(o_ref, g_ref, wr_ref, br_ref, lg_ref, xt_ref, tm, min(tm, 256))


def out_proj(y_mix, y_mem, w_o, layer, x, g_ffn, w_router, b_router, tm):
    T, Wa = y_mix.shape
    Wb = y_mem.shape[1]
    _, K, N = w_o.shape
    assert Wa + Wb == K
    tn = _col_tile(N)
    lane_tiles = N // LANES
    return pl.pallas_call(
        functools.partial(_out_proj_kernel, layer=layer, tm=tm, tn=tn),
        grid=(T // tm,),
        in_specs=[pl.BlockSpec((tm, Wa), lambda i: (i, 0)),
                  pl.BlockSpec((tm, Wb), lambda i: (i, 0)),
                  pl.BlockSpec(memory_space=pl.ANY),
                  pl.BlockSpec((tm, N), lambda i: (i, 0)),
                  pl.BlockSpec((1, N), lambda i: (0, 0)),
                  pl.BlockSpec((N, LANES), lambda i: (0, 0)),
                  pl.BlockSpec((1, LANES), lambda i: (0, 0))],
        out_specs=(pl.BlockSpec((tm, N), lambda i: (i, 0)),
                   pl.BlockSpec((tm, LANES), lambda i: (i, 0)),
                   pl.BlockSpec((tm * lane_tiles, LANES), lambda i: (i, 0))),
        out_shape=(jax.ShapeDtypeStruct((T, N), F32),
                   jax.ShapeDtypeStruct((T, LANES), F32),
                   jax.ShapeDtypeStruct((T * lane_tiles, LANES), F32)),
        scratch_shapes=[pltpu.VMEM((K, N), BF16), pltpu.VMEM((2, K, tn), F32), pltpu.SemaphoreType.DMA((2,))],
        compiler_params=_params(("arbitrary",), 56),
        name="out_proj",
    )(y_mix, y_mem, w_o, x, g_ffn.reshape(1, N), w_router, b_router)


def _mem_attn_kernel(q_ref, kv_ref, qg_ref, kg_ref, o_ref, *, heads, hd):
    width = heads * hd
    scale = hd ** -0.5
    for h in range(heads):
        qn = _rms_rows(q_ref[:, h * hd:(h + 1) * hd], qg_ref[...]) * scale
        kn = _rms_rows(kv_ref[0, :, h * hd:(h + 1) * hd], kg_ref[...])
        v = kv_ref[0, :, width + h * hd:width + (h + 1) * hd]
        s = lax.dot_general(qn.astype(BF16), kn.astype(BF16), (((1,), (1,)), ((), ())),
                            preferred_element_type=F32)
        m = jnp.max(s, axis=-1, keepdims=True)
        p = jnp.exp(s - m)
        l = jnp.sum(p, axis=-1, keepdims=True)
        o = jnp.dot(p.astype(BF16), v, preferred_element_type=F32) / l
        o_ref[:, h * hd:(h + 1) * hd] = o.astype(o_ref.dtype)


def mem_attention(proj, q_col_block, kv, q_g, k_g, batch, ts):
    T = proj.shape[0]
    _, M, two_w = kv.shape
    width = two_w // 2
    hd = width // MEM_HEADS
    per_b = T // batch // ts
    return pl.pallas_call(
        functools.partial(_mem_attn_kernel, heads=MEM_HEADS, hd=hd),
        grid=(batch, per_b),
        in_specs=[pl.BlockSpec((ts, width), lambda b, i: (b * per_b + i, q_col_block)),
                  pl.BlockSpec((1, M, two_w), lambda b, i: (b, 0, 0)),
                  pl.BlockSpec((1, hd), lambda b, i: (0, 0)),
                  pl.BlockSpec((1, hd), lambda b, i: (0, 0))],
        out_specs=pl.BlockSpec((ts, width), lambda b, i: (b * per_b + i, 0)),
        out_shape=jax.ShapeDtypeStruct((T, width), BF16),
        compiler_params=_params(("parallel", "arbitrary"), 32),
        name="mem_attention",
    )(proj, kv, q_g.reshape(1, hd), k_g.reshape(1, hd))


def _gelu_tanh(x):
    return 0.5 * x * (1.0 + jnp.tanh(0.7978845608028654 * (x + 0.044715 * x * x * x)))


def _rglru_kernel(u_ref, gb_ref, cw_ref, cb_ref, wr_ref, wi_ref, br_ref, bi_ref, lam_ref,
                  o_ref, a_s, b_s, *, seq, nb, blk, tc):
    wb = nb * blk
    neg_lam = -lam_ref[...]
    softplus = jnp.maximum(neg_lam, 0.0) + jnp.log1p(jnp.exp(-jnp.abs(neg_lam)))

    def gates(c, carry):
        t0 = pl.multiple_of(c * tc, tc)
        cur = u_ref[0, pl.ds(t0, tc), :].astype(F32)
        p0 = pl.multiple_of(jnp.maximum(t0 - 16, 0), 16)
        prev = u_ref[0, pl.ds(p0, 16), :].astype(F32)
        prev = jnp.where(c > 0, prev, 0.0)
        full = jnp.concatenate([prev, cur], axis=0)
        y = cb_ref[...] + cw_ref[0:1, :] * cur
        for k in range(1, CONV_WIDTH):
            y = y + cw_ref[k:k + 1, :] * full[16 - k:16 - k + tc, :]
        r_parts, i_parts = [], []
        for n in range(nb):
            yb = y[:, n * blk:(n + 1) * blk].astype(BF16)
            r_parts.append(jnp.dot(yb, wr_ref[n].astype(BF16), preferred_element_type=F32))
            i_parts.append(jnp.dot(yb, wi_ref[n].astype(BF16), preferred_element_type=F32))
        r = _sigmoid(jnp.concatenate(r_parts, axis=1) + br_ref[...])
        ig = _sigmoid(jnp.concatenate(i_parts, axis=1) + bi_ref[...])
        log_a = (-LRU_C) * r * softplus
        a = jnp.exp(log_a)
        a_s[pl.ds(t0, tc), :] = a
        b_s[pl.ds(t0, tc), :] = jnp.sqrt(-jnp.tanh(log_a) * (a * a + 1.0)) * (ig * y)
        return carry

    lax.fori_loop(0, seq // tc, gates, 0)

    row = lax.broadcasted_iota(jnp.int32, (8, wb), 0)

    def scan(c, h):
        t0 = pl.multiple_of(c * 8, 8)
        a = a_s[pl.ds(t0, 8), :]
        b = b_s[pl.ds(t0, 8), :]
        for s in (1, 2, 4):
            a_sh = pltpu.roll(a, s, axis=0)
            b_sh = pltpu.roll(b, s, axis=0)
            keep = row >= s
            b = jnp.where(keep, a * b_sh + b, b)
            a = jnp.where(keep, a * a_sh, a)
        hc = a * h + b
        b_s[pl.ds(t0, 8), :] = hc
        return hc[7:8, :]

    lax.fori_loop(0, seq // 8, scan, jnp.zeros((1, wb), F32), unroll=4)

    def gate_out(c, carry):
        t0 = pl.multiple_of(c * tc, tc)
        g = gb_ref[0, pl.ds(t0, tc), :].astype(F32)
        o_ref[0, pl.ds(t0, tc), :] = (_gelu_tanh(g) * b_s[pl.ds(t0, tc), :]).astype(o_ref.dtype)
        return carry

    lax.fori_loop(0, seq // tc, gate_out, 0)


def rglru(proj, conv_w, conv_b, w_r, b_r, w_i, b_i, lam, nb, tc):
    B, S, _ = proj.shape
    W = lam.shape[0]
    n_blocks, blk, _ = w_r.shape
    wb = nb * blk
    ncb = W // wb
    vec = lambda: pl.BlockSpec((1, wb), lambda b, j: (0, j))
    return pl.pallas_call(
        functools.partial(_rglru_kernel, seq=S, nb=nb, blk=blk, tc=tc),
        grid=(B, ncb),
        in_specs=[pl.BlockSpec((1, S, wb), lambda b, j: (b, 0, j)),
                  pl.BlockSpec((1, S, wb), lambda b, j: (b, 0, ncb + j)),
                  pl.BlockSpec((CONV_WIDTH, wb), lambda b, j: (0, j)),
                  vec(),
                  pl.BlockSpec((nb, blk, blk), lambda b, j: (j, 0, 0)),
                  pl.BlockSpec((nb, blk, blk), lambda b, j: (j, 0, 0)),
                  vec(), vec(), vec()],
        out_specs=pl.BlockSpec((1, S, wb), lambda b, j: (b, 0, j)),
        out_shape=jax.ShapeDtypeStruct((B, S, W), BF16),
        scratch_shapes=[pltpu.VMEM((S, wb), F32), pltpu.VMEM((S, wb), F32)],
        compiler_params=_params(("parallel", "arbitrary"), 32),
        name="rglru",
    )(proj, proj, conv_w, conv_b.reshape(1, W), w_r, w_i, b_r.reshape(1, W), b_i.reshape(1, W),
      lam.reshape(1, W))


def _heads_kernel(*refs, rank, heads, with_v):
    if with_v:
        lat_ref, gl_ref, w_ref, hg_ref, cos_ref, sin_ref, wv_ref, o_ref, v_ref = refs
    else:
        lat_ref, gl_ref, w_ref, hg_ref, cos_ref, sin_ref, o_ref = refs
    lat = lat_ref[...].astype(F32)
    cn = _rms_rows(lat[:, :rank], gl_ref[...])
    full = cn if lat.shape[1] == rank else jnp.concatenate([cn, lat[:, rank:]], axis=1)
    full = full.astype(BF16)
    lane = lax.broadcasted_iota(jnp.int32, cos_ref.shape, 1)
    half = QK_ROPE // 2
    for h in range(heads):
        t = jnp.dot(full, w_ref[:, h * HEAD_PAD:(h + 1) * HEAD_PAD], preferred_element_type=F32)
        ss = jnp.sum(t * t, axis=-1, keepdims=True) * (1.0 / QK_DIM)
        tn = t * lax.rsqrt(ss + EPS) * hg_ref[...]
        rp = tn[:, QK_NOPE:]
        swapped = jnp.where(lane < half, pltpu.roll(rp, LANES - half, axis=1),
                            pltpu.roll(rp, half, axis=1))
        rot = rp * cos_ref[...] + swapped * sin_ref[...]
        o_ref[:, h * HEAD_PAD:h * HEAD_PAD + QK_NOPE] = tn[:, :QK_NOPE].astype(o_ref.dtype)
        o_ref[:, h * HEAD_PAD + QK_NOPE:(h + 1) * HEAD_PAD] = rot.astype(o_ref.dtype)
    if with_v:
        v = jnp.dot(cn.astype(BF16), wv_ref[...], preferred_element_type=F32).astype(v_ref.dtype)
        ones = jnp.ones((v.shape[0], HEAD_PAD - V_DIM), v_ref.dtype)
        for h in range(heads):
            v_ref[:, h * HEAD_PAD:h * HEAD_PAD + V_DIM] = v[:, h * V_DIM:(h + 1) * V_DIM]
            v_ref[:, h * HEAD_PAD + V_DIM:(h + 1) * HEAD_PAD] = ones


def latent_heads(lat, lat_cols, rank, g_lat, w_pad, head_gain, cos_t, sin_t, heads, tm, w_v=None):
    T = lat.shape[0]
    n_out = heads * HEAD_PAD
    in_specs = [pl.BlockSpec((tm, lat_cols), lambda i: (i, 0)),
                pl.BlockSpec((1, rank), lambda i: (0, 0)),
                pl.BlockSpec((lat_cols, n_out), lambda i: (0, 0)),
                pl.BlockSpec((1, HEAD_PAD), lambda i: (0, 0)),
                pl.BlockSpec((tm, LANES), lambda i: (i, 0)),
                pl.BlockSpec((tm, LANES), lambda i: (i, 0))]
    args = [lat, g_lat.reshape(1, rank), w_pad, head_gain, cos_t, sin_t]
    out_shape = jax.ShapeDtypeStruct((T, n_out), BF16)
    out_specs = pl.BlockSpec((tm, n_out), lambda i: (i, 0))
    if w_v is not None:
        in_specs.append(pl.BlockSpec(w_v.shape, lambda i: (0, 0)))
        args.append(w_v)
        out_shape = (out_shape, jax.ShapeDtypeStruct((T, n_out), BF16))
        out_specs = (out_specs, pl.BlockSpec((tm, n_out), lambda i: (i, 0)))
    return pl.pallas_call(
        functools.partial(_heads_kernel, rank=rank, heads=heads, with_v=w_v is not None),
        grid=(T // tm,),
        in_specs=in_specs, out_specs=out_specs, out_shape=out_shape,
        compiler_params=_params(("parallel",), 48),
        name="latent_heads_kv" if w_v is not None else "latent_heads_q",
    )(*args)


def _flash_kernel(q_ref, k_ref, v_ref, o_ref, s_s, m_s, acc_s, *, tq, sub, n_kt):
    i = pl.program_id(2)
    m_s[...] = jnp.full(m_s.shape, -1e30, F32)
    acc_s[...] = jnp.zeros(acc_s.shape, F32)
    n_sub = tq // sub

    def scores(kt, slot):
        k = k_ref[0, pl.ds(kt * tq, tq), :]
        for h in range(n_sub):
            rows = slice(h * sub, (h + 1) * sub)
            s_s[slot, rows, :] = lax.dot_general(q_ref[0, rows, :], k, (((1,), (1,)), ((), ())),
                                                 preferred_element_type=F32)

    def softmax_pv(k0, slot, masked):
        for h in range(n_sub):
            rows = slice(h * sub, (h + 1) * sub)
            nk = (h + 1) * sub if masked else tq
            s = s_s[slot, rows, :nk]
            if masked:
                r = lax.broadcasted_iota(jnp.int32, s.shape, 0) + h * sub
                c = lax.broadcasted_iota(jnp.int32, s.shape, 1)
                s = jnp.where(c <= r, s, -1e30)
            v = v_ref[0, pl.ds(k0, nk), :]
            m_prev = m_s[rows, :]
            m_new = jnp.maximum(m_prev, jnp.max(s, axis=-1, keepdims=True))
            alpha = jnp.exp2(m_prev - m_new)
            p = jnp.exp2(s - jnp.concatenate([m_new] * (nk // LANES), axis=1))
            acc_s[rows, :] = (jnp.concatenate([alpha, alpha], axis=1) * acc_s[rows, :]
                              + jnp.dot(p.astype(BF16), v, preferred_element_type=F32))
            m_s[rows, :] = m_new

    scores(0, 0)
    for kt in range(n_kt - 1):
        @pl.when(kt < i)
        def _(kt=kt):
            scores(kt + 1, (kt + 1) % 2)
            softmax_pv(kt * tq, kt % 2, False)
    softmax_pv(pl.multiple_of(i * tq, tq), i % 2, True)
    o_ref[0] = (acc_s[:, :V_DIM] / acc_s[:, V_DIM:]).astype(o_ref.dtype)


def flash_attention(q, k, v, heads, tq):
    B, S, _ = q.shape
    return pl.pallas_call(
        functools.partial(_flash_kernel, tq=tq, sub=SUB_FLASH, n_kt=S // tq),
        grid=(B, heads, S // tq),
        in_specs=[pl.BlockSpec((1, tq, HEAD_PAD), lambda b, h, i: (b, i, h)),
                  pl.BlockSpec((1, S, HEAD_PAD), lambda b, h, i: (b, 0, h)),
                  pl.BlockSpec((1, S, HEAD_PAD), lambda b, h, i: (b, 0, h))],
        out_specs=pl.BlockSpec((1, tq, V_DIM), lambda b, h, i: (b, i, h)),
        out_shape=jax.ShapeDtypeStruct((B, S, heads * V_DIM), BF16),
        scratch_shapes=[pltpu.VMEM((2, tq, tq), F32), pltpu.VMEM((tq, LANES), F32),
                        pltpu.VMEM((tq, HEAD_PAD), F32)],
        compiler_params=_params(("parallel", "parallel", "arbitrary"), 32),
        name="flash_attention",
    )(q, k, v)


def _router_rows(x_ref, g_ref, w_ref, b_ref, o_ref, xt_ref, tm, chunk):
    w = w_ref[...]
    w_hi = w.astype(BF16)
    w_lo = (w - w_hi.astype(F32)).astype(BF16)
    lane_tiles = x_ref.shape[1] // LANES
    for r0 in range(0, tm, chunk):
        xn = _rms_rows(x_ref[r0:r0 + chunk, :], g_ref[...])
        x_hi = xn.astype(BF16)
        x_lo = (xn - x_hi.astype(F32)).astype(BF16)
        acc = jnp.dot(x_hi, w_hi, preferred_element_type=F32)
        acc += jnp.dot(x_lo, w_hi, preferred_element_type=F32)
        acc += jnp.dot(x_hi, w_lo, preferred_element_type=F32)
        o_ref[r0:r0 + chunk, :] = acc + b_ref[...]
        for j in range(lane_tiles):
            xt_ref[pl.ds(r0 * lane_tiles + j, chunk, stride=lane_tiles), :] = xn[:, j * LANES:(j + 1) * LANES]


def _plan_kernel(lg_ref, out_ref, tiles_ref, row_tok_ref, c_s, info_s, inv_s, *, n_tok, n_groups, epg, tm, chunk):
    n_exp = n_groups * epg
    lane = lax.broadcasted_iota(jnp.int32, (chunk, LANES), 1).astype(F32)
    rr = lax.broadcasted_iota(jnp.int32, (chunk, chunk), 0)
    cc = lax.broadcasted_iota(jnp.int32, (chunk, chunk), 1)
    tri = jnp.where(cc <= rr, 1.0, 0.0).astype(BF16)
    neg_inf = -jnp.inf

    def first_argmax(vals, vmax):
        return jnp.min(jnp.where(vals == vmax, lane, float(LANES)), axis=1, keepdims=True)

    def decide(c, carry):
        r0 = pl.multiple_of(c * chunk, chunk)
        lg = lg_ref[pl.ds(r0, chunk), :]
        gl = jnp.where(lane < n_groups, lg, neg_inf)
        gmax = jnp.max(gl, axis=1, keepdims=True)
        g_idx = first_argmax(gl, gmax)
        p_top = 1.0 / jnp.sum(jnp.exp(gl - gmax), axis=1, keepdims=True)
        lo = n_groups + g_idx * epg
        el = jnp.where(jnp.logical_and(lane >= lo, lane < lo + epg), lg, neg_inf)
        l1 = jnp.max(el, axis=1, keepdims=True)
        i1 = first_argmax(el, l1)
        el2 = jnp.where(lane == i1, neg_inf, el)
        l2 = jnp.max(el2, axis=1, keepdims=True)
        i2 = first_argmax(el2, l2)
        d = jnp.exp(l2 - l1)
        w1 = 1.0 / (1.0 + d)
        e1 = i1 - n_groups
        e2 = i2 - n_groups
        onehot = jnp.where(jnp.logical_or(lane == e1, lane == e2), 1.0, 0.0)
        cs = jnp.dot(tri, onehot.astype(BF16), preferred_element_type=F32) + carry
        c_s[pl.ds(r0, chunk), :] = cs
        info_s[pl.ds(r0, chunk), :] = jnp.where(
            lane == 0, e1, jnp.where(lane == 1, e2, jnp.where(
                lane == 2, p_top * w1, jnp.where(lane == 3, p_top * (d * w1), 0.0))))
        return cs[chunk - 1:chunk, :]

    counts = lax.fori_loop(0, n_tok // chunk, decide, jnp.zeros((1, LANES), F32))

    tiles = jnp.floor((counts + (tm - 1)) * (1.0 / tm))
    jj = lax.broadcasted_iota(jnp.int32, (LANES, LANES), 0)
    ee = lax.broadcasted_iota(jnp.int32, (LANES, LANES), 1)
    upper = jnp.where(jj <= ee, 1.0, 0.0).astype(BF16)
    tile_end = jnp.dot(jnp.broadcast_to(tiles, (8, LANES)).astype(BF16), upper,
                       preferred_element_type=F32)[0:1, :]
    tile_start = tile_end - tiles
    row_start = tile_start * tm

    inv_s[...] = jnp.zeros(inv_s.shape, F32)
    off_lane = lax.broadcasted_iota(jnp.int32, (chunk, tm), 1).astype(F32)
    tile_row = lax.broadcasted_iota(jnp.int32, (LANES, chunk), 0).astype(F32)
    tok_local = lax.broadcasted_iota(jnp.int32, (1, chunk), 1)

    def place(c, carry):
        r0 = pl.multiple_of(c * chunk, chunk)
        info = info_s[pl.ds(r0, chunk), :]
        base = row_start + c_s[pl.ds(r0, chunk), :] - 1.0
        pos = [jnp.sum(jnp.where(lane == info[:, k:k + 1], base, 0.0), axis=1, keepdims=True)
               for k in range(TOP_K_IN_GROUP)]
        packed = jnp.where(lane == 0, pos[0], jnp.where(lane == 1, pos[1], info))
        out_ref[pl.ds(r0, chunk), :] = packed
        pos_rows = jnp.transpose(packed)
        tok = r0 + tok_local
        tok_hi = lax.shift_right_logical(tok, LANES.bit_length() - 1).astype(F32)
        tok_lo = (tok & (LANES - 1)).astype(F32)
        for k in range(TOP_K_IN_GROUP):
            tile_of = jnp.floor(pos[k] * (1.0 / tm))
            onehot_off = jnp.where(off_lane == pos[k] - tile_of * tm, 1.0, 0.0).astype(BF16)
            sel = tile_row == jnp.floor(pos_rows[k:k + 1, :] * (1.0 / tm))
            ids = jnp.concatenate([jnp.where(sel, tok_hi, 0.0), jnp.where(sel, tok_lo, 0.0)], axis=0)
            inv_s[...] += jnp.dot(ids.astype(BF16), onehot_off, preferred_element_type=F32)
        return carry

    lax.fori_loop(0, n_tok // chunk, place, 0)
    row_tok_ref[...] = (inv_s[:LANES, :] * LANES + inv_s[LANES:, :]).astype(jnp.int32)

    ti = jj.astype(F32)
    lane_sq = ee.astype(F32)
    is_exp = lane_sq < n_exp
    n_used = tile_end[:, n_exp - 1:n_exp]
    ti_c = jnp.minimum(ti, n_used - 1.0)
    tile_e = jnp.sum(jnp.where(jnp.logical_and(is_exp, tile_end <= ti_c), 1.0, 0.0), axis=1, keepdims=True)
    first = jnp.sum(jnp.where(jnp.logical_and(jnp.logical_and(is_exp, tile_start == ti), tiles > 0.0),
                              1.0, 0.0), axis=1, keepdims=True)
    nonempty = jnp.logical_and(is_exp, tiles > 0.0)
    def next_nonempty(e):
        nxt = jnp.min(jnp.where(jnp.logical_and(nonempty, lane_sq > e), lane_sq, float(LANES)),
                      axis=1, keepdims=True)
        return jnp.where(nxt >= LANES, e, nxt)

    next_e = next_nonempty(tile_e)
    next2_e = next_nonempty(next_e)
    group = jnp.sum(jnp.where(jnp.logical_and(nonempty, lane_sq < tile_e), 1.0, 0.0), axis=1, keepdims=True)
    wslot = group - 2.0 * jnp.floor(group * 0.5)
    fields = (tile_e, first, n_used, next_e, wslot, next2_e)
    assert len(fields) == N_TILE_FIELDS
    packed = jnp.zeros((LANES, LANES), F32)
    for f, val in enumerate(fields):
        packed = jnp.where(lane_sq == f, val, packed)
    tiles_ref[...] = packed.astype(jnp.int32)


def routing_plan(logits, n_groups, epg, tm):
    T = logits.shape[0]
    n_exp = n_groups * epg
    n_tiles = T * TOP_K_IN_GROUP // tm + n_exp
    assert n_tiles <= LANES and T <= LANES * LANES
    out, tiles, row_tok = pl.pallas_call(
        functools.partial(_plan_kernel, n_tok=T, n_groups=n_groups, epg=epg, tm=tm, chunk=PLAN_CHUNK),
        grid=(1,),
        in_specs=[pl.BlockSpec((T, LANES), lambda i: (0, 0))],
        out_specs=(pl.BlockSpec((T, LANES), lambda i: (0, 0)),
                   pl.BlockSpec((LANES, LANES), lambda i: (0, 0)),
                   pl.BlockSpec((LANES, tm), lambda i: (0, 0))),
        out_shape=(jax.ShapeDtypeStruct((T, LANES), F32), jax.ShapeDtypeStruct((LANES, LANES), jnp.int32),
                   jax.ShapeDtypeStruct((LANES, tm), jnp.int32)),
        scratch_shapes=[pltpu.VMEM((T, LANES), F32), pltpu.VMEM((T, LANES), F32),
                        pltpu.VMEM((2 * LANES, tm), F32)],
        compiler_params=_params(("arbitrary",), 40),
        name="routing_plan",
    )(logits)
    pos = out[:, :TOP_K_IN_GROUP].astype(jnp.int32).reshape(-1)
    tile_info = tiles[:n_tiles, :N_TILE_FIELDS].T.reshape(-1)
    return out, pos, tile_info, row_tok[:n_tiles].reshape(-1), n_tiles


def _row_copy(src_hbm, tok, dst_row, sem, rows=1):
    start = tok if rows == 1 else pl.multiple_of(tok * rows, rows)
    return pltpu.make_async_copy(src_hbm.at[pl.ds(start, rows)], dst_row, sem)


def _start_row_gather(src_hbm, idx_ref, base, n, dst_row, sem, rows=1):
    def body(r, carry):
        _row_copy(src_hbm, idx_ref[base + r], dst_row(r), sem, rows).start()
        return carry
    lax.fori_loop(0, n, body, 0, unroll=8)


def _wait_row_gather(src_hbm, n, dst_all, sem):
    pltpu.make_async_copy(src_hbm.at[pl.ds(0, n)], dst_all, sem).wait()


def _gmm_kernel(info_ref, row_tok_ref, x_hbm, wg_hbm, wu_hbm, wd_hbm, o_ref,
                xbuf, sem, wg_f, wu_f, wd_f, wsem, wg_bf, wu_bf, wd_bf, *, tm, n_tiles, w_off, n_chunks, tok_rows):
    i = pl.program_id(0)
    expert = info_ref[i]
    first = info_ref[n_tiles + i]
    n_used = info_ref[2 * n_tiles]
    next_expert = info_ref[3 * n_tiles + i]
    wslot = info_ref[4 * n_tiles + i]
    next2_expert = info_ref[5 * n_tiles + i]
    slot = i % GATHER_BUFS
    ahead = GATHER_BUFS - 1

    def wait_slot(s):
        _wait_row_gather(x_hbm, tm * tok_rows, xbuf.at[s], sem.at[s])

    def weight_copies(e, s):
        copies = []
        for hbm, buf in ((wg_hbm, wg_f), (wu_hbm, wu_f), (wd_hbm, wd_f)):
            rows = hbm.shape[1] // n_chunks
            for c in range(n_chunks):
                copies.append(pltpu.make_async_copy(hbm.at[w_off + e, pl.ds(c * rows, rows)],
                                                    buf.at[s, pl.ds(c * rows, rows)], wsem.at[s]))
        return copies

    def start_weights(e, s):
        for c, cp in enumerate(weight_copies(e, s)):
            cp.start(priority=c % DMA_PRIORITIES)

    @pl.when(jnp.logical_and(i == 0, n_used > 0))
    def _():
        start_weights(expert, 0)

        @pl.when(next_expert != expert)
        def _():
            start_weights(next_expert, 1)

        for t in range(ahead):
            _start_row_gather(x_hbm, row_tok_ref, min(t, n_tiles - 1) * tm, tm,
                              lambda r, t=t: xbuf.at[t, pl.ds(pl.multiple_of(r * tok_rows, tok_rows), tok_rows)],
                              sem.at[t], tok_rows)

    @pl.when(i < n_used)
    def _():
        @pl.when(first == 1)
        def _():
            for cp in weight_copies(expert, wslot):
                cp.wait()
            for src, dst in ((wg_f, wg_bf), (wu_f, wu_bf), (wd_f, wd_bf)):
                rows = dst.shape[0] // n_chunks

                def convert(c, carry, src=src, dst=dst, rows=rows):
                    r0 = pl.multiple_of(c * rows, rows)
                    dst[pl.ds(r0, rows), :] = src[wslot, pl.ds(r0, rows), :].astype(BF16)
                    return carry
                lax.fori_loop(0, n_chunks, convert, 0)

            @pl.when(next2_expert != next_expert)
            def _():
                start_weights(next2_expert, wslot)

        wait_slot(slot)
        xn = jnp.concatenate([xbuf[slot, pl.ds(j, tm, stride=tok_rows), :].astype(BF16)
                              for j in range(tok_rows)], axis=1)
        base = jnp.minimum(i + ahead, n_tiles - 1) * tm
        nslot = (i + ahead) % GATHER_BUFS
        for r in range(tm):
            _row_copy(x_hbm, row_tok_ref[base + r], xbuf.at[nslot, pl.ds(r * tok_rows, tok_rows)],
                      sem.at[nslot], tok_rows).start()
        hg = jnp.dot(xn, wg_bf[...], preferred_element_type=F32)
        hu = jnp.dot(xn, wu_bf[...], preferred_element_type=F32)
        hh = (hg * jax.nn.sigmoid(hg)) * hu
        o_ref[...] = jnp.dot(hh.astype(BF16), wd_bf[...], preferred_element_type=F32)

    @pl.when(i >= n_used)
    def _():
        @pl.when(jnp.logical_and(i < n_used + ahead, n_used > 0))
        def _():
            wait_slot(slot)
        o_ref[...] = jnp.zeros(o_ref.shape, o_ref.dtype)

    @pl.when(i == n_tiles - 1)
    def _():
        for d in range(1, ahead + 1):
            @pl.when(n_used >= n_tiles - ahead + d)
            def _(d=d):
                wait_slot((n_tiles - 1 + d) % GATHER_BUFS)


def grouped_expert_mlp(xt, tile_info, row_tok, w_gate, w_up, w_down, layer, n_tiles, tm):
    _, D, F = w_gate.shape
    tok_rows = D // LANES
    layer_idx, depth = layer
    off = layer_idx * (w_gate.shape[0] // depth)
    any_spec = pl.BlockSpec(memory_space=pl.ANY)
    grid_spec = pltpu.PrefetchScalarGridSpec(
        num_scalar_prefetch=2,
        grid=(n_tiles,),
        in_specs=[any_spec, any_spec, any_spec, any_spec],
        out_specs=pl.BlockSpec((tm, D), lambda i, ti, rt: (i, 0)),
        scratch_shapes=[pltpu.VMEM((GATHER_BUFS, tm * tok_rows, LANES), F32),
                        pltpu.SemaphoreType.DMA((GATHER_BUFS,)),
                        pltpu.VMEM((2, D, F), F32), pltpu.VMEM((2, D, F), F32), pltpu.VMEM((2, F, D), F32),
                        pltpu.SemaphoreType.DMA((2,)),
                        pltpu.VMEM((D, F), BF16), pltpu.VMEM((D, F), BF16), pltpu.VMEM((F, D), BF16)],
    )
    return pl.pallas_call(
        functools.partial(_gmm_kernel, tm=tm, n_tiles=n_tiles, w_off=off, n_chunks=WEIGHT_DMA_CHUNKS,
                          tok_rows=tok_rows),
        grid_spec=grid_spec,
        out_shape=jax.ShapeDtypeStruct((n_tiles * tm, D), F32),
        compiler_params=_params(("arbitrary",), 56),
        name="grouped_expert_mlp",
    )(tile_info, row_tok, xt, w_gate, w_up, w_down)


def _combine_kernel(pos_ref, x_ref, plan_ref, y_hbm, o_ref, ybuf, sem, *, tc, top_k):
    i = pl.program_id(0)
    n = pl.num_programs(0)
    slot = i % 2

    def start(tile, s):
        base = tile * (tc * top_k)
        for r in range(tc):
            for k in range(top_k):
                _row_copy(y_hbm, pos_ref[base + r * top_k + k], ybuf.at[s, k, pl.ds(r, 1)], sem.at[s]).start()

    @pl.when(i == 0)
    def _():
        start(0, 0)

    for k in range(top_k):
        _wait_row_gather(y_hbm, tc, ybuf.at[slot, k], sem.at[slot])

    @pl.when(i + 1 < n)
    def _():
        start(i + 1, 1 - slot)

    acc = x_ref[...]
    for k in range(top_k):
        acc = acc + plan_ref[:, top_k + k:top_k + k + 1] * ybuf[slot, k]
    o_ref[...] = acc


def moe_combine(x, y_sorted, plan_out, pos, tc):
    T, D = x.shape
    top_k = TOP_K_IN_GROUP
    grid_spec = pltpu.PrefetchScalarGridSpec(
        num_scalar_prefetch=1,
        grid=(T // tc,),
        in_specs=[pl.BlockSpec((tc, D), lambda i, p: (i, 0)),
                  pl.BlockSpec((tc, LANES), lambda i, p: (i, 0)),
                  pl.BlockSpec(memory_space=pl.ANY)],
        out_specs=pl.BlockSpec((tc, D), lambda i, p: (i, 0)),
        scratch_shapes=[pltpu.VMEM((2, top_k, tc, D), F32), pltpu.SemaphoreType.DMA((2,))],
    )
    return pl.pallas_call(
        functools.partial(_combine_kernel, tc=tc, top_k=top_k),
        grid_spec=grid_spec,
        out_shape=jax.ShapeDtypeStruct((T, D), F32),
        compiler_params=_params(("arbitrary",), 32),
        name="moe_combine",
    )(pos, x, plan_out, y_sorted)


def _combine_mm_kernel(pos_ref, x_ref, plan_ref, y_hbm, g_ref, w_hbm, xnew_ref, o_ref,
                       ybuf, ysem, xn_ref, w_bf, stage, wsem, *, layer, tc, chunk, tn, top_k):
    i = pl.program_id(0)
    n = pl.num_programs(0)
    slot = i % 2

    def start(tile, s):
        base = tile * (tc * top_k)
        for r in range(tc):
            for k in range(top_k):
                _row_copy(y_hbm, pos_ref[base + r * top_k + k], ybuf.at[s, k, pl.ds(r, 1)], ysem.at[s]).start()

    def wait(s):
        for k in range(top_k):
            _wait_row_gather(y_hbm, tc, ybuf.at[s, k], ysem.at[s])

    @pl.when(i == 0)
    def _():
        start(0, 0)
        _stage_weight_bf16(w_hbm, layer, w_bf, stage, wsem, tn)

    wait(slot)
    for r0 in range(0, tc, chunk):
        rows = slice(r0, r0 + chunk)
        acc = x_ref[rows, :]
        for k in range(top_k):
            acc = acc + plan_ref[rows, top_k + k:top_k + k + 1] * ybuf[slot, k, rows, :]
        xnew_ref[rows, :] = acc
        xn_ref[rows, :] = _rms_rows(acc, g_ref[...]).astype(BF16)
    start(jnp.minimum(i + 1, n - 1), 1 - slot)
    for j in range(o_ref.shape[1] // tn):
        o_ref[:, j * tn:(j + 1) * tn] = jnp.dot(xn_ref[...], w_bf[:, j * tn:(j + 1) * tn],
                                                preferred_element_type=F32).astype(o_ref.dtype)

    @pl.when(i == n - 1)
    def _():
        wait(1 - slot)


def combine_norm_matmul(x, y_sorted, plan_out, pos, g, w, layer, out_dtype, tc):
    T, D = x.shape
    N = w.shape[2]
    top_k = TOP_K_IN_GROUP
    tn = _col_tile(N)
    grid_spec = pltpu.PrefetchScalarGridSpec(
        num_scalar_prefetch=1,
        grid=(T // tc,),
        in_specs=[pl.BlockSpec((tc, D), lambda i, p: (i, 0)),
                  pl.BlockSpec((tc, LANES), lambda i, p: (i, 0)),
                  pl.BlockSpec(memory_space=pl.ANY),
                  pl.BlockSpec((1, D), lambda i, p: (0, 0)),
                  pl.BlockSpec(memory_space=pl.ANY)],
        out_specs=(pl.BlockSpec((tc, D), lambda i, p: (i, 0)),
                   pl.BlockSpec((tc, N), lambda i, p: (i, 0))),
        scratch_shapes=[pltpu.VMEM((2, top_k, tc, D), F32), pltpu.SemaphoreType.DMA((2,)),
                        pltpu.VMEM((tc, D), BF16), pltpu.VMEM((D, N), BF16), pltpu.VMEM((2, D, tn), F32),
                        pltpu.SemaphoreType.DMA((2,))],
    )
    return pl.pallas_call(
        functools.partial(_combine_mm_kernel, layer=layer, tc=tc, chunk=min(tc, 128), tn=tn, top_k=top_k),
        grid_spec=grid_spec,
        out_shape=(jax.ShapeDtypeStruct((T, D), F32), jax.ShapeDtypeStruct((T, N), out_dtype)),
        compiler_params=_params(("arbitrary",), 56),
        name="combine_norm_matmul",
    )(pos, x, plan_out, y_sorted, g.reshape(1, D), w)


def hier_moe_experts(logits, xt, n_groups, epg, w_gate, w_up, w_down, layer):
    plan_out, pos, tile_info, row_tok, n_tiles = routing_plan(logits, n_groups, epg, TM_EXPERT)
    y_sorted = grouped_expert_mlp(xt, tile_info, row_tok, w_gate, w_up, w_down, layer, n_tiles, TM_EXPERT)
    return y_sorted, plan_out, pos


def _pad_head_cols(w, heads, real):
    r = w.shape[0]
    return jnp.pad(w.reshape(r, heads, real), ((0, 0), (0, 0), (0, HEAD_PAD - real))).reshape(r, heads * HEAD_PAD)


def _rope_tables(positions):
    half = QK_ROPE // 2
    inv_freq = ROPE_THETA ** (-jnp.arange(half, dtype=F32) / half)
    ang = positions.astype(F32).reshape(-1, 1) * inv_freq[None, :]
    cos, sin = jnp.cos(ang), jnp.sin(ang)
    zeros = jnp.zeros((ang.shape[0], LANES - QK_ROPE), F32)
    return (jnp.concatenate([cos, cos, zeros], axis=1),
            jnp.concatenate([-sin, sin, zeros], axis=1))


def kernel(x, mem, positions, norm_mix_g, norm_ffn_g, w_o, mem_norm_g, w_mem_kv, mem_q_norm_g, mem_k_norm_g, w_in_a, conv_w, conv_b, w_lru_r, b_lru_r, w_lru_i, b_lru_i, lru_lambda, kv_in_norm_g, w_dkv, kv_latent_norm_g, w_uk, w_uv, k_head_norm_g, w_in_b, q_latent_norm_g, w_uq, q_head_norm_g, w_router_grp, b_router_grp, w_router_exp, b_router_exp, w_exp_gate, w_exp_up, w_exp_down):
    B, S, D = x.shape
    T = B * S
    M = mem.shape[1]
    depth = norm_mix_g.shape[0]
    n_a = w_in_a.shape[0]
    lru_w = lru_lambda.shape[1]
    mem_w = w_mem_kv.shape[2] // 2
    heads = w_uv.shape[1] // V_DIM
    kv_rank = kv_latent_norm_g.shape[0]
    q_rank = q_latent_norm_g.shape[1]
    n_groups, epg = w_exp_gate.shape[1], w_exp_gate.shape[2]
    n_exp = n_groups * epg
    d_exp = w_exp_gate.shape[-1]

    xr = x.reshape(T, D)
    mem2 = mem.reshape(B * M, D)
    cos_t, sin_t = _rope_tables(positions)
    wg_all = w_exp_gate.reshape(depth * n_exp, D, d_exp)
    wu_all = w_exp_up.reshape(depth * n_exp, D, d_exp)
    wd_all = w_exp_down.reshape(depth * n_exp, d_exp, D)
    pad_r = LANES - n_groups - n_exp
    w_router = jnp.concatenate([w_router_grp, w_router_exp, jnp.zeros((depth, D, pad_r), F32)], axis=2)
    b_router = jnp.concatenate([b_router_grp, b_router_exp, jnp.zeros((depth, pad_r), F32)], axis=1)

    def shared_kv(x_res):
        lat_cols = kv_rank + LANES
        w_dkv_pad = jnp.pad(w_dkv, ((0, 0), (0, lat_cols - w_dkv.shape[1])))[None]
        ckv = norm_matmul(x_res, kv_in_norm_g, w_dkv_pad, 0, F32, tm=TM_PROJ)
        eye = jnp.pad(jnp.eye(QK_ROPE, dtype=F32), ((0, LANES - QK_ROPE), (0, 0)))
        eye3 = jnp.broadcast_to(eye[:, None, :], (LANES, heads, QK_ROPE))
        rope_rows = jnp.pad(eye3, ((0, 0), (0, 0), (QK_NOPE, HEAD_PAD - QK_DIM))).reshape(LANES, heads * HEAD_PAD)
        wk = jnp.concatenate([_pad_head_cols(w_uk, heads, QK_NOPE), rope_rows], axis=0).astype(BF16)
        kg = jnp.pad(k_head_norm_g, (0, HEAD_PAD - QK_DIM)).reshape(1, HEAD_PAD)
        k_flat, v_flat = latent_heads(ckv, lat_cols, kv_rank, kv_latent_norm_g, wk, kg, cos_t, sin_t,
                                      heads, tm=TM_HEADS, w_v=w_uv.astype(BF16))
        return k_flat.reshape(B, S, -1), v_flat.reshape(B, S, -1)

    def in_proj(x_res, pending, g, w, idx):
        if pending is None:
            return x_res, norm_matmul(x_res, g, w, idx, BF16, tm=TM_PROJ)
        return combine_norm_matmul(x_res, *pending, g, w, idx, BF16, TC_COMBINE)

    k_sh = v_sh = None
    pending = None
    for l in range(depth):
        kv_mem = norm_matmul(mem2, mem_norm_g[l], w_mem_kv, l, BF16, tm=TM_PROJ).reshape(B, M, 2 * mem_w)
        if l < n_a:
            xr, proj = in_proj(xr, pending, norm_mix_g[l], w_in_a, l)
            y_mix = rglru(proj.reshape(B, S, -1), conv_w[l], conv_b[l], w_lru_r[l], b_lru_r[l],
                          w_lru_i[l], b_lru_i[l], lru_lambda[l], nb=LRU_BLOCKS_PER_STEP,
                          tc=TC_LRU).reshape(T, lru_w)
            q_col = 2 * lru_w // mem_w
        else:
            j = l - n_a
            xr, proj = in_proj(xr, pending, norm_mix_g[l], w_in_b, j)
            if l == n_a:
                k_sh, v_sh = shared_kv(xr)
            wq = _pad_head_cols(w_uq[j], heads, QK_DIM).astype(BF16)
            qg = jnp.pad(q_head_norm_g[j] * (QK_DIM ** -0.5 * math.log2(math.e)),
                         (0, HEAD_PAD - QK_DIM)).reshape(1, HEAD_PAD)
            q = latent_heads(proj, q_rank, q_rank, q_latent_norm_g[j], wq, qg, cos_t, sin_t, heads, tm=TM_HEADS)
            y_mix = flash_attention(q.reshape(B, S, -1), k_sh, v_sh, heads, tq=TQ_FLASH).reshape(T, heads * V_DIM)
            q_col = q_rank // mem_w
        y_mem = mem_attention(proj, q_col, kv_mem, mem_q_norm_g[l], mem_k_norm_g[l], B, ts=TS_MEM)
        xr, logits, xt = out_proj(y_mix, y_mem, w_o, l, xr, norm_ffn_g[l], w_router[l],
                                  b_router[l].reshape(1, LANES), tm=TM_PROJ)
        pending = hier_moe_experts(logits, xt, n_groups, epg, wg_all, wu_all, wd_all, (l, depth))
    xr = moe_combine(xr, *pending, TC_COMBINE)
    return xr.reshape(B, S, D)
```

```python
import functools
import math

import jax
import jax.numpy as jnp
from jax import lax
from jax.experimental import pallas as pl
from jax.experimental.pallas import tpu as pltpu

F32 = jnp.float32
BF16 = jnp.bfloat16
EPS = 1e-6

MEM_HEADS = 4
CONV_WIDTH = 4
LRU_C = 8.0
QK_NOPE = 128
QK_ROPE = 64
QK_DIM = QK_NOPE + QK_ROPE
V_DIM = 128
ROPE_THETA = 10000.0
TOP_K_IN_GROUP = 2

LANES = 128
HEAD_PAD = 2 * LANES
MIB = 1024 * 1024
DMA_PRIORITIES = 2

TM_PROJ = 512
TN_PROJ = 512
TS_MEM = 512
TQ_FLASH = 512
SUB_FLASH = 256
HEADS_PER_FLASH_STEP = 4
TM_HEADS = 512
TM_EXPERT = 256
WEIGHT_DMA_CHUNKS = 4
GATHER_BUFS = 3
TC_COMBINE = 256
LRU_BLOCKS_PER_STEP = 3
TC_LRU = 256
PLAN_CHUNK = 256
N_TILE_FIELDS = 6


def _params(sem, vmem_mib):
    return pltpu.CompilerParams(dimension_semantics=sem, vmem_limit_bytes=vmem_mib * MIB)


def _sigmoid(z):
    return 0.5 * jnp.tanh(0.5 * z) + 0.5


def _rms_rows(x, g):
    x = x.astype(F32)
    ms = jnp.mean(x * x, axis=-1, keepdims=True)
    return x * lax.rsqrt(ms + EPS) * g


def _col_tile(n):
    return TN_PROJ if n % TN_PROJ == 0 else n


def _stage_weight_bf16(w_hbm, layer, w_bf, stage, sem, tn):
    n = w_bf.shape[1] // tn

    def copy(j):
        return pltpu.make_async_copy(w_hbm.at[layer, :, pl.ds(j * tn, tn)], stage.at[j % 2], sem.at[j % 2])

    copy(0).start()
    for j in range(n):
        if j + 1 < n:
            copy(j + 1).start()
        copy(j).wait()
        w_bf[:, j * tn:(j + 1) * tn] = stage[j % 2].astype(BF16)


def _norm_mm_kernel(x_ref, g_ref, w_hbm, o_ref, xn_ref, w_bf, stage, sem, *, layer, tm, chunk, tn):
    @pl.when(pl.program_id(0) == 0)
    def _():
        _stage_weight_bf16(w_hbm, layer, w_bf, stage, sem, tn)

    for r0 in range(0, tm, chunk):
        xn_ref[r0:r0 + chunk, :] = _rms_rows(x_ref[r0:r0 + chunk, :], g_ref[...]).astype(BF16)

    def cols(j, carry):
        c0 = pl.multiple_of(j * tn, tn)
        o_ref[:, pl.ds(c0, tn)] = jnp.dot(xn_ref[...], w_bf[:, pl.ds(c0, tn)],
                                          preferred_element_type=F32).astype(o_ref.dtype)
        return carry
    lax.fori_loop(0, o_ref.shape[1] // tn, cols, 0)


def norm_matmul(x, g, w, layer, out_dtype, tm):
    T, K = x.shape
    N = w.shape[2]
    tm = min(tm, T)
    chunk = min(tm, 256)
    tn = _col_tile(N)
    return pl.pallas_call(
        functools.partial(_norm_mm_kernel, layer=layer, tm=tm, chunk=chunk, tn=tn),
        grid=(T // tm,),
        in_specs=[pl.BlockSpec((tm, K), lambda i: (i, 0)),
                  pl.BlockSpec((1, K), lambda i: (0, 0)),
                  pl.BlockSpec(memory_space=pl.ANY)],
        out_specs=pl.BlockSpec((tm, N), lambda i: (i, 0)),
        out_shape=jax.ShapeDtypeStruct((T, N), out_dtype),
        scratch_shapes=[pltpu.VMEM((tm, K), BF16), pltpu.VMEM((K, N), BF16), pltpu.VMEM((2, K, tn), F32),
                        pltpu.SemaphoreType.DMA((2,))],
        compiler_params=_params(("arbitrary",), 56),
        name="norm_matmul",
    )(x, g.reshape(1, K), w)


def _out_proj_kernel(ya_ref, yb_ref, w_hbm, x_ref, g_ref, wr_ref, br_ref, o_ref, lg_ref, xt_ref,
                     w_bf, stage, sem, *, layer, tm, tn):
    @pl.when(pl.program_id(0) == 0)
    def _():
        _stage_weight_bf16(w_hbm, layer, w_bf, stage, sem, tn)

    wa_rows = ya_ref.shape[1]

    def cols(j, carry):
        c0 = pl.multiple_of(j * tn, tn)
        acc = jnp.dot(ya_ref[...], w_bf[:wa_rows, pl.ds(c0, tn)], preferred_element_type=F32)
        acc += jnp.dot(yb_ref[...], w_bf[wa_rows:, pl.ds(c0, tn)], preferred_element_type=F32)
        o_ref[:, pl.ds(c0, tn)] = x_ref[:, pl.ds(c0, tn)] + acc
        return carry
    lax.fori_loop(0, o_ref.shape[1] // tn, cols, 0)
    _router_rows(o_ref, g_ref, wr_ref, br_ref, lg_ref, xt_ref, tm, min(tm, 256))


def out_proj(y_mix, y_mem, w_o, layer, x, g_ffn, w_router, b_router, tm):
    T, Wa = y_mix.shape
    Wb = y_mem.shape[1]
    _, K, N = w_o.shape
    assert Wa + Wb == K
    tn = _col_tile(N)
    lane_tiles = N // LANES
    return pl.pallas_call(
        functools.partial(_out_proj_kernel, layer=layer, tm=tm, tn=tn),
        grid=(T // tm,),
        in_specs=[pl.BlockSpec((tm, Wa), lambda i: (i, 0)),
                  pl.BlockSpec((tm, Wb), lambda i: (i, 0)),
                  pl.BlockSpec(memory_space=pl.ANY),
                  pl.BlockSpec((tm, N), lambda i: (i, 0)),
                  pl.BlockSpec((1, N), lambda i: (0, 0)),
                  pl.BlockSpec((N, LANES), lambda i: (0, 0)),
                  pl.BlockSpec((1, LANES), lambda i: (0, 0))],
        out_specs=(pl.BlockSpec((tm, N), lambda i: (i, 0)),
                   pl.BlockSpec((tm, LANES), lambda i: (i, 0)),
                   pl.BlockSpec((tm * lane_tiles, LANES), lambda i: (i, 0))),
        out_shape=(jax.ShapeDtypeStruct((T, N), F32),
                   jax.ShapeDtypeStruct((T, LANES), F32),
                   jax.ShapeDtypeStruct((T * lane_tiles, LANES), F32)),
        scratch_shapes=[pltpu.VMEM((K, N), BF16), pltpu.VMEM((2, K, tn), F32), pltpu.SemaphoreType.DMA((2,))],
        compiler_params=_params(("arbitrary",), 56),
        name="out_proj",
    )(y_mix, y_mem, w_o, x, g_ffn.reshape(1, N), w_router, b_router)


def _mem_attn_kernel(q_ref, kv_ref, qg_ref, kg_ref, o_ref, *, heads, hd):
    width = heads * hd
    scale = hd ** -0.5
    for h in range(heads):
        qn = _rms_rows(q_ref[:, h * hd:(h + 1) * hd], qg_ref[...]) * scale
        kn = _rms_rows(kv_ref[0, :, h * hd:(h + 1) * hd], kg_ref[...])
        v = kv_ref[0, :, width + h * hd:width + (h + 1) * hd]
        s = lax.dot_general(qn.astype(BF16), kn.astype(BF16), (((1,), (1,)), ((), ())),
                            preferred_element_type=F32)
        m = jnp.max(s, axis=-1, keepdims=True)
        p = jnp.exp(s - m)
        l = jnp.sum(p, axis=-1, keepdims=True)
        o = jnp.dot(p.astype(BF16), v, preferred_element_type=F32) / l
        o_ref[:, h * hd:(h + 1) * hd] = o.astype(o_ref.dtype)


def mem_attention(proj, q_col_block, kv, q_g, k_g, batch, ts):
    T = proj.shape[0]
    _, M, two_w = kv.shape
    width = two_w // 2
    hd = width // MEM_HEADS
    per_b = T // batch // ts
    return pl.pallas_call(
        functools.partial(_mem_attn_kernel, heads=MEM_HEADS, hd=hd),
        grid=(batch, per_b),
        in_specs=[pl.BlockSpec((ts, width), lambda b, i: (b * per_b + i, q_col_block)),
                  pl.BlockSpec((1, M, two_w), lambda b, i: (b, 0, 0)),
                  pl.BlockSpec((1, hd), lambda b, i: (0, 0)),
                  pl.BlockSpec((1, hd), lambda b, i: (0, 0))],
        out_specs=pl.BlockSpec((ts, width), lambda b, i: (b * per_b + i, 0)),
        out_shape=jax.ShapeDtypeStruct((T, width), BF16),
        compiler_params=_params(("parallel", "arbitrary"), 32),
        name="mem_attention",
    )(proj, kv, q_g.reshape(1, hd), k_g.reshape(1, hd))


def _gelu_tanh(x):
    return 0.5 * x * (1.0 + jnp.tanh(0.7978845608028654 * (x + 0.044715 * x * x * x)))


def _rglru_kernel(u_ref, gb_ref, cw_ref, cb_ref, wr_ref, wi_ref, br_ref, bi_ref, lam_ref,
                  o_ref, a_s, b_s, *, seq, nb, blk, tc):
    wb = nb * blk
    neg_lam = -lam_ref[...]
    softplus = jnp.maximum(neg_lam, 0.0) + jnp.log1p(jnp.exp(-jnp.abs(neg_lam)))

    def gates(c, carry):
        t0 = pl.multiple_of(c * tc, tc)
        cur = u_ref[0, pl.ds(t0, tc), :].astype(F32)
        p0 = pl.multiple_of(jnp.maximum(t0 - 16, 0), 16)
        prev = u_ref[0, pl.ds(p0, 16), :].astype(F32)
        prev = jnp.where(c > 0, prev, 0.0)
        full = jnp.concatenate([prev, cur], axis=0)
        y = cb_ref[...] + cw_ref[0:1, :] * cur
        for k in range(1, CONV_WIDTH):
            y = y + cw_ref[k:k + 1, :] * full[16 - k:16 - k + tc, :]
        r_parts, i_parts = [], []
        for n in range(nb):
            yb = y[:, n * blk:(n + 1) * blk].astype(BF16)
            r_parts.append(jnp.dot(yb, wr_ref[n].astype(BF16), preferred_element_type=F32))
            i_parts.append(jnp.dot(yb, wi_ref[n].astype(BF16), preferred_element_type=F32))
        r = _sigmoid(jnp.concatenate(r_parts, axis=1) + br_ref[...])
        ig = _sigmoid(jnp.concatenate(i_parts, axis=1) + bi_ref[...])
        log_a = (-LRU_C) * r * softplus
        a = jnp.exp(log_a)
        a_s[pl.ds(t0, tc), :] = a
        b_s[pl.ds(t0, tc), :] = jnp.sqrt(-jnp.tanh(log_a) * (a * a + 1.0)) * (ig * y)
        return carry

    lax.fori_loop(0, seq // tc, gates, 0)

    row = lax.broadcasted_iota(jnp.int32, (8, wb), 0)

    def scan(c, h):
        t0 = pl.multiple_of(c * 8, 8)
        a = a_s[pl.ds(t0, 8), :]
        b = b_s[pl.ds(t0, 8), :]
        for s in (1, 2, 4):
            a_sh = pltpu.roll(a, s, axis=0)
            b_sh = pltpu.roll(b, s, axis=0)
            keep = row >= s
            b = jnp.where(keep, a * b_sh + b, b)
            a = jnp.where(keep, a * a_sh, a)
        hc = a * h + b
        b_s[pl.ds(t0, 8), :] = hc
        return hc[7:8, :]

    lax.fori_loop(0, seq // 8, scan, jnp.zeros((1, wb), F32), unroll=4)

    def gate_out(c, carry):
        t0 = pl.multiple_of(c * tc, tc)
        g = gb_ref[0, pl.ds(t0, tc), :].astype(F32)
        o_ref[0, pl.ds(t0, tc), :] = (_gelu_tanh(g) * b_s[pl.ds(t0, tc), :]).astype(o_ref.dtype)
        return carry

    lax.fori_loop(0, seq // tc, gate_out, 0)


def rglru(proj, conv_w, conv_b, w_r, b_r, w_i, b_i, lam, nb, tc):
    B, S, _ = proj.shape
    W = lam.shape[0]
    n_blocks, blk, _ = w_r.shape
    wb = nb * blk
    ncb = W // wb
    vec = lambda: pl.BlockSpec((1, wb), lambda b, j: (0, j))
    return pl.pallas_call(
        functools.partial(_rglru_kernel, seq=S, nb=nb, blk=blk, tc=tc),
        grid=(B, ncb),
        in_specs=[pl.BlockSpec((1, S, wb), lambda b, j: (b, 0, j)),
                  pl.BlockSpec((1, S, wb), lambda b, j: (b, 0, ncb + j)),
                  pl.BlockSpec((CONV_WIDTH, wb), lambda b, j: (0, j)),
                  vec(),
                  pl.BlockSpec((nb, blk, blk), lambda b, j: (j, 0, 0)),
                  pl.BlockSpec((nb, blk, blk), lambda b, j: (j, 0, 0)),
                  vec(), vec(), vec()],
        out_specs=pl.BlockSpec((1, S, wb), lambda b, j: (b, 0, j)),
        out_shape=jax.ShapeDtypeStruct((B, S, W), BF16),
        scratch_shapes=[pltpu.VMEM((S, wb), F32), pltpu.VMEM((S, wb), F32)],
        compiler_params=_params(("parallel", "arbitrary"), 32),
        name="rglru",
    )(proj, proj, conv_w, conv_b.reshape(1, W), w_r, w_i, b_r.reshape(1, W), b_i.reshape(1, W),
      lam.reshape(1, W))


def _heads_kernel(*refs, rank, heads, with_v):
    if with_v:
        lat_ref, gl_ref, w_ref, hg_ref, cos_ref, sin_ref, wv_ref, o_ref, v_ref = refs
    else:
        lat_ref, gl_ref, w_ref, hg_ref, cos_ref, sin_ref, o_ref = refs
    lat = lat_ref[...].astype(F32)
    cn = _rms_rows(lat[:, :rank], gl_ref[...])
    full = cn if lat.shape[1] == rank else jnp.concatenate([cn, lat[:, rank:]], axis=1)
    full = full.astype(BF16)
    lane = lax.broadcasted_iota(jnp.int32, cos_ref.shape, 1)
    half = QK_ROPE // 2
    for h in range(heads):
        t = jnp.dot(full, w_ref[:, h * HEAD_PAD:(h + 1) * HEAD_PAD], preferred_element_type=F32)
        ss = jnp.sum(t * t, axis=-1, keepdims=True) * (1.0 / QK_DIM)
        tn = t * lax.rsqrt(ss + EPS) * hg_ref[...]
        rp = tn[:, QK_NOPE:]
        swapped = jnp.where(lane < half, pltpu.roll(rp, LANES - half, axis=1),
                            pltpu.roll(rp, half, axis=1))
        rot = rp * cos_ref[...] + swapped * sin_ref[...]
        o_ref[:, h * HEAD_PAD:h * HEAD_PAD + QK_NOPE] = tn[:, :QK_NOPE].astype(o_ref.dtype)
        o_ref[:, h * HEAD_PAD + QK_NOPE:(h + 1) * HEAD_PAD] = rot.astype(o_ref.dtype)
    if with_v:
        v = jnp.dot(cn.astype(BF16), wv_ref[...], preferred_element_type=F32).astype(v_ref.dtype)
        ones = jnp.ones((v.shape[0], HEAD_PAD - V_DIM), v_ref.dtype)
        for h in range(heads):
            v_ref[:, h * HEAD_PAD:h * HEAD_PAD + V_DIM] = v[:, h * V_DIM:(h + 1) * V_DIM]
            v_ref[:, h * HEAD_PAD + V_DIM:(h + 1) * HEAD_PAD] = ones


def latent_heads(lat, lat_cols, rank, g_lat, w_pad, head_gain, cos_t, sin_t, heads, tm, w_v=None):
    T = lat.shape[0]
    n_out = heads * HEAD_PAD
    in_specs = [pl.BlockSpec((tm, lat_cols), lambda i: (i, 0)),
                pl.BlockSpec((1, rank), lambda i: (0, 0)),
                pl.BlockSpec((lat_cols, n_out), lambda i: (0, 0)),
                pl.BlockSpec((1, HEAD_PAD), lambda i: (0, 0)),
                pl.BlockSpec((tm, LANES), lambda i: (i, 0)),
                pl.BlockSpec((tm, LANES), lambda i: (i, 0))]
    args = [lat, g_lat.reshape(1, rank), w_pad, head_gain, cos_t, sin_t]
    out_shape = jax.ShapeDtypeStruct((T, n_out), BF16)
    out_specs = pl.BlockSpec((tm, n_out), lambda i: (i, 0))
    if w_v is not None:
        in_specs.append(pl.BlockSpec(w_v.shape, lambda i: (0, 0)))
        args.append(w_v)
        out_shape = (out_shape, jax.ShapeDtypeStruct((T, n_out), BF16))
        out_specs = (out_specs, pl.BlockSpec((tm, n_out), lambda i: (i, 0)))
    return pl.pallas_call(
        functools.partial(_heads_kernel, rank=rank, heads=heads, with_v=w_v is not None),
        grid=(T // tm,),
        in_specs=in_specs, out_specs=out_specs, out_shape=out_shape,
        compiler_params=_params(("parallel",), 48),
        name="latent_heads_kv" if w_v is not None else "latent_heads_q",
    )(*args)


def _flash_kernel(q_ref, k_ref, v_ref, o_ref, s_s, m_s, acc_s, *, tq, sub, n_kt):
    i = pl.program_id(2)
    m_s[...] = jnp.full(m_s.shape, -1e30, F32)
    acc_s[...] = jnp.zeros(acc_s.shape, F32)
    n_sub = tq // sub
    n_heads = m_s.shape[0]

    def scores(hd, kt, slot):
        cols = slice(hd * HEAD_PAD, (hd + 1) * HEAD_PAD)
        k = k_ref[0, pl.ds(kt * tq, tq), cols]
        for h in range(n_sub):
            rows = slice(h * sub, (h + 1) * sub)
            s_s[hd, slot, rows, :] = lax.dot_general(q_ref[0, rows, cols], k, (((1,), (1,)), ((), ())),
                                                     preferred_element_type=F32)

    def softmax_pv(hd, k0, slot, masked):
        cols = slice(hd * HEAD_PAD, (hd + 1) * HEAD_PAD)
        for h in range(n_sub):
            rows = slice(h * sub, (h + 1) * sub)
            nk = (h + 1) * sub if masked else tq
            s = s_s[hd, slot, rows, :nk]
            if masked:
                r = lax.broadcasted_iota(jnp.int32, s.shape, 0) + h * sub
                c = lax.broadcasted_iota(jnp.int32, s.shape, 1)
                s = jnp.where(c <= r, s, -1e30)
            v = v_ref[0, pl.ds(k0, nk), cols]
            m_prev = m_s[hd, rows, :]
            m_new = jnp.maximum(m_prev, jnp.max(s, axis=-1, keepdims=True))
            alpha = jnp.exp2(m_prev - m_new)
            p = jnp.exp2(s - jnp.concatenate([m_new] * (nk // LANES), axis=1))
            acc_s[hd, rows, :] = (jnp.concatenate([alpha, alpha], axis=1) * acc_s[hd, rows, :]
                                  + jnp.dot(p.astype(BF16), v, preferred_element_type=F32))
            m_s[hd, rows, :] = m_new

    for hd in range(n_heads):
        scores(hd, 0, 0)
    for kt in range(n_kt - 1):
        @pl.when(kt < i)
        def _(kt=kt):
            for hd in range(n_heads):
                scores(hd, kt + 1, (kt + 1) % 2)
                softmax_pv(hd, kt * tq, kt % 2, False)
    for hd in range(n_heads):
        softmax_pv(hd, pl.multiple_of(i * tq, tq), i % 2, True)
        o_ref[0, :, hd * V_DIM:(hd + 1) * V_DIM] = (acc_s[hd, :, :V_DIM] / acc_s[hd, :, V_DIM:]).astype(o_ref.dtype)


def flash_attention(q, k, v, heads, tq):
    B, S, _ = q.shape
    hp = HEADS_PER_FLASH_STEP
    assert heads % hp == 0
    return pl.pallas_call(
        functools.partial(_flash_kernel, tq=tq, sub=SUB_FLASH, n_kt=S // tq),
        grid=(B, heads // hp, S // tq),
        in_specs=[pl.BlockSpec((1, tq, hp * HEAD_PAD), lambda b, h, i: (b, i, h)),
                  pl.BlockSpec((1, S, hp * HEAD_PAD), lambda b, h, i: (b, 0, h)),
                  pl.BlockSpec((1, S, hp * HEAD_PAD), lambda b, h, i: (b, 0, h))],
        out_specs=pl.BlockSpec((1, tq, hp * V_DIM), lambda b, h, i: (b, i, h)),
        out_shape=jax.ShapeDtypeStruct((B, S, heads * V_DIM), BF16),
        scratch_shapes=[pltpu.VMEM((hp, 2, tq, tq), F32), pltpu.VMEM((hp, tq, LANES), F32),
                        pltpu.VMEM((hp, tq, HEAD_PAD), F32)],
        compiler_params=_params(("parallel", "parallel", "arbitrary"), 40),
        name="flash_attention",
    )(q, k, v)


def _router_rows(x_ref, g_ref, w_ref, b_ref, o_ref, xt_ref, tm, chunk):
    w = w_ref[...]
    w_hi = w.astype(BF16)
    w_lo = (w - w_hi.astype(F32)).astype(BF16)
    lane_tiles = x_ref.shape[1] // LANES
    for r0 in range(0, tm, chunk):
        xn = _rms_rows(x_ref[r0:r0 + chunk, :], g_ref[...])
        x_hi = xn.astype(BF16)
        x_lo = (xn - x_hi.astype(F32)).astype(BF16)
        acc = jnp.dot(x_hi, w_hi, preferred_element_type=F32)
        acc += jnp.dot(x_lo, w_hi, preferred_element_type=F32)
        acc += jnp.dot(x_hi, w_lo, preferred_element_type=F32)
        o_ref[r0:r0 + chunk, :] = acc + b_ref[...]
        for j in range(lane_tiles):
            xt_ref[pl.ds(r0 * lane_tiles + j, chunk, stride=lane_tiles), :] = xn[:, j * LANES:(j + 1) * LANES]


def _plan_kernel(lg_ref, out_ref, tiles_ref, row_tok_ref, c_s, info_s, inv_s, *, n_tok, n_groups, epg, tm, chunk):
    n_exp = n_groups * epg
    lane = lax.broadcasted_iota(jnp.int32, (chunk, LANES), 1).astype(F32)
    rr = lax.broadcasted_iota(jnp.int32, (chunk, chunk), 0)
    cc = lax.broadcasted_iota(jnp.int32, (chunk, chunk), 1)
    tri = jnp.where(cc <= rr, 1.0, 0.0).astype(BF16)
    neg_inf = -jnp.inf

    def first_argmax(vals, vmax):
        return jnp.min(jnp.where(vals == vmax, lane, float(LANES)), axis=1, keepdims=True)

    def decide(c, carry):
        r0 = pl.multiple_of(c * chunk, chunk)
        lg = lg_ref[pl.ds(r0, chunk), :]
        gl = jnp.where(lane < n_groups, lg, neg_inf)
        gmax = jnp.max(gl, axis=1, keepdims=True)
        g_idx = first_argmax(gl, gmax)
        p_top = 1.0 / jnp.sum(jnp.exp(gl - gmax), axis=1, keepdims=True)
        lo = n_groups + g_idx * epg
        el = jnp.where(jnp.logical_and(lane >= lo, lane < lo + epg), lg, neg_inf)
        l1 = jnp.max(el, axis=1, keepdims=True)
        i1 = first_argmax(el, l1)
        el2 = jnp.where(lane == i1, neg_inf, el)
        l2 = jnp.max(el2, axis=1, keepdims=True)
        i2 = first_argmax(el2, l2)
        d = jnp.exp(l2 - l1)
        w1 = 1.0 / (1.0 + d)
        e1 = i1 - n_groups
        e2 = i2 - n_groups
        onehot = jnp.where(jnp.logical_or(lane == e1, lane == e2), 1.0, 0.0)
        cs = jnp.dot(tri, onehot.astype(BF16), preferred_element_type=F32) + carry
        c_s[pl.ds(r0, chunk), :] = cs
        info_s[pl.ds(r0, chunk), :] = jnp.where(
            lane == 0, e1, jnp.where(lane == 1, e2, jnp.where(
                lane == 2, p_top * w1, jnp.where(lane == 3, p_top * (d * w1), 0.0))))
        return cs[chunk - 1:chunk, :]

    counts = lax.fori_loop(0, n_tok // chunk, decide, jnp.zeros((1, LANES), F32))

    tiles = jnp.floor((counts + (tm - 1)) * (1.0 / tm))
    jj = lax.broadcasted_iota(jnp.int32, (LANES, LANES), 0)
    ee = lax.broadcasted_iota(jnp.int32, (LANES, LANES), 1)
    upper = jnp.where(jj <= ee, 1.0, 0.0).astype(BF16)
    tile_end = jnp.dot(jnp.broadcast_to(tiles, (8, LANES)).astype(BF16), upper,
                       preferred_element_type=F32)[0:1, :]
    tile_start = tile_end - tiles
    row_start = tile_start * tm

    inv_s[...] = jnp.zeros(inv_s.shape, F32)
    off_lane = lax.broadcasted_iota(jnp.int32, (chunk, tm), 1).astype(F32)
    tile_row = lax.broadcasted_iota(jnp.int32, (LANES, chunk), 0).astype(F32)
    tok_local = lax.broadcasted_iota(jnp.int32, (1, chunk), 1)

    def place(c, carry):
        r0 = pl.multiple_of(c * chunk, chunk)
        info = info_s[pl.ds(r0, chunk), :]
        base = row_start + c_s[pl.ds(r0, chunk), :] - 1.0
        pos = [jnp.sum(jnp.where(lane == info[:, k:k + 1], base, 0.0), axis=1, keepdims=True)
               for k in range(TOP_K_IN_GROUP)]
        packed = jnp.where(lane == 0, pos[0], jnp.where(lane == 1, pos[1], info))
        out_ref[pl.ds(r0, chunk), :] = packed
        pos_rows = jnp.transpose(packed)
        tok = r0 + tok_local
        tok_hi = lax.shift_right_logical(tok, LANES.bit_length() - 1).astype(F32)
        tok_lo = (tok & (LANES - 1)).astype(F32)
        for k in range(TOP_K_IN_GROUP):
            tile_of = jnp.floor(pos[k] * (1.0 / tm))
            onehot_off = jnp.where(off_lane == pos[k] - tile_of * tm, 1.0, 0.0).astype(BF16)
            sel = tile_row == jnp.floor(pos_rows[k:k + 1, :] * (1.0 / tm))
            ids = jnp.concatenate([jnp.where(sel, tok_hi, 0.0), jnp.where(sel, tok_lo, 0.0)], axis=0)
            inv_s[...] += jnp.dot(ids.astype(BF16), onehot_off, preferred_element_type=F32)
        return carry

    lax.fori_loop(0, n_tok // chunk, place, 0)
    row_tok_ref[...] = (inv_s[:LANES, :] * LANES + inv_s[LANES:, :]).astype(jnp.int32)

    ti = jj.astype(F32)
    lane_sq = ee.astype(F32)
    is_exp = lane_sq < n_exp
    n_used = tile_end[:, n_exp - 1:n_exp]
    ti_c = jnp.minimum(ti, n_used - 1.0)
    tile_e = jnp.sum(jnp.where(jnp.logical_and(is_exp, tile_end <= ti_c), 1.0, 0.0), axis=1, keepdims=True)
    first = jnp.sum(jnp.where(jnp.logical_and(jnp.logical_and(is_exp, tile_start == ti), tiles > 0.0),
                              1.0, 0.0), axis=1, keepdims=True)
    nonempty = jnp.logical_and(is_exp, tiles > 0.0)
    def next_nonempty(e):
        nxt = jnp.min(jnp.where(jnp.logical_and(nonempty, lane_sq > e), lane_sq, float(LANES)),
                      axis=1, keepdims=True)
        return jnp.where(nxt >= LANES, e, nxt)

    next_e = next_nonempty(tile_e)
    next2_e = next_nonempty(next_e)
    group = jnp.sum(jnp.where(jnp.logical_and(nonempty, lane_sq < tile_e), 1.0, 0.0), axis=1, keepdims=True)
    wslot = group - 2.0 * jnp.floor(group * 0.5)
    fields = (tile_e, first, n_used, next_e, wslot, next2_e)
    assert len(fields) == N_TILE_FIELDS
    packed = jnp.zeros((LANES, LANES), F32)
    for f, val in enumerate(fields):
        packed = jnp.where(lane_sq == f, val, packed)
    tiles_ref[...] = packed.astype(jnp.int32)


def routing_plan(logits, n_groups, epg, tm):
    T = logits.shape[0]
    n_exp = n_groups * epg
    n_tiles = T * TOP_K_IN_GROUP // tm + n_exp
    assert n_tiles <= LANES and T <= LANES * LANES
    out, tiles, row_tok = pl.pallas_call(
        functools.partial(_plan_kernel, n_tok=T, n_groups=n_groups, epg=epg, tm=tm, chunk=PLAN_CHUNK),
        grid=(1,),
        in_specs=[pl.BlockSpec((T, LANES), lambda i: (0, 0))],
        out_specs=(pl.BlockSpec((T, LANES), lambda i: (0, 0)),
                   pl.BlockSpec((LANES, LANES), lambda i: (0, 0)),
                   pl.BlockSpec((LANES, tm), lambda i: (0, 0))),
        out_shape=(jax.ShapeDtypeStruct((T, LANES), F32), jax.ShapeDtypeStruct((LANES, LANES), jnp.int32),
                   jax.ShapeDtypeStruct((LANES, tm), jnp.int32)),
        scratch_shapes=[pltpu.VMEM((T, LANES), F32), pltpu.VMEM((T, LANES), F32),
                        pltpu.VMEM((2 * LANES, tm), F32)],
        compiler_params=_params(("arbitrary",), 40),
        name="routing_plan",
    )(logits)
    pos = out[:, :TOP_K_IN_GROUP].astype(jnp.int32).reshape(-1)
    tile_info = tiles[:n_tiles, :N_TILE_FIELDS].T.reshape(-1)
    return out, pos, tile_info, row_tok[:n_tiles].reshape(-1), n_tiles


def _row_copy(src_hbm, tok, dst_row, sem, rows=1):
    start = tok if rows == 1 else pl.multiple_of(tok * rows, rows)
    return pltpu.make_async_copy(src_hbm.at[pl.ds(start, rows)], dst_row, sem)


def _start_row_gather(src_hbm, idx_ref, base, n, dst_row, sem, rows=1):
    def body(r, carry):
        _row_copy(src_hbm, idx_ref[base + r], dst_row(r), sem, rows).start()
        return carry
    lax.fori_loop(0, n, body, 0, unroll=8)


def _wait_row_gather(src_hbm, n, dst_all, sem):
    pltpu.make_async_copy(src_hbm.at[pl.ds(0, n)], dst_all, sem).wait()


def _gmm_kernel(info_ref, row_tok_ref, x_hbm, wg_hbm, wu_hbm, wd_hbm, o_ref,
                xbuf, sem, wg_f, wu_f, wd_f, wsem, wg_bf, wu_bf, wd_bf, *, tm, n_tiles, w_off, n_chunks, tok_rows):
    i = pl.program_id(0)
    expert = info_ref[i]
    first = info_ref[n_tiles + i]
    n_used = info_ref[2 * n_tiles]
    next_expert = info_ref[3 * n_tiles + i]
    wslot = info_ref[4 * n_tiles + i]
    next2_expert = info_ref[5 * n_tiles + i]
    slot = i % GATHER_BUFS
    ahead = GATHER_BUFS - 1

    def wait_slot(s):
        _wait_row_gather(x_hbm, tm * tok_rows, xbuf.at[s], sem.at[s])

    def weight_copies(e, s):
        copies = []
        for hbm, buf in ((wg_hbm, wg_f), (wu_hbm, wu_f), (wd_hbm, wd_f)):
            rows = hbm.shape[1] // n_chunks
            for c in range(n_chunks):
                copies.append(pltpu.make_async_copy(hbm.at[w_off + e, pl.ds(c * rows, rows)],
                                                    buf.at[s, pl.ds(c * rows, rows)], wsem.at[s]))
        return copies

    def start_weights(e, s):
        for c, cp in enumerate(weight_copies(e, s)):
            cp.start(priority=c % DMA_PRIORITIES)

    @pl.when(jnp.logical_and(i == 0, n_used > 0))
    def _():
        start_weights(expert, 0)

        @pl.when(next_expert != expert)
        def _():
            start_weights(next_expert, 1)

        for t in range(ahead):
            _start_row_gather(x_hbm, row_tok_ref, min(t, n_tiles - 1) * tm, tm,
                              lambda r, t=t: xbuf.at[t, pl.ds(pl.multiple_of(r * tok_rows, tok_rows), tok_rows)],
                              sem.at[t], tok_rows)

    @pl.when(i < n_used)
    def _():
        @pl.when(first == 1)
        def _():
            for cp in weight_copies(expert, wslot):
                cp.wait()
            for src, dst in ((wg_f, wg_bf), (wu_f, wu_bf), (wd_f, wd_bf)):
                rows = dst.shape[0] // n_chunks

                def convert(c, carry, src=src, dst=dst, rows=rows):
                    r0 = pl.multiple_of(c * rows, rows)
                    dst[pl.ds(r0, rows), :] = src[wslot, pl.ds(r0, rows), :].astype(BF16)
                    return carry
                lax.fori_loop(0, n_chunks, convert, 0)

            @pl.when(next2_expert != next_expert)
            def _():
                start_weights(next2_expert, wslot)

        wait_slot(slot)
        xn = jnp.concatenate([xbuf[slot, pl.ds(j, tm, stride=tok_rows), :].astype(BF16)
                              for j in range(tok_rows)], axis=1)
        base = jnp.minimum(i + ahead, n_tiles - 1) * tm
        nslot = (i + ahead) % GATHER_BUFS
        for r in range(tm):
            _row_copy(x_hbm, row_tok_ref[base + r], xbuf.at[nslot, pl.ds(r * tok_rows, tok_rows)],
                      sem.at[nslot], tok_rows).start(priority=r % DMA_PRIORITIES)
        hg = jnp.dot(xn, wg_bf[...], preferred_element_type=F32)
        hu = jnp.dot(xn, wu_bf[...], preferred_element_type=F32)
        hh = (hg * jax.nn.sigmoid(hg)) * hu
        o_ref[...] = jnp.dot(hh.astype(BF16), wd_bf[...], preferred_element_type=F32)

    @pl.when(i >= n_used)
    def _():
        @pl.when(jnp.logical_and(i < n_used + ahead, n_used > 0))
        def _():
            wait_slot(slot)
        o_ref[...] = jnp.zeros(o_ref.shape, o_ref.dtype)

    @pl.when(i == n_tiles - 1)
    def _():
        for d in range(1, ahead + 1):
            @pl.when(n_used >= n_tiles - ahead + d)
            def _(d=d):
                wait_slot((n_tiles - 1 + d) % GATHER_BUFS)


def grouped_expert_mlp(xt, tile_info, row_tok, w_gate, w_up, w_down, layer, n_tiles, tm):
    _, D, F = w_gate.shape
    tok_rows = D // LANES
    layer_idx, depth = layer
    off = layer_idx * (w_gate.shape[0] // depth)
    any_spec = pl.BlockSpec(memory_space=pl.ANY)
    grid_spec = pltpu.PrefetchScalarGridSpec(
        num_scalar_prefetch=2,
        grid=(n_tiles,),
        in_specs=[any_spec, any_spec, any_spec, any_spec],
        out_specs=pl.BlockSpec((tm, D), lambda i, ti, rt: (i, 0)),
        scratch_shapes=[pltpu.VMEM((GATHER_BUFS, tm * tok_rows, LANES), F32),
                        pltpu.SemaphoreType.DMA((GATHER_BUFS,)),
                        pltpu.VMEM((2, D, F), F32), pltpu.VMEM((2, D, F), F32), pltpu.VMEM((2, F, D), F32),
                        pltpu.SemaphoreType.DMA((2,)),
                        pltpu.VMEM((D, F), BF16), pltpu.VMEM((D, F), BF16), pltpu.VMEM((F, D), BF16)],
    )
    return pl.pallas_call(
        functools.partial(_gmm_kernel, tm=tm, n_tiles=n_tiles, w_off=off, n_chunks=WEIGHT_DMA_CHUNKS,
                          tok_rows=tok_rows),
        grid_spec=grid_spec,
        out_shape=jax.ShapeDtypeStruct((n_tiles * tm, D), F32),
        compiler_params=_params(("arbitrary",), 56),
        name="grouped_expert_mlp",
    )(tile_info, row_tok, xt, w_gate, w_up, w_down)


def _combine_kernel(pos_ref, x_ref, plan_ref, y_hbm, o_ref, ybuf, sem, *, tc, top_k):
    i = pl.program_id(0)
    n = pl.num_programs(0)
    slot = i % 2

    def start(tile, s):
        base = tile * (tc * top_k)
        for r in range(tc):
            for k in range(top_k):
                _row_copy(y_hbm, pos_ref[base + r * top_k + k], ybuf.at[s, k, pl.ds(r, 1)], sem.at[s]).start()

    @pl.when(i == 0)
    def _():
        start(0, 0)

    for k in range(top_k):
        _wait_row_gather(y_hbm, tc, ybuf.at[slot, k], sem.at[slot])

    @pl.when(i + 1 < n)
    def _():
        start(i + 1, 1 - slot)

    acc = x_ref[...]
    for k in range(top_k):
        acc = acc + plan_ref[:, top_k + k:top_k + k + 1] * ybuf[slot, k]
    o_ref[...] = acc


def moe_combine(x, y_sorted, plan_out, pos, tc):
    T, D = x.shape
    top_k = TOP_K_IN_GROUP
    grid_spec = pltpu.PrefetchScalarGridSpec(
        num_scalar_prefetch=1,
        grid=(T // tc,),
        in_specs=[pl.BlockSpec((tc, D), lambda i, p: (i, 0)),
                  pl.BlockSpec((tc, LANES), lambda i, p: (i, 0)),
                  pl.BlockSpec(memory_space=pl.ANY)],
        out_specs=pl.BlockSpec((tc, D), lambda i, p: (i, 0)),
        scratch_shapes=[pltpu.VMEM((2, top_k, tc, D), F32), pltpu.SemaphoreType.DMA((2,))],
    )
    return pl.pallas_call(
        functools.partial(_combine_kernel, tc=tc, top_k=top_k),
        grid_spec=grid_spec,
        out_shape=jax.ShapeDtypeStruct((T, D), F32),
        compiler_params=_params(("arbitrary",), 32),
        name="moe_combine",
    )(pos, x, plan_out, y_sorted)


def _combine_mm_kernel(pos_ref, x_ref, plan_ref, y_hbm, g_ref, w_hbm, xnew_ref, o_ref,
                       ybuf, ysem, xn_ref, w_bf, stage, wsem, *, layer, tc, chunk, tn, top_k):
    i = pl.program_id(0)
    n = pl.num_programs(0)
    slot = i % 2

    def start(tile, s):
        base = tile * (tc * top_k)
        for r in range(tc):
            for k in range(top_k):
                _row_copy(y_hbm, pos_ref[base + r * top_k + k], ybuf.at[s, k, pl.ds(r, 1)], ysem.at[s]).start()

    def wait(s):
        for k in range(top_k):
            _wait_row_gather(y_hbm, tc, ybuf.at[s, k], ysem.at[s])

    @pl.when(i == 0)
    def _():
        start(0, 0)
        _stage_weight_bf16(w_hbm, layer, w_bf, stage, wsem, tn)

    wait(slot)
    for r0 in range(0, tc, chunk):
        rows = slice(r0, r0 + chunk)
        acc = x_ref[rows, :]
        for k in range(top_k):
            acc = acc + plan_ref[rows, top_k + k:top_k + k + 1] * ybuf[slot, k, rows, :]
        xnew_ref[rows, :] = acc
        xn_ref[rows, :] = _rms_rows(acc, g_ref[...]).astype(BF16)
    start(jnp.minimum(i + 1, n - 1), 1 - slot)
    for j in range(o_ref.shape[1] // tn):
        o_ref[:, j * tn:(j + 1) * tn] = jnp.dot(xn_ref[...], w_bf[:, j * tn:(j + 1) * tn],
                                                preferred_element_type=F32).astype(o_ref.dtype)

    @pl.when(i == n - 1)
    def _():
        wait(1 - slot)


def combine_norm_matmul(x, y_sorted, plan_out, pos, g, w, layer, out_dtype, tc):
    T, D = x.shape
    N = w.shape[2]
    top_k = TOP_K_IN_GROUP
    tn = _col_tile(N)
    grid_spec = pltpu.PrefetchScalarGridSpec(
        num_scalar_prefetch=1,
        grid=(T // tc,),
        in_specs=[pl.BlockSpec((tc, D), lambda i, p: (i, 0)),
                  pl.BlockSpec((tc, LANES), lambda i, p: (i, 0)),
                  pl.BlockSpec(memory_space=pl.ANY),
                  pl.BlockSpec((1, D), lambda i, p: (0, 0)),
                  pl.BlockSpec(memory_space=pl.ANY)],
        out_specs=(pl.BlockSpec((tc, D), lambda i, p: (i, 0)),
                   pl.BlockSpec((tc, N), lambda i, p: (i, 0))),
        scratch_shapes=[pltpu.VMEM((2, top_k, tc, D), F32), pltpu.SemaphoreType.DMA((2,)),
                        pltpu.VMEM((tc, D), BF16), pltpu.VMEM((D, N), BF16), pltpu.VMEM((2, D, tn), F32),
                        pltpu.SemaphoreType.DMA((2,))],
    )
    return pl.pallas_call(
        functools.partial(_combine_mm_kernel, layer=layer, tc=tc, chunk=min(tc, 128), tn=tn, top_k=top_k),
        grid_spec=grid_spec,
        out_shape=(jax.ShapeDtypeStruct((T, D), F32), jax.ShapeDtypeStruct((T, N), out_dtype)),
        compiler_params=_params(("arbitrary",), 56),
        name="combine_norm_matmul",
    )(pos, x, plan_out, y_sorted, g.reshape(1, D), w)


def hier_moe_experts(logits, xt, n_groups, epg, w_gate, w_up, w_down, layer):
    plan_out, pos, tile_info, row_tok, n_tiles = routing_plan(logits, n_groups, epg, TM_EXPERT)
    y_sorted = grouped_expert_mlp(xt, tile_info, row_tok, w_gate, w_up, w_down, layer, n_tiles, TM_EXPERT)
    return y_sorted, plan_out, pos


def _pad_head_cols(w, heads, real):
    r = w.shape[0]
    return jnp.pad(w.reshape(r, heads, real), ((0, 0), (0, 0), (0, HEAD_PAD - real))).reshape(r, heads * HEAD_PAD)


def _rope_tables(positions):
    half = QK_ROPE // 2
    inv_freq = ROPE_THETA ** (-jnp.arange(half, dtype=F32) / half)
    ang = positions.astype(F32).reshape(-1, 1) * inv_freq[None, :]
    cos, sin = jnp.cos(ang), jnp.sin(ang)
    zeros = jnp.zeros((ang.shape[0], LANES - QK_ROPE), F32)
    return (jnp.concatenate([cos, cos, zeros], axis=1),
            jnp.concatenate([-sin, sin, zeros], axis=1))


def kernel(x, mem, positions, norm_mix_g, norm_ffn_g, w_o, mem_norm_g, w_mem_kv, mem_q_norm_g, mem_k_norm_g, w_in_a, conv_w, conv_b, w_lru_r, b_lru_r, w_lru_i, b_lru_i, lru_lambda, kv_in_norm_g, w_dkv, kv_latent_norm_g, w_uk, w_uv, k_head_norm_g, w_in_b, q_latent_norm_g, w_uq, q_head_norm_g, w_router_grp, b_router_grp, w_router_exp, b_router_exp, w_exp_gate, w_exp_up, w_exp_down):
    B, S, D = x.shape
    T = B * S
    M = mem.shape[1]
    depth = norm_mix_g.shape[0]
    n_a = w_in_a.shape[0]
    lru_w = lru_lambda.shape[1]
    mem_w = w_mem_kv.shape[2] // 2
    heads = w_uv.shape[1] // V_DIM
    kv_rank = kv_latent_norm_g.shape[0]
    q_rank = q_latent_norm_g.shape[1]
    n_groups, epg = w_exp_gate.shape[1], w_exp_gate.shape[2]
    n_exp = n_groups * epg
    d_exp = w_exp_gate.shape[-1]

    xr = x.reshape(T, D)
    mem2 = mem.reshape(B * M, D)
    cos_t, sin_t = _rope_tables(positions)
    wg_all = w_exp_gate.reshape(depth * n_exp, D, d_exp)
    wu_all = w_exp_up.reshape(depth * n_exp, D, d_exp)
    wd_all = w_exp_down.reshape(depth * n_exp, d_exp, D)
    pad_r = LANES - n_groups - n_exp
    w_router = jnp.concatenate([w_router_grp, w_router_exp, jnp.zeros((depth, D, pad_r), F32)], axis=2)
    b_router = jnp.concatenate([b_router_grp, b_router_exp, jnp.zeros((depth, pad_r), F32)], axis=1)

    def shared_kv(x_res):
        lat_cols = kv_rank + LANES
        w_dkv_pad = jnp.pad(w_dkv, ((0, 0), (0, lat_cols - w_dkv.shape[1])))[None]
        ckv = norm_matmul(x_res, kv_in_norm_g, w_dkv_pad, 0, F32, tm=TM_PROJ)
        eye = jnp.pad(jnp.eye(QK_ROPE, dtype=F32), ((0, LANES - QK_ROPE), (0, 0)))
        eye3 = jnp.broadcast_to(eye[:, None, :], (LANES, heads, QK_ROPE))
        rope_rows = jnp.pad(eye3, ((0, 0), (0, 0), (QK_NOPE, HEAD_PAD - QK_DIM))).reshape(LANES, heads * HEAD_PAD)
        wk = jnp.concatenate([_pad_head_cols(w_uk, heads, QK_NOPE), rope_rows], axis=0).astype(BF16)
        kg = jnp.pad(k_head_norm_g, (0, HEAD_PAD - QK_DIM)).reshape(1, HEAD_PAD)
        k_flat, v_flat = latent_heads(ckv, lat_cols, kv_rank, kv_latent_norm_g, wk, kg, cos_t, sin_t,
                                      heads, tm=TM_HEADS, w_v=w_uv.astype(BF16))
        return k_flat.reshape(B, S, -1), v_flat.reshape(B, S, -1)

    def in_proj(x_res, pending, g, w, idx):
        if pending is None:
            return x_res, norm_matmul(x_res, g, w, idx, BF16, tm=TM_PROJ)
        return combine_norm_matmul(x_res, *pending, g, w, idx, BF16, TC_COMBINE)

    k_sh = v_sh = None
    pending = None
    for l in range(depth):
        kv_mem = norm_matmul(mem2, mem_norm_g[l], w_mem_kv, l, BF16, tm=TM_PROJ).reshape(B, M, 2 * mem_w)
        if l < n_a:
            xr, proj = in_proj(xr, pending, norm_mix_g[l], w_in_a, l)
            y_mix = rglru(proj.reshape(B, S, -1), conv_w[l], conv_b[l], w_lru_r[l], b_lru_r[l],
                          w_lru_i[l], b_lru_i[l], lru_lambda[l], nb=LRU_BLOCKS_PER_STEP,
                          tc=TC_LRU).reshape(T, lru_w)
            q_col = 2 * lru_w // mem_w
        else:
            j = l - n_a
            xr, proj = in_proj(xr, pending, norm_mix_g[l], w_in_b, j)
            if l == n_a:
                k_sh, v_sh = shared_kv(xr)
            wq = _pad_head_cols(w_uq[j], heads, QK_DIM).astype(BF16)
            qg = jnp.pad(q_head_norm_g[j] * (QK_DIM ** -0.5 * math.log2(math.e)),
                         (0, HEAD_PAD - QK_DIM)).reshape(1, HEAD_PAD)
            q = latent_heads(proj, q_rank, q_rank, q_latent_norm_g[j], wq, qg, cos_t, sin_t, heads, tm=TM_HEADS)
            y_mix = flash_attention(q.reshape(B, S, -1), k_sh, v_sh, heads, tq=TQ_FLASH).reshape(T, heads * V_DIM)
            q_col = q_rank // mem_w
        y_mem = mem_attention(proj, q_col, kv_mem, mem_q_norm_g[l], mem_k_norm_g[l], B, ts=TS_MEM)
        xr, logits, xt = out_proj(y_mix, y_mem, w_o, l, xr, norm_ffn_g[l], w_router[l],
                                  b_router[l].reshape(1, LANES), tm=TM_PROJ)
        pending = hier_moe_experts(logits, xt, n_groups, epg, wg_all, wu_all, wd_all, (l, depth))
    xr = moe_combine(xr, *pending, TC_COMBINE)
    return xr.reshape(B, S, D)
```

```python
import functools
import math

import jax
import jax.numpy as jnp
from jax import lax
from jax.experimental import pallas as pl
from jax.experimental.pallas import tpu as pltpu

F32 = jnp.float32
BF16 = jnp.bfloat16
EPS = 1e-6

MEM_HEADS = 4
CONV_WIDTH = 4
LRU_C = 8.0
QK_NOPE = 128
QK_ROPE = 64
QK_DIM = QK_NOPE + QK_ROPE
V_DIM = 128
ROPE_THETA = 10000.0
TOP_K_IN_GROUP = 2

LANES = 128
HEAD_PAD = 2 * LANES
MIB = 1024 * 1024
DMA_PRIORITIES = 2

TM_PROJ = 512
TN_PROJ = 512
TS_MEM = 512
TQ_FLASH = 512
SUB_FLASH = 256
HEADS_PER_FLASH_STEP = 6
TM_HEADS = 512
TM_EXPERT = 256
WEIGHT_DMA_CHUNKS = 4
GATHER_BUFS = 3
TC_COMBINE = 256
LRU_BLOCKS_PER_STEP = 6
TC_LRU = 256
PLAN_CHUNK = 256
N_TILE_FIELDS = 6


def _params(sem, vmem_mib):
    return pltpu.CompilerParams(dimension_semantics=sem, vmem_limit_bytes=vmem_mib * MIB)


HI_HALF = -65536


def _pack_bf16_pair(lo, hi):
    lo_bits = pltpu.bitcast(lo.astype(BF16).astype(F32), jnp.int32)
    hi_bits = pltpu.bitcast(hi.astype(BF16).astype(F32), jnp.int32)
    return ((lo_bits >> 16) & 0xFFFF) | (hi_bits & HI_HALF)


def _unpack_bf16_pair(word):
    return pltpu.bitcast(word << 16, F32), pltpu.bitcast(word & HI_HALF, F32)


def _token_major_store(ref, r0, rows, packed):
    n = packed.shape[1] // LANES
    for j in range(n):
        ref[pl.ds(r0 * n + j, rows, stride=n), :] = packed[:, j * LANES:(j + 1) * LANES]


def _token_major_load(load_rows, rows, n):
    return jnp.concatenate([load_rows(pl.ds(j, rows, stride=n)) for j in range(n)], axis=1)


def _sigmoid(z):
    return 0.5 * jnp.tanh(0.5 * z) + 0.5


def _rms_rows(x, g):
    x = x.astype(F32)
    ms = jnp.mean(x * x, axis=-1, keepdims=True)
    return x * lax.rsqrt(ms + EPS) * g


def _col_tile(n):
    return TN_PROJ if n % TN_PROJ == 0 else n


def _stage_weight_bf16(w_hbm, layer, w_bf, stage, sem, tn):
    n = w_bf.shape[1] // tn

    def copy(j):
        return pltpu.make_async_copy(w_hbm.at[layer, :, pl.ds(j * tn, tn)], stage.at[j % 2], sem.at[j % 2])

    copy(0).start()
    for j in range(n):
        if j + 1 < n:
            copy(j + 1).start()
        copy(j).wait()
        w_bf[:, j * tn:(j + 1) * tn] = stage[j % 2].astype(BF16)


def _norm_mm_kernel(x_ref, g_ref, w_hbm, o_ref, xn_ref, w_bf, stage, sem, *, layer, tm, chunk, tn):
    @pl.when(pl.program_id(0) == 0)
    def _():
        _stage_weight_bf16(w_hbm, layer, w_bf, stage, sem, tn)

    for r0 in range(0, tm, chunk):
        xn_ref[r0:r0 + chunk, :] = _rms_rows(x_ref[r0:r0 + chunk, :], g_ref[...]).astype(BF16)

    def cols(j, carry):
        c0 = pl.multiple_of(j * tn, tn)
        o_ref[:, pl.ds(c0, tn)] = jnp.dot(xn_ref[...], w_bf[:, pl.ds(c0, tn)],
                                          preferred_element_type=F32).astype(o_ref.dtype)
        return carry
    lax.fori_loop(0, o_ref.shape[1] // tn, cols, 0)


def norm_matmul(x, g, w, layer, out_dtype, tm):
    T, K = x.shape
    N = w.shape[2]
    tm = min(tm, T)
    chunk = min(tm, 256)
    tn = _col_tile(N)
    return pl.pallas_call(
        functools.partial(_norm_mm_kernel, layer=layer, tm=tm, chunk=chunk, tn=tn),
        grid=(T // tm,),
        in_specs=[pl.BlockSpec((tm, K), lambda i: (i, 0)),
                  pl.BlockSpec((1, K), lambda i: (0, 0)),
                  pl.BlockSpec(memory_space=pl.ANY)],
        out_specs=pl.BlockSpec((tm, N), lambda i: (i, 0)),
        out_shape=jax.ShapeDtypeStruct((T, N), out_dtype),
        scratch_shapes=[pltpu.VMEM((tm, K), BF16), pltpu.VMEM((K, N), BF16), pltpu.VMEM((2, K, tn), F32),
                        pltpu.SemaphoreType.DMA((2,))],
        compiler_params=_params(("arbitrary",), 56),
        name="norm_matmul",
    )(x, g.reshape(1, K), w)


def _out_proj_kernel(ya_ref, yb_ref, w_hbm, x_ref, g_ref, wr_ref, br_ref, o_ref, lg_ref, xt_ref,
                     w_bf, stage, sem, *, layer, tm, tn):
    @pl.when(pl.program_id(0) == 0)
    def _():
        _stage_weight_bf16(w_hbm, layer, w_bf, stage, sem, tn)

    wa_rows = ya_ref.shape[1]

    def cols(j, carry):
        c0 = pl.multiple_of(j * tn, tn)
        acc = jnp.dot(ya_ref[...], w_bf[:wa_rows, pl.ds(c0, tn)], preferred_element_type=F32)
        acc += jnp.dot(yb_ref[...], w_bf[wa_rows:, pl.ds(c0, tn)], preferred_element_type=F32)
        o_ref[:, pl.ds(c0, tn)] = x_ref[:, pl.ds(c0, tn)] + acc
        return carry
    lax.fori_loop(0, o_ref.shape[1] // tn, cols, 0)
    _router_rows(o_ref, g_ref, wr_ref, br_ref, lg_ref, xt_ref, tm, min(tm, 256))


def out_proj(y_mix, y_mem, w_o, layer, x, g_ffn, w_router, b_router, tm):
    T, Wa = y_mix.shape
    Wb = y_mem.shape[1]
    _, K, N = w_o.shape
    assert Wa + Wb == K
    tn = _col_tile(N)
    lane_tiles = N // (2 * LANES)
    return pl.pallas_call(
        functools.partial(_out_proj_kernel, layer=layer, tm=tm, tn=tn),
        grid=(T // tm,),
        in_specs=[pl.BlockSpec((tm, Wa), lambda i: (i, 0)),
                  pl.BlockSpec((tm, Wb), lambda i: (i, 0)),
                  pl.BlockSpec(memory_space=pl.ANY),
                  pl.BlockSpec((tm, N), lambda i: (i, 0)),
                  pl.BlockSpec((1, N), lambda i: (0, 0)),
                  pl.BlockSpec((N, LANES), lambda i: (0, 0)),
                  pl.BlockSpec((1, LANES), lambda i: (0, 0))],
        out_specs=(pl.BlockSpec((tm, N), lambda i: (i, 0)),
                   pl.BlockSpec((tm, LANES), lambda i: (i, 0)),
                   pl.BlockSpec((tm * lane_tiles, LANES), lambda i: (i, 0))),
        out_shape=(jax.ShapeDtypeStruct((T, N), F32),
                   jax.ShapeDtypeStruct((T, LANES), F32),
                   jax.ShapeDtypeStruct((T * lane_tiles, LANES), jnp.int32)),
        scratch_shapes=[pltpu.VMEM((K, N), BF16), pltpu.VMEM((2, K, tn), F32), pltpu.SemaphoreType.DMA((2,))],
        compiler_params=_params(("arbitrary",), 56),
        name="out_proj",
    )(y_mix, y_mem, w_o, x, g_ffn.reshape(1, N), w_router, b_router)


def _mem_attn_kernel(q_ref, kv_ref, qg_ref, kg_ref, o_ref, *, heads, hd):
    width = heads * hd
    scale = hd ** -0.5
    for h in range(heads):
        qn = _rms_rows(q_ref[:, h * hd:(h + 1) * hd], qg_ref[...]) * scale
        kn = _rms_rows(kv_ref[0, :, h * hd:(h + 1) * hd], kg_ref[...])
        v = kv_ref[0, :, width + h * hd:width + (h + 1) * hd]
        s = lax.dot_general(qn.astype(BF16), kn.astype(BF16), (((1,), (1,)), ((), ())),
                            preferred_element_type=F32)
        m = jnp.max(s, axis=-1, keepdims=True)
        p = jnp.exp(s - m)
        l = jnp.sum(p, axis=-1, keepdims=True)
        o = jnp.dot(p.astype(BF16), v, preferred_element_type=F32) / l
        o_ref[:, h * hd:(h + 1) * hd] = o.astype(o_ref.dtype)


def mem_attention(proj, q_col_block, kv, q_g, k_g, batch, ts):
    T = proj.shape[0]
    _, M, two_w = kv.shape
    width = two_w // 2
    hd = width // MEM_HEADS
    per_b = T // batch // ts
    return pl.pallas_call(
        functools.partial(_mem_attn_kernel, heads=MEM_HEADS, hd=hd),
        grid=(batch, per_b),
        in_specs=[pl.BlockSpec((ts, width), lambda b, i: (b * per_b + i, q_col_block)),
                  pl.BlockSpec((1, M, two_w), lambda b, i: (b, 0, 0)),
                  pl.BlockSpec((1, hd), lambda b, i: (0, 0)),
                  pl.BlockSpec((1, hd), lambda b, i: (0, 0))],
        out_specs=pl.BlockSpec((ts, width), lambda b, i: (b * per_b + i, 0)),
        out_shape=jax.ShapeDtypeStruct((T, width), BF16),
        compiler_params=_params(("parallel", "arbitrary"), 32),
        name="mem_attention",
    )(proj, kv, q_g.reshape(1, hd), k_g.reshape(1, hd))


def _gelu_tanh(x):
    return 0.5 * x * (1.0 + jnp.tanh(0.7978845608028654 * (x + 0.044715 * x * x * x)))


def _rglru_kernel(u_ref, gb_ref, cw_ref, cb_ref, wr_ref, wi_ref, br_ref, bi_ref, lam_ref,
                  o_ref, a_s, b_s, *, seq, nb, blk, tc):
    wb = nb * blk
    neg_lam = -lam_ref[...]
    softplus = jnp.maximum(neg_lam, 0.0) + jnp.log1p(jnp.exp(-jnp.abs(neg_lam)))

    def gates(c, carry):
        t0 = pl.multiple_of(c * tc, tc)
        cur = u_ref[0, pl.ds(t0, tc), :].astype(F32)
        p0 = pl.multiple_of(jnp.maximum(t0 - 16, 0), 16)
        prev = u_ref[0, pl.ds(p0, 16), :].astype(F32)
        prev = jnp.where(c > 0, prev, 0.0)
        full = jnp.concatenate([prev, cur], axis=0)
        y = cb_ref[...] + cw_ref[0:1, :] * cur
        for k in range(1, CONV_WIDTH):
            y = y + cw_ref[k:k + 1, :] * full[16 - k:16 - k + tc, :]
        r_parts, i_parts = [], []
        for n in range(nb):
            yb = y[:, n * blk:(n + 1) * blk].astype(BF16)
            r_parts.append(jnp.dot(yb, wr_ref[n].astype(BF16), preferred_element_type=F32))
            i_parts.append(jnp.dot(yb, wi_ref[n].astype(BF16), preferred_element_type=F32))
        r = _sigmoid(jnp.concatenate(r_parts, axis=1) + br_ref[...])
        ig = _sigmoid(jnp.concatenate(i_parts, axis=1) + bi_ref[...])
        log_a = (-LRU_C) * r * softplus
        a = jnp.exp(log_a)
        a_s[pl.ds(t0, tc), :] = a
        b_s[pl.ds(t0, tc), :] = jnp.sqrt(-jnp.tanh(log_a) * (a * a + 1.0)) * (ig * y)
        return carry

    lax.fori_loop(0, seq // tc, gates, 0)

    row = lax.broadcasted_iota(jnp.int32, (8, wb), 0)

    def scan(c, h):
        t0 = pl.multiple_of(c * 8, 8)
        a = a_s[pl.ds(t0, 8), :]
        b = b_s[pl.ds(t0, 8), :]
        for s in (1, 2, 4):
            a_sh = pltpu.roll(a, s, axis=0)
            b_sh = pltpu.roll(b, s, axis=0)
            keep = row >= s
            b = jnp.where(keep, a * b_sh + b, b)
            a = jnp.where(keep, a * a_sh, a)
        hc = a * h + b
        b_s[pl.ds(t0, 8), :] = hc
        return hc[7:8, :]

    lax.fori_loop(0, seq // 8, scan, jnp.zeros((1, wb), F32), unroll=4)

    def gate_out(c, carry):
        t0 = pl.multiple_of(c * tc, tc)
        g = gb_ref[0, pl.ds(t0, tc), :].astype(F32)
        o_ref[0, pl.ds(t0, tc), :] = (_gelu_tanh(g) * b_s[pl.ds(t0, tc), :]).astype(o_ref.dtype)
        return carry

    lax.fori_loop(0, seq // tc, gate_out, 0)


def rglru(proj, conv_w, conv_b, w_r, b_r, w_i, b_i, lam, nb, tc):
    B, S, _ = proj.shape
    W = lam.shape[0]
    n_blocks, blk, _ = w_r.shape
    wb = nb * blk
    ncb = W // wb
    vec = lambda: pl.BlockSpec((1, wb), lambda b, j: (0, j))
    return pl.pallas_call(
        functools.partial(_rglru_kernel, seq=S, nb=nb, blk=blk, tc=tc),
        grid=(B, ncb),
        in_specs=[pl.BlockSpec((1, S, wb), lambda b, j: (b, 0, j)),
                  pl.BlockSpec((1, S, wb), lambda b, j: (b, 0, ncb + j)),
                  pl.BlockSpec((CONV_WIDTH, wb), lambda b, j: (0, j)),
                  vec(),
                  pl.BlockSpec((nb, blk, blk), lambda b, j: (j, 0, 0)),
                  pl.BlockSpec((nb, blk, blk), lambda b, j: (j, 0, 0)),
                  vec(), vec(), vec()],
        out_specs=pl.BlockSpec((1, S, wb), lambda b, j: (b, 0, j)),
        out_shape=jax.ShapeDtypeStruct((B, S, W), BF16),
        scratch_shapes=[pltpu.VMEM((S, wb), F32), pltpu.VMEM((S, wb), F32)],
        compiler_params=_params(("parallel", "arbitrary"), 48),
        name="rglru",
    )(proj, proj, conv_w, conv_b.reshape(1, W), w_r, w_i, b_r.reshape(1, W), b_i.reshape(1, W),
      lam.reshape(1, W))


def _heads_kernel(*refs, rank, heads, with_v):
    if with_v:
        lat_ref, gl_ref, w_ref, hg_ref, cos_ref, sin_ref, wv_ref, o_ref, v_ref = refs
    else:
        lat_ref, gl_ref, w_ref, hg_ref, cos_ref, sin_ref, o_ref = refs
    lat = lat_ref[...].astype(F32)
    cn = _rms_rows(lat[:, :rank], gl_ref[...])
    full = cn if lat.shape[1] == rank else jnp.concatenate([cn, lat[:, rank:]], axis=1)
    full = full.astype(BF16)
    lane = lax.broadcasted_iota(jnp.int32, cos_ref.shape, 1)
    half = QK_ROPE // 2
    for h in range(heads):
        t = jnp.dot(full, w_ref[:, h * HEAD_PAD:(h + 1) * HEAD_PAD], preferred_element_type=F32)
        ss = jnp.sum(t * t, axis=-1, keepdims=True) * (1.0 / QK_DIM)
        tn = t * lax.rsqrt(ss + EPS) * hg_ref[...]
        rp = tn[:, QK_NOPE:]
        swapped = jnp.where(lane < half, pltpu.roll(rp, LANES - half, axis=1),
                            pltpu.roll(rp, half, axis=1))
        rot = rp * cos_ref[...] + swapped * sin_ref[...]
        o_ref[:, h * HEAD_PAD:h * HEAD_PAD + QK_NOPE] = tn[:, :QK_NOPE].astype(o_ref.dtype)
        o_ref[:, h * HEAD_PAD + QK_NOPE:(h + 1) * HEAD_PAD] = rot.astype(o_ref.dtype)
    if with_v:
        v = jnp.dot(cn.astype(BF16), wv_ref[...], preferred_element_type=F32).astype(v_ref.dtype)
        ones = jnp.ones((v.shape[0], HEAD_PAD - V_DIM), v_ref.dtype)
        for h in range(heads):
            v_ref[:, h * HEAD_PAD:h * HEAD_PAD + V_DIM] = v[:, h * V_DIM:(h + 1) * V_DIM]
            v_ref[:, h * HEAD_PAD + V_DIM:(h + 1) * HEAD_PAD] = ones


def latent_heads(lat, lat_cols, rank, g_lat, w_pad, head_gain, cos_t, sin_t, heads, tm, w_v=None):
    T = lat.shape[0]
    n_out = heads * HEAD_PAD
    in_specs = [pl.BlockSpec((tm, lat_cols), lambda i: (i, 0)),
                pl.BlockSpec((1, rank), lambda i: (0, 0)),
                pl.BlockSpec((lat_cols, n_out), lambda i: (0, 0)),
                pl.BlockSpec((1, HEAD_PAD), lambda i: (0, 0)),
                pl.BlockSpec((tm, LANES), lambda i: (i, 0)),
                pl.BlockSpec((tm, LANES), lambda i: (i, 0))]
    args = [lat, g_lat.reshape(1, rank), w_pad, head_gain, cos_t, sin_t]
    out_shape = jax.ShapeDtypeStruct((T, n_out), BF16)
    out_specs = pl.BlockSpec((tm, n_out), lambda i: (i, 0))
    if w_v is not None:
        in_specs.append(pl.BlockSpec(w_v.shape, lambda i: (0, 0)))
        args.append(w_v)
        out_shape = (out_shape, jax.ShapeDtypeStruct((T, n_out), BF16))
        out_specs = (out_specs, pl.BlockSpec((tm, n_out), lambda i: (i, 0)))
    return pl.pallas_call(
        functools.partial(_heads_kernel, rank=rank, heads=heads, with_v=w_v is not None),
        grid=(T // tm,),
        in_specs=in_specs, out_specs=out_specs, out_shape=out_shape,
        compiler_params=_params(("parallel",), 48),
        name="latent_heads_kv" if w_v is not None else "latent_heads_q",
    )(*args)


def _flash_kernel(q_ref, k_ref, v_ref, o_ref, s_s, m_s, acc_s, *, tq, sub, n_kt):
    i = pl.program_id(2)
    m_s[...] = jnp.full(m_s.shape, -1e30, F32)
    acc_s[...] = jnp.zeros(acc_s.shape, F32)
    n_sub = tq // sub
    n_heads = m_s.shape[0]

    def scores(hd, kt, slot):
        cols = slice(hd * HEAD_PAD, (hd + 1) * HEAD_PAD)
        k = k_ref[0, pl.ds(kt * tq, tq), cols]
        for h in range(n_sub):
            rows = slice(h * sub, (h + 1) * sub)
            s_s[hd, slot, rows, :] = lax.dot_general(q_ref[0, rows, cols], k, (((1,), (1,)), ((), ())),
                                                     preferred_element_type=F32)

    def softmax_pv(hd, k0, slot, masked):
        cols = slice(hd * HEAD_PAD, (hd + 1) * HEAD_PAD)
        for h in range(n_sub):
            rows = slice(h * sub, (h + 1) * sub)
            nk = (h + 1) * sub if masked else tq
            s = s_s[hd, slot, rows, :nk]
            if masked:
                r = lax.broadcasted_iota(jnp.int32, s.shape, 0) + h * sub
                c = lax.broadcasted_iota(jnp.int32, s.shape, 1)
                s = jnp.where(c <= r, s, -1e30)
            v = v_ref[0, pl.ds(k0, nk), cols]
            m_prev = m_s[hd, rows, :]
            m_new = jnp.maximum(m_prev, jnp.max(s, axis=-1, keepdims=True))
            alpha = jnp.exp2(m_prev - m_new)
            p = jnp.exp2(s - jnp.concatenate([m_new] * (nk // LANES), axis=1))
            acc_s[hd, rows, :] = (jnp.concatenate([alpha, alpha], axis=1) * acc_s[hd, rows, :]
                                  + jnp.dot(p.astype(BF16), v, preferred_element_type=F32))
            m_s[hd, rows, :] = m_new

    for hd in range(n_heads):
        scores(hd, 0, 0)
    for kt in range(n_kt - 1):
        @pl.when(kt < i)
        def _(kt=kt):
            for hd in range(n_heads):
                scores(hd, kt + 1, (kt + 1) % 2)
                softmax_pv(hd, kt * tq, kt % 2, False)
    for hd in range(n_heads):
        softmax_pv(hd, pl.multiple_of(i * tq, tq), i % 2, True)
        o_ref[0, :, hd * V_DIM:(hd + 1) * V_DIM] = (acc_s[hd, :, :V_DIM] / acc_s[hd, :, V_DIM:]).astype(o_ref.dtype)


def flash_attention(q, k, v, heads, tq):
    B, S, _ = q.shape
    hp = HEADS_PER_FLASH_STEP
    assert heads % hp == 0
    return pl.pallas_call(
        functools.partial(_flash_kernel, tq=tq, sub=SUB_FLASH, n_kt=S // tq),
        grid=(B, heads // hp, S // tq),
        in_specs=[pl.BlockSpec((1, tq, hp * HEAD_PAD), lambda b, h, i: (b, i, h)),
                  pl.BlockSpec((1, S, hp * HEAD_PAD), lambda b, h, i: (b, 0, h)),
                  pl.BlockSpec((1, S, hp * HEAD_PAD), lambda b, h, i: (b, 0, h))],
        out_specs=pl.BlockSpec((1, tq, hp * V_DIM), lambda b, h, i: (b, i, h)),
        out_shape=jax.ShapeDtypeStruct((B, S, heads * V_DIM), BF16),
        scratch_shapes=[pltpu.VMEM((hp, 2, tq, tq), F32), pltpu.VMEM((hp, tq, LANES), F32),
                        pltpu.VMEM((hp, tq, HEAD_PAD), F32)],
        compiler_params=_params(("parallel", "parallel", "arbitrary"), 56),
        name="flash_attention",
    )(q, k, v)


def _router_rows(x_ref, g_ref, w_ref, b_ref, o_ref, xt_ref, tm, chunk):
    w = w_ref[...]
    w_hi = w.astype(BF16)
    w_lo = (w - w_hi.astype(F32)).astype(BF16)
    half = x_ref.shape[1] // 2
    for r0 in range(0, tm, chunk):
        xn = _rms_rows(x_ref[r0:r0 + chunk, :], g_ref[...])
        x_hi = xn.astype(BF16)
        x_lo = (xn - x_hi.astype(F32)).astype(BF16)
        acc = jnp.dot(x_hi, w_hi, preferred_element_type=F32)
        acc += jnp.dot(x_lo, w_hi, preferred_element_type=F32)
        acc += jnp.dot(x_hi, w_lo, preferred_element_type=F32)
        o_ref[r0:r0 + chunk, :] = acc + b_ref[...]
        _token_major_store(xt_ref, r0, chunk, _pack_bf16_pair(xn[:, :half], xn[:, half:]))


def _plan_kernel(lg_ref, out_ref, tiles_ref, row_tok_ref, c_s, info_s, inv_s, *, n_tok, n_groups, epg, tm, chunk):
    n_exp = n_groups * epg
    lane = lax.broadcasted_iota(jnp.int32, (chunk, LANES), 1).astype(F32)
    rr = lax.broadcasted_iota(jnp.int32, (chunk, chunk), 0)
    cc = lax.broadcasted_iota(jnp.int32, (chunk, chunk), 1)
    tri = jnp.where(cc <= rr, 1.0, 0.0).astype(BF16)
    neg_inf = -jnp.inf

    def first_argmax(vals, vmax):
        return jnp.min(jnp.where(vals == vmax, lane, float(LANES)), axis=1, keepdims=True)

    def decide(c, carry):
        r0 = pl.multiple_of(c * chunk, chunk)
        lg = lg_ref[pl.ds(r0, chunk), :]
        gl = jnp.where(lane < n_groups, lg, neg_inf)
        gmax = jnp.max(gl, axis=1, keepdims=True)
        g_idx = first_argmax(gl, gmax)
        p_top = 1.0 / jnp.sum(jnp.exp(gl - gmax), axis=1, keepdims=True)
        lo = n_groups + g_idx * epg
        el = jnp.where(jnp.logical_and(lane >= lo, lane < lo + epg), lg, neg_inf)
        l1 = jnp.max(el, axis=1, keepdims=True)
        i1 = first_argmax(el, l1)
        el2 = jnp.where(lane == i1, neg_inf, el)
        l2 = jnp.max(el2, axis=1, keepdims=True)
        i2 = first_argmax(el2, l2)
        d = jnp.exp(l2 - l1)
        w1 = 1.0 / (1.0 + d)
        e1 = i1 - n_groups
        e2 = i2 - n_groups
        onehot = jnp.where(jnp.logical_or(lane == e1, lane == e2), 1.0, 0.0)
        cs = jnp.dot(tri, onehot.astype(BF16), preferred_element_type=F32) + carry
        c_s[pl.ds(r0, chunk), :] = cs
        info_s[pl.ds(r0, chunk), :] = jnp.where(
            lane == 0, e1, jnp.where(lane == 1, e2, jnp.where(
                lane == 2, p_top * w1, jnp.where(lane == 3, p_top * (d * w1), 0.0))))
        return cs[chunk - 1:chunk, :]

    counts = lax.fori_loop(0, n_tok // chunk, decide, jnp.zeros((1, LANES), F32))

    tiles = jnp.floor((counts + (tm - 1)) * (1.0 / tm))
    jj = lax.broadcasted_iota(jnp.int32, (LANES, LANES), 0)
    ee = lax.broadcasted_iota(jnp.int32, (LANES, LANES), 1)
    upper = jnp.where(jj <= ee, 1.0, 0.0).astype(BF16)
    tile_end = jnp.dot(jnp.broadcast_to(tiles, (8, LANES)).astype(BF16), upper,
                       preferred_element_type=F32)[0:1, :]
    tile_start = tile_end - tiles
    row_start = tile_start * tm

    inv_s[...] = jnp.zeros(inv_s.shape, F32)
    off_lane = lax.broadcasted_iota(jnp.int32, (chunk, tm), 1).astype(F32)
    tile_row = lax.broadcasted_iota(jnp.int32, (LANES, chunk), 0).astype(F32)
    tok_local = lax.broadcasted_iota(jnp.int32, (1, chunk), 1)

    def place(c, carry):
        r0 = pl.multiple_of(c * chunk, chunk)
        info = info_s[pl.ds(r0, chunk), :]
        base = row_start + c_s[pl.ds(r0, chunk), :] - 1.0
        pos = [jnp.sum(jnp.where(lane == info[:, k:k + 1], base, 0.0), axis=1, keepdims=True)
               for k in range(TOP_K_IN_GROUP)]
        packed = jnp.where(lane == 0, pos[0], jnp.where(lane == 1, pos[1], info))
        out_ref[pl.ds(r0, chunk), :] = packed
        pos_rows = jnp.transpose(packed)
        tok = r0 + tok_local
        tok_hi = lax.shift_right_logical(tok, LANES.bit_length() - 1).astype(F32)
        tok_lo = (tok & (LANES - 1)).astype(F32)
        for k in range(TOP_K_IN_GROUP):
            tile_of = jnp.floor(pos[k] * (1.0 / tm))
            onehot_off = jnp.where(off_lane == pos[k] - tile_of * tm, 1.0, 0.0).astype(BF16)
            sel = tile_row == jnp.floor(pos_rows[k:k + 1, :] * (1.0 / tm))
            ids = jnp.concatenate([jnp.where(sel, tok_hi, 0.0), jnp.where(sel, tok_lo, 0.0)], axis=0)
            inv_s[...] += jnp.dot(ids.astype(BF16), onehot_off, preferred_element_type=F32)
        return carry

    lax.fori_loop(0, n_tok // chunk, place, 0)
    row_tok_ref[...] = (inv_s[:LANES, :] * LANES + inv_s[LANES:, :]).astype(jnp.int32)

    ti = jj.astype(F32)
    lane_sq = ee.astype(F32)
    is_exp = lane_sq < n_exp
    n_used = tile_end[:, n_exp - 1:n_exp]
    ti_c = jnp.minimum(ti, n_used - 1.0)
    tile_e = jnp.sum(jnp.where(jnp.logical_and(is_exp, tile_end <= ti_c), 1.0, 0.0), axis=1, keepdims=True)
    first = jnp.sum(jnp.where(jnp.logical_and(jnp.logical_and(is_exp, tile_start == ti), tiles > 0.0),
                              1.0, 0.0), axis=1, keepdims=True)
    nonempty = jnp.logical_and(is_exp, tiles > 0.0)
    def next_nonempty(e):
        nxt = jnp.min(jnp.where(jnp.logical_and(nonempty, lane_sq > e), lane_sq, float(LANES)),
                      axis=1, keepdims=True)
        return jnp.where(nxt >= LANES, e, nxt)

    next_e = next_nonempty(tile_e)
    next2_e = next_nonempty(next_e)
    group = jnp.sum(jnp.where(jnp.logical_and(nonempty, lane_sq < tile_e), 1.0, 0.0), axis=1, keepdims=True)
    wslot = group - 2.0 * jnp.floor(group * 0.5)
    fields = (tile_e, first, n_used, next_e, wslot, next2_e)
    assert len(fields) == N_TILE_FIELDS
    packed = jnp.zeros((LANES, LANES), F32)
    for f, val in enumerate(fields):
        packed = jnp.where(lane_sq == f, val, packed)
    tiles_ref[...] = packed.astype(jnp.int32)


def routing_plan(logits, n_groups, epg, tm):
    T = logits.shape[0]
    n_exp = n_groups * epg
    n_tiles = T * TOP_K_IN_GROUP // tm + n_exp
    assert n_tiles <= LANES and T <= LANES * LANES
    out, tiles, row_tok = pl.pallas_call(
        functools.partial(_plan_kernel, n_tok=T, n_groups=n_groups, epg=epg, tm=tm, chunk=PLAN_CHUNK),
        grid=(1,),
        in_specs=[pl.BlockSpec((T, LANES), lambda i: (0, 0))],
        out_specs=(pl.BlockSpec((T, LANES), lambda i: (0, 0)),
                   pl.BlockSpec((LANES, LANES), lambda i: (0, 0)),
                   pl.BlockSpec((LANES, tm), lambda i: (0, 0))),
        out_shape=(jax.ShapeDtypeStruct((T, LANES), F32), jax.ShapeDtypeStruct((LANES, LANES), jnp.int32),
                   jax.ShapeDtypeStruct((LANES, tm), jnp.int32)),
        scratch_shapes=[pltpu.VMEM((T, LANES), F32), pltpu.VMEM((T, LANES), F32),
                        pltpu.VMEM((2 * LANES, tm), F32)],
        compiler_params=_params(("arbitrary",), 40),
        name="routing_plan",
    )(logits)
    pos = out[:, :TOP_K_IN_GROUP].astype(jnp.int32).reshape(-1)
    tile_info = tiles[:n_tiles, :N_TILE_FIELDS].T.reshape(-1)
    return out, pos, tile_info, row_tok[:n_tiles].reshape(-1), n_tiles


def _row_copy(src_hbm, tok, dst_row, sem, rows=1):
    start = tok if rows == 1 else pl.multiple_of(tok * rows, rows)
    return pltpu.make_async_copy(src_hbm.at[pl.ds(start, rows)], dst_row, sem)


def _start_row_gather(src_hbm, idx_ref, base, n, dst_row, sem, rows=1):
    def body(r, carry):
        _row_copy(src_hbm, idx_ref[base + r], dst_row(r), sem, rows).start()
        return carry
    lax.fori_loop(0, n, body, 0, unroll=8)


def _wait_row_gather(src_hbm, n, dst_all, sem):
    pltpu.make_async_copy(src_hbm.at[pl.ds(0, n)], dst_all, sem).wait()


def _gmm_kernel(info_ref, row_tok_ref, x_hbm, wg_hbm, wu_hbm, wd_hbm, o_ref,
                xbuf, sem, wg_f, wu_f, wd_f, wsem, wg_bf, wu_bf, wd_bf, *, tm, n_tiles, w_off, n_chunks, tok_rows):
    i = pl.program_id(0)
    expert = info_ref[i]
    first = info_ref[n_tiles + i]
    n_used = info_ref[2 * n_tiles]
    next_expert = info_ref[3 * n_tiles + i]
    wslot = info_ref[4 * n_tiles + i]
    next2_expert = info_ref[5 * n_tiles + i]
    slot = i % GATHER_BUFS
    ahead = GATHER_BUFS - 1

    def wait_slot(s):
        _wait_row_gather(x_hbm, tm * tok_rows, xbuf.at[s], sem.at[s])

    def weight_copies(e, s):
        copies = []
        for hbm, buf in ((wg_hbm, wg_f), (wu_hbm, wu_f), (wd_hbm, wd_f)):
            rows = hbm.shape[1] // n_chunks
            for c in range(n_chunks):
                copies.append(pltpu.make_async_copy(hbm.at[w_off + e, pl.ds(c * rows, rows)],
                                                    buf.at[s, pl.ds(c * rows, rows)], wsem.at[s]))
        return copies

    def start_weights(e, s):
        for c, cp in enumerate(weight_copies(e, s)):
            cp.start(priority=c % DMA_PRIORITIES)

    @pl.when(jnp.logical_and(i == 0, n_used > 0))
    def _():
        start_weights(expert, 0)

        @pl.when(next_expert != expert)
        def _():
            start_weights(next_expert, 1)

        for t in range(ahead):
            _start_row_gather(x_hbm, row_tok_ref, min(t, n_tiles - 1) * tm, tm,
                              lambda r, t=t: xbuf.at[t, pl.ds(pl.multiple_of(r * tok_rows, tok_rows), tok_rows)],
                              sem.at[t], tok_rows)

    @pl.when(i < n_used)
    def _():
        @pl.when(first == 1)
        def _():
            for cp in weight_copies(expert, wslot):
                cp.wait()
            for src, dst in ((wg_f, wg_bf), (wu_f, wu_bf), (wd_f, wd_bf)):
                rows = dst.shape[0] // n_chunks

                def convert(c, carry, src=src, dst=dst, rows=rows):
                    r0 = pl.multiple_of(c * rows, rows)
                    dst[pl.ds(r0, rows), :] = src[wslot, pl.ds(r0, rows), :].astype(BF16)
                    return carry
                lax.fori_loop(0, n_chunks, convert, 0)

            @pl.when(next2_expert != next_expert)
            def _():
                start_weights(next2_expert, wslot)

        wait_slot(slot)
        x_lo, x_hi = _unpack_bf16_pair(_token_major_load(lambda d: xbuf[slot, d, :], tm, tok_rows))
        xn = jnp.concatenate([x_lo.astype(BF16), x_hi.astype(BF16)], axis=1)
        base = jnp.minimum(i + ahead, n_tiles - 1) * tm
        nslot = (i + ahead) % GATHER_BUFS
        for r in range(tm):
            _row_copy(x_hbm, row_tok_ref[base + r], xbuf.at[nslot, pl.ds(r * tok_rows, tok_rows)],
                      sem.at[nslot], tok_rows).start(priority=r % DMA_PRIORITIES)
        hg = jnp.dot(xn, wg_bf[...], preferred_element_type=F32)
        hu = jnp.dot(xn, wu_bf[...], preferred_element_type=F32)
        hh = (hg * jax.nn.sigmoid(hg)) * hu
        y = jnp.dot(hh.astype(BF16), wd_bf[...], preferred_element_type=F32)
        half = y.shape[1] // 2
        _token_major_store(o_ref, 0, tm, _pack_bf16_pair(y[:, :half], y[:, half:]))

    @pl.when(i >= n_used)
    def _():
        @pl.when(jnp.logical_and(i < n_used + ahead, n_used > 0))
        def _():
            wait_slot(slot)
        o_ref[...] = jnp.zeros(o_ref.shape, o_ref.dtype)

    @pl.when(i == n_tiles - 1)
    def _():
        for d in range(1, ahead + 1):
            @pl.when(n_used >= n_tiles - ahead + d)
            def _(d=d):
                wait_slot((n_tiles - 1 + d) % GATHER_BUFS)


def grouped_expert_mlp(xt, tile_info, row_tok, w_gate, w_up, w_down, layer, n_tiles, tm):
    _, D, F = w_gate.shape
    tok_rows = D // (2 * LANES)
    layer_idx, depth = layer
    off = layer_idx * (w_gate.shape[0] // depth)
    any_spec = pl.BlockSpec(memory_space=pl.ANY)
    grid_spec = pltpu.PrefetchScalarGridSpec(
        num_scalar_prefetch=2,
        grid=(n_tiles,),
        in_specs=[any_spec, any_spec, any_spec, any_spec],
        out_specs=pl.BlockSpec((tm * tok_rows, LANES), lambda i, ti, rt: (i, 0)),
        scratch_shapes=[pltpu.VMEM((GATHER_BUFS, tm * tok_rows, LANES), jnp.int32),
                        pltpu.SemaphoreType.DMA((GATHER_BUFS,)),
                        pltpu.VMEM((2, D, F), F32), pltpu.VMEM((2, D, F), F32), pltpu.VMEM((2, F, D), F32),
                        pltpu.SemaphoreType.DMA((2,)),
                        pltpu.VMEM((D, F), BF16), pltpu.VMEM((D, F), BF16), pltpu.VMEM((F, D), BF16)],
    )
    return pl.pallas_call(
        functools.partial(_gmm_kernel, tm=tm, n_tiles=n_tiles, w_off=off, n_chunks=WEIGHT_DMA_CHUNKS,
                          tok_rows=tok_rows),
        grid_spec=grid_spec,
        out_shape=jax.ShapeDtypeStruct((n_tiles * tm * tok_rows, LANES), jnp.int32),
        compiler_params=_params(("arbitrary",), 56),
        name="grouped_expert_mlp",
    )(tile_info, row_tok, xt, w_gate, w_up, w_down)


def _start_expert_rows(y_hbm, pos_ref, tile, ybuf, s, sem, tc, top_k, yrows):
    base = tile * (tc * top_k)
    for r in range(tc):
        for k in range(top_k):
            _row_copy(y_hbm, pos_ref[base + r * top_k + k], ybuf.at[s, k, pl.ds(r * yrows, yrows)],
                      sem.at[s], yrows).start()


def _wait_expert_rows(y_hbm, ybuf, s, sem, tc, top_k, yrows):
    for k in range(top_k):
        _wait_row_gather(y_hbm, tc * yrows, ybuf.at[s, k], sem.at[s])


def _expert_rows(ybuf, s, k, r0, rows, yrows):
    lo, hi = _unpack_bf16_pair(_token_major_load(
        lambda d: ybuf[s, k, pl.ds(r0 * yrows + d.start, rows, stride=yrows), :], rows, yrows))
    return jnp.concatenate([lo, hi], axis=1)


def _combine_kernel(pos_ref, x_ref, plan_ref, y_hbm, o_ref, ybuf, sem, *, tc, top_k, yrows):
    i = pl.program_id(0)
    n = pl.num_programs(0)
    slot = i % 2

    @pl.when(i == 0)
    def _():
        _start_expert_rows(y_hbm, pos_ref, 0, ybuf, 0, sem, tc, top_k, yrows)

    _wait_expert_rows(y_hbm, ybuf, slot, sem, tc, top_k, yrows)

    @pl.when(i + 1 < n)
    def _():
        _start_expert_rows(y_hbm, pos_ref, i + 1, ybuf, 1 - slot, sem, tc, top_k, yrows)

    acc = x_ref[...]
    for k in range(top_k):
        acc = acc + plan_ref[:, top_k + k:top_k + k + 1] * _expert_rows(ybuf, slot, k, 0, tc, yrows)
    o_ref[...] = acc


def moe_combine(x, y_sorted, plan_out, pos, tc):
    T, D = x.shape
    top_k = TOP_K_IN_GROUP
    yrows = D // (2 * LANES)
    grid_spec = pltpu.PrefetchScalarGridSpec(
        num_scalar_prefetch=1,
        grid=(T // tc,),
        in_specs=[pl.BlockSpec((tc, D), lambda i, p: (i, 0)),
                  pl.BlockSpec((tc, LANES), lambda i, p: (i, 0)),
                  pl.BlockSpec(memory_space=pl.ANY)],
        out_specs=pl.BlockSpec((tc, D), lambda i, p: (i, 0)),
        scratch_shapes=[pltpu.VMEM((2, top_k, tc * yrows, LANES), jnp.int32), pltpu.SemaphoreType.DMA((2,))],
    )
    return pl.pallas_call(
        functools.partial(_combine_kernel, tc=tc, top_k=top_k, yrows=yrows),
        grid_spec=grid_spec,
        out_shape=jax.ShapeDtypeStruct((T, D), F32),
        compiler_params=_params(("arbitrary",), 32),
        name="moe_combine",
    )(pos, x, plan_out, y_sorted)


def _combine_mm_kernel(pos_ref, x_ref, plan_ref, y_hbm, g_ref, w_hbm, xnew_ref, o_ref,
                       ybuf, ysem, xn_ref, w_bf, stage, wsem, *, layer, tc, chunk, tn, top_k, yrows):
    i = pl.program_id(0)
    n = pl.num_programs(0)
    slot = i % 2

    def start(tile, s):
        _start_expert_rows(y_hbm, pos_ref, tile, ybuf, s, ysem, tc, top_k, yrows)

    def wait(s):
        _wait_expert_rows(y_hbm, ybuf, s, ysem, tc, top_k, yrows)

    @pl.when(i == 0)
    def _():
        start(0, 0)
        _stage_weight_bf16(w_hbm, layer, w_bf, stage, wsem, tn)

    wait(slot)
    for r0 in range(0, tc, chunk):
        rows = slice(r0, r0 + chunk)
        acc = x_ref[rows, :]
        for k in range(top_k):
            acc = acc + plan_ref[rows, top_k + k:top_k + k + 1] * _expert_rows(ybuf, slot, k, r0, chunk, yrows)
        xnew_ref[rows, :] = acc
        xn_ref[rows, :] = _rms_rows(acc, g_ref[...]).astype(BF16)
    start(jnp.minimum(i + 1, n - 1), 1 - slot)
    for j in range(o_ref.shape[1] // tn):
        o_ref[:, j * tn:(j + 1) * tn] = jnp.dot(xn_ref[...], w_bf[:, j * tn:(j + 1) * tn],
                                                preferred_element_type=F32).astype(o_ref.dtype)

    @pl.when(i == n - 1)
    def _():
        wait(1 - slot)


def combine_norm_matmul(x, y_sorted, plan_out, pos, g, w, layer, out_dtype, tc):
    T, D = x.shape
    N = w.shape[2]
    top_k = TOP_K_IN_GROUP
    tn = _col_tile(N)
    yrows = D // (2 * LANES)
    grid_spec = pltpu.PrefetchScalarGridSpec(
        num_scalar_prefetch=1,
        grid=(T // tc,),
        in_specs=[pl.BlockSpec((tc, D), lambda i, p: (i, 0)),
                  pl.BlockSpec((tc, LANES), lambda i, p: (i, 0)),
                  pl.BlockSpec(memory_space=pl.ANY),
                  pl.BlockSpec((1, D), lambda i, p: (0, 0)),
                  pl.BlockSpec(memory_space=pl.ANY)],
        out_specs=(pl.BlockSpec((tc, D), lambda i, p: (i, 0)),
                   pl.BlockSpec((tc, N), lambda i, p: (i, 0))),
        scratch_shapes=[pltpu.VMEM((2, top_k, tc * yrows, LANES), jnp.int32), pltpu.SemaphoreType.DMA((2,)),
                        pltpu.VMEM((tc, D), BF16), pltpu.VMEM((D, N), BF16), pltpu.VMEM((2, D, tn), F32),
                        pltpu.SemaphoreType.DMA((2,))],
    )
    return pl.pallas_call(
        functools.partial(_combine_mm_kernel, layer=layer, tc=tc, chunk=min(tc, 128), tn=tn, top_k=top_k,
                          yrows=yrows),
        grid_spec=grid_spec,
        out_shape=(jax.ShapeDtypeStruct((T, D), F32), jax.ShapeDtypeStruct((T, N), out_dtype)),
        compiler_params=_params(("arbitrary",), 56),
        name="combine_norm_matmul",
    )(pos, x, plan_out, y_sorted, g.reshape(1, D), w)


def hier_moe_experts(logits, xt, n_groups, epg, w_gate, w_up, w_down, layer):
    plan_out, pos, tile_info, row_tok, n_tiles = routing_plan(logits, n_groups, epg, TM_EXPERT)
    y_sorted = grouped_expert_mlp(xt, tile_info, row_tok, w_gate, w_up, w_down, layer, n_tiles, TM_EXPERT)
    return y_sorted, plan_out, pos


def _pad_head_cols(w, heads, real):
    r = w.shape[0]
    return jnp.pad(w.reshape(r, heads, real), ((0, 0), (0, 0), (0, HEAD_PAD - real))).reshape(r, heads * HEAD_PAD)


def _rope_tables(positions):
    half = QK_ROPE // 2
    inv_freq = ROPE_THETA ** (-jnp.arange(half, dtype=F32) / half)
    ang = positions.astype(F32).reshape(-1, 1) * inv_freq[None, :]
    cos, sin = jnp.cos(ang), jnp.sin(ang)
    zeros = jnp.zeros((ang.shape[0], LANES - QK_ROPE), F32)
    return (jnp.concatenate([cos, cos, zeros], axis=1),
            jnp.concatenate([-sin, sin, zeros], axis=1))


def kernel(x, mem, positions, norm_mix_g, norm_ffn_g, w_o, mem_norm_g, w_mem_kv, mem_q_norm_g, mem_k_norm_g, w_in_a, conv_w, conv_b, w_lru_r, b_lru_r, w_lru_i, b_lru_i, lru_lambda, kv_in_norm_g, w_dkv, kv_latent_norm_g, w_uk, w_uv, k_head_norm_g, w_in_b, q_latent_norm_g, w_uq, q_head_norm_g, w_router_grp, b_router_grp, w_router_exp, b_router_exp, w_exp_gate, w_exp_up, w_exp_down):
    B, S, D = x.shape
    T = B * S
    M = mem.shape[1]
    depth = norm_mix_g.shape[0]
    n_a = w_in_a.shape[0]
    lru_w = lru_lambda.shape[1]
    mem_w = w_mem_kv.shape[2] // 2
    heads = w_uv.shape[1] // V_DIM
    kv_rank = kv_latent_norm_g.shape[0]
    q_rank = q_latent_norm_g.shape[1]
    n_groups, epg = w_exp_gate.shape[1], w_exp_gate.shape[2]
    n_exp = n_groups * epg
    d_exp = w_exp_gate.shape[-1]

    xr = x.reshape(T, D)
    mem2 = mem.reshape(B * M, D)
    cos_t, sin_t = _rope_tables(positions)
    wg_all = w_exp_gate.reshape(depth * n_exp, D, d_exp)
    wu_all = w_exp_up.reshape(depth * n_exp, D, d_exp)
    wd_all = w_exp_down.reshape(depth * n_exp, d_exp, D)
    pad_r = LANES - n_groups - n_exp
    w_router = jnp.concatenate([w_router_grp, w_router_exp, jnp.zeros((depth, D, pad_r), F32)], axis=2)
    b_router = jnp.concatenate([b_router_grp, b_router_exp, jnp.zeros((depth, pad_r), F32)], axis=1)

    def shared_kv(x_res):
        lat_cols = kv_rank + LANES
        w_dkv_pad = jnp.pad(w_dkv, ((0, 0), (0, lat_cols - w_dkv.shape[1])))[None]
        ckv = norm_matmul(x_res, kv_in_norm_g, w_dkv_pad, 0, F32, tm=TM_PROJ)
        eye = jnp.pad(jnp.eye(QK_ROPE, dtype=F32), ((0, LANES - QK_ROPE), (0, 0)))
        eye3 = jnp.broadcast_to(eye[:, None, :], (LANES, heads, QK_ROPE))
        rope_rows = jnp.pad(eye3, ((0, 0), (0, 0), (QK_NOPE, HEAD_PAD - QK_DIM))).reshape(LANES, heads * HEAD_PAD)
        wk = jnp.concatenate([_pad_head_cols(w_uk, heads, QK_NOPE), rope_rows], axis=0).astype(BF16)
        kg = jnp.pad(k_head_norm_g, (0, HEAD_PAD - QK_DIM)).reshape(1, HEAD_PAD)
        k_flat, v_flat = latent_heads(ckv, lat_cols, kv_rank, kv_latent_norm_g, wk, kg, cos_t, sin_t,
                                      heads, tm=TM_HEADS, w_v=w_uv.astype(BF16))
        return k_flat.reshape(B, S, -1), v_flat.reshape(B, S, -1)

    def in_proj(x_res, pending, g, w, idx):
        if pending is None:
            return x_res, norm_matmul(x_res, g, w, idx, BF16, tm=TM_PROJ)
        return combine_norm_matmul(x_res, *pending, g, w, idx, BF16, TC_COMBINE)

    k_sh = v_sh = None
    pending = None
    for l in range(depth):
        kv_mem = norm_matmul(mem2, mem_norm_g[l], w_mem_kv, l, BF16, tm=TM_PROJ).reshape(B, M, 2 * mem_w)
        if l < n_a:
            xr, proj = in_proj(xr, pending, norm_mix_g[l], w_in_a, l)
            y_mix = rglru(proj.reshape(B, S, -1), conv_w[l], conv_b[l], w_lru_r[l], b_lru_r[l],
                          w_lru_i[l], b_lru_i[l], lru_lambda[l], nb=LRU_BLOCKS_PER_STEP,
                          tc=TC_LRU).reshape(T, lru_w)
            q_col = 2 * lru_w // mem_w
        else:
            j = l - n_a
            xr, proj = in_proj(xr, pending, norm_mix_g[l], w_in_b, j)
            if l == n_a:
                k_sh, v_sh = shared_kv(xr)
            wq = _pad_head_cols(w_uq[j], heads, QK_DIM).astype(BF16)
            qg = jnp.pad(q_head_norm_g[j] * (QK_DIM ** -0.5 * math.log2(math.e)),
                         (0, HEAD_PAD - QK_DIM)).reshape(1, HEAD_PAD)
            q = latent_heads(proj, q_rank, q_rank, q_latent_norm_g[j], wq, qg, cos_t, sin_t, heads, tm=TM_HEADS)
            y_mix = flash_attention(q.reshape(B, S, -1), k_sh, v_sh, heads, tq=TQ_FLASH).reshape(T, heads * V_DIM)
            q_col = q_rank // mem_w
        y_mem = mem_attention(proj, q_col, kv_mem, mem_q_norm_g[l], mem_k_norm_g[l], B, ts=TS_MEM)
        xr, logits, xt = out_proj(y_mix, y_mem, w_o, l, xr, norm_ffn_g[l], w_router[l],
                                  b_router[l].reshape(1, LANES), tm=TM_PROJ)
        pending = hier_moe_experts(logits, xt, n_groups, epg, wg_all, wu_all, wd_all, (l, depth))
    xr = moe_combine(xr, *pending, TC_COMBINE)
    return xr.reshape(B, S, D)
```

```python
import functools
import math

import jax
import jax.numpy as jnp
from jax import lax
from jax.experimental import pallas as pl
from jax.experimental.pallas import tpu as pltpu

F32 = jnp.float32
BF16 = jnp.bfloat16
EPS = 1e-6

MEM_HEADS = 4
CONV_WIDTH = 4
LRU_C = 8.0
QK_NOPE = 128
QK_ROPE = 64
QK_DIM = QK_NOPE + QK_ROPE
V_DIM = 128
ROPE_THETA = 10000.0
TOP_K_IN_GROUP = 2

LANES = 128
HEAD_PAD = 2 * LANES
MIB = 1024 * 1024
DMA_PRIORITIES = 2

TM_PROJ = 512
TN_PROJ = 512
TS_MEM = 512
TQ_FLASH = 512
SUB_FLASH = 256
HEADS_PER_FLASH_STEP = 6
TM_HEADS = 512
TM_EXPERT = 256
WEIGHT_DMA_CHUNKS = 16
WEIGHT_CONVERT_CHUNKS = 4
GATHER_BUFS = 3
TC_COMBINE = 256
LRU_BLOCKS_PER_STEP = 6
TC_LRU = 256
PLAN_CHUNK = 256
N_TILE_FIELDS = 6


def _params(sem, vmem_mib):
    return pltpu.CompilerParams(dimension_semantics=sem, vmem_limit_bytes=vmem_mib * MIB)


def _sigmoid(z):
    return 0.5 * jnp.tanh(0.5 * z) + 0.5


def _rms_rows(x, g):
    x = x.astype(F32)
    ms = jnp.mean(x * x, axis=-1, keepdims=True)
    return x * lax.rsqrt(ms + EPS) * g


def _col_tile(n):
    return TN_PROJ if n % TN_PROJ == 0 else n


def _stage_weight_bf16(w_hbm, layer, w_bf, stage, sem, tn):
    n = w_bf.shape[1] // tn

    def copy(j):
        return pltpu.make_async_copy(w_hbm.at[layer, :, pl.ds(j * tn, tn)], stage.at[j % 2], sem.at[j % 2])

    copy(0).start()
    for j in range(n):
        if j + 1 < n:
            copy(j + 1).start()
        copy(j).wait()
        w_bf[:, j * tn:(j + 1) * tn] = stage[j % 2].astype(BF16)


def _norm_mm_kernel(x_ref, g_ref, w_hbm, o_ref, xn_ref, w_bf, stage, sem, *, layer, tm, chunk, tn):
    @pl.when(pl.program_id(0) == 0)
    def _():
        _stage_weight_bf16(w_hbm, layer, w_bf, stage, sem, tn)

    for r0 in range(0, tm, chunk):
        xn_ref[r0:r0 + chunk, :] = _rms_rows(x_ref[r0:r0 + chunk, :], g_ref[...]).astype(BF16)

    def cols(j, carry):
        c0 = pl.multiple_of(j * tn, tn)
        o_ref[:, pl.ds(c0, tn)] = jnp.dot(xn_ref[...], w_bf[:, pl.ds(c0, tn)],
                                          preferred_element_type=F32).astype(o_ref.dtype)
        return carry
    lax.fori_loop(0, o_ref.shape[1] // tn, cols, 0)


def norm_matmul(x, g, w, layer, out_dtype, tm):
    T, K = x.shape
    N = w.shape[2]
    tm = min(tm, T)
    chunk = min(tm, 256)
    tn = _col_tile(N)
    return pl.pallas_call(
        functools.partial(_norm_mm_kernel, layer=layer, tm=tm, chunk=chunk, tn=tn),
        grid=(T // tm,),
        in_specs=[pl.BlockSpec((tm, K), lambda i: (i, 0)),
                  pl.BlockSpec((1, K), lambda i: (0, 0)),
                  pl.BlockSpec(memory_space=pl.ANY)],
        out_specs=pl.BlockSpec((tm, N), lambda i: (i, 0)),
        out_shape=jax.ShapeDtypeStruct((T, N), out_dtype),
        scratch_shapes=[pltpu.VMEM((tm, K), BF16), pltpu.VMEM((K, N), BF16), pltpu.VMEM((2, K, tn), F32),
                        pltpu.SemaphoreType.DMA((2,))],
        compiler_params=_params(("arbitrary",), 56),
        name="norm_matmul",
    )(x, g.reshape(1, K), w)


def _out_proj_kernel(ya_ref, yb_ref, w_hbm, x_ref, g_ref, wr_ref, br_ref, o_ref, lg_ref, xt_ref,
                     w_bf, stage, sem, *, layer, tm, tn):
    @pl.when(pl.program_id(0) == 0)
    def _():
        _stage_weight_bf16(w_hbm, layer, w_bf, stage, sem, tn)

    wa_rows = ya_ref.shape[1]

    def cols(j, carry):
        c0 = pl.multiple_of(j * tn, tn)
        acc = jnp.dot(ya_ref[...], w_bf[:wa_rows, pl.ds(c0, tn)], preferred_element_type=F32)
        acc += jnp.dot(yb_ref[...], w_bf[wa_rows:, pl.ds(c0, tn)], preferred_element_type=F32)
        o_ref[:, pl.ds(c0, tn)] = x_ref[:, pl.ds(c0, tn)] + acc
        return carry
    lax.fori_loop(0, o_ref.shape[1] // tn, cols, 0)
    _router_rows(o_ref, g_ref, wr_ref, br_ref, lg_ref, xt_ref, tm, min(tm, 256))


def out_proj(y_mix, y_mem, w_o, layer, x, g_ffn, w_router, b_router, tm):
    T, Wa = y_mix.shape
    Wb = y_mem.shape[1]
    _, K, N = w_o.shape
    assert Wa + Wb == K
    tn = _col_tile(N)
    lane_tiles = N // LANES
    return pl.pallas_call(
        functools.partial(_out_proj_kernel, layer=layer, tm=tm, tn=tn),
        grid=(T // tm,),
        in_specs=[pl.BlockSpec((tm, Wa), lambda i: (i, 0)),
                  pl.BlockSpec((tm, Wb), lambda i: (i, 0)),
                  pl.BlockSpec(memory_space=pl.ANY),
                  pl.BlockSpec((tm, N), lambda i: (i, 0)),
                  pl.BlockSpec((1, N), lambda i: (0, 0)),
                  pl.BlockSpec((N, LANES), lambda i: (0, 0)),
                  pl.BlockSpec((1, LANES), lambda i: (0, 0))],
        out_specs=(pl.BlockSpec((tm, N), lambda i: (i, 0)),
                   pl.BlockSpec((tm, LANES), lambda i: (i, 0)),
                   pl.BlockSpec((tm * lane_tiles, LANES), lambda i: (i, 0))),
        out_shape=(jax.ShapeDtypeStruct((T, N), F32),
                   jax.ShapeDtypeStruct((T, LANES), F32),
                   jax.ShapeDtypeStruct((T * lane_tiles, LANES), F32)),
        scratch_shapes=[pltpu.VMEM((K, N), BF16), pltpu.VMEM((2, K, tn), F32), pltpu.SemaphoreType.DMA((2,))],
        compiler_params=_params(("arbitrary",), 56),
        name="out_proj",
    )(y_mix, y_mem, w_o, x, g_ffn.reshape(1, N), w_router, b_router)


def _mem_attn_kernel(q_ref, kv_ref, qg_ref, kg_ref, o_ref, *, heads, hd):
    width = heads * hd
    scale = hd ** -0.5
    for h in range(heads):
        qn = _rms_rows(q_ref[:, h * hd:(h + 1) * hd], qg_ref[...]) * scale
        kn = _rms_rows(kv_ref[0, :, h * hd:(h + 1) * hd], kg_ref[...])
        v = kv_ref[0, :, width + h * hd:width + (h + 1) * hd]
        s = lax.dot_general(qn.astype(BF16), kn.astype(BF16), (((1,), (1,)), ((), ())),
                            preferred_element_type=F32)
        m = jnp.max(s, axis=-1, keepdims=True)
        p = jnp.exp(s - m)
        l = jnp.sum(p, axis=-1, keepdims=True)
        o = jnp.dot(p.astype(BF16), v, preferred_element_type=F32) / l
        o_ref[:, h * hd:(h + 1) * hd] = o.astype(o_ref.dtype)


def mem_attention(proj, q_col_block, kv, q_g, k_g, batch, ts):
    T = proj.shape[0]
    _, M, two_w = kv.shape
    width = two_w // 2
    hd = width // MEM_HEADS
    per_b = T // batch // ts
    return pl.pallas_call(
        functools.partial(_mem_attn_kernel, heads=MEM_HEADS, hd=hd),
        grid=(batch, per_b),
        in_specs=[pl.BlockSpec((ts, width), lambda b, i: (b * per_b + i, q_col_block)),
                  pl.BlockSpec((1, M, two_w), lambda b, i: (b, 0, 0)),
                  pl.BlockSpec((1, hd), lambda b, i: (0, 0)),
                  pl.BlockSpec((1, hd), lambda b, i: (0, 0))],
        out_specs=pl.BlockSpec((ts, width), lambda b, i: (b * per_b + i, 0)),
        out_shape=jax.ShapeDtypeStruct((T, width), BF16),
        compiler_params=_params(("parallel", "arbitrary"), 32),
        name="mem_attention",
    )(proj, kv, q_g.reshape(1, hd), k_g.reshape(1, hd))


def _gelu_tanh(x):
    return 0.5 * x * (1.0 + jnp.tanh(0.7978845608028654 * (x + 0.044715 * x * x * x)))


def _rglru_kernel(u_ref, gb_ref, cw_ref, cb_ref, wr_ref, wi_ref, br_ref, bi_ref, lam_ref,
                  o_ref, a_s, b_s, *, seq, nb, blk, tc):
    wb = nb * blk
    neg_lam = -lam_ref[...]
    softplus = jnp.maximum(neg_lam, 0.0) + jnp.log1p(jnp.exp(-jnp.abs(neg_lam)))

    def gates(c, carry):
        t0 = pl.multiple_of(c * tc, tc)
        cur = u_ref[0, pl.ds(t0, tc), :].astype(F32)
        p0 = pl.multiple_of(jnp.maximum(t0 - 16, 0), 16)
        prev = u_ref[0, pl.ds(p0, 16), :].astype(F32)
        prev = jnp.where(c > 0, prev, 0.0)
        full = jnp.concatenate([prev, cur], axis=0)
        y = cb_ref[...] + cw_ref[0:1, :] * cur
        for k in range(1, CONV_WIDTH):
            y = y + cw_ref[k:k + 1, :] * full[16 - k:16 - k + tc, :]
        r_parts, i_parts = [], []
        for n in range(nb):
            yb = y[:, n * blk:(n + 1) * blk].astype(BF16)
            r_parts.append(jnp.dot(yb, wr_ref[n].astype(BF16), preferred_element_type=F32))
            i_parts.append(jnp.dot(yb, wi_ref[n].astype(BF16), preferred_element_type=F32))
        r = _sigmoid(jnp.concatenate(r_parts, axis=1) + br_ref[...])
        ig = _sigmoid(jnp.concatenate(i_parts, axis=1) + bi_ref[...])
        log_a = (-LRU_C) * r * softplus
        a = jnp.exp(log_a)
        a_s[pl.ds(t0, tc), :] = a
        b_s[pl.ds(t0, tc), :] = jnp.sqrt(-jnp.tanh(log_a) * (a * a + 1.0)) * (ig * y)
        return carry

    lax.fori_loop(0, seq // tc, gates, 0)

    row = lax.broadcasted_iota(jnp.int32, (8, wb), 0)

    def scan(c, h):
        t0 = pl.multiple_of(c * 8, 8)
        a = a_s[pl.ds(t0, 8), :]
        b = b_s[pl.ds(t0, 8), :]
        for s in (1, 2, 4):
            a_sh = pltpu.roll(a, s, axis=0)
            b_sh = pltpu.roll(b, s, axis=0)
            keep = row >= s
            b = jnp.where(keep, a * b_sh + b, b)
            a = jnp.where(keep, a * a_sh, a)
        hc = a * h + b
        b_s[pl.ds(t0, 8), :] = hc
        return hc[7:8, :]

    lax.fori_loop(0, seq // 8, scan, jnp.zeros((1, wb), F32), unroll=4)

    def gate_out(c, carry):
        t0 = pl.multiple_of(c * tc, tc)
        g = gb_ref[0, pl.ds(t0, tc), :].astype(F32)
        o_ref[0, pl.ds(t0, tc), :] = (_gelu_tanh(g) * b_s[pl.ds(t0, tc), :]).astype(o_ref.dtype)
        return carry

    lax.fori_loop(0, seq // tc, gate_out, 0)


def rglru(proj, conv_w, conv_b, w_r, b_r, w_i, b_i, lam, nb, tc):
    B, S, _ = proj.shape
    W = lam.shape[0]
    n_blocks, blk, _ = w_r.shape
    wb = nb * blk
    ncb = W // wb
    vec = lambda: pl.BlockSpec((1, wb), lambda b, j: (0, j))
    return pl.pallas_call(
        functools.partial(_rglru_kernel, seq=S, nb=nb, blk=blk, tc=tc),
        grid=(B, ncb),
        in_specs=[pl.BlockSpec((1, S, wb), lambda b, j: (b, 0, j)),
                  pl.BlockSpec((1, S, wb), lambda b, j: (b, 0, ncb + j)),
                  pl.BlockSpec((CONV_WIDTH, wb), lambda b, j: (0, j)),
                  vec(),
                  pl.BlockSpec((nb, blk, blk), lambda b, j: (j, 0, 0)),
                  pl.BlockSpec((nb, blk, blk), lambda b, j: (j, 0, 0)),
                  vec(), vec(), vec()],
        out_specs=pl.BlockSpec((1, S, wb), lambda b, j: (b, 0, j)),
        out_shape=jax.ShapeDtypeStruct((B, S, W), BF16),
        scratch_shapes=[pltpu.VMEM((S, wb), F32), pltpu.VMEM((S, wb), F32)],
        compiler_params=_params(("parallel", "arbitrary"), 48),
        name="rglru",
    )(proj, proj, conv_w, conv_b.reshape(1, W), w_r, w_i, b_r.reshape(1, W), b_i.reshape(1, W),
      lam.reshape(1, W))


def _heads_kernel(*refs, rank, heads, with_v):
    if with_v:
        lat_ref, gl_ref, w_ref, hg_ref, cos_ref, sin_ref, wv_ref, o_ref, v_ref = refs
    else:
        lat_ref, gl_ref, w_ref, hg_ref, cos_ref, sin_ref, o_ref = refs
    lat = lat_ref[...].astype(F32)
    cn = _rms_rows(lat[:, :rank], gl_ref[...])
    full = cn if lat.shape[1] == rank else jnp.concatenate([cn, lat[:, rank:]], axis=1)
    full = full.astype(BF16)
    lane = lax.broadcasted_iota(jnp.int32, cos_ref.shape, 1)
    half = QK_ROPE // 2
    for h in range(heads):
        t = jnp.dot(full, w_ref[:, h * HEAD_PAD:(h + 1) * HEAD_PAD], preferred_element_type=F32)
        ss = jnp.sum(t * t, axis=-1, keepdims=True) * (1.0 / QK_DIM)
        tn = t * lax.rsqrt(ss + EPS) * hg_ref[...]
        rp = tn[:, QK_NOPE:]
        swapped = jnp.where(lane < half, pltpu.roll(rp, LANES - half, axis=1),
                            pltpu.roll(rp, half, axis=1))
        rot = rp * cos_ref[...] + swapped * sin_ref[...]
        o_ref[:, h * HEAD_PAD:h * HEAD_PAD + QK_NOPE] = tn[:, :QK_NOPE].astype(o_ref.dtype)
        o_ref[:, h * HEAD_PAD + QK_NOPE:(h + 1) * HEAD_PAD] = rot.astype(o_ref.dtype)
    if with_v:
        v = jnp.dot(cn.astype(BF16), wv_ref[...], preferred_element_type=F32).astype(v_ref.dtype)
        ones = jnp.ones((v.shape[0], HEAD_PAD - V_DIM), v_ref.dtype)
        for h in range(heads):
            v_ref[:, h * HEAD_PAD:h * HEAD_PAD + V_DIM] = v[:, h * V_DIM:(h + 1) * V_DIM]
            v_ref[:, h * HEAD_PAD + V_DIM:(h + 1) * HEAD_PAD] = ones


def latent_heads(lat, lat_cols, rank, g_lat, w_pad, head_gain, cos_t, sin_t, heads, tm, w_v=None):
    T = lat.shape[0]
    n_out = heads * HEAD_PAD
    in_specs = [pl.BlockSpec((tm, lat_cols), lambda i: (i, 0)),
                pl.BlockSpec((1, rank), lambda i: (0, 0)),
                pl.BlockSpec((lat_cols, n_out), lambda i: (0, 0)),
                pl.BlockSpec((1, HEAD_PAD), lambda i: (0, 0)),
                pl.BlockSpec((tm, LANES), lambda i: (i, 0)),
                pl.BlockSpec((tm, LANES), lambda i: (i, 0))]
    args = [lat, g_lat.reshape(1, rank), w_pad, head_gain, cos_t, sin_t]
    out_shape = jax.ShapeDtypeStruct((T, n_out), BF16)
    out_specs = pl.BlockSpec((tm, n_out), lambda i: (i, 0))
    if w_v is not None:
        in_specs.append(pl.BlockSpec(w_v.shape, lambda i: (0, 0)))
        args.append(w_v)
        out_shape = (out_shape, jax.ShapeDtypeStruct((T, n_out), BF16))
        out_specs = (out_specs, pl.BlockSpec((tm, n_out), lambda i: (i, 0)))
    return pl.pallas_call(
        functools.partial(_heads_kernel, rank=rank, heads=heads, with_v=w_v is not None),
        grid=(T // tm,),
        in_specs=in_specs, out_specs=out_specs, out_shape=out_shape,
        compiler_params=_params(("parallel",), 48),
        name="latent_heads_kv" if w_v is not None else "latent_heads_q",
    )(*args)


def _flash_kernel(q_ref, k_ref, v_ref, o_ref, s_s, m_s, acc_s, *, tq, sub, n_kt):
    i = pl.program_id(2)
    m_s[...] = jnp.full(m_s.shape, -1e30, F32)
    acc_s[...] = jnp.zeros(acc_s.shape, F32)
    n_sub = tq // sub
    n_heads = m_s.shape[0]

    def scores(hd, kt, slot):
        cols = slice(hd * HEAD_PAD, (hd + 1) * HEAD_PAD)
        k = k_ref[0, pl.ds(kt * tq, tq), cols]
        for h in range(n_sub):
            rows = slice(h * sub, (h + 1) * sub)
            s_s[hd, slot, rows, :] = lax.dot_general(q_ref[0, rows, cols], k, (((1,), (1,)), ((), ())),
                                                     preferred_element_type=F32)

    def softmax_pv(hd, k0, slot, masked):
        cols = slice(hd * HEAD_PAD, (hd + 1) * HEAD_PAD)
        for h in range(n_sub):
            rows = slice(h * sub, (h + 1) * sub)
            nk = (h + 1) * sub if masked else tq
            s = s_s[hd, slot, rows, :nk]
            if masked:
                r = lax.broadcasted_iota(jnp.int32, s.shape, 0) + h * sub
                c = lax.broadcasted_iota(jnp.int32, s.shape, 1)
                s = jnp.where(c <= r, s, -1e30)
            v = v_ref[0, pl.ds(k0, nk), cols]
            m_prev = m_s[hd, rows, :]
            m_new = jnp.maximum(m_prev, jnp.max(s, axis=-1, keepdims=True))
            alpha = jnp.exp2(m_prev - m_new)
            p = jnp.exp2(s - jnp.concatenate([m_new] * (nk // LANES), axis=1))
            acc_s[hd, rows, :] = (jnp.concatenate([alpha, alpha], axis=1) * acc_s[hd, rows, :]
                                  + jnp.dot(p.astype(BF16), v, preferred_element_type=F32))
            m_s[hd, rows, :] = m_new

    for hd in range(n_heads):
        scores(hd, 0, 0)
    for kt in range(n_kt - 1):
        @pl.when(kt < i)
        def _(kt=kt):
            for hd in range(n_heads):
                scores(hd, kt + 1, (kt + 1) % 2)
                softmax_pv(hd, kt * tq, kt % 2, False)
    for hd in range(n_heads):
        softmax_pv(hd, pl.multiple_of(i * tq, tq), i % 2, True)
        o_ref[0, :, hd * V_DIM:(hd + 1) * V_DIM] = (acc_s[hd, :, :V_DIM] / acc_s[hd, :, V_DIM:]).astype(o_ref.dtype)


def flash_attention(q, k, v, heads, tq):
    B, S, _ = q.shape
    hp = HEADS_PER_FLASH_STEP
    assert heads % hp == 0
    return pl.pallas_call(
        functools.partial(_flash_kernel, tq=tq, sub=SUB_FLASH, n_kt=S // tq),
        grid=(B, heads // hp, S // tq),
        in_specs=[pl.BlockSpec((1, tq, hp * HEAD_PAD), lambda b, h, i: (b, i, h)),
                  pl.BlockSpec((1, S, hp * HEAD_PAD), lambda b, h, i: (b, 0, h)),
                  pl.BlockSpec((1, S, hp * HEAD_PAD), lambda b, h, i: (b, 0, h))],
        out_specs=pl.BlockSpec((1, tq, hp * V_DIM), lambda b, h, i: (b, i, h)),
        out_shape=jax.ShapeDtypeStruct((B, S, heads * V_DIM), BF16),
        scratch_shapes=[pltpu.VMEM((hp, 2, tq, tq), F32), pltpu.VMEM((hp, tq, LANES), F32),
                        pltpu.VMEM((hp, tq, HEAD_PAD), F32)],
        compiler_params=_params(("parallel", "parallel", "arbitrary"), 56),
        name="flash_attention",
    )(q, k, v)


def _router_rows(x_ref, g_ref, w_ref, b_ref, o_ref, xt_ref, tm, chunk):
    w = w_ref[...]
    w_hi = w.astype(BF16)
    w_lo = (w - w_hi.astype(F32)).astype(BF16)
    lane_tiles = x_ref.shape[1] // LANES
    for r0 in range(0, tm, chunk):
        xn = _rms_rows(x_ref[r0:r0 + chunk, :], g_ref[...])
        x_hi = xn.astype(BF16)
        x_lo = (xn - x_hi.astype(F32)).astype(BF16)
        acc = jnp.dot(x_hi, w_hi, preferred_element_type=F32)
        acc += jnp.dot(x_lo, w_hi, preferred_element_type=F32)
        acc += jnp.dot(x_hi, w_lo, preferred_element_type=F32)
        o_ref[r0:r0 + chunk, :] = acc + b_ref[...]
        for j in range(lane_tiles):
            xt_ref[pl.ds(r0 * lane_tiles + j, chunk, stride=lane_tiles), :] = xn[:, j * LANES:(j + 1) * LANES]


def _plan_kernel(lg_ref, out_ref, tiles_ref, row_tok_ref, c_s, info_s, inv_s, *, n_tok, n_groups, epg, tm, chunk):
    n_exp = n_groups * epg
    lane = lax.broadcasted_iota(jnp.int32, (chunk, LANES), 1).astype(F32)
    rr = lax.broadcasted_iota(jnp.int32, (chunk, chunk), 0)
    cc = lax.broadcasted_iota(jnp.int32, (chunk, chunk), 1)
    tri = jnp.where(cc <= rr, 1.0, 0.0).astype(BF16)
    neg_inf = -jnp.inf

    def first_argmax(vals, vmax):
        return jnp.min(jnp.where(vals == vmax, lane, float(LANES)), axis=1, keepdims=True)

    def decide(c, carry):
        r0 = pl.multiple_of(c * chunk, chunk)
        lg = lg_ref[pl.ds(r0, chunk), :]
        gl = jnp.where(lane < n_groups, lg, neg_inf)
        gmax = jnp.max(gl, axis=1, keepdims=True)
        g_idx = first_argmax(gl, gmax)
        p_top = 1.0 / jnp.sum(jnp.exp(gl - gmax), axis=1, keepdims=True)
        lo = n_groups + g_idx * epg
        el = jnp.where(jnp.logical_and(lane >= lo, lane < lo + epg), lg, neg_inf)
        l1 = jnp.max(el, axis=1, keepdims=True)
        i1 = first_argmax(el, l1)
        el2 = jnp.where(lane == i1, neg_inf, el)
        l2 = jnp.max(el2, axis=1, keepdims=True)
        i2 = first_argmax(el2, l2)
        d = jnp.exp(l2 - l1)
        w1 = 1.0 / (1.0 + d)
        e1 = i1 - n_groups
        e2 = i2 - n_groups
        onehot = jnp.where(jnp.logical_or(lane == e1, lane == e2), 1.0, 0.0)
        cs = jnp.dot(tri, onehot.astype(BF16), preferred_element_type=F32) + carry
        c_s[pl.ds(r0, chunk), :] = cs
        info_s[pl.ds(r0, chunk), :] = jnp.where(
            lane == 0, e1, jnp.where(lane == 1, e2, jnp.where(
                lane == 2, p_top * w1, jnp.where(lane == 3, p_top * (d * w1), 0.0))))
        return cs[chunk - 1:chunk, :]

    counts = lax.fori_loop(0, n_tok // chunk, decide, jnp.zeros((1, LANES), F32))

    tiles = jnp.floor((counts + (tm - 1)) * (1.0 / tm))
    jj = lax.broadcasted_iota(jnp.int32, (LANES, LANES), 0)
    ee = lax.broadcasted_iota(jnp.int32, (LANES, LANES), 1)
    upper = jnp.where(jj <= ee, 1.0, 0.0).astype(BF16)
    tile_end = jnp.dot(jnp.broadcast_to(tiles, (8, LANES)).astype(BF16), upper,
                       preferred_element_type=F32)[0:1, :]
    tile_start = tile_end - tiles
    row_start = tile_start * tm

    inv_s[...] = jnp.zeros(inv_s.shape, F32)
    off_lane = lax.broadcasted_iota(jnp.int32, (chunk, tm), 1).astype(F32)
    tile_row = lax.broadcasted_iota(jnp.int32, (LANES, chunk), 0).astype(F32)
    tok_local = lax.broadcasted_iota(jnp.int32, (1, chunk), 1)

    def place(c, carry):
        r0 = pl.multiple_of(c * chunk, chunk)
        info = info_s[pl.ds(r0, chunk), :]
        base = row_start + c_s[pl.ds(r0, chunk), :] - 1.0
        pos = [jnp.sum(jnp.where(lane == info[:, k:k + 1], base, 0.0), axis=1, keepdims=True)
               for k in range(TOP_K_IN_GROUP)]
        packed = jnp.where(lane == 0, pos[0], jnp.where(lane == 1, pos[1], info))
        out_ref[pl.ds(r0, chunk), :] = packed
        pos_rows = jnp.transpose(packed)
        tok = r0 + tok_local
        tok_hi = lax.shift_right_logical(tok, LANES.bit_length() - 1).astype(F32)
        tok_lo = (tok & (LANES - 1)).astype(F32)
        for k in range(TOP_K_IN_GROUP):
            tile_of = jnp.floor(pos[k] * (1.0 / tm))
            onehot_off = jnp.where(off_lane == pos[k] - tile_of * tm, 1.0, 0.0).astype(BF16)
            sel = tile_row == jnp.floor(pos_rows[k:k + 1, :] * (1.0 / tm))
            ids = jnp.concatenate([jnp.where(sel, tok_hi, 0.0), jnp.where(sel, tok_lo, 0.0)], axis=0)
            inv_s[...] += jnp.dot(ids.astype(BF16), onehot_off, preferred_element_type=F32)
        return carry

    lax.fori_loop(0, n_tok // chunk, place, 0)
    row_tok_ref[...] = (inv_s[:LANES, :] * LANES + inv_s[LANES:, :]).astype(jnp.int32)

    ti = jj.astype(F32)
    lane_sq = ee.astype(F32)
    is_exp = lane_sq < n_exp
    n_used = tile_end[:, n_exp - 1:n_exp]
    ti_c = jnp.minimum(ti, n_used - 1.0)
    tile_e = jnp.sum(jnp.where(jnp.logical_and(is_exp, tile_end <= ti_c), 1.0, 0.0), axis=1, keepdims=True)
    first = jnp.sum(jnp.where(jnp.logical_and(jnp.logical_and(is_exp, tile_start == ti), tiles > 0.0),
                              1.0, 0.0), axis=1, keepdims=True)
    nonempty = jnp.logical_and(is_exp, tiles > 0.0)
    def next_nonempty(e):
        nxt = jnp.min(jnp.where(jnp.logical_and(nonempty, lane_sq > e), lane_sq, float(LANES)),
                      axis=1, keepdims=True)
        return jnp.where(nxt >= LANES, e, nxt)

    next_e = next_nonempty(tile_e)
    next2_e = next_nonempty(next_e)
    group = jnp.sum(jnp.where(jnp.logical_and(nonempty, lane_sq < tile_e), 1.0, 0.0), axis=1, keepdims=True)
    wslot = group - 2.0 * jnp.floor(group * 0.5)
    fields = (tile_e, first, n_used, next_e, wslot, next2_e)
    assert len(fields) == N_TILE_FIELDS
    packed = jnp.zeros((LANES, LANES), F32)
    for f, val in enumerate(fields):
        packed = jnp.where(lane_sq == f, val, packed)
    tiles_ref[...] = packed.astype(jnp.int32)


def routing_plan(logits, n_groups, epg, tm):
    T = logits.shape[0]
    n_exp = n_groups * epg
    n_tiles = T * TOP_K_IN_GROUP // tm + n_exp
    assert n_tiles <= LANES and T <= LANES * LANES
    out, tiles, row_tok = pl.pallas_call(
        functools.partial(_plan_kernel, n_tok=T, n_groups=n_groups, epg=epg, tm=tm, chunk=PLAN_CHUNK),
        grid=(1,),
        in_specs=[pl.BlockSpec((T, LANES), lambda i: (0, 0))],
        out_specs=(pl.BlockSpec((T, LANES), lambda i: (0, 0)),
                   pl.BlockSpec((LANES, LANES), lambda i: (0, 0)),
                   pl.BlockSpec((LANES, tm), lambda i: (0, 0))),
        out_shape=(jax.ShapeDtypeStruct((T, LANES), F32), jax.ShapeDtypeStruct((LANES, LANES), jnp.int32),
                   jax.ShapeDtypeStruct((LANES, tm), jnp.int32)),
        scratch_shapes=[pltpu.VMEM((T, LANES), F32), pltpu.VMEM((T, LANES), F32),
                        pltpu.VMEM((2 * LANES, tm), F32)],
        compiler_params=_params(("arbitrary",), 40),
        name="routing_plan",
    )(logits)
    pos = out[:, :TOP_K_IN_GROUP].astype(jnp.int32).reshape(-1)
    tile_info = tiles[:n_tiles, :N_TILE_FIELDS].T.reshape(-1)
    return out, pos, tile_info, row_tok[:n_tiles].reshape(-1), n_tiles


def _row_copy(src_hbm, tok, dst_row, sem, rows=1):
    start = tok if rows == 1 else pl.multiple_of(tok * rows, rows)
    return pltpu.make_async_copy(src_hbm.at[pl.ds(start, rows)], dst_row, sem)


def _start_row_gather(src_hbm, idx_ref, base, n, dst_row, sem, rows=1):
    def body(r, carry):
        _row_copy(src_hbm, idx_ref[base + r], dst_row(r), sem, rows).start()
        return carry
    lax.fori_loop(0, n, body, 0, unroll=8)


def _wait_row_gather(src_hbm, n, dst_all, sem):
    pltpu.make_async_copy(src_hbm.at[pl.ds(0, n)], dst_all, sem).wait()


def _gmm_kernel(info_ref, row_tok_ref, x_hbm, wg_hbm, wu_hbm, wd_hbm, o_ref,
                xbuf, sem, wg_f, wu_f, wd_f, wsem, wg_bf, wu_bf, wd_bf, *, tm, n_tiles, w_off, n_chunks, tok_rows):
    i = pl.program_id(0)
    expert = info_ref[i]
    first = info_ref[n_tiles + i]
    n_used = info_ref[2 * n_tiles]
    next_expert = info_ref[3 * n_tiles + i]
    wslot = info_ref[4 * n_tiles + i]
    next2_expert = info_ref[5 * n_tiles + i]
    slot = i % GATHER_BUFS
    ahead = GATHER_BUFS - 1

    def wait_slot(s):
        _wait_row_gather(x_hbm, tm * tok_rows, xbuf.at[s], sem.at[s])

    def weight_copies(e, s):
        copies = []
        for hbm, buf in ((wg_hbm, wg_f), (wu_hbm, wu_f), (wd_hbm, wd_f)):
            rows = hbm.shape[1] // n_chunks
            for c in range(n_chunks):
                copies.append(pltpu.make_async_copy(hbm.at[w_off + e, pl.ds(c * rows, rows)],
                                                    buf.at[s, pl.ds(c * rows, rows)], wsem.at[s]))
        return copies

    def start_weights(e, s):
        for c, cp in enumerate(weight_copies(e, s)):
            cp.start(priority=c % DMA_PRIORITIES)

    @pl.when(jnp.logical_and(i == 0, n_used > 0))
    def _():
        start_weights(expert, 0)

        @pl.when(next_expert != expert)
        def _():
            start_weights(next_expert, 1)

        for t in range(ahead):
            _start_row_gather(x_hbm, row_tok_ref, min(t, n_tiles - 1) * tm, tm,
                              lambda r, t=t: xbuf.at[t, pl.ds(pl.multiple_of(r * tok_rows, tok_rows), tok_rows)],
                              sem.at[t], tok_rows)

    @pl.when(i < n_used)
    def _():
        @pl.when(first == 1)
        def _():
            for cp in weight_copies(expert, wslot):
                cp.wait()
            for src, dst in ((wg_f, wg_bf), (wu_f, wu_bf), (wd_f, wd_bf)):
                rows = dst.shape[0] // WEIGHT_CONVERT_CHUNKS

                def convert(c, carry, src=src, dst=dst, rows=rows):
                    r0 = pl.multiple_of(c * rows, rows)
                    dst[pl.ds(r0, rows), :] = src[wslot, pl.ds(r0, rows), :].astype(BF16)
                    return carry
                lax.fori_loop(0, WEIGHT_CONVERT_CHUNKS, convert, 0)

            @pl.when(next2_expert != next_expert)
            def _():
                start_weights(next2_expert, wslot)

        wait_slot(slot)
        xn = jnp.concatenate([xbuf[slot, pl.ds(j, tm, stride=tok_rows), :].astype(BF16)
                              for j in range(tok_rows)], axis=1)
        base = jnp.minimum(i + ahead, n_tiles - 1) * tm
        nslot = (i + ahead) % GATHER_BUFS
        for r in range(tm):
            _row_copy(x_hbm, row_tok_ref[base + r], xbuf.at[nslot, pl.ds(r * tok_rows, tok_rows)],
                      sem.at[nslot], tok_rows).start(priority=r % DMA_PRIORITIES)
        hg = jnp.dot(xn, wg_bf[...], preferred_element_type=F32)
        hu = jnp.dot(xn, wu_bf[...], preferred_element_type=F32)
        hh = (hg * jax.nn.sigmoid(hg)) * hu
        o_ref[...] = jnp.dot(hh.astype(BF16), wd_bf[...], preferred_element_type=F32)

    @pl.when(i >= n_used)
    def _():
        @pl.when(jnp.logical_and(i < n_used + ahead, n_used > 0))
        def _():
            wait_slot(slot)
        o_ref[...] = jnp.zeros(o_ref.shape, o_ref.dtype)

    @pl.when(i == n_tiles - 1)
    def _():
        for d in range(1, ahead + 1):
            @pl.when(n_used >= n_tiles - ahead + d)
            def _(d=d):
                wait_slot((n_tiles - 1 + d) % GATHER_BUFS)


def grouped_expert_mlp(xt, tile_info, row_tok, w_gate, w_up, w_down, layer, n_tiles, tm):
    _, D, F = w_gate.shape
    tok_rows = D // LANES
    layer_idx, depth = layer
    off = layer_idx * (w_gate.shape[0] // depth)
    any_spec = pl.BlockSpec(memory_space=pl.ANY)
    grid_spec = pltpu.PrefetchScalarGridSpec(
        num_scalar_prefetch=2,
        grid=(n_tiles,),
        in_specs=[any_spec, any_spec, any_spec, any_spec],
        out_specs=pl.BlockSpec((tm, D), lambda i, ti, rt: (i, 0)),
        scratch_shapes=[pltpu.VMEM((GATHER_BUFS, tm * tok_rows, LANES), F32),
                        pltpu.SemaphoreType.DMA((GATHER_BUFS,)),
                        pltpu.VMEM((2, D, F), F32), pltpu.VMEM((2, D, F), F32), pltpu.VMEM((2, F, D), F32),
                        pltpu.SemaphoreType.DMA((2,)),
                        pltpu.VMEM((D, F), BF16), pltpu.VMEM((D, F), BF16), pltpu.VMEM((F, D), BF16)],
    )
    return pl.pallas_call(
        functools.partial(_gmm_kernel, tm=tm, n_tiles=n_tiles, w_off=off, n_chunks=WEIGHT_DMA_CHUNKS,
                          tok_rows=tok_rows),
        grid_spec=grid_spec,
        out_shape=jax.ShapeDtypeStruct((n_tiles * tm, D), F32),
        compiler_params=_params(("arbitrary",), 56),
        name="grouped_expert_mlp",
    )(tile_info, row_tok, xt, w_gate, w_up, w_down)


def _combine_kernel(pos_ref, x_ref, plan_ref, y_hbm, o_ref, ybuf, sem, *, tc, top_k):
    i = pl.program_id(0)
    n = pl.num_programs(0)
    slot = i % 2

    def start(tile, s):
        base = tile * (tc * top_k)
        for r in range(tc):
            for k in range(top_k):
                _row_copy(y_hbm, pos_ref[base + r * top_k + k], ybuf.at[s, k, pl.ds(r, 1)], sem.at[s]).start()

    @pl.when(i == 0)
    def _():
        start(0, 0)

    for k in range(top_k):
        _wait_row_gather(y_hbm, tc, ybuf.at[slot, k], sem.at[slot])

    @pl.when(i + 1 < n)
    def _():
        start(i + 1, 1 - slot)

    acc = x_ref[...]
    for k in range(top_k):
        acc = acc + plan_ref[:, top_k + k:top_k + k + 1] * ybuf[slot, k]
    o_ref[...] = acc


def moe_combine(x, y_sorted, plan_out, pos, tc):
    T, D = x.shape
    top_k = TOP_K_IN_GROUP
    grid_spec = pltpu.PrefetchScalarGridSpec(
        num_scalar_prefetch=1,
        grid=(T // tc,),
        in_specs=[pl.BlockSpec((tc, D), lambda i, p: (i, 0)),
                  pl.BlockSpec((tc, LANES), lambda i, p: (i, 0)),
                  pl.BlockSpec(memory_space=pl.ANY)],
        out_specs=pl.BlockSpec((tc, D), lambda i, p: (i, 0)),
        scratch_shapes=[pltpu.VMEM((2, top_k, tc, D), F32), pltpu.SemaphoreType.DMA((2,))],
    )
    return pl.pallas_call(
        functools.partial(_combine_kernel, tc=tc, top_k=top_k),
        grid_spec=grid_spec,
        out_shape=jax.ShapeDtypeStruct((T, D), F32),
        compiler_params=_params(("arbitrary",), 32),
        name="moe_combine",
    )(pos, x, plan_out, y_sorted)


def _combine_mm_kernel(pos_ref, x_ref, plan_ref, y_hbm, g_ref, w_hbm, xnew_ref, o_ref,
                       ybuf, ysem, xn_ref, w_bf, stage, wsem, *, layer, tc, chunk, tn, top_k):
    i = pl.program_id(0)
    n = pl.num_programs(0)
    slot = i % 2

    def start(tile, s):
        base = tile * (tc * top_k)
        for r in range(tc):
            for k in range(top_k):
                _row_copy(y_hbm, pos_ref[base + r * top_k + k], ybuf.at[s, k, pl.ds(r, 1)], ysem.at[s]).start()

    def wait(s):
        for k in range(top_k):
            _wait_row_gather(y_hbm, tc, ybuf.at[s, k], ysem.at[s])

    @pl.when(i == 0)
    def _():
        start(0, 0)
        _stage_weight_bf16(w_hbm, layer, w_bf, stage, wsem, tn)

    wait(slot)
    for r0 in range(0, tc, chunk):
        rows = slice(r0, r0 + chunk)
        acc = x_ref[rows, :]
        for k in range(top_k):
            acc = acc + plan_ref[rows, top_k + k:top_k + k + 1] * ybuf[slot, k, rows, :]
        xnew_ref[rows, :] = acc
        xn_ref[rows, :] = _rms_rows(acc, g_ref[...]).astype(BF16)
    start(jnp.minimum(i + 1, n - 1), 1 - slot)
    for j in range(o_ref.shape[1] // tn):
        o_ref[:, j * tn:(j + 1) * tn] = jnp.dot(xn_ref[...], w_bf[:, j * tn:(j + 1) * tn],
                                                preferred_element_type=F32).astype(o_ref.dtype)

    @pl.when(i == n - 1)
    def _():
        wait(1 - slot)


def combine_norm_matmul(x, y_sorted, plan_out, pos, g, w, layer, out_dtype, tc):
    T, D = x.shape
    N = w.shape[2]
    top_k = TOP_K_IN_GROUP
    tn = _col_tile(N)
    grid_spec = pltpu.PrefetchScalarGridSpec(
        num_scalar_prefetch=1,
        grid=(T // tc,),
        in_specs=[pl.BlockSpec((tc, D), lambda i, p: (i, 0)),
                  pl.BlockSpec((tc, LANES), lambda i, p: (i, 0)),
                  pl.BlockSpec(memory_space=pl.ANY),
                  pl.BlockSpec((1, D), lambda i, p: (0, 0)),
                  pl.BlockSpec(memory_space=pl.ANY)],
        out_specs=(pl.BlockSpec((tc, D), lambda i, p: (i, 0)),
                   pl.BlockSpec((tc, N), lambda i, p: (i, 0))),
        scratch_shapes=[pltpu.VMEM((2, top_k, tc, D), F32), pltpu.SemaphoreType.DMA((2,)),
                        pltpu.VMEM((tc, D), BF16), pltpu.VMEM((D, N), BF16), pltpu.VMEM((2, D, tn), F32),
                        pltpu.SemaphoreType.DMA((2,))],
    )
    return pl.pallas_call(
        functools.partial(_combine_mm_kernel, layer=layer, tc=tc, chunk=min(tc, 128), tn=tn, top_k=top_k),
        grid_spec=grid_spec,
        out_shape=(jax.ShapeDtypeStruct((T, D), F32), jax.ShapeDtypeStruct((T, N), out_dtype)),
        compiler_params=_params(("arbitrary",), 56),
        name="combine_norm_matmul",
    )(pos, x, plan_out, y_sorted, g.reshape(1, D), w)


def hier_moe_experts(logits, xt, n_groups, epg, w_gate, w_up, w_down, layer):
    plan_out, pos, tile_info, row_tok, n_tiles = routing_plan(logits, n_groups, epg, TM_EXPERT)
    y_sorted = grouped_expert_mlp(xt, tile_info, row_tok, w_gate, w_up, w_down, layer, n_tiles, TM_EXPERT)
    return y_sorted, plan_out, pos


def _pad_head_cols(w, heads, real):
    r = w.shape[0]
    return jnp.pad(w.reshape(r, heads, real), ((0, 0), (0, 0), (0, HEAD_PAD - real))).reshape(r, heads * HEAD_PAD)


def _rope_tables(positions):
    half = QK_ROPE // 2
    inv_freq = ROPE_THETA ** (-jnp.arange(half, dtype=F32) / half)
    ang = positions.astype(F32).reshape(-1, 1) * inv_freq[None, :]
    cos, sin = jnp.cos(ang), jnp.sin(ang)
    zeros = jnp.zeros((ang.shape[0], LANES - QK_ROPE), F32)
    return (jnp.concatenate([cos, cos, zeros], axis=1),
            jnp.concatenate([-sin, sin, zeros], axis=1))


def kernel(x, mem, positions, norm_mix_g, norm_ffn_g, w_o, mem_norm_g, w_mem_kv, mem_q_norm_g, mem_k_norm_g, w_in_a, conv_w, conv_b, w_lru_r, b_lru_r, w_lru_i, b_lru_i, lru_lambda, kv_in_norm_g, w_dkv, kv_latent_norm_g, w_uk, w_uv, k_head_norm_g, w_in_b, q_latent_norm_g, w_uq, q_head_norm_g, w_router_grp, b_router_grp, w_router_exp, b_router_exp, w_exp_gate, w_exp_up, w_exp_down):
    B, S, D = x.shape
    T = B * S
    M = mem.shape[1]
    depth = norm_mix_g.shape[0]
    n_a = w_in_a.shape[0]
    lru_w = lru_lambda.shape[1]
    mem_w = w_mem_kv.shape[2] // 2
    heads = w_uv.shape[1] // V_DIM
    kv_rank = kv_latent_norm_g.shape[0]
    q_rank = q_latent_norm_g.shape[1]
    n_groups, epg = w_exp_gate.shape[1], w_exp_gate.shape[2]
    n_exp = n_groups * epg
    d_exp = w_exp_gate.shape[-1]

    xr = x.reshape(T, D)
    mem2 = mem.reshape(B * M, D)
    cos_t, sin_t = _rope_tables(positions)
    wg_all = w_exp_gate.reshape(depth * n_exp, D, d_exp)
    wu_all = w_exp_up.reshape(depth * n_exp, D, d_exp)
    wd_all = w_exp_down.reshape(depth * n_exp, d_exp, D)
    pad_r = LANES - n_groups - n_exp
    w_router = jnp.concatenate([w_router_grp, w_router_exp, jnp.zeros((depth, D, pad_r), F32)], axis=2)
    b_router = jnp.concatenate([b_router_grp, b_router_exp, jnp.zeros((depth, pad_r), F32)], axis=1)

    def shared_kv(x_res):
        lat_cols = kv_rank + LANES
        w_dkv_pad = jnp.pad(w_dkv, ((0, 0), (0, lat_cols - w_dkv.shape[1])))[None]
        ckv = norm_matmul(x_res, kv_in_norm_g, w_dkv_pad, 0, F32, tm=TM_PROJ)
        eye = jnp.pad(jnp.eye(QK_ROPE, dtype=F32), ((0, LANES - QK_ROPE), (0, 0)))
        eye3 = jnp.broadcast_to(eye[:, None, :], (LANES, heads, QK_ROPE))
        rope_rows = jnp.pad(eye3, ((0, 0), (0, 0), (QK_NOPE, HEAD_PAD - QK_DIM))).reshape(LANES, heads * HEAD_PAD)
        wk = jnp.concatenate([_pad_head_cols(w_uk, heads, QK_NOPE), rope_rows], axis=0).astype(BF16)
        kg = jnp.pad(k_head_norm_g, (0, HEAD_PAD - QK_DIM)).reshape(1, HEAD_PAD)
        k_flat, v_flat = latent_heads(ckv, lat_cols, kv_rank, kv_latent_norm_g, wk, kg, cos_t, sin_t,
                                      heads, tm=TM_HEADS, w_v=w_uv.astype(BF16))
        return k_flat.reshape(B, S, -1), v_flat.reshape(B, S, -1)

    def in_proj(x_res, pending, g, w, idx):
        if pending is None:
            return x_res, norm_matmul(x_res, g, w, idx, BF16, tm=TM_PROJ)
        return combine_norm_matmul(x_res, *pending, g, w, idx, BF16, TC_COMBINE)

    k_sh = v_sh = None
    pending = None
    for l in range(depth):
        kv_mem = norm_matmul(mem2, mem_norm_g[l], w_mem_kv, l, BF16, tm=TM_PROJ).reshape(B, M, 2 * mem_w)
        if l < n_a:
            xr, proj = in_proj(xr, pending, norm_mix_g[l], w_in_a, l)
            y_mix = rglru(proj.reshape(B, S, -1), conv_w[l], conv_b[l], w_lru_r[l], b_lru_r[l],
                          w_lru_i[l], b_lru_i[l], lru_lambda[l], nb=LRU_BLOCKS_PER_STEP,
                          tc=TC_LRU).reshape(T, lru_w)
            q_col = 2 * lru_w // mem_w
        else:
            j = l - n_a
            xr, proj = in_proj(xr, pending, norm_mix_g[l], w_in_b, j)
            if l == n_a:
                k_sh, v_sh = shared_kv(xr)
            wq = _pad_head_cols(w_uq[j], heads, QK_DIM).astype(BF16)
            qg = jnp.pad(q_head_norm_g[j] * (QK_DIM ** -0.5 * math.log2(math.e)),
                         (0, HEAD_PAD - QK_DIM)).reshape(1, HEAD_PAD)
            q = latent_heads(proj, q_rank, q_rank, q_latent_norm_g[j], wq, qg, cos_t, sin_t, heads, tm=TM_HEADS)
            y_mix = flash_attention(q.reshape(B, S, -1), k_sh, v_sh, heads, tq=TQ_FLASH).reshape(T, heads * V_DIM)
            q_col = q_rank // mem_w
        y_mem = mem_attention(proj, q_col, kv_mem, mem_q_norm_g[l], mem_k_norm_g[l], B, ts=TS_MEM)
        xr, logits, xt = out_proj(y_mix, y_mem, w_o, l, xr, norm_ffn_g[l], w_router[l],
                                  b_router[l].reshape(1, LANES), tm=TM_PROJ)
        pending = hier_moe_experts(logits, xt, n_groups, epg, wg_all, wu_all, wd_all, (l, depth))
    xr = moe_combine(xr, *pending, TC_COMBINE)
    return xr.reshape(B, S, D)
```

```python
import functools
import math

import jax
import jax.numpy as jnp
from jax import lax
from jax.experimental import pallas as pl
from jax.experimental.pallas import tpu as pltpu

F32 = jnp.float32
BF16 = jnp.bfloat16
EPS = 1e-6

MEM_HEADS = 4
CONV_WIDTH = 4
LRU_C = 8.0
QK_NOPE = 128
QK_ROPE = 64
QK_DIM = QK_NOPE + QK_ROPE
V_DIM = 128
ROPE_THETA = 10000.0
TOP_K_IN_GROUP = 2

LANES = 128
HEAD_PAD = 2 * LANES
MIB = 1024 * 1024
DMA_PRIORITIES = 2

TM_PROJ = 512
TN_PROJ = 512
TS_MEM = 512
TQ_FLASH = 512
SUB_FLASH = 256
HEADS_PER_FLASH_STEP = 6
TM_HEADS = 512
TM_EXPERT = 256
WEIGHT_DMA_CHUNKS = 16
WEIGHT_CONVERT_CHUNKS = 4
GATHER_BUFS = 3
TC_COMBINE = 256
LRU_BLOCKS_PER_STEP = 6
TC_LRU = 256
PLAN_CHUNK = 256
N_TILE_FIELDS = 6


def _params(sem, vmem_mib):
    return pltpu.CompilerParams(dimension_semantics=sem, vmem_limit_bytes=vmem_mib * MIB)


def _sigmoid(z):
    return 0.5 * jnp.tanh(0.5 * z) + 0.5


def _rms_rows(x, g):
    x = x.astype(F32)
    ms = jnp.mean(x * x, axis=-1, keepdims=True)
    return x * lax.rsqrt(ms + EPS) * g


def _col_tile(n):
    return TN_PROJ if n % TN_PROJ == 0 else n


def _stage_weight_bf16(w_hbm, layer, w_bf, stage, sem, tn):
    n = w_bf.shape[1] // tn

    def copy(j):
        return pltpu.make_async_copy(w_hbm.at[layer, :, pl.ds(j * tn, tn)], stage.at[j % 2], sem.at[j % 2])

    copy(0).start()
    for j in range(n):
        if j + 1 < n:
            copy(j + 1).start()
        copy(j).wait()
        w_bf[:, j * tn:(j + 1) * tn] = stage[j % 2].astype(BF16)


def _norm_mm_kernel(x_ref, g_ref, w_hbm, o_ref, xn_ref, w_bf, stage, sem, *, layer, tm, chunk, tn):
    @pl.when(pl.program_id(0) == 0)
    def _():
        _stage_weight_bf16(w_hbm, layer, w_bf, stage, sem, tn)

    for r0 in range(0, tm, chunk):
        xn_ref[r0:r0 + chunk, :] = _rms_rows(x_ref[r0:r0 + chunk, :], g_ref[...]).astype(BF16)

    for j in range(o_ref.shape[1] // tn):
        o_ref[:, j * tn:(j + 1) * tn] = jnp.dot(xn_ref[...], w_bf[:, j * tn:(j + 1) * tn],
                                                preferred_element_type=F32).astype(o_ref.dtype)


def norm_matmul(x, g, w, layer, out_dtype, tm):
    T, K = x.shape
    N = w.shape[2]
    tm = min(tm, T)
    chunk = min(tm, 256)
    tn = _col_tile(N)
    return pl.pallas_call(
        functools.partial(_norm_mm_kernel, layer=layer, tm=tm, chunk=chunk, tn=tn),
        grid=(T // tm,),
        in_specs=[pl.BlockSpec((tm, K), lambda i: (i, 0)),
                  pl.BlockSpec((1, K), lambda i: (0, 0)),
                  pl.BlockSpec(memory_space=pl.ANY)],
        out_specs=pl.BlockSpec((tm, N), lambda i: (i, 0)),
        out_shape=jax.ShapeDtypeStruct((T, N), out_dtype),
        scratch_shapes=[pltpu.VMEM((tm, K), BF16), pltpu.VMEM((K, N), BF16), pltpu.VMEM((2, K, tn), F32),
                        pltpu.SemaphoreType.DMA((2,))],
        compiler_params=_params(("arbitrary",), 56),
        name="norm_matmul",
    )(x, g.reshape(1, K), w)


def _out_proj_kernel(ya_ref, yb_ref, w_hbm, x_ref, g_ref, wr_ref, br_ref, o_ref, lg_ref, xt_ref,
                     w_bf, stage, sem, *, layer, tm, tn):
    @pl.when(pl.program_id(0) == 0)
    def _():
        _stage_weight_bf16(w_hbm, layer, w_bf, stage, sem, tn)

    wa_rows = ya_ref.shape[1]

    for j in range(o_ref.shape[1] // tn):
        cols = slice(j * tn, (j + 1) * tn)
        acc = jnp.dot(ya_ref[...], w_bf[:wa_rows, cols], preferred_element_type=F32)
        acc += jnp.dot(yb_ref[...], w_bf[wa_rows:, cols], preferred_element_type=F32)
        o_ref[:, cols] = x_ref[:, cols] + acc
    _router_rows(o_ref, g_ref, wr_ref, br_ref, lg_ref, xt_ref, tm, min(tm, 256))


def out_proj(y_mix, y_mem, w_o, layer, x, g_ffn, w_router, b_router, tm):
    T, Wa = y_mix.shape
    Wb = y_mem.shape[1]
    _, K, N = w_o.shape
    assert Wa + Wb == K
    tn = _col_tile(N)
    lane_tiles = N // LANES
    return pl.pallas_call(
        functools.partial(_out_proj_kernel, layer=layer, tm=tm, tn=tn),
        grid=(T // tm,),
        in_specs=[pl.BlockSpec((tm, Wa), lambda i: (i, 0)),
                  pl.BlockSpec((tm, Wb), lambda i: (i, 0)),
                  pl.BlockSpec(memory_space=pl.ANY),
                  pl.BlockSpec((tm, N), lambda i: (i, 0)),
                  pl.BlockSpec((1, N), lambda i: (0, 0)),
                  pl.BlockSpec((N, LANES), lambda i: (0, 0)),
                  pl.BlockSpec((1, LANES), lambda i: (0, 0))],
        out_specs=(pl.BlockSpec((tm, N), lambda i: (i, 0)),
                   pl.BlockSpec((tm, LANES), lambda i: (i, 0)),
                   pl.BlockSpec((tm * lane_tiles, LANES), lambda i: (i, 0))),
        out_shape=(jax.ShapeDtypeStruct((T, N), F32),
                   jax.ShapeDtypeStruct((T, LANES), F32),
                   jax.ShapeDtypeStruct((T * lane_tiles, LANES), F32)),
        scratch_shapes=[pltpu.VMEM((K, N), BF16), pltpu.VMEM((2, K, tn), F32), pltpu.SemaphoreType.DMA((2,))],
        compiler_params=_params(("arbitrary",), 56),
        name="out_proj",
    )(y_mix, y_mem, w_o, x, g_ffn.reshape(1, N), w_router, b_router)


def _mem_attn_kernel(q_ref, kv_ref, qg_ref, kg_ref, o_ref, *, heads, hd):
    width = heads * hd
    scale = hd ** -0.5
    for h in range(heads):
        qn = _rms_rows(q_ref[:, h * hd:(h + 1) * hd], qg_ref[...]) * scale
        kn = _rms_rows(kv_ref[0, :, h * hd:(h + 1) * hd], kg_ref[...])
        v = kv_ref[0, :, width + h * hd:width + (h + 1) * hd]
        s = lax.dot_general(qn.astype(BF16), kn.astype(BF16), (((1,), (1,)), ((), ())),
                            preferred_element_type=F32)
        m = jnp.max(s, axis=-1, keepdims=True)
        p = jnp.exp(s - m)
        l = jnp.sum(p, axis=-1, keepdims=True)
        o = jnp.dot(p.astype(BF16), v, preferred_element_type=F32) / l
        o_ref[:, h * hd:(h + 1) * hd] = o.astype(o_ref.dtype)


def mem_attention(proj, q_col_block, kv, q_g, k_g, batch, ts):
    T = proj.shape[0]
    _, M, two_w = kv.shape
    width = two_w // 2
    hd = width // MEM_HEADS
    per_b = T // batch // ts
    return pl.pallas_call(
        functools.partial(_mem_attn_kernel, heads=MEM_HEADS, hd=hd),
        grid=(batch, per_b),
        in_specs=[pl.BlockSpec((ts, width), lambda b, i: (b * per_b + i, q_col_block)),
                  pl.BlockSpec((1, M, two_w), lambda b, i: (b, 0, 0)),
                  pl.BlockSpec((1, hd), lambda b, i: (0, 0)),
                  pl.BlockSpec((1, hd), lambda b, i: (0, 0))],
        out_specs=pl.BlockSpec((ts, width), lambda b, i: (b * per_b + i, 0)),
        out_shape=jax.ShapeDtypeStruct((T, width), BF16),
        compiler_params=_params(("parallel", "arbitrary"), 32),
        name="mem_attention",
    )(proj, kv, q_g.reshape(1, hd), k_g.reshape(1, hd))


def _gelu_tanh(x):
    return 0.5 * x * (1.0 + jnp.tanh(0.7978845608028654 * (x + 0.044715 * x * x * x)))


def _rglru_kernel(u_ref, gb_ref, cw_ref, cb_ref, wr_ref, wi_ref, br_ref, bi_ref, lam_ref,
                  o_ref, a_s, b_s, *, seq, nb, blk, tc):
    wb = nb * blk
    neg_lam = -lam_ref[...]
    softplus = jnp.maximum(neg_lam, 0.0) + jnp.log1p(jnp.exp(-jnp.abs(neg_lam)))

    def gates(c, carry):
        t0 = pl.multiple_of(c * tc, tc)
        cur = u_ref[0, pl.ds(t0, tc), :].astype(F32)
        p0 = pl.multiple_of(jnp.maximum(t0 - 16, 0), 16)
        prev = u_ref[0, pl.ds(p0, 16), :].astype(F32)
        prev = jnp.where(c > 0, prev, 0.0)
        full = jnp.concatenate([prev, cur], axis=0)
        y = cb_ref[...] + cw_ref[0:1, :] * cur
        for k in range(1, CONV_WIDTH):
            y = y + cw_ref[k:k + 1, :] * full[16 - k:16 - k + tc, :]
        r_parts, i_parts = [], []
        for n in range(nb):
            yb = y[:, n * blk:(n + 1) * blk].astype(BF16)
            r_parts.append(jnp.dot(yb, wr_ref[n].astype(BF16), preferred_element_type=F32))
            i_parts.append(jnp.dot(yb, wi_ref[n].astype(BF16), preferred_element_type=F32))
        r = _sigmoid(jnp.concatenate(r_parts, axis=1) + br_ref[...])
        ig = _sigmoid(jnp.concatenate(i_parts, axis=1) + bi_ref[...])
        log_a = (-LRU_C) * r * softplus
        a = jnp.exp(log_a)
        a_s[pl.ds(t0, tc), :] = a
        b_s[pl.ds(t0, tc), :] = jnp.sqrt(-jnp.tanh(log_a) * (a * a + 1.0)) * (ig * y)
        return carry

    lax.fori_loop(0, seq // tc, gates, 0)

    row = lax.broadcasted_iota(jnp.int32, (8, wb), 0)

    def scan(c, h):
        t0 = pl.multiple_of(c * 8, 8)
        a = a_s[pl.ds(t0, 8), :]
        b = b_s[pl.ds(t0, 8), :]
        for s in (1, 2, 4):
            a_sh = pltpu.roll(a, s, axis=0)
            b_sh = pltpu.roll(b, s, axis=0)
            keep = row >= s
            b = jnp.where(keep, a * b_sh + b, b)
            a = jnp.where(keep, a * a_sh, a)
        hc = a * h + b
        b_s[pl.ds(t0, 8), :] = hc
        return hc[7:8, :]

    lax.fori_loop(0, seq // 8, scan, jnp.zeros((1, wb), F32), unroll=4)

    def gate_out(c, carry):
        t0 = pl.multiple_of(c * tc, tc)
        g = gb_ref[0, pl.ds(t0, tc), :].astype(F32)
        o_ref[0, pl.ds(t0, tc), :] = (_gelu_tanh(g) * b_s[pl.ds(t0, tc), :]).astype(o_ref.dtype)
        return carry

    lax.fori_loop(0, seq // tc, gate_out, 0)


def rglru(proj, conv_w, conv_b, w_r, b_r, w_i, b_i, lam, nb, tc):
    B, S, _ = proj.shape
    W = lam.shape[0]
    n_blocks, blk, _ = w_r.shape
    wb = nb * blk
    ncb = W // wb
    vec = lambda: pl.BlockSpec((1, wb), lambda b, j: (0, j))
    return pl.pallas_call(
        functools.partial(_rglru_kernel, seq=S, nb=nb, blk=blk, tc=tc),
        grid=(B, ncb),
        in_specs=[pl.BlockSpec((1, S, wb), lambda b, j: (b, 0, j)),
                  pl.BlockSpec((1, S, wb), lambda b, j: (b, 0, ncb + j)),
                  pl.BlockSpec((CONV_WIDTH, wb), lambda b, j: (0, j)),
                  vec(),
                  pl.BlockSpec((nb, blk, blk), lambda b, j: (j, 0, 0)),
                  pl.BlockSpec((nb, blk, blk), lambda b, j: (j, 0, 0)),
                  vec(), vec(), vec()],
        out_specs=pl.BlockSpec((1, S, wb), lambda b, j: (b, 0, j)),
        out_shape=jax.ShapeDtypeStruct((B, S, W), BF16),
        scratch_shapes=[pltpu.VMEM((S, wb), F32), pltpu.VMEM((S, wb), F32)],
        compiler_params=_params(("parallel", "arbitrary"), 48),
        name="rglru",
    )(proj, proj, conv_w, conv_b.reshape(1, W), w_r, w_i, b_r.reshape(1, W), b_i.reshape(1, W),
      lam.reshape(1, W))


def _heads_kernel(*refs, rank, heads, with_v):
    if with_v:
        lat_ref, gl_ref, w_ref, hg_ref, cos_ref, sin_ref, wv_ref, o_ref, v_ref = refs
    else:
        lat_ref, gl_ref, w_ref, hg_ref, cos_ref, sin_ref, o_ref = refs
    lat = lat_ref[...].astype(F32)
    cn = _rms_rows(lat[:, :rank], gl_ref[...])
    full = cn if lat.shape[1] == rank else jnp.concatenate([cn, lat[:, rank:]], axis=1)
    full = full.astype(BF16)
    lane = lax.broadcasted_iota(jnp.int32, cos_ref.shape, 1)
    half = QK_ROPE // 2
    for h in range(heads):
        t = jnp.dot(full, w_ref[:, h * HEAD_PAD:(h + 1) * HEAD_PAD], preferred_element_type=F32)
        ss = jnp.sum(t * t, axis=-1, keepdims=True) * (1.0 / QK_DIM)
        tn = t * lax.rsqrt(ss + EPS) * hg_ref[...]
        rp = tn[:, QK_NOPE:]
        swapped = jnp.where(lane < half, pltpu.roll(rp, LANES - half, axis=1),
                            pltpu.roll(rp, half, axis=1))
        rot = rp * cos_ref[...] + swapped * sin_ref[...]
        o_ref[:, h * HEAD_PAD:h * HEAD_PAD + QK_NOPE] = tn[:, :QK_NOPE].astype(o_ref.dtype)
        o_ref[:, h * HEAD_PAD + QK_NOPE:(h + 1) * HEAD_PAD] = rot.astype(o_ref.dtype)
    if with_v:
        v = jnp.dot(cn.astype(BF16), wv_ref[...], preferred_element_type=F32).astype(v_ref.dtype)
        ones = jnp.ones((v.shape[0], HEAD_PAD - V_DIM), v_ref.dtype)
        for h in range(heads):
            v_ref[:, h * HEAD_PAD:h * HEAD_PAD + V_DIM] = v[:, h * V_DIM:(h + 1) * V_DIM]
            v_ref[:, h * HEAD_PAD + V_DIM:(h + 1) * HEAD_PAD] = ones


def latent_heads(lat, lat_cols, rank, g_lat, w_pad, head_gain, cos_t, sin_t, heads, tm, w_v=None):
    T = lat.shape[0]
    n_out = heads * HEAD_PAD
    in_specs = [pl.BlockSpec((tm, lat_cols), lambda i: (i, 0)),
                pl.BlockSpec((1, rank), lambda i: (0, 0)),
                pl.BlockSpec((lat_cols, n_out), lambda i: (0, 0)),
                pl.BlockSpec((1, HEAD_PAD), lambda i: (0, 0)),
                pl.BlockSpec((tm, LANES), lambda i: (i, 0)),
                pl.BlockSpec((tm, LANES), lambda i: (i, 0))]
    args = [lat, g_lat.reshape(1, rank), w_pad, head_gain, cos_t, sin_t]
    out_shape = jax.ShapeDtypeStruct((T, n_out), BF16)
    out_specs = pl.BlockSpec((tm, n_out), lambda i: (i, 0))
    if w_v is not None:
        in_specs.append(pl.BlockSpec(w_v.shape, lambda i: (0, 0)))
        args.append(w_v)
        out_shape = (out_shape, jax.ShapeDtypeStruct((T, n_out), BF16))
        out_specs = (out_specs, pl.BlockSpec((tm, n_out), lambda i: (i, 0)))
    return pl.pallas_call(
        functools.partial(_heads_kernel, rank=rank, heads=heads, with_v=w_v is not None),
        grid=(T // tm,),
        in_specs=in_specs, out_specs=out_specs, out_shape=out_shape,
        compiler_params=_params(("parallel",), 48),
        name="latent_heads_kv" if w_v is not None else "latent_heads_q",
    )(*args)


def _flash_kernel(q_ref, k_ref, v_ref, o_ref, s_s, m_s, acc_s, *, tq, sub, n_kt):
    i = pl.program_id(2)
    m_s[...] = jnp.full(m_s.shape, -1e30, F32)
    acc_s[...] = jnp.zeros(acc_s.shape, F32)
    n_sub = tq // sub
    n_heads = m_s.shape[0]

    def scores(hd, kt, slot):
        cols = slice(hd * HEAD_PAD, (hd + 1) * HEAD_PAD)
        k = k_ref[0, pl.ds(kt * tq, tq), cols]
        for h in range(n_sub):
            rows = slice(h * sub, (h + 1) * sub)
            s_s[hd, slot, rows, :] = lax.dot_general(q_ref[0, rows, cols], k, (((1,), (1,)), ((), ())),
                                                     preferred_element_type=F32)

    def softmax_pv(hd, k0, slot, masked):
        cols = slice(hd * HEAD_PAD, (hd + 1) * HEAD_PAD)
        for h in range(n_sub):
            rows = slice(h * sub, (h + 1) * sub)
            nk = (h + 1) * sub if masked else tq
            s = s_s[hd, slot, rows, :nk]
            if masked:
                r = lax.broadcasted_iota(jnp.int32, s.shape, 0) + h * sub
                c = lax.broadcasted_iota(jnp.int32, s.shape, 1)
                s = jnp.where(c <= r, s, -1e30)
            v = v_ref[0, pl.ds(k0, nk), cols]
            m_prev = m_s[hd, rows, :]
            m_new = jnp.maximum(m_prev, jnp.max(s, axis=-1, keepdims=True))
            alpha = jnp.exp2(m_prev - m_new)
            p = jnp.exp2(s - jnp.concatenate([m_new] * (nk // LANES), axis=1))
            acc_s[hd, rows, :] = (jnp.concatenate([alpha, alpha], axis=1) * acc_s[hd, rows, :]
                                  + jnp.dot(p.astype(BF16), v, preferred_element_type=F32))
            m_s[hd, rows, :] = m_new

    for hd in range(n_heads):
        scores(hd, 0, 0)
    for kt in range(n_kt - 1):
        @pl.when(kt < i)
        def _(kt=kt):
            for hd in range(n_heads):
                scores(hd, kt + 1, (kt + 1) % 2)
                softmax_pv(hd, kt * tq, kt % 2, False)
    for hd in range(n_heads):
        softmax_pv(hd, pl.multiple_of(i * tq, tq), i % 2, True)
        o_ref[0, :, hd * V_DIM:(hd + 1) * V_DIM] = (acc_s[hd, :, :V_DIM] / acc_s[hd, :, V_DIM:]).astype(o_ref.dtype)


def flash_attention(q, k, v, heads, tq):
    B, S, _ = q.shape
    hp = HEADS_PER_FLASH_STEP
    assert heads % hp == 0
    return pl.pallas_call(
        functools.partial(_flash_kernel, tq=tq, sub=SUB_FLASH, n_kt=S // tq),
        grid=(B, heads // hp, S // tq),
        in_specs=[pl.BlockSpec((1, tq, hp * HEAD_PAD), lambda b, h, i: (b, i, h)),
                  pl.BlockSpec((1, S, hp * HEAD_PAD), lambda b, h, i: (b, 0, h)),
                  pl.BlockSpec((1, S, hp * HEAD_PAD), lambda b, h, i: (b, 0, h))],
        out_specs=pl.BlockSpec((1, tq, hp * V_DIM), lambda b, h, i: (b, i, h)),
        out_shape=jax.ShapeDtypeStruct((B, S, heads * V_DIM), BF16),
        scratch_shapes=[pltpu.VMEM((hp, 2, tq, tq), F32), pltpu.VMEM((hp, tq, LANES), F32),
                        pltpu.VMEM((hp, tq, HEAD_PAD), F32)],
        compiler_params=_params(("parallel", "parallel", "arbitrary"), 56),
        name="flash_attention",
    )(q, k, v)


def _router_rows(x_ref, g_ref, w_ref, b_ref, o_ref, xt_ref, tm, chunk):
    w = w_ref[...]
    w_hi = w.astype(BF16)
    w_lo = (w - w_hi.astype(F32)).astype(BF16)
    lane_tiles = x_ref.shape[1] // LANES
    for r0 in range(0, tm, chunk):
        xn = _rms_rows(x_ref[r0:r0 + chunk, :], g_ref[...])
        x_hi = xn.astype(BF16)
        x_lo = (xn - x_hi.astype(F32)).astype(BF16)
        acc = jnp.dot(x_hi, w_hi, preferred_element_type=F32)
        acc += jnp.dot(x_lo, w_hi, preferred_element_type=F32)
        acc += jnp.dot(x_hi, w_lo, preferred_element_type=F32)
        o_ref[r0:r0 + chunk, :] = acc + b_ref[...]
        for j in range(lane_tiles):
            xt_ref[pl.ds(r0 * lane_tiles + j, chunk, stride=lane_tiles), :] = xn[:, j * LANES:(j + 1) * LANES]


def _plan_kernel(lg_ref, out_ref, tiles_ref, row_tok_ref, c_s, info_s, inv_s, *, n_tok, n_groups, epg, tm, chunk):
    n_exp = n_groups * epg
    lane = lax.broadcasted_iota(jnp.int32, (chunk, LANES), 1).astype(F32)
    rr = lax.broadcasted_iota(jnp.int32, (chunk, chunk), 0)
    cc = lax.broadcasted_iota(jnp.int32, (chunk, chunk), 1)
    tri = jnp.where(cc <= rr, 1.0, 0.0).astype(BF16)
    neg_inf = -jnp.inf

    def first_argmax(vals, vmax):
        return jnp.min(jnp.where(vals == vmax, lane, float(LANES)), axis=1, keepdims=True)

    def decide(c, carry):
        r0 = pl.multiple_of(c * chunk, chunk)
        lg = lg_ref[pl.ds(r0, chunk), :]
        gl = jnp.where(lane < n_groups, lg, neg_inf)
        gmax = jnp.max(gl, axis=1, keepdims=True)
        g_idx = first_argmax(gl, gmax)
        p_top = 1.0 / jnp.sum(jnp.exp(gl - gmax), axis=1, keepdims=True)
        lo = n_groups + g_idx * epg
        el = jnp.where(jnp.logical_and(lane >= lo, lane < lo + epg), lg, neg_inf)
        l1 = jnp.max(el, axis=1, keepdims=True)
        i1 = first_argmax(el, l1)
        el2 = jnp.where(lane == i1, neg_inf, el)
        l2 = jnp.max(el2, axis=1, keepdims=True)
        i2 = first_argmax(el2, l2)
        d = jnp.exp(l2 - l1)
        w1 = 1.0 / (1.0 + d)
        e1 = i1 - n_groups
        e2 = i2 - n_groups
        onehot = jnp.where(jnp.logical_or(lane == e1, lane == e2), 1.0, 0.0)
        cs = jnp.dot(tri, onehot.astype(BF16), preferred_element_type=F32) + carry
        c_s[pl.ds(r0, chunk), :] = cs
        info_s[pl.ds(r0, chunk), :] = jnp.where(
            lane == 0, e1, jnp.where(lane == 1, e2, jnp.where(
                lane == 2, p_top * w1, jnp.where(lane == 3, p_top * (d * w1), 0.0))))
        return cs[chunk - 1:chunk, :]

    counts = lax.fori_loop(0, n_tok // chunk, decide, jnp.zeros((1, LANES), F32))

    tiles = jnp.floor((counts + (tm - 1)) * (1.0 / tm))
    jj = lax.broadcasted_iota(jnp.int32, (LANES, LANES), 0)
    ee = lax.broadcasted_iota(jnp.int32, (LANES, LANES), 1)
    upper = jnp.where(jj <= ee, 1.0, 0.0).astype(BF16)
    tile_end = jnp.dot(jnp.broadcast_to(tiles, (8, LANES)).astype(BF16), upper,
                       preferred_element_type=F32)[0:1, :]
    tile_start = tile_end - tiles
    row_start = tile_start * tm

    inv_s[...] = jnp.zeros(inv_s.shape, F32)
    off_lane = lax.broadcasted_iota(jnp.int32, (chunk, tm), 1).astype(F32)
    tile_row = lax.broadcasted_iota(jnp.int32, (LANES, chunk), 0).astype(F32)
    tok_local = lax.broadcasted_iota(jnp.int32, (1, chunk), 1)

    def place(c, carry):
        r0 = pl.multiple_of(c * chunk, chunk)
        info = info_s[pl.ds(r0, chunk), :]
        base = row_start + c_s[pl.ds(r0, chunk), :] - 1.0
        pos = [jnp.sum(jnp.where(lane == info[:, k:k + 1], base, 0.0), axis=1, keepdims=True)
               for k in range(TOP_K_IN_GROUP)]
        packed = jnp.where(lane == 0, pos[0], jnp.where(lane == 1, pos[1], info))
        out_ref[pl.ds(r0, chunk), :] = packed
        pos_rows = jnp.transpose(packed)
        tok = r0 + tok_local
        tok_hi = lax.shift_right_logical(tok, LANES.bit_length() - 1).astype(F32)
        tok_lo = (tok & (LANES - 1)).astype(F32)
        for k in range(TOP_K_IN_GROUP):
            tile_of = jnp.floor(pos[k] * (1.0 / tm))
            onehot_off = jnp.where(off_lane == pos[k] - tile_of * tm, 1.0, 0.0).astype(BF16)
            sel = tile_row == jnp.floor(pos_rows[k:k + 1, :] * (1.0 / tm))
            ids = jnp.concatenate([jnp.where(sel, tok_hi, 0.0), jnp.where(sel, tok_lo, 0.0)], axis=0)
            inv_s[...] += jnp.dot(ids.astype(BF16), onehot_off, preferred_element_type=F32)
        return carry

    lax.fori_loop(0, n_tok // chunk, place, 0)
    row_tok_ref[...] = (inv_s[:LANES, :] * LANES + inv_s[LANES:, :]).astype(jnp.int32)

    ti = jj.astype(F32)
    lane_sq = ee.astype(F32)
    is_exp = lane_sq < n_exp
    n_used = tile_end[:, n_exp - 1:n_exp]
    ti_c = jnp.minimum(ti, n_used - 1.0)
    tile_e = jnp.sum(jnp.where(jnp.logical_and(is_exp, tile_end <= ti_c), 1.0, 0.0), axis=1, keepdims=True)
    first = jnp.sum(jnp.where(jnp.logical_and(jnp.logical_and(is_exp, tile_start == ti), tiles > 0.0),
                              1.0, 0.0), axis=1, keepdims=True)
    nonempty = jnp.logical_and(is_exp, tiles > 0.0)
    def next_nonempty(e):
        nxt = jnp.min(jnp.where(jnp.logical_and(nonempty, lane_sq > e), lane_sq, float(LANES)),
                      axis=1, keepdims=True)
        return jnp.where(nxt >= LANES, e, nxt)

    next_e = next_nonempty(tile_e)
    next2_e = next_nonempty(next_e)
    group = jnp.sum(jnp.where(jnp.logical_and(nonempty, lane_sq < tile_e), 1.0, 0.0), axis=1, keepdims=True)
    wslot = group - 2.0 * jnp.floor(group * 0.5)
    fields = (tile_e, first, n_used, next_e, wslot, next2_e)
    assert len(fields) == N_TILE_FIELDS
    packed = jnp.zeros((LANES, LANES), F32)
    for f, val in enumerate(fields):
        packed = jnp.where(lane_sq == f, val, packed)
    tiles_ref[...] = packed.astype(jnp.int32)


def routing_plan(logits, n_groups, epg, tm):
    T = logits.shape[0]
    n_exp = n_groups * epg
    n_tiles = T * TOP_K_IN_GROUP // tm + n_exp
    assert n_tiles <= LANES and T <= LANES * LANES
    out, tiles, row_tok = pl.pallas_call(
        functools.partial(_plan_kernel, n_tok=T, n_groups=n_groups, epg=epg, tm=tm, chunk=PLAN_CHUNK),
        grid=(1,),
        in_specs=[pl.BlockSpec((T, LANES), lambda i: (0, 0))],
        out_specs=(pl.BlockSpec((T, LANES), lambda i: (0, 0)),
                   pl.BlockSpec((LANES, LANES), lambda i: (0, 0)),
                   pl.BlockSpec((LANES, tm), lambda i: (0, 0))),
        out_shape=(jax.ShapeDtypeStruct((T, LANES), F32), jax.ShapeDtypeStruct((LANES, LANES), jnp.int32),
                   jax.ShapeDtypeStruct((LANES, tm), jnp.int32)),
        scratch_shapes=[pltpu.VMEM((T, LANES), F32), pltpu.VMEM((T, LANES), F32),
                        pltpu.VMEM((2 * LANES, tm), F32)],
        compiler_params=_params(("arbitrary",), 40),
        name="routing_plan",
    )(logits)
    pos = out[:, :TOP_K_IN_GROUP].astype(jnp.int32).reshape(-1)
    tile_info = tiles[:n_tiles, :N_TILE_FIELDS].T.reshape(-1)
    return out, pos, tile_info, row_tok[:n_tiles].reshape(-1), n_tiles


def _row_copy(src_hbm, tok, dst_row, sem, rows=1):
    start = tok if rows == 1 else pl.multiple_of(tok * rows, rows)
    return pltpu.make_async_copy(src_hbm.at[pl.ds(start, rows)], dst_row, sem)


def _start_row_gather(src_hbm, idx_ref, base, n, dst_row, sem, rows=1):
    def body(r, carry):
        _row_copy(src_hbm, idx_ref[base + r], dst_row(r), sem, rows).start()
        return carry
    lax.fori_loop(0, n, body, 0, unroll=8)


def _wait_row_gather(src_hbm, n, dst_all, sem):
    pltpu.make_async_copy(src_hbm.at[pl.ds(0, n)], dst_all, sem).wait()


def _gmm_kernel(info_ref, row_tok_ref, x_hbm, wg_hbm, wu_hbm, wd_hbm, o_ref,
                xbuf, sem, wg_f, wu_f, wd_f, wsem, wg_bf, wu_bf, wd_bf, *, tm, n_tiles, w_off, n_chunks, tok_rows):
    i = pl.program_id(0)
    expert = info_ref[i]
    first = info_ref[n_tiles + i]
    n_used = info_ref[2 * n_tiles]
    next_expert = info_ref[3 * n_tiles + i]
    wslot = info_ref[4 * n_tiles + i]
    next2_expert = info_ref[5 * n_tiles + i]
    slot = i % GATHER_BUFS
    ahead = GATHER_BUFS - 1

    def wait_slot(s):
        _wait_row_gather(x_hbm, tm * tok_rows, xbuf.at[s], sem.at[s])

    def weight_copies(e, s):
        copies = []
        for hbm, buf in ((wg_hbm, wg_f), (wu_hbm, wu_f), (wd_hbm, wd_f)):
            rows = hbm.shape[1] // n_chunks
            for c in range(n_chunks):
                copies.append(pltpu.make_async_copy(hbm.at[w_off + e, pl.ds(c * rows, rows)],
                                                    buf.at[s, pl.ds(c * rows, rows)], wsem.at[s]))
        return copies

    def start_weights(e, s):
        for c, cp in enumerate(weight_copies(e, s)):
            cp.start(priority=c % DMA_PRIORITIES)

    @pl.when(jnp.logical_and(i == 0, n_used > 0))
    def _():
        start_weights(expert, 0)

        @pl.when(next_expert != expert)
        def _():
            start_weights(next_expert, 1)

        for t in range(ahead):
            _start_row_gather(x_hbm, row_tok_ref, min(t, n_tiles - 1) * tm, tm,
                              lambda r, t=t: xbuf.at[t, pl.ds(pl.multiple_of(r * tok_rows, tok_rows), tok_rows)],
                              sem.at[t], tok_rows)

    @pl.when(i < n_used)
    def _():
        @pl.when(first == 1)
        def _():
            for cp in weight_copies(expert, wslot):
                cp.wait()
            for src, dst in ((wg_f, wg_bf), (wu_f, wu_bf), (wd_f, wd_bf)):
                rows = dst.shape[0] // WEIGHT_CONVERT_CHUNKS

                def convert(c, carry, src=src, dst=dst, rows=rows):
                    r0 = pl.multiple_of(c * rows, rows)
                    dst[pl.ds(r0, rows), :] = src[wslot, pl.ds(r0, rows), :].astype(BF16)
                    return carry
                lax.fori_loop(0, WEIGHT_CONVERT_CHUNKS, convert, 0)

            @pl.when(next2_expert != next_expert)
            def _():
                start_weights(next2_expert, wslot)

        wait_slot(slot)
        xn = jnp.concatenate([xbuf[slot, pl.ds(j, tm, stride=tok_rows), :].astype(BF16)
                              for j in range(tok_rows)], axis=1)
        base = jnp.minimum(i + ahead, n_tiles - 1) * tm
        nslot = (i + ahead) % GATHER_BUFS
        for r in range(tm):
            _row_copy(x_hbm, row_tok_ref[base + r], xbuf.at[nslot, pl.ds(r * tok_rows, tok_rows)],
                      sem.at[nslot], tok_rows).start(priority=r % DMA_PRIORITIES)
        hg = jnp.dot(xn, wg_bf[...], preferred_element_type=F32)
        hu = jnp.dot(xn, wu_bf[...], preferred_element_type=F32)
        hh = (hg * jax.nn.sigmoid(hg)) * hu
        o_ref[...] = jnp.dot(hh.astype(BF16), wd_bf[...], preferred_element_type=F32)

    @pl.when(i >= n_used)
    def _():
        @pl.when(jnp.logical_and(i < n_used + ahead, n_used > 0))
        def _():
            wait_slot(slot)
        o_ref[...] = jnp.zeros(o_ref.shape, o_ref.dtype)

    @pl.when(i == n_tiles - 1)
    def _():
        for d in range(1, ahead + 1):
            @pl.when(n_used >= n_tiles - ahead + d)
            def _(d=d):
                wait_slot((n_tiles - 1 + d) % GATHER_BUFS)


def grouped_expert_mlp(xt, tile_info, row_tok, w_gate, w_up, w_down, layer, n_tiles, tm):
    _, D, F = w_gate.shape
    tok_rows = D // LANES
    layer_idx, depth = layer
    off = layer_idx * (w_gate.shape[0] // depth)
    any_spec = pl.BlockSpec(memory_space=pl.ANY)
    grid_spec = pltpu.PrefetchScalarGridSpec(
        num_scalar_prefetch=2,
        grid=(n_tiles,),
        in_specs=[any_spec, any_spec, any_spec, any_spec],
        out_specs=pl.BlockSpec((tm, D), lambda i, ti, rt: (i, 0)),
        scratch_shapes=[pltpu.VMEM((GATHER_BUFS, tm * tok_rows, LANES), F32),
                        pltpu.SemaphoreType.DMA((GATHER_BUFS,)),
                        pltpu.VMEM((2, D, F), F32), pltpu.VMEM((2, D, F), F32), pltpu.VMEM((2, F, D), F32),
                        pltpu.SemaphoreType.DMA((2,)),
                        pltpu.VMEM((D, F), BF16), pltpu.VMEM((D, F), BF16), pltpu.VMEM((F, D), BF16)],
    )
    return pl.pallas_call(
        functools.partial(_gmm_kernel, tm=tm, n_tiles=n_tiles, w_off=off, n_chunks=WEIGHT_DMA_CHUNKS,
                          tok_rows=tok_rows),
        grid_spec=grid_spec,
        out_shape=jax.ShapeDtypeStruct((n_tiles * tm, D), F32),
        compiler_params=_params(("arbitrary",), 56),
        name="grouped_expert_mlp",
    )(tile_info, row_tok, xt, w_gate, w_up, w_down)


def _combine_kernel(pos_ref, x_ref, plan_ref, y_hbm, o_ref, ybuf, sem, *, tc, top_k):
    i = pl.program_id(0)
    n = pl.num_programs(0)
    slot = i % 2

    def start(tile, s):
        base = tile * (tc * top_k)
        for r in range(tc):
            for k in range(top_k):
                _row_copy(y_hbm, pos_ref[base + r * top_k + k], ybuf.at[s, k, pl.ds(r, 1)], sem.at[s]).start()

    @pl.when(i == 0)
    def _():
        start(0, 0)

    for k in range(top_k):
        _wait_row_gather(y_hbm, tc, ybuf.at[slot, k], sem.at[slot])

    @pl.when(i + 1 < n)
    def _():
        start(i + 1, 1 - slot)

    acc = x_ref[...]
    for k in range(top_k):
        acc = acc + plan_ref[:, top_k + k:top_k + k + 1] * ybuf[slot, k]
    o_ref[...] = acc


def moe_combine(x, y_sorted, plan_out, pos, tc):
    T, D = x.shape
    top_k = TOP_K_IN_GROUP
    grid_spec = pltpu.PrefetchScalarGridSpec(
        num_scalar_prefetch=1,
        grid=(T // tc,),
        in_specs=[pl.BlockSpec((tc, D), lambda i, p: (i, 0)),
                  pl.BlockSpec((tc, LANES), lambda i, p: (i, 0)),
                  pl.BlockSpec(memory_space=pl.ANY)],
        out_specs=pl.BlockSpec((tc, D), lambda i, p: (i, 0)),
        scratch_shapes=[pltpu.VMEM((2, top_k, tc, D), F32), pltpu.SemaphoreType.DMA((2,))],
    )
    return pl.pallas_call(
        functools.partial(_combine_kernel, tc=tc, top_k=top_k),
        grid_spec=grid_spec,
        out_shape=jax.ShapeDtypeStruct((T, D), F32),
        compiler_params=_params(("arbitrary",), 32),
        name="moe_combine",
    )(pos, x, plan_out, y_sorted)


def _combine_mm_kernel(pos_ref, x_ref, plan_ref, y_hbm, g_ref, w_hbm, xnew_ref, o_ref,
                       ybuf, ysem, xn_ref, w_bf, stage, wsem, *, layer, tc, chunk, tn, top_k):
    i = pl.program_id(0)
    n = pl.num_programs(0)
    slot = i % 2

    def start(tile, s):
        base = tile * (tc * top_k)
        for r in range(tc):
            for k in range(top_k):
                _row_copy(y_hbm, pos_ref[base + r * top_k + k], ybuf.at[s, k, pl.ds(r, 1)], ysem.at[s]).start()

    def wait(s):
        for k in range(top_k):
            _wait_row_gather(y_hbm, tc, ybuf.at[s, k], ysem.at[s])

    @pl.when(i == 0)
    def _():
        start(0, 0)
        _stage_weight_bf16(w_hbm, layer, w_bf, stage, wsem, tn)

    wait(slot)
    for r0 in range(0, tc, chunk):
        rows = slice(r0, r0 + chunk)
        acc = x_ref[rows, :]
        for k in range(top_k):
            acc = acc + plan_ref[rows, top_k + k:top_k + k + 1] * ybuf[slot, k, rows, :]
        xnew_ref[rows, :] = acc
        xn_ref[rows, :] = _rms_rows(acc, g_ref[...]).astype(BF16)
    start(jnp.minimum(i + 1, n - 1), 1 - slot)
    for j in range(o_ref.shape[1] // tn):
        o_ref[:, j * tn:(j + 1) * tn] = jnp.dot(xn_ref[...], w_bf[:, j * tn:(j + 1) * tn],
                                                preferred_element_type=F32).astype(o_ref.dtype)

    @pl.when(i == n - 1)
    def _():
        wait(1 - slot)


def combine_norm_matmul(x, y_sorted, plan_out, pos, g, w, layer, out_dtype, tc):
    T, D = x.shape
    N = w.shape[2]
    top_k = TOP_K_IN_GROUP
    tn = _col_tile(N)
    grid_spec = pltpu.PrefetchScalarGridSpec(
        num_scalar_prefetch=1,
        grid=(T // tc,),
        in_specs=[pl.BlockSpec((tc, D), lambda i, p: (i, 0)),
                  pl.BlockSpec((tc, LANES), lambda i, p: (i, 0)),
                  pl.BlockSpec(memory_space=pl.ANY),
                  pl.BlockSpec((1, D), lambda i, p: (0, 0)),
                  pl.BlockSpec(memory_space=pl.ANY)],
        out_specs=(pl.BlockSpec((tc, D), lambda i, p: (i, 0)),
                   pl.BlockSpec((tc, N), lambda i, p: (i, 0))),
        scratch_shapes=[pltpu.VMEM((2, top_k, tc, D), F32), pltpu.SemaphoreType.DMA((2,)),
                        pltpu.VMEM((tc, D), BF16), pltpu.VMEM((D, N), BF16), pltpu.VMEM((2, D, tn), F32),
                        pltpu.SemaphoreType.DMA((2,))],
    )
    return pl.pallas_call(
        functools.partial(_combine_mm_kernel, layer=layer, tc=tc, chunk=min(tc, 128), tn=tn, top_k=top_k),
        grid_spec=grid_spec,
        out_shape=(jax.ShapeDtypeStruct((T, D), F32), jax.ShapeDtypeStruct((T, N), out_dtype)),
        compiler_params=_params(("arbitrary",), 56),
        name="combine_norm_matmul",
    )(pos, x, plan_out, y_sorted, g.reshape(1, D), w)


def hier_moe_experts(logits, xt, n_groups, epg, w_gate, w_up, w_down, layer):
    plan_out, pos, tile_info, row_tok, n_tiles = routing_plan(logits, n_groups, epg, TM_EXPERT)
    y_sorted = grouped_expert_mlp(xt, tile_info, row_tok, w_gate, w_up, w_down, layer, n_tiles, TM_EXPERT)
    return y_sorted, plan_out, pos


def _pad_head_cols(w, heads, real):
    r = w.shape[0]
    return jnp.pad(w.reshape(r, heads, real), ((0, 0), (0, 0), (0, HEAD_PAD - real))).reshape(r, heads * HEAD_PAD)


def _rope_tables(positions):
    half = QK_ROPE // 2
    inv_freq = ROPE_THETA ** (-jnp.arange(half, dtype=F32) / half)
    ang = positions.astype(F32).reshape(-1, 1) * inv_freq[None, :]
    cos, sin = jnp.cos(ang), jnp.sin(ang)
    zeros = jnp.zeros((ang.shape[0], LANES - QK_ROPE), F32)
    return (jnp.concatenate([cos, cos, zeros], axis=1),
            jnp.concatenate([-sin, sin, zeros], axis=1))


def kernel(x, mem, positions, norm_mix_g, norm_ffn_g, w_o, mem_norm_g, w_mem_kv, mem_q_norm_g, mem_k_norm_g, w_in_a, conv_w, conv_b, w_lru_r, b_lru_r, w_lru_i, b_lru_i, lru_lambda, kv_in_norm_g, w_dkv, kv_latent_norm_g, w_uk, w_uv, k_head_norm_g, w_in_b, q_latent_norm_g, w_uq, q_head_norm_g, w_router_grp, b_router_grp, w_router_exp, b_router_exp, w_exp_gate, w_exp_up, w_exp_down):
    B, S, D = x.shape
    T = B * S
    M = mem.shape[1]
    depth = norm_mix_g.shape[0]
    n_a = w_in_a.shape[0]
    lru_w = lru_lambda.shape[1]
    mem_w = w_mem_kv.shape[2] // 2
    heads = w_uv.shape[1] // V_DIM
    kv_rank = kv_latent_norm_g.shape[0]
    q_rank = q_latent_norm_g.shape[1]
    n_groups, epg = w_exp_gate.shape[1], w_exp_gate.shape[2]
    n_exp = n_groups * epg
    d_exp = w_exp_gate.shape[-1]

    xr = x.reshape(T, D)
    mem2 = mem.reshape(B * M, D)
    cos_t, sin_t = _rope_tables(positions)
    wg_all = w_exp_gate.reshape(depth * n_exp, D, d_exp)
    wu_all = w_exp_up.reshape(depth * n_exp, D, d_exp)
    wd_all = w_exp_down.reshape(depth * n_exp, d_exp, D)
    pad_r = LANES - n_groups - n_exp
    w_router = jnp.concatenate([w_router_grp, w_router_exp, jnp.zeros((depth, D, pad_r), F32)], axis=2)
    b_router = jnp.concatenate([b_router_grp, b_router_exp, jnp.zeros((depth, pad_r), F32)], axis=1)

    def shared_kv(x_res):
        lat_cols = kv_rank + LANES
        w_dkv_pad = jnp.pad(w_dkv, ((0, 0), (0, lat_cols - w_dkv.shape[1])))[None]
        ckv = norm_matmul(x_res, kv_in_norm_g, w_dkv_pad, 0, F32, tm=TM_PROJ)
        eye = jnp.pad(jnp.eye(QK_ROPE, dtype=F32), ((0, LANES - QK_ROPE), (0, 0)))
        eye3 = jnp.broadcast_to(eye[:, None, :], (LANES, heads, QK_ROPE))
        rope_rows = jnp.pad(eye3, ((0, 0), (0, 0), (QK_NOPE, HEAD_PAD - QK_DIM))).reshape(LANES, heads * HEAD_PAD)
        wk = jnp.concatenate([_pad_head_cols(w_uk, heads, QK_NOPE), rope_rows], axis=0).astype(BF16)
        kg = jnp.pad(k_head_norm_g, (0, HEAD_PAD - QK_DIM)).reshape(1, HEAD_PAD)
        k_flat, v_flat = latent_heads(ckv, lat_cols, kv_rank, kv_latent_norm_g, wk, kg, cos_t, sin_t,
                                      heads, tm=TM_HEADS, w_v=w_uv.astype(BF16))
        return k_flat.reshape(B, S, -1), v_flat.reshape(B, S, -1)

    def in_proj(x_res, pending, g, w, idx):
        if pending is None:
            return x_res, norm_matmul(x_res, g, w, idx, BF16, tm=TM_PROJ)
        return combine_norm_matmul(x_res, *pending, g, w, idx, BF16, TC_COMBINE)

    k_sh = v_sh = None
    pending = None
    for l in range(depth):
        kv_mem = norm_matmul(mem2, mem_norm_g[l], w_mem_kv, l, BF16, tm=TM_PROJ).reshape(B, M, 2 * mem_w)
        if l < n_a:
            xr, proj = in_proj(xr, pending, norm_mix_g[l], w_in_a, l)
            y_mix = rglru(proj.reshape(B, S, -1), conv_w[l], conv_b[l], w_lru_r[l], b_lru_r[l],
                          w_lru_i[l], b_lru_i[l], lru_lambda[l], nb=LRU_BLOCKS_PER_STEP,
                          tc=TC_LRU).reshape(T, lru_w)
            q_col = 2 * lru_w // mem_w
        else:
            j = l - n_a
            xr, proj = in_proj(xr, pending, norm_mix_g[l], w_in_b, j)
            if l == n_a:
                k_sh, v_sh = shared_kv(xr)
            wq = _pad_head_cols(w_uq[j], heads, QK_DIM).astype(BF16)
            qg = jnp.pad(q_head_norm_g[j] * (QK_DIM ** -0.5 * math.log2(math.e)),
                         (0, HEAD_PAD - QK_DIM)).reshape(1, HEAD_PAD)
            q = latent_heads(proj, q_rank, q_rank, q_latent_norm_g[j], wq, qg, cos_t, sin_t, heads, tm=TM_HEADS)
            y_mix = flash_attention(q.reshape(B, S, -1), k_sh, v_sh, heads, tq=TQ_FLASH).reshape(T, heads * V_DIM)
            q_col = q_rank // mem_w
        y_mem = mem_attention(proj, q_col, kv_mem, mem_q_norm_g[l], mem_k_norm_g[l], B, ts=TS_MEM)
        xr, logits, xt = out_proj(y_mix, y_mem, w_o, l, xr, norm_ffn_g[l], w_router[l],
                                  b_router[l].reshape(1, LANES), tm=TM_PROJ)
        pending = hier_moe_experts(logits, xt, n_groups, epg, wg_all, wu_all, wd_all, (l, depth))
    xr = moe_combine(xr, *pending, TC_COMBINE)
    return xr.reshape(B, S, D)
```

```python
import functools
import math

import jax
import jax.numpy as jnp
from jax import lax
from jax.experimental import pallas as pl
from jax.experimental.pallas import tpu as pltpu

F32 = jnp.float32
BF16 = jnp.bfloat16
EPS = 1e-6

MEM_HEADS = 4
CONV_WIDTH = 4
LRU_C = 8.0
QK_NOPE = 128
QK_ROPE = 64
QK_DIM = QK_NOPE + QK_ROPE
V_DIM = 128
ROPE_THETA = 10000.0
TOP_K_IN_GROUP = 2

LANES = 128
HEAD_PAD = 2 * LANES
MIB = 1024 * 1024
DMA_PRIORITIES = 2

TM_PROJ = 512
TN_PROJ = 512
TS_MEM = 512
TQ_FLASH = 512
SUB_FLASH = 256
HEADS_PER_FLASH_STEP = 6
TM_HEADS = 512
TM_EXPERT = 256
WEIGHT_DMA_CHUNKS = 16
WEIGHT_CONVERT_CHUNKS = 4
GATHER_BUFS = 3
TC_COMBINE = 256
LRU_BLOCKS_PER_STEP = 6
TC_LRU = 256
PLAN_CHUNK = 256
N_TILE_FIELDS = 6


def _params(sem, vmem_mib):
    return pltpu.CompilerParams(dimension_semantics=sem, vmem_limit_bytes=vmem_mib * MIB)


def _sigmoid(z):
    return 0.5 * jnp.tanh(0.5 * z) + 0.5


def _rms_rows(x, g):
    x = x.astype(F32)
    ms = jnp.mean(x * x, axis=-1, keepdims=True)
    return x * lax.rsqrt(ms + EPS) * g


def _col_tile(n):
    return TN_PROJ if n % TN_PROJ == 0 else n


def _stage_weight_bf16(w_hbm, layer, w_bf, stage, sem, tn):
    n = w_bf.shape[1] // tn

    def copy(j):
        return pltpu.make_async_copy(w_hbm.at[layer, :, pl.ds(j * tn, tn)], stage.at[j % 2], sem.at[j % 2])

    copy(0).start()
    for j in range(n):
        if j + 1 < n:
            copy(j + 1).start()
        copy(j).wait()
        w_bf[:, j * tn:(j + 1) * tn] = stage[j % 2].astype(BF16)


def _norm_mm_kernel(x_ref, g_ref, w_hbm, o_ref, xn_ref, w_bf, stage, sem, *, layer, tm, chunk, tn):
    @pl.when(pl.program_id(0) == 0)
    def _():
        _stage_weight_bf16(w_hbm, layer, w_bf, stage, sem, tn)

    for r0 in range(0, tm, chunk):
        xn_ref[r0:r0 + chunk, :] = _rms_rows(x_ref[r0:r0 + chunk, :], g_ref[...]).astype(BF16)

    for j in range(o_ref.shape[1] // tn):
        o_ref[:, j * tn:(j + 1) * tn] = jnp.dot(xn_ref[...], w_bf[:, j * tn:(j + 1) * tn],
                                                preferred_element_type=F32).astype(o_ref.dtype)


def norm_matmul(x, g, w, layer, out_dtype, tm):
    T, K = x.shape
    N = w.shape[2]
    tm = min(tm, T)
    chunk = min(tm, 256)
    tn = _col_tile(N)
    return pl.pallas_call(
        functools.partial(_norm_mm_kernel, layer=layer, tm=tm, chunk=chunk, tn=tn),
        grid=(T // tm,),
        in_specs=[pl.BlockSpec((tm, K), lambda i: (i, 0)),
                  pl.BlockSpec((1, K), lambda i: (0, 0)),
                  pl.BlockSpec(memory_space=pl.ANY)],
        out_specs=pl.BlockSpec((tm, N), lambda i: (i, 0)),
        out_shape=jax.ShapeDtypeStruct((T, N), out_dtype),
        scratch_shapes=[pltpu.VMEM((tm, K), BF16), pltpu.VMEM((K, N), BF16), pltpu.VMEM((2, K, tn), F32),
                        pltpu.SemaphoreType.DMA((2,))],
        compiler_params=_params(("arbitrary",), 56),
        name="norm_matmul",
    )(x, g.reshape(1, K), w)


def _out_proj_kernel(ya_ref, yb_ref, w_hbm, x_ref, g_ref, wr_ref, br_ref, o_ref, lg_ref, xt_ref,
                     w_bf, stage, sem, *, layer, tm, tn):
    @pl.when(pl.program_id(0) == 0)
    def _():
        _stage_weight_bf16(w_hbm, layer, w_bf, stage, sem, tn)

    wa_rows = ya_ref.shape[1]

    for j in range(o_ref.shape[1] // tn):
        cols = slice(j * tn, (j + 1) * tn)
        acc = jnp.dot(ya_ref[...], w_bf[:wa_rows, cols], preferred_element_type=F32)
        acc += jnp.dot(yb_ref[...], w_bf[wa_rows:, cols], preferred_element_type=F32)
        o_ref[:, cols] = x_ref[:, cols] + acc
    _router_rows(o_ref, g_ref, wr_ref, br_ref, lg_ref, xt_ref, tm, min(tm, 256))


def out_proj(y_mix, y_mem, w_o, layer, x, g_ffn, w_router, b_router, tm):
    T, Wa = y_mix.shape
    Wb = y_mem.shape[1]
    _, K, N = w_o.shape
    assert Wa + Wb == K
    tn = _col_tile(N)
    lane_tiles = N // LANES
    return pl.pallas_call(
        functools.partial(_out_proj_kernel, layer=layer, tm=tm, tn=tn),
        grid=(T // tm,),
        in_specs=[pl.BlockSpec((tm, Wa), lambda i: (i, 0)),
                  pl.BlockSpec((tm, Wb), lambda i: (i, 0)),
                  pl.BlockSpec(memory_space=pl.ANY),
                  pl.BlockSpec((tm, N), lambda i: (i, 0)),
                  pl.BlockSpec((1, N), lambda i: (0, 0)),
                  pl.BlockSpec((N, LANES), lambda i: (0, 0)),
                  pl.BlockSpec((1, LANES), lambda i: (0, 0))],
        out_specs=(pl.BlockSpec((tm, N), lambda i: (i, 0)),
                   pl.BlockSpec((tm, LANES), lambda i: (i, 0)),
                   pl.BlockSpec((tm * lane_tiles, LANES), lambda i: (i, 0))),
        out_shape=(jax.ShapeDtypeStruct((T, N), F32),
                   jax.ShapeDtypeStruct((T, LANES), F32),
                   jax.ShapeDtypeStruct((T * lane_tiles, LANES), F32)),
        scratch_shapes=[pltpu.VMEM((K, N), BF16), pltpu.VMEM((2, K, tn), F32), pltpu.SemaphoreType.DMA((2,))],
        compiler_params=_params(("arbitrary",), 56),
        name="out_proj",
    )(y_mix, y_mem, w_o, x, g_ffn.reshape(1, N), w_router, b_router)


def _mem_attn_kernel(q_ref, kv_ref, qg_ref, kg_ref, o_ref, *, heads, hd):
    width = heads * hd
    scale = hd ** -0.5
    for h in range(heads):
        qn = _rms_rows(q_ref[:, h * hd:(h + 1) * hd], qg_ref[...]) * scale
        kn = _rms_rows(kv_ref[0, :, h * hd:(h + 1) * hd], kg_ref[...])
        v = kv_ref[0, :, width + h * hd:width + (h + 1) * hd]
        s = lax.dot_general(qn.astype(BF16), kn.astype(BF16), (((1,), (1,)), ((), ())),
                            preferred_element_type=F32)
        m = jnp.max(s, axis=-1, keepdims=True)
        p = jnp.exp(s - m)
        l = jnp.sum(p, axis=-1, keepdims=True)
        o = jnp.dot(p.astype(BF16), v, preferred_element_type=F32) / l
        o_ref[:, h * hd:(h + 1) * hd] = o.astype(o_ref.dtype)


def mem_attention(proj, q_col_block, kv, q_g, k_g, batch, ts):
    T = proj.shape[0]
    _, M, two_w = kv.shape
    width = two_w // 2
    hd = width // MEM_HEADS
    per_b = T // batch // ts
    return pl.pallas_call(
        functools.partial(_mem_attn_kernel, heads=MEM_HEADS, hd=hd),
        grid=(batch, per_b),
        in_specs=[pl.BlockSpec((ts, width), lambda b, i: (b * per_b + i, q_col_block)),
                  pl.BlockSpec((1, M, two_w), lambda b, i: (b, 0, 0)),
                  pl.BlockSpec((1, hd), lambda b, i: (0, 0)),
                  pl.BlockSpec((1, hd), lambda b, i: (0, 0))],
        out_specs=pl.BlockSpec((ts, width), lambda b, i: (b * per_b + i, 0)),
        out_shape=jax.ShapeDtypeStruct((T, width), BF16),
        compiler_params=_params(("parallel", "arbitrary"), 32),
        name="mem_attention",
    )(proj, kv, q_g.reshape(1, hd), k_g.reshape(1, hd))


def _gelu_tanh(x):
    return 0.5 * x * (1.0 + jnp.tanh(0.7978845608028654 * (x + 0.044715 * x * x * x)))


def _rglru_kernel(u_ref, gb_ref, cw_ref, cb_ref, wr_ref, wi_ref, br_ref, bi_ref, lam_ref,
                  o_ref, a_s, b_s, *, seq, nb, blk, tc):
    wb = nb * blk
    neg_lam = -lam_ref[...]
    softplus = jnp.maximum(neg_lam, 0.0) + jnp.log1p(jnp.exp(-jnp.abs(neg_lam)))

    def gates(c, carry):
        t0 = pl.multiple_of(c * tc, tc)
        cur = u_ref[0, pl.ds(t0, tc), :].astype(F32)
        p0 = pl.multiple_of(jnp.maximum(t0 - 16, 0), 16)
        prev = u_ref[0, pl.ds(p0, 16), :].astype(F32)
        prev = jnp.where(c > 0, prev, 0.0)
        full = jnp.concatenate([prev, cur], axis=0)
        y = cb_ref[...] + cw_ref[0:1, :] * cur
        for k in range(1, CONV_WIDTH):
            y = y + cw_ref[k:k + 1, :] * full[16 - k:16 - k + tc, :]
        r_parts, i_parts = [], []
        for n in range(nb):
            yb = y[:, n * blk:(n + 1) * blk].astype(BF16)
            r_parts.append(jnp.dot(yb, wr_ref[n].astype(BF16), preferred_element_type=F32))
            i_parts.append(jnp.dot(yb, wi_ref[n].astype(BF16), preferred_element_type=F32))
        r = _sigmoid(jnp.concatenate(r_parts, axis=1) + br_ref[...])
        ig = _sigmoid(jnp.concatenate(i_parts, axis=1) + bi_ref[...])
        log_a = (-LRU_C) * r * softplus
        a = jnp.exp(log_a)
        a_s[pl.ds(t0, tc), :] = a
        b_s[pl.ds(t0, tc), :] = jnp.sqrt(-jnp.tanh(log_a) * (a * a + 1.0)) * (ig * y)
        return carry

    lax.fori_loop(0, seq // tc, gates, 0)

    row = lax.broadcasted_iota(jnp.int32, (8, wb), 0)

    def scan(c, h):
        t0 = pl.multiple_of(c * 8, 8)
        a = a_s[pl.ds(t0, 8), :]
        b = b_s[pl.ds(t0, 8), :]
        for s in (1, 2, 4):
            a_sh = pltpu.roll(a, s, axis=0)
            b_sh = pltpu.roll(b, s, axis=0)
            keep = row >= s
            b = jnp.where(keep, a * b_sh + b, b)
            a = jnp.where(keep, a * a_sh, a)
        hc = a * h + b
        b_s[pl.ds(t0, 8), :] = hc
        return hc[7:8, :]

    lax.fori_loop(0, seq // 8, scan, jnp.zeros((1, wb), F32), unroll=4)

    def gate_out(c, carry):
        t0 = pl.multiple_of(c * tc, tc)
        g = gb_ref[0, pl.ds(t0, tc), :].astype(F32)
        o_ref[0, pl.ds(t0, tc), :] = (_gelu_tanh(g) * b_s[pl.ds(t0, tc), :]).astype(o_ref.dtype)
        return carry

    lax.fori_loop(0, seq // tc, gate_out, 0)


def rglru(proj, conv_w, conv_b, w_r, b_r, w_i, b_i, lam, nb, tc):
    B, S, _ = proj.shape
    W = lam.shape[0]
    n_blocks, blk, _ = w_r.shape
    wb = nb * blk
    ncb = W // wb
    vec = lambda: pl.BlockSpec((1, wb), lambda b, j: (0, j))
    return pl.pallas_call(
        functools.partial(_rglru_kernel, seq=S, nb=nb, blk=blk, tc=tc),
        grid=(B, ncb),
        in_specs=[pl.BlockSpec((1, S, wb), lambda b, j: (b, 0, j)),
                  pl.BlockSpec((1, S, wb), lambda b, j: (b, 0, ncb + j)),
                  pl.BlockSpec((CONV_WIDTH, wb), lambda b, j: (0, j)),
                  vec(),
                  pl.BlockSpec((nb, blk, blk), lambda b, j: (j, 0, 0)),
                  pl.BlockSpec((nb, blk, blk), lambda b, j: (j, 0, 0)),
                  vec(), vec(), vec()],
        out_specs=pl.BlockSpec((1, S, wb), lambda b, j: (b, 0, j)),
        out_shape=jax.ShapeDtypeStruct((B, S, W), BF16),
        scratch_shapes=[pltpu.VMEM((S, wb), F32), pltpu.VMEM((S, wb), F32)],
        compiler_params=_params(("parallel", "arbitrary"), 48),
        name="rglru",
    )(proj, proj, conv_w, conv_b.reshape(1, W), w_r, w_i, b_r.reshape(1, W), b_i.reshape(1, W),
      lam.reshape(1, W))


def _heads_kernel(*refs, rank, heads, with_v):
    if with_v:
        lat_ref, gl_ref, w_ref, hg_ref, cos_ref, sin_ref, wv_ref, o_ref, v_ref = refs
    else:
        lat_ref, gl_ref, w_ref, hg_ref, cos_ref, sin_ref, o_ref = refs
    lat = lat_ref[...].astype(F32)
    cn = _rms_rows(lat[:, :rank], gl_ref[...])
    full = cn if lat.shape[1] == rank else jnp.concatenate([cn, lat[:, rank:]], axis=1)
    full = full.astype(BF16)
    lane = lax.broadcasted_iota(jnp.int32, cos_ref.shape, 1)
    half = QK_ROPE // 2
    for h in range(heads):
        t = jnp.dot(full, w_ref[:, h * HEAD_PAD:(h + 1) * HEAD_PAD], preferred_element_type=F32)
        ss = jnp.sum(t * t, axis=-1, keepdims=True) * (1.0 / QK_DIM)
        tn = t * lax.rsqrt(ss + EPS) * hg_ref[...]
        rp = tn[:, QK_NOPE:]
        swapped = jnp.where(lane < half, pltpu.roll(rp, LANES - half, axis=1),
                            pltpu.roll(rp, half, axis=1))
        rot = rp * cos_ref[...] + swapped * sin_ref[...]
        o_ref[:, h * HEAD_PAD:h * HEAD_PAD + QK_NOPE] = tn[:, :QK_NOPE].astype(o_ref.dtype)
        o_ref[:, h * HEAD_PAD + QK_NOPE:(h + 1) * HEAD_PAD] = rot.astype(o_ref.dtype)
    if with_v:
        v = jnp.dot(cn.astype(BF16), wv_ref[...], preferred_element_type=F32).astype(v_ref.dtype)
        ones = jnp.ones((v.shape[0], HEAD_PAD - V_DIM), v_ref.dtype)
        for h in range(heads):
            v_ref[:, h * HEAD_PAD:h * HEAD_PAD + V_DIM] = v[:, h * V_DIM:(h + 1) * V_DIM]
            v_ref[:, h * HEAD_PAD + V_DIM:(h + 1) * HEAD_PAD] = ones


def latent_heads(lat, lat_cols, rank, g_lat, w_pad, head_gain, cos_t, sin_t, heads, tm, w_v=None):
    T = lat.shape[0]
    n_out = heads * HEAD_PAD
    in_specs = [pl.BlockSpec((tm, lat_cols), lambda i: (i, 0)),
                pl.BlockSpec((1, rank), lambda i: (0, 0)),
                pl.BlockSpec((lat_cols, n_out), lambda i: (0, 0)),
                pl.BlockSpec((1, HEAD_PAD), lambda i: (0, 0)),
                pl.BlockSpec((tm, LANES), lambda i: (i, 0)),
                pl.BlockSpec((tm, LANES), lambda i: (i, 0))]
    args = [lat, g_lat.reshape(1, rank), w_pad, head_gain, cos_t, sin_t]
    out_shape = jax.ShapeDtypeStruct((T, n_out), BF16)
    out_specs = pl.BlockSpec((tm, n_out), lambda i: (i, 0))
    if w_v is not None:
        in_specs.append(pl.BlockSpec(w_v.shape, lambda i: (0, 0)))
        args.append(w_v)
        out_shape = (out_shape, jax.ShapeDtypeStruct((T, n_out), BF16))
        out_specs = (out_specs, pl.BlockSpec((tm, n_out), lambda i: (i, 0)))
    return pl.pallas_call(
        functools.partial(_heads_kernel, rank=rank, heads=heads, with_v=w_v is not None),
        grid=(T // tm,),
        in_specs=in_specs, out_specs=out_specs, out_shape=out_shape,
        compiler_params=_params(("parallel",), 48),
        name="latent_heads_kv" if w_v is not None else "latent_heads_q",
    )(*args)


def _flash_kernel(q_ref, k_ref, v_ref, o_ref, s_s, m_s, acc_s, *, tq, sub, n_kt):
    i = pl.program_id(2)
    m_s[...] = jnp.full(m_s.shape, -1e30, F32)
    acc_s[...] = jnp.zeros(acc_s.shape, F32)
    n_sub = tq // sub
    n_heads = m_s.shape[0]

    def scores(hd, kt, slot):
        cols = slice(hd * HEAD_PAD, (hd + 1) * HEAD_PAD)
        k = k_ref[0, pl.ds(kt * tq, tq), cols]
        for h in range(n_sub):
            rows = slice(h * sub, (h + 1) * sub)
            s_s[hd, slot, rows, :] = lax.dot_general(q_ref[0, rows, cols], k, (((1,), (1,)), ((), ())),
                                                     preferred_element_type=F32)

    def softmax_pv(hd, k0, slot, masked):
        cols = slice(hd * HEAD_PAD, (hd + 1) * HEAD_PAD)
        for h in range(n_sub):
            rows = slice(h * sub, (h + 1) * sub)
            nk = (h + 1) * sub if masked else tq
            s = s_s[hd, slot, rows, :nk]
            if masked:
                r = lax.broadcasted_iota(jnp.int32, s.shape, 0) + h * sub
                c = lax.broadcasted_iota(jnp.int32, s.shape, 1)
                s = jnp.where(c <= r, s, -1e30)
            v = v_ref[0, pl.ds(k0, nk), cols]
            m_prev = m_s[hd, rows, :]
            m_new = jnp.maximum(m_prev, jnp.max(s, axis=-1, keepdims=True))
            alpha = jnp.exp2(m_prev - m_new)
            p = jnp.exp2(s - jnp.concatenate([m_new] * (nk // LANES), axis=1))
            acc_s[hd, rows, :] = (jnp.concatenate([alpha, alpha], axis=1) * acc_s[hd, rows, :]
                                  + jnp.dot(p.astype(BF16), v, preferred_element_type=F32))
            m_s[hd, rows, :] = m_new

    for hd in range(n_heads):
        scores(hd, 0, 0)
    for kt in range(n_kt - 1):
        @pl.when(kt < i)
        def _(kt=kt):
            for hd in range(n_heads):
                scores(hd, kt + 1, (kt + 1) % 2)
                softmax_pv(hd, kt * tq, kt % 2, False)
    for hd in range(n_heads):
        softmax_pv(hd, pl.multiple_of(i * tq, tq), i % 2, True)
        o_ref[0, :, hd * V_DIM:(hd + 1) * V_DIM] = (acc_s[hd, :, :V_DIM] / acc_s[hd, :, V_DIM:]).astype(o_ref.dtype)


def flash_attention(q, k, v, heads, tq):
    B, S, _ = q.shape
    hp = HEADS_PER_FLASH_STEP
    assert heads % hp == 0
    return pl.pallas_call(
        functools.partial(_flash_kernel, tq=tq, sub=SUB_FLASH, n_kt=S // tq),
        grid=(B, heads // hp, S // tq),
        in_specs=[pl.BlockSpec((1, tq, hp * HEAD_PAD), lambda b, h, i: (b, i, h)),
                  pl.BlockSpec((1, S, hp * HEAD_PAD), lambda b, h, i: (b, 0, h)),
                  pl.BlockSpec((1, S, hp * HEAD_PAD), lambda b, h, i: (b, 0, h))],
        out_specs=pl.BlockSpec((1, tq, hp * V_DIM), lambda b, h, i: (b, i, h)),
        out_shape=jax.ShapeDtypeStruct((B, S, heads * V_DIM), BF16),
        scratch_shapes=[pltpu.VMEM((hp, 2, tq, tq), F32), pltpu.VMEM((hp, tq, LANES), F32),
                        pltpu.VMEM((hp, tq, HEAD_PAD), F32)],
        compiler_params=_params(("parallel", "parallel", "arbitrary"), 56),
        name="flash_attention",
    )(q, k, v)


def _router_rows(x_ref, g_ref, w_ref, b_ref, o_ref, xt_ref, tm, chunk):
    w = w_ref[...]
    w_hi = w.astype(BF16)
    w_lo = (w - w_hi.astype(F32)).astype(BF16)
    lane_tiles = x_ref.shape[1] // LANES
    for r0 in range(0, tm, chunk):
        xn = _rms_rows(x_ref[r0:r0 + chunk, :], g_ref[...])
        x_hi = xn.astype(BF16)
        x_lo = (xn - x_hi.astype(F32)).astype(BF16)
        acc = jnp.dot(x_hi, w_hi, preferred_element_type=F32)
        acc += jnp.dot(x_lo, w_hi, preferred_element_type=F32)
        acc += jnp.dot(x_hi, w_lo, preferred_element_type=F32)
        o_ref[r0:r0 + chunk, :] = acc + b_ref[...]
        for j in range(lane_tiles):
            xt_ref[pl.ds(r0 * lane_tiles + j, chunk, stride=lane_tiles), :] = xn[:, j * LANES:(j + 1) * LANES]


def _plan_kernel(lg_ref, out_ref, tiles_ref, row_tok_ref, c_s, info_s, inv_s, *, n_tok, n_groups, epg, tm, chunk):
    n_exp = n_groups * epg
    lane = lax.broadcasted_iota(jnp.int32, (chunk, LANES), 1).astype(F32)
    rr = lax.broadcasted_iota(jnp.int32, (chunk, chunk), 0)
    cc = lax.broadcasted_iota(jnp.int32, (chunk, chunk), 1)
    tri = jnp.where(cc <= rr, 1.0, 0.0).astype(BF16)
    neg_inf = -jnp.inf

    def first_argmax(vals, vmax):
        return jnp.min(jnp.where(vals == vmax, lane, float(LANES)), axis=1, keepdims=True)

    def decide(c, carry):
        r0 = pl.multiple_of(c * chunk, chunk)
        lg = lg_ref[pl.ds(r0, chunk), :]
        gl = jnp.where(lane < n_groups, lg, neg_inf)
        gmax = jnp.max(gl, axis=1, keepdims=True)
        g_idx = first_argmax(gl, gmax)
        p_top = 1.0 / jnp.sum(jnp.exp(gl - gmax), axis=1, keepdims=True)
        lo = n_groups + g_idx * epg
        el = jnp.where(jnp.logical_and(lane >= lo, lane < lo + epg), lg, neg_inf)
        l1 = jnp.max(el, axis=1, keepdims=True)
        i1 = first_argmax(el, l1)
        el2 = jnp.where(lane == i1, neg_inf, el)
        l2 = jnp.max(el2, axis=1, keepdims=True)
        i2 = first_argmax(el2, l2)
        d = jnp.exp(l2 - l1)
        w1 = 1.0 / (1.0 + d)
        e1 = i1 - n_groups
        e2 = i2 - n_groups
        onehot = jnp.where(jnp.logical_or(lane == e1, lane == e2), 1.0, 0.0)
        cs = jnp.dot(tri, onehot.astype(BF16), preferred_element_type=F32) + carry
        c_s[pl.ds(r0, chunk), :] = cs
        info_s[pl.ds(r0, chunk), :] = jnp.where(
            lane == 0, e1, jnp.where(lane == 1, e2, jnp.where(
                lane == 2, p_top * w1, jnp.where(lane == 3, p_top * (d * w1), 0.0))))
        return cs[chunk - 1:chunk, :]

    counts = lax.fori_loop(0, n_tok // chunk, decide, jnp.zeros((1, LANES), F32), unroll=2)

    tiles = jnp.floor((counts + (tm - 1)) * (1.0 / tm))
    jj = lax.broadcasted_iota(jnp.int32, (LANES, LANES), 0)
    ee = lax.broadcasted_iota(jnp.int32, (LANES, LANES), 1)
    upper = jnp.where(jj <= ee, 1.0, 0.0).astype(BF16)
    tile_end = jnp.dot(jnp.broadcast_to(tiles, (8, LANES)).astype(BF16), upper,
                       preferred_element_type=F32)[0:1, :]
    tile_start = tile_end - tiles
    row_start = tile_start * tm

    inv_s[...] = jnp.zeros(inv_s.shape, F32)
    off_lane = lax.broadcasted_iota(jnp.int32, (chunk, tm), 1).astype(F32)
    tile_row = lax.broadcasted_iota(jnp.int32, (LANES, chunk), 0).astype(F32)
    tok_local = lax.broadcasted_iota(jnp.int32, (1, chunk), 1)

    def place(c, carry):
        r0 = pl.multiple_of(c * chunk, chunk)
        info = info_s[pl.ds(r0, chunk), :]
        base = row_start + c_s[pl.ds(r0, chunk), :] - 1.0
        pos = [jnp.sum(jnp.where(lane == info[:, k:k + 1], base, 0.0), axis=1, keepdims=True)
               for k in range(TOP_K_IN_GROUP)]
        packed = jnp.where(lane == 0, pos[0], jnp.where(lane == 1, pos[1], info))
        out_ref[pl.ds(r0, chunk), :] = packed
        pos_rows = jnp.transpose(packed)
        tok = r0 + tok_local
        tok_hi = lax.shift_right_logical(tok, LANES.bit_length() - 1).astype(F32)
        tok_lo = (tok & (LANES - 1)).astype(F32)
        for k in range(TOP_K_IN_GROUP):
            tile_of = jnp.floor(pos[k] * (1.0 / tm))
            onehot_off = jnp.where(off_lane == pos[k] - tile_of * tm, 1.0, 0.0).astype(BF16)
            sel = tile_row == jnp.floor(pos_rows[k:k + 1, :] * (1.0 / tm))
            ids = jnp.concatenate([jnp.where(sel, tok_hi, 0.0), jnp.where(sel, tok_lo, 0.0)], axis=0)
            inv_s[...] += jnp.dot(ids.astype(BF16), onehot_off, preferred_element_type=F32)
        return carry

    lax.fori_loop(0, n_tok // chunk, place, 0, unroll=2)
    row_tok_ref[...] = (inv_s[:LANES, :] * LANES + inv_s[LANES:, :]).astype(jnp.int32)

    ti = jj.astype(F32)
    lane_sq = ee.astype(F32)
    is_exp = lane_sq < n_exp
    n_used = tile_end[:, n_exp - 1:n_exp]
    ti_c = jnp.minimum(ti, n_used - 1.0)
    tile_e = jnp.sum(jnp.where(jnp.logical_and(is_exp, tile_end <= ti_c), 1.0, 0.0), axis=1, keepdims=True)
    first = jnp.sum(jnp.where(jnp.logical_and(jnp.logical_and(is_exp, tile_start == ti), tiles > 0.0),
                              1.0, 0.0), axis=1, keepdims=True)
    nonempty = jnp.logical_and(is_exp, tiles > 0.0)
    def next_nonempty(e):
        nxt = jnp.min(jnp.where(jnp.logical_and(nonempty, lane_sq > e), lane_sq, float(LANES)),
                      axis=1, keepdims=True)
        return jnp.where(nxt >= LANES, e, nxt)

    next_e = next_nonempty(tile_e)
    next2_e = next_nonempty(next_e)
    group = jnp.sum(jnp.where(jnp.logical_and(nonempty, lane_sq < tile_e), 1.0, 0.0), axis=1, keepdims=True)
    wslot = group - 2.0 * jnp.floor(group * 0.5)
    fields = (tile_e, first, n_used, next_e, wslot, next2_e)
    assert len(fields) == N_TILE_FIELDS
    packed = jnp.zeros((LANES, LANES), F32)
    for f, val in enumerate(fields):
        packed = jnp.where(lane_sq == f, val, packed)
    tiles_ref[...] = packed.astype(jnp.int32)


def routing_plan(logits, n_groups, epg, tm):
    T = logits.shape[0]
    n_exp = n_groups * epg
    n_tiles = T * TOP_K_IN_GROUP // tm + n_exp
    assert n_tiles <= LANES and T <= LANES * LANES
    out, tiles, row_tok = pl.pallas_call(
        functools.partial(_plan_kernel, n_tok=T, n_groups=n_groups, epg=epg, tm=tm, chunk=PLAN_CHUNK),
        grid=(1,),
        in_specs=[pl.BlockSpec((T, LANES), lambda i: (0, 0))],
        out_specs=(pl.BlockSpec((T, LANES), lambda i: (0, 0)),
                   pl.BlockSpec((LANES, LANES), lambda i: (0, 0)),
                   pl.BlockSpec((LANES, tm), lambda i: (0, 0))),
        out_shape=(jax.ShapeDtypeStruct((T, LANES), F32), jax.ShapeDtypeStruct((LANES, LANES), jnp.int32),
                   jax.ShapeDtypeStruct((LANES, tm), jnp.int32)),
        scratch_shapes=[pltpu.VMEM((T, LANES), F32), pltpu.VMEM((T, LANES), F32),
                        pltpu.VMEM((2 * LANES, tm), F32)],
        compiler_params=_params(("arbitrary",), 40),
        name="routing_plan",
    )(logits)
    pos = out[:, :TOP_K_IN_GROUP].astype(jnp.int32).reshape(-1)
    tile_info = tiles[:n_tiles, :N_TILE_FIELDS].T.reshape(-1)
    return out, pos, tile_info, row_tok[:n_tiles].reshape(-1), n_tiles


def _row_copy(src_hbm, tok, dst_row, sem, rows=1):
    start = tok if rows == 1 else pl.multiple_of(tok * rows, rows)
    return pltpu.make_async_copy(src_hbm.at[pl.ds(start, rows)], dst_row, sem)


def _start_row_gather(src_hbm, idx_ref, base, n, dst_row, sem, rows=1):
    def body(r, carry):
        _row_copy(src_hbm, idx_ref[base + r], dst_row(r), sem, rows).start()
        return carry
    lax.fori_loop(0, n, body, 0, unroll=8)


def _wait_row_gather(src_hbm, n, dst_all, sem):
    pltpu.make_async_copy(src_hbm.at[pl.ds(0, n)], dst_all, sem).wait()


def _gmm_kernel(info_ref, row_tok_ref, x_hbm, wg_hbm, wu_hbm, wd_hbm, o_ref,
                xbuf, sem, wg_f, wu_f, wd_f, wsem, wg_bf, wu_bf, wd_bf, *, tm, n_tiles, w_off, n_chunks, tok_rows):
    i = pl.program_id(0)
    expert = info_ref[i]
    first = info_ref[n_tiles + i]
    n_used = info_ref[2 * n_tiles]
    next_expert = info_ref[3 * n_tiles + i]
    wslot = info_ref[4 * n_tiles + i]
    next2_expert = info_ref[5 * n_tiles + i]
    slot = i % GATHER_BUFS
    ahead = GATHER_BUFS - 1

    def wait_slot(s):
        _wait_row_gather(x_hbm, tm * tok_rows, xbuf.at[s], sem.at[s])

    def weight_copies(e, s):
        copies = []
        for hbm, buf in ((wg_hbm, wg_f), (wu_hbm, wu_f), (wd_hbm, wd_f)):
            rows = hbm.shape[1] // n_chunks
            for c in range(n_chunks):
                copies.append(pltpu.make_async_copy(hbm.at[w_off + e, pl.ds(c * rows, rows)],
                                                    buf.at[s, pl.ds(c * rows, rows)], wsem.at[s]))
        return copies

    def start_weights(e, s):
        for c, cp in enumerate(weight_copies(e, s)):
            cp.start(priority=c % DMA_PRIORITIES)

    @pl.when(jnp.logical_and(i == 0, n_used > 0))
    def _():
        start_weights(expert, 0)

        @pl.when(next_expert != expert)
        def _():
            start_weights(next_expert, 1)

        for t in range(ahead):
            _start_row_gather(x_hbm, row_tok_ref, min(t, n_tiles - 1) * tm, tm,
                              lambda r, t=t: xbuf.at[t, pl.ds(pl.multiple_of(r * tok_rows, tok_rows), tok_rows)],
                              sem.at[t], tok_rows)

    @pl.when(i < n_used)
    def _():
        @pl.when(first == 1)
        def _():
            for cp in weight_copies(expert, wslot):
                cp.wait()
            for src, dst in ((wg_f, wg_bf), (wu_f, wu_bf), (wd_f, wd_bf)):
                rows = dst.shape[0] // WEIGHT_CONVERT_CHUNKS

                def convert(c, carry, src=src, dst=dst, rows=rows):
                    r0 = pl.multiple_of(c * rows, rows)
                    dst[pl.ds(r0, rows), :] = src[wslot, pl.ds(r0, rows), :].astype(BF16)
                    return carry
                lax.fori_loop(0, WEIGHT_CONVERT_CHUNKS, convert, 0)

            @pl.when(next2_expert != next_expert)
            def _():
                start_weights(next2_expert, wslot)

        wait_slot(slot)
        xn = jnp.concatenate([xbuf[slot, pl.ds(j, tm, stride=tok_rows), :].astype(BF16)
                              for j in range(tok_rows)], axis=1)
        base = jnp.minimum(i + ahead, n_tiles - 1) * tm
        nslot = (i + ahead) % GATHER_BUFS
        for r in range(tm):
            _row_copy(x_hbm, row_tok_ref[base + r], xbuf.at[nslot, pl.ds(r * tok_rows, tok_rows)],
                      sem.at[nslot], tok_rows).start(priority=r % DMA_PRIORITIES)
        hg = jnp.dot(xn, wg_bf[...], preferred_element_type=F32)
        hu = jnp.dot(xn, wu_bf[...], preferred_element_type=F32)
        hh = (hg * jax.nn.sigmoid(hg)) * hu
        o_ref[...] = jnp.dot(hh.astype(BF16), wd_bf[...], preferred_element_type=F32)

    @pl.when(i >= n_used)
    def _():
        @pl.when(jnp.logical_and(i < n_used + ahead, n_used > 0))
        def _():
            wait_slot(slot)
        o_ref[...] = jnp.zeros(o_ref.shape, o_ref.dtype)

    @pl.when(i == n_tiles - 1)
    def _():
        for d in range(1, ahead + 1):
            @pl.when(n_used >= n_tiles - ahead + d)
            def _(d=d):
                wait_slot((n_tiles - 1 + d) % GATHER_BUFS)


def grouped_expert_mlp(xt, tile_info, row_tok, w_gate, w_up, w_down, layer, n_tiles, tm):
    _, D, F = w_gate.shape
    tok_rows = D // LANES
    layer_idx, depth = layer
    off = layer_idx * (w_gate.shape[0] // depth)
    any_spec = pl.BlockSpec(memory_space=pl.ANY)
    grid_spec = pltpu.PrefetchScalarGridSpec(
        num_scalar_prefetch=2,
        grid=(n_tiles,),
        in_specs=[any_spec, any_spec, any_spec, any_spec],
        out_specs=pl.BlockSpec((tm, D), lambda i, ti, rt: (i, 0)),
        scratch_shapes=[pltpu.VMEM((GATHER_BUFS, tm * tok_rows, LANES), F32),
                        pltpu.SemaphoreType.DMA((GATHER_BUFS,)),
                        pltpu.VMEM((2, D, F), F32), pltpu.VMEM((2, D, F), F32), pltpu.VMEM((2, F, D), F32),
                        pltpu.SemaphoreType.DMA((2,)),
                        pltpu.VMEM((D, F), BF16), pltpu.VMEM((D, F), BF16), pltpu.VMEM((F, D), BF16)],
    )
    return pl.pallas_call(
        functools.partial(_gmm_kernel, tm=tm, n_tiles=n_tiles, w_off=off, n_chunks=WEIGHT_DMA_CHUNKS,
                          tok_rows=tok_rows),
        grid_spec=grid_spec,
        out_shape=jax.ShapeDtypeStruct((n_tiles * tm, D), F32),
        compiler_params=_params(("arbitrary",), 56),
        name="grouped_expert_mlp",
    )(tile_info, row_tok, xt, w_gate, w_up, w_down)


def _combine_kernel(pos_ref, x_ref, plan_ref, y_hbm, o_ref, ybuf, sem, *, tc, top_k):
    i = pl.program_id(0)
    n = pl.num_programs(0)
    slot = i % 2

    def start(tile, s):
        base = tile * (tc * top_k)
        for r in range(tc):
            for k in range(top_k):
                _row_copy(y_hbm, pos_ref[base + r * top_k + k], ybuf.at[s, k, pl.ds(r, 1)], sem.at[s]).start()

    @pl.when(i == 0)
    def _():
        start(0, 0)

    for k in range(top_k):
        _wait_row_gather(y_hbm, tc, ybuf.at[slot, k], sem.at[slot])

    @pl.when(i + 1 < n)
    def _():
        start(i + 1, 1 - slot)

    acc = x_ref[...]
    for k in range(top_k):
        acc = acc + plan_ref[:, top_k + k:top_k + k + 1] * ybuf[slot, k]
    o_ref[...] = acc


def moe_combine(x, y_sorted, plan_out, pos, tc):
    T, D = x.shape
    top_k = TOP_K_IN_GROUP
    grid_spec = pltpu.PrefetchScalarGridSpec(
        num_scalar_prefetch=1,
        grid=(T // tc,),
        in_specs=[pl.BlockSpec((tc, D), lambda i, p: (i, 0)),
                  pl.BlockSpec((tc, LANES), lambda i, p: (i, 0)),
                  pl.BlockSpec(memory_space=pl.ANY)],
        out_specs=pl.BlockSpec((tc, D), lambda i, p: (i, 0)),
        scratch_shapes=[pltpu.VMEM((2, top_k, tc, D), F32), pltpu.SemaphoreType.DMA((2,))],
    )
    return pl.pallas_call(
        functools.partial(_combine_kernel, tc=tc, top_k=top_k),
        grid_spec=grid_spec,
        out_shape=jax.ShapeDtypeStruct((T, D), F32),
        compiler_params=_params(("arbitrary",), 32),
        name="moe_combine",
    )(pos, x, plan_out, y_sorted)


def _combine_mm_kernel(pos_ref, x_ref, plan_ref, y_hbm, g_ref, w_hbm, xnew_ref, o_ref,
                       ybuf, ysem, xn_ref, w_bf, stage, wsem, *, layer, tc, chunk, tn, top_k):
    i = pl.program_id(0)
    n = pl.num_programs(0)
    slot = i % 2

    def start(tile, s):
        base = tile * (tc * top_k)
        for r in range(tc):
            for k in range(top_k):
                _row_copy(y_hbm, pos_ref[base + r * top_k + k], ybuf.at[s, k, pl.ds(r, 1)], ysem.at[s]).start()

    def wait(s):
        for k in range(top_k):
            _wait_row_gather(y_hbm, tc, ybuf.at[s, k], ysem.at[s])

    @pl.when(i == 0)
    def _():
        start(0, 0)
        _stage_weight_bf16(w_hbm, layer, w_bf, stage, wsem, tn)

    wait(slot)
    for r0 in range(0, tc, chunk):
        rows = slice(r0, r0 + chunk)
        acc = x_ref[rows, :]
        for k in range(top_k):
            acc = acc + plan_ref[rows, top_k + k:top_k + k + 1] * ybuf[slot, k, rows, :]
        xnew_ref[rows, :] = acc
        xn_ref[rows, :] = _rms_rows(acc, g_ref[...]).astype(BF16)
    start(jnp.minimum(i + 1, n - 1), 1 - slot)
    for j in range(o_ref.shape[1] // tn):
        o_ref[:, j * tn:(j + 1) * tn] = jnp.dot(xn_ref[...], w_bf[:, j * tn:(j + 1) * tn],
                                                preferred_element_type=F32).astype(o_ref.dtype)

    @pl.when(i == n - 1)
    def _():
        wait(1 - slot)


def combine_norm_matmul(x, y_sorted, plan_out, pos, g, w, layer, out_dtype, tc):
    T, D = x.shape
    N = w.shape[2]
    top_k = TOP_K_IN_GROUP
    tn = _col_tile(N)
    grid_spec = pltpu.PrefetchScalarGridSpec(
        num_scalar_prefetch=1,
        grid=(T // tc,),
        in_specs=[pl.BlockSpec((tc, D), lambda i, p: (i, 0)),
                  pl.BlockSpec((tc, LANES), lambda i, p: (i, 0)),
                  pl.BlockSpec(memory_space=pl.ANY),
                  pl.BlockSpec((1, D), lambda i, p: (0, 0)),
                  pl.BlockSpec(memory_space=pl.ANY)],
        out_specs=(pl.BlockSpec((tc, D), lambda i, p: (i, 0)),
                   pl.BlockSpec((tc, N), lambda i, p: (i, 0))),
        scratch_shapes=[pltpu.VMEM((2, top_k, tc, D), F32), pltpu.SemaphoreType.DMA((2,)),
                        pltpu.VMEM((tc, D), BF16), pltpu.VMEM((D, N), BF16), pltpu.VMEM((2, D, tn), F32),
                        pltpu.SemaphoreType.DMA((2,))],
    )
    return pl.pallas_call(
        functools.partial(_combine_mm_kernel, layer=layer, tc=tc, chunk=min(tc, 128), tn=tn, top_k=top_k),
        grid_spec=grid_spec,
        out_shape=(jax.ShapeDtypeStruct((T, D), F32), jax.ShapeDtypeStruct((T, N), out_dtype)),
        compiler_params=_params(("arbitrary",), 56),
        name="combine_norm_matmul",
    )(pos, x, plan_out, y_sorted, g.reshape(1, D), w)


def hier_moe_experts(logits, xt, n_groups, epg, w_gate, w_up, w_down, layer):
    plan_out, pos, tile_info, row_tok, n_tiles = routing_plan(logits, n_groups, epg, TM_EXPERT)
    y_sorted = grouped_expert_mlp(xt, tile_info, row_tok, w_gate, w_up, w_down, layer, n_tiles, TM_EXPERT)
    return y_sorted, plan_out, pos


def _pad_head_cols(w, heads, real):
    r = w.shape[0]
    return jnp.pad(w.reshape(r, heads, real), ((0, 0), (0, 0), (0, HEAD_PAD - real))).reshape(r, heads * HEAD_PAD)


def _rope_tables(positions):
    half = QK_ROPE // 2
    inv_freq = ROPE_THETA ** (-jnp.arange(half, dtype=F32) / half)
    ang = positions.astype(F32).reshape(-1, 1) * inv_freq[None, :]
    cos, sin = jnp.cos(ang), jnp.sin(ang)
    zeros = jnp.zeros((ang.shape[0], LANES - QK_ROPE), F32)
    return (jnp.concatenate([cos, cos, zeros], axis=1),
            jnp.concatenate([-sin, sin, zeros], axis=1))


def kernel(x, mem, positions, norm_mix_g, norm_ffn_g, w_o, mem_norm_g, w_mem_kv, mem_q_norm_g, mem_k_norm_g, w_in_a, conv_w, conv_b, w_lru_r, b_lru_r, w_lru_i, b_lru_i, lru_lambda, kv_in_norm_g, w_dkv, kv_latent_norm_g, w_uk, w_uv, k_head_norm_g, w_in_b, q_latent_norm_g, w_uq, q_head_norm_g, w_router_grp, b_router_grp, w_router_exp, b_router_exp, w_exp_gate, w_exp_up, w_exp_down):
    B, S, D = x.shape
    T = B * S
    M = mem.shape[1]
    depth = norm_mix_g.shape[0]
    n_a = w_in_a.shape[0]
    lru_w = lru_lambda.shape[1]
    mem_w = w_mem_kv.shape[2] // 2
    heads = w_uv.shape[1] // V_DIM
    kv_rank = kv_latent_norm_g.shape[0]
    q_rank = q_latent_norm_g.shape[1]
    n_groups, epg = w_exp_gate.shape[1], w_exp_gate.shape[2]
    n_exp = n_groups * epg
    d_exp = w_exp_gate.shape[-1]

    xr = x.reshape(T, D)
    mem2 = mem.reshape(B * M, D)
    cos_t, sin_t = _rope_tables(positions)
    wg_all = w_exp_gate.reshape(depth * n_exp, D, d_exp)
    wu_all = w_exp_up.reshape(depth * n_exp, D, d_exp)
    wd_all = w_exp_down.reshape(depth * n_exp, d_exp, D)
    pad_r = LANES - n_groups - n_exp
    w_router = jnp.concatenate([w_router_grp, w_router_exp, jnp.zeros((depth, D, pad_r), F32)], axis=2)
    b_router = jnp.concatenate([b_router_grp, b_router_exp, jnp.zeros((depth, pad_r), F32)], axis=1)

    def shared_kv(x_res):
        lat_cols = kv_rank + LANES
        w_dkv_pad = jnp.pad(w_dkv, ((0, 0), (0, lat_cols - w_dkv.shape[1])))[None]
        ckv = norm_matmul(x_res, kv_in_norm_g, w_dkv_pad, 0, F32, tm=TM_PROJ)
        eye = jnp.pad(jnp.eye(QK_ROPE, dtype=F32), ((0, LANES - QK_ROPE), (0, 0)))
        eye3 = jnp.broadcast_to(eye[:, None, :], (LANES, heads, QK_ROPE))
        rope_rows = jnp.pad(eye3, ((0, 0), (0, 0), (QK_NOPE, HEAD_PAD - QK_DIM))).reshape(LANES, heads * HEAD_PAD)
        wk = jnp.concatenate([_pad_head_cols(w_uk, heads, QK_NOPE), rope_rows], axis=0).astype(BF16)
        kg = jnp.pad(k_head_norm_g, (0, HEAD_PAD - QK_DIM)).reshape(1, HEAD_PAD)
        k_flat, v_flat = latent_heads(ckv, lat_cols, kv_rank, kv_latent_norm_g, wk, kg, cos_t, sin_t,
                                      heads, tm=TM_HEADS, w_v=w_uv.astype(BF16))
        return k_flat.reshape(B, S, -1), v_flat.reshape(B, S, -1)

    def in_proj(x_res, pending, g, w, idx):
        if pending is None:
            return x_res, norm_matmul(x_res, g, w, idx, BF16, tm=TM_PROJ)
        return combine_norm_matmul(x_res, *pending, g, w, idx, BF16, TC_COMBINE)

    k_sh = v_sh = None
    pending = None
    for l in range(depth):
        kv_mem = norm_matmul(mem2, mem_norm_g[l], w_mem_kv, l, BF16, tm=TM_PROJ).reshape(B, M, 2 * mem_w)
        if l < n_a:
            xr, proj = in_proj(xr, pending, norm_mix_g[l], w_in_a, l)
            y_mix = rglru(proj.reshape(B, S, -1), conv_w[l], conv_b[l], w_lru_r[l], b_lru_r[l],
                          w_lru_i[l], b_lru_i[l], lru_lambda[l], nb=LRU_BLOCKS_PER_STEP,
                          tc=TC_LRU).reshape(T, lru_w)
            q_col = 2 * lru_w // mem_w
        else:
            j = l - n_a
            xr, proj = in_proj(xr, pending, norm_mix_g[l], w_in_b, j)
            if l == n_a:
                k_sh, v_sh = shared_kv(xr)
            wq = _pad_head_cols(w_uq[j], heads, QK_DIM).astype(BF16)
            qg = jnp.pad(q_head_norm_g[j] * (QK_DIM ** -0.5 * math.log2(math.e)),
                         (0, HEAD_PAD - QK_DIM)).reshape(1, HEAD_PAD)
            q = latent_heads(proj, q_rank, q_rank, q_latent_norm_g[j], wq, qg, cos_t, sin_t, heads, tm=TM_HEADS)
            y_mix = flash_attention(q.reshape(B, S, -1), k_sh, v_sh, heads, tq=TQ_FLASH).reshape(T, heads * V_DIM)
            q_col = q_rank // mem_w
        y_mem = mem_attention(proj, q_col, kv_mem, mem_q_norm_g[l], mem_k_norm_g[l], B, ts=TS_MEM)
        xr, logits, xt = out_proj(y_mix, y_mem, w_o, l, xr, norm_ffn_g[l], w_router[l],
                                  b_router[l].reshape(1, LANES), tm=TM_PROJ)
        pending = hier_moe_experts(logits, xt, n_groups, epg, wg_all, wu_all, wd_all, (l, depth))
    xr = moe_combine(xr, *pending, TC_COMBINE)
    return xr.reshape(B, S, D)
```
